```python
import math
import jax, jax.numpy as jnp
from jax import lax
import numpy as np

D_MODEL = 1024
BATCH = 8
SEQ = 2048
DEPTH = 1
DEC_BATCH = 128
DEC_SEQ = 8
PAST_LEN = 16384
PAGE_SIZE = 128

D_MIX = D_MODEL
D_A = D_MIX // 2
D_B = D_MIX - D_A
HEAD_DIM_A = 128
N_HEADS_A = D_A // HEAD_DIM_A
CHUNK = 128
SSM_GROUP = 16
N_SSM_GROUPS = D_B // SSM_GROUP
SSM_STATE = 64
N_EXPERT_GROUPS = 4
EXPERTS_PER_GROUP = 8
N_EXPERTS = N_EXPERT_GROUPS * EXPERTS_PER_GROUP
TOP_K_INNER = 2
D_EXPERT = D_MODEL // 4
EXPERT_BLOCK = 128
EPS = 1e-6

kernel_name = 'hymba_gmlp_s5_hmoe_step'


def rms_norm(x, g):
    xf = x.astype(jnp.float32)
    y = xf * lax.rsqrt(jnp.mean(xf * xf, axis=-1, keepdims=True) + EPS)
    return (y * g.astype(jnp.float32)).astype(x.dtype)


def modulate(x, g, shift, scale):
    return rms_norm(x, g) * (1 + scale) + shift


def head_layer_norm(v, g, b):
    vf = v.astype(jnp.float32)
    mu = jnp.mean(vf, axis=-1, keepdims=True)
    var = jnp.mean(jnp.square(vf - mu), axis=-1, keepdims=True)
    y = (vf - mu) * lax.rsqrt(var + EPS)
    return (y * g.astype(jnp.float32) + b.astype(jnp.float32)).astype(v.dtype)


def chunk_spatial_mix(v, w_s, b_s):
    Bn, L, H, dh = v.shape
    n_chunks = -(-L // CHUNK)
    pad = n_chunks * CHUNK - L
    vp = jnp.pad(v, ((0, 0), (0, pad), (0, 0), (0, 0))).reshape(Bn, n_chunks, CHUNK, H, dh)
    mask = jnp.tril(jnp.ones((CHUNK, CHUNK), dtype=bool))[None]
    w = jnp.where(mask, w_s, jnp.zeros_like(w_s)).astype(v.dtype)
    out = jnp.einsum('hts,bcshd->bcthd', w, vp) + b_s.T.astype(v.dtype)[None, None, :, :, None]
    return out.reshape(Bn, n_chunks * CHUNK, H, dh)[:, :L]


def ssm_discretize(lam_re, lam_im, log_dt, b_re, b_im):
    dt = jnp.exp(log_dt.astype(jnp.float32))[:, None]
    lr = lam_re.astype(jnp.float32)
    li = lam_im.astype(jnp.float32)
    mag = jnp.exp(lr * dt)
    ar = mag * jnp.cos(li * dt)
    ai = mag * jnp.sin(li * dt)
    den = lr * lr + li * li
    cr = ((ar - 1) * lr + ai * li) / den
    ci = (ai * lr - (ar - 1) * li) / den
    br_, bi_ = b_re.astype(jnp.float32), b_im.astype(jnp.float32)
    br = cr[..., None] * br_ - ci[..., None] * bi_
    bi = cr[..., None] * bi_ + ci[..., None] * br_
    return ar, ai, br, bi


def ssm_combine(left, right):
    a1r, a1i, b1r, b1i = left
    a2r, a2i, b2r, b2i = right
    ar = a2r * a1r - a2i * a1i
    ai = a2r * a1i + a2i * a1r
    br = a2r * b1r - a2i * b1i + b2r
    bi = a2r * b1i + a2i * b1r + b2i
    return ar, ai, br, bi


def s5_mixer(xs, h0_re, h0_im, lam_re, lam_im, log_dt, b_re, b_im, c_re, c_im, d_skip, w_glu, b_glu):
    Bn, L, _ = xs.shape
    u = xs.astype(jnp.float32).reshape(Bn, L, N_SSM_GROUPS, SSM_GROUP)
    ar, ai, br, bi = ssm_discretize(lam_re, lam_im, log_dt, b_re, b_im)
    bu_re = jnp.einsum('blgh,gph->blgp', u, br)
    bu_im = jnp.einsum('blgh,gph->blgp', u, bi)
    h0r = h0_re.astype(jnp.float32)
    h0i = h0_im.astype(jnp.float32)
    bu_re = bu_re.at[:, 0].add(ar * h0r - ai * h0i)
    bu_im = bu_im.at[:, 0].add(ar * h0i + ai * h0r)
    a_re = jnp.broadcast_to(ar, bu_re.shape)
    a_im = jnp.broadcast_to(ai, bu_im.shape)
    _, _, h_re, h_im = lax.associative_scan(ssm_combine, (a_re, a_im, bu_re, bu_im), axis=1)
    y = (jnp.einsum('blgp,ghp->blgh', h_re, c_re.astype(jnp.float32))
         - jnp.einsum('blgp,ghp->blgh', h_im, c_im.astype(jnp.float32))
         + d_skip.astype(jnp.float32) * u)
    y = jax.nn.gelu(y.reshape(Bn, L, D_B))
    y = y * jax.nn.sigmoid(y @ w_glu.astype(jnp.float32) + b_glu.astype(jnp.float32))
    return y.astype(xs.dtype), h_re[:, -1].astype(h0_re.dtype), h_im[:, -1].astype(h0_im.dtype)


def hier_moe(x, w_group, b_group, w_expert, b_expert, w1, w3, w2):
    Bn, L, D = x.shape
    xt = x.reshape(-1, D)
    T = xt.shape[0]
    g_prob = jax.nn.softmax((xt @ w_group + b_group).astype(jnp.float32), axis=-1)
    g_w, g_idx = lax.top_k(g_prob, 1)
    e_logits = (xt @ w_expert + b_expert).astype(jnp.float32).reshape(T, N_EXPERT_GROUPS, EXPERTS_PER_GROUP)
    e_in = jnp.take_along_axis(e_logits, g_idx[:, :, None], axis=1)[:, 0]
    e_top, e_idx = lax.top_k(e_in, TOP_K_INNER)
    gate = g_w * jax.nn.softmax(e_top, axis=-1)
    expert = g_idx * EXPERTS_PER_GROUP + e_idx
    n_assign = T * TOP_K_INNER
    flat_e = expert.reshape(-1)
    flat_tok = jnp.repeat(jnp.arange(T), TOP_K_INNER)
    flat_g = gate.reshape(-1)
    order = jnp.argsort(flat_e)
    e_s, tok_s, g_s = flat_e[order], flat_tok[order], flat_g[order]
    counts = jnp.bincount(flat_e, length=N_EXPERTS)
    starts = jnp.cumsum(counts) - counts
    padded = (counts + EXPERT_BLOCK - 1) // EXPERT_BLOCK * EXPERT_BLOCK
    pend = jnp.cumsum(padded)
    pstarts = pend - padded
    dest = pstarts[e_s] + (jnp.arange(n_assign) - starts[e_s])
    n_blocks = (n_assign + N_EXPERTS * (EXPERT_BLOCK - 1) + EXPERT_BLOCK - 1) // EXPERT_BLOCK
    buf = jnp.zeros((n_blocks * EXPERT_BLOCK, D), x.dtype).at[dest].set(xt[tok_s])
    block_expert = jnp.clip(jnp.searchsorted(pend, jnp.arange(n_blocks) * EXPERT_BLOCK, side='right'), 0, N_EXPERTS - 1)

    def run_block(args):
        xb, e = args
        h = jax.nn.silu(xb @ w1[e]) * (xb @ w3[e])
        return h @ w2[e]

    yb = lax.map(run_block, (buf.reshape(n_blocks, EXPERT_BLOCK, D), block_expert))
    y_s = yb.reshape(-1, D)[dest].astype(jnp.float32)
    out = jnp.zeros((T, D), jnp.float32).at[tok_s].add(g_s[:, None] * y_s)
    return out.astype(x.dtype).reshape(Bn, L, D)


def trunk_layer(x, c, h0_re, h0_im, w_ada, b_ada, g_norm1, g_norm2, w_in, ln_g, ln_b, w_s, b_s,
                lam_re, lam_im, log_dt, ssm_b_re, ssm_b_im, ssm_c_re, ssm_c_im, ssm_d, w_glu, b_glu,
                w_out, w_group, b_group, w_expert, b_expert, w1, w3, w2):
    Bn, L, _ = x.shape
    ada = jax.nn.silu(c) @ w_ada + b_ada
    sh1, sc1, gt1, sh2, sc2, gt2 = jnp.split(ada[:, None, :], 6, axis=-1)
    h = modulate(x, g_norm1, sh1, sc1)
    proj = h @ w_in
    u = jax.nn.gelu(proj[..., :D_A])
    v = jax.nn.gelu(proj[..., D_A:2 * D_A]).reshape(Bn, L, N_HEADS_A, HEAD_DIM_A)
    s_in = proj[..., 2 * D_A:]
    v = head_layer_norm(v, ln_g, ln_b)
    a_out = u * chunk_spatial_mix(v, w_s, b_s).reshape(Bn, L, D_A)
    b_out, h_re, h_im = s5_mixer(s_in, h0_re, h0_im, lam_re, lam_im, log_dt, ssm_b_re, ssm_b_im,
                                 ssm_c_re, ssm_c_im, ssm_d, w_glu, b_glu)
    mix = jnp.concatenate([a_out, b_out], axis=-1) @ w_out
    x = x + gt1 * mix
    h = modulate(x, g_norm2, sh2, sc2)
    x = x + gt2 * hier_moe(h, w_group, b_group, w_expert, b_expert, w1, w3, w2)
    return x, h_re, h_im, v.reshape(Bn, L, D_A)


def setup_inputs(seed: int = 0) -> dict:
    key = jax.random.key(seed)
    ks = iter(jax.random.split(key, 40))
    f32 = jnp.float32

    def nrm(shape, scale):
        return jax.random.normal(next(ks), shape, f32) * scale

    x_prompt = nrm((BATCH, SEQ, D_MODEL), 1.0)
    x_sample = nrm((DEC_BATCH, DEC_SEQ, D_MODEL), 1.0)
    state_ssm_re = nrm((DEPTH, DEC_BATCH, N_SSM_GROUPS, SSM_STATE), 0.5)
    state_ssm_im = nrm((DEPTH, DEC_BATCH, N_SSM_GROUPS, SSM_STATE), 0.5)
    c_prompt = nrm((BATCH, D_MODEL), 1.0)
    c_sample = nrm((DEC_BATCH, D_MODEL), 1.0)
    w_ada = nrm((DEPTH, D_MODEL, 6 * D_MODEL), 0.5 * D_MODEL ** -0.5)
    b_ada = nrm((DEPTH, 6 * D_MODEL), 0.01)
    g_norm1 = 1.0 + nrm((DEPTH, D_MODEL), 0.02)
    g_norm2 = 1.0 + nrm((DEPTH, D_MODEL), 0.02)
    w_in = nrm((DEPTH, D_MODEL, 2 * D_A + D_B), D_MODEL ** -0.5)
    ln_g = 1.0 + nrm((DEPTH, N_HEADS_A, HEAD_DIM_A), 0.02)
    ln_b = nrm((DEPTH, N_HEADS_A, HEAD_DIM_A), 0.02)
    row_scale = (1.0 / jnp.sqrt(jnp.arange(1, CHUNK + 1, dtype=f32)))[:, None]
    w_s = nrm((DEPTH, N_HEADS_A, CHUNK, CHUNK), 1.0) * row_scale
    b_s = 1.0 + nrm((DEPTH, N_HEADS_A, CHUNK), 0.02)
    lam_re = -0.5 + nrm((DEPTH, N_SSM_GROUPS, SSM_STATE), 0.01)
    lam_im = jnp.pi * jnp.arange(SSM_STATE, dtype=f32) + nrm((DEPTH, N_SSM_GROUPS, SSM_STATE), 0.01)
    log_dt = jax.random.uniform(next(ks), (DEPTH, N_SSM_GROUPS), f32, math.log(0.001), math.log(0.1))
    ssm_b_re = nrm((DEPTH, N_SSM_GROUPS, SSM_STATE, SSM_GROUP), (2 * SSM_GROUP) ** -0.5)
    ssm_b_im = nrm((DEPTH, N_SSM_GROUPS, SSM_STATE, SSM_GROUP), (2 * SSM_GROUP) ** -0.5)
    ssm_c_re = nrm((DEPTH, N_SSM_GROUPS, SSM_GROUP, SSM_STATE), (2 * SSM_STATE) ** -0.5)
    ssm_c_im = nrm((DEPTH, N_SSM_GROUPS, SSM_GROUP, SSM_STATE), (2 * SSM_STATE) ** -0.5)
    ssm_d = nrm((DEPTH, N_SSM_GROUPS, SSM_GROUP), 1.0)
    w_glu = nrm((DEPTH, D_B, D_B), D_B ** -0.5)
    b_glu = nrm((DEPTH, D_B), 0.01)
    w_out = nrm((DEPTH, D_MIX, D_MODEL), D_MIX ** -0.5)
    w_group = nrm((DEPTH, D_MODEL, N_EXPERT_GROUPS), D_MODEL ** -0.5)
    b_group = nrm((DEPTH, N_EXPERT_GROUPS), 0.01)
    w_expert = nrm((DEPTH, D_MODEL, N_EXPERTS), D_MODEL ** -0.5)
    b_expert = nrm((DEPTH, N_EXPERTS), 0.01)
    w1 = nrm((DEPTH, N_EXPERTS, D_MODEL, D_EXPERT), D_MODEL ** -0.5)
    w3 = nrm((DEPTH, N_EXPERTS, D_MODEL, D_EXPERT), D_MODEL ** -0.5)
    w2 = nrm((DEPTH, N_EXPERTS, D_EXPERT, D_MODEL), D_EXPERT ** -0.5)
    g_final = 1.0 + nrm((D_MODEL,), 0.02)
    return {'x_prompt': x_prompt, 'x_sample': x_sample, 'state_ssm_re': state_ssm_re, 'state_ssm_im': state_ssm_im,
            'c_prompt': c_prompt, 'c_sample': c_sample, 'w_ada': w_ada, 'b_ada': b_ada,
            'g_norm1': g_norm1, 'g_norm2': g_norm2, 'w_in': w_in, 'ln_g': ln_g, 'ln_b': ln_b,
            'w_s': w_s, 'b_s': b_s, 'lam_re': lam_re, 'lam_im': lam_im, 'log_dt': log_dt,
            'ssm_b_re': ssm_b_re, 'ssm_b_im': ssm_b_im, 'ssm_c_re': ssm_c_re, 'ssm_c_im': ssm_c_im,
            'ssm_d': ssm_d, 'w_glu': w_glu, 'b_glu': b_glu, 'w_out': w_out,
            'w_group': w_group, 'b_group': b_group, 'w_expert': w_expert, 'b_expert': b_expert,
            'w1': w1, 'w3': w3, 'w2': w2, 'g_final': g_final}


def reference(x_prompt, x_sample, state_ssm_re, state_ssm_im, c_prompt, c_sample, w_ada, b_ada,
              g_norm1, g_norm2, w_in, ln_g, ln_b, w_s, b_s, lam_re, lam_im, log_dt,
              ssm_b_re, ssm_b_im, ssm_c_re, ssm_c_im, ssm_d, w_glu, b_glu, w_out,
              w_group, b_group, w_expert, b_expert, w1, w3, w2, g_final):
    xp, xs = x_prompt, x_sample
    p_re, p_im, s_re, s_im, s_v = [], [], [], [], []
    for l in range(DEPTH):
        lw = (w_ada[l], b_ada[l], g_norm1[l], g_norm2[l], w_in[l], ln_g[l], ln_b[l], w_s[l], b_s[l],
              lam_re[l], lam_im[l], log_dt[l], ssm_b_re[l], ssm_b_im[l], ssm_c_re[l], ssm_c_im[l], ssm_d[l],
              w_glu[l], b_glu[l], w_out[l], w_group[l], b_group[l], w_expert[l], b_expert[l],
              w1[l], w3[l], w2[l])
        h0 = jnp.zeros((xp.shape[0], N_SSM_GROUPS, SSM_STATE), state_ssm_re.dtype)
        xp, hr, hi, _ = trunk_layer(xp, c_prompt, h0, h0, *lw)
        p_re.append(hr)
        p_im.append(hi)
        xs, hr, hi, v = trunk_layer(xs, c_sample, state_ssm_re[l], state_ssm_im[l], *lw)
        s_re.append(hr)
        s_im.append(hi)
        s_v.append(v)
    y_prompt = rms_norm(xp, g_final)
    y_sample = rms_norm(xs, g_final)
    return (y_prompt, y_sample, jnp.stack(p_re), jnp.stack(p_im), jnp.stack(s_re), jnp.stack(s_im), jnp.stack(s_v))
```

```python
import functools

import jax
import jax.numpy as jnp
from jax import lax
from jax.experimental import pallas as pl
from jax.experimental.pallas import tpu as pltpu

F32 = jnp.float32
BF16 = jnp.bfloat16

D_MODEL = 1024
D_A = 512
D_B = 512
N_HEADS = 4
HEAD_DIM = 128
CHUNK = 128
N_SSM_GROUPS = 32
SSM_GROUP = 16
SSM_STATE = 64
N_SLABS = 4
SLAB_GROUPS = N_SSM_GROUPS // N_SLABS
SLAB_STATES = SLAB_GROUPS * SSM_STATE
SLAB_COLS = 2 * SLAB_STATES
STATE_COLS = N_SLABS * SLAB_COLS
N_EXPERT_GROUPS = 4
EXPERTS_PER_GROUP = 8
N_EXPERTS = 32
D_EXPERT = 256
EPS = 1e-6

LANES = 128
SUBLANES = 8
ROUTE_LANES = LANES
MOE_BLOCK = 128
SMEM_PAD = 1024
VMEM_LIMIT = 56 * 1024 * 1024


def _rms(xf, g):
    ms = jnp.mean(xf * xf, axis=-1, keepdims=True)
    return xf * lax.rsqrt(ms + EPS) * g


def _ada_kernel(c_ref, w_ref, b_ref, o_ref):
    s = jax.nn.silu(c_ref[...]).astype(BF16)
    o_ref[...] = jnp.dot(s, w_ref[...], preferred_element_type=F32) + b_ref[...]


def _ada(c_all, w_bf, b):
    m = c_all.shape[0]
    n = w_bf.shape[1]
    bn = 1024
    return pl.pallas_call(
        _ada_kernel,
        grid=(n // bn,),
        in_specs=[pl.BlockSpec((m, D_MODEL), lambda j: (0, 0)),
                  pl.BlockSpec((D_MODEL, bn), lambda j: (0, j)),
                  pl.BlockSpec((1, bn), lambda j: (0, j))],
        out_specs=pl.BlockSpec((m, bn), lambda j: (0, j)),
        out_shape=jax.ShapeDtypeStruct((m, n), F32),
        name="ada",
    )(c_all, w_bf, b)


def _disc_kernel(lre_ref, lim_ref, ldt_ref, bre_ref, bim_ref, ar_ref, ai_ref, br_ref, bi_ref):
    dt = jnp.exp(ldt_ref[...])
    lr = lre_ref[...]
    li = lim_ref[...]
    mag = jnp.exp(lr * dt)
    ar = mag * jnp.cos(li * dt)
    ai = mag * jnp.sin(li * dt)
    den = lr * lr + li * li
    cr = ((ar - 1) * lr + ai * li) / den
    ci = (ai * lr - (ar - 1) * li) / den
    ar_ref[...] = ar
    ai_ref[...] = ai
    bre = bre_ref[...]
    bim = bim_ref[...]
    br_ref[...] = cr * bre - ci * bim
    bi_ref[...] = cr * bim + ci * bre


def _discretize(lam_re, lam_im, log_dt, b_re, b_im):
    g, p, h = b_re.shape
    o1 = jax.ShapeDtypeStruct((g, 1, p), F32)
    o2 = jax.ShapeDtypeStruct((g, h, p), F32)
    return pl.pallas_call(_disc_kernel, out_shape=(o1, o1, o2, o2), name="ssm_disc")(
        lam_re.reshape(g, 1, p), lam_im.reshape(g, 1, p), log_dt.reshape(g, 1, 1),
        b_re.transpose(0, 2, 1), b_im.transpose(0, 2, 1))


def _front(r0, nrows, h_scr, win_ref, lng_ref, lnb_ref, u_scr, vn_scr, s_scr):
    rows = slice(r0, r0 + nrows)
    proj = jnp.dot(h_scr[rows, :], win_ref[...], preferred_element_type=F32)
    u_scr[rows, :] = jax.nn.gelu(proj[:, :D_A])
    vraw = jax.nn.gelu(proj[:, D_A:2 * D_A])
    for h in range(N_HEADS):
        cols = slice(h * HEAD_DIM, (h + 1) * HEAD_DIM)
        vh = vraw[:, cols]
        mu = jnp.mean(vh, axis=-1, keepdims=True)
        dv = vh - mu
        var = jnp.mean(dv * dv, axis=-1, keepdims=True)
        vn_scr[rows, cols] = dv * lax.rsqrt(var + EPS) * lng_ref[:, cols] + lnb_ref[:, cols]
    for k in range(N_SLABS):
        s_scr[k, rows, :] = proj[:, 2 * D_A + k * LANES:2 * D_A + (k + 1) * LANES]


def _scan_slab(bu_scr, state, avec_ref, k, rows_per_step, steps):
    cw = 256
    unroll = min(SUBLANES, steps)
    for c in range(SLAB_STATES // cw):
        c_re = slice(c * cw, (c + 1) * cw)
        c_im = slice(SLAB_STATES + c * cw, SLAB_STATES + (c + 1) * cw)
        s_re = slice(k * SLAB_COLS + c * cw, k * SLAB_COLS + (c + 1) * cw)
        s_im = slice(k * SLAB_COLS + SLAB_STATES + c * cw, k * SLAB_COLS + SLAB_STATES + (c + 1) * cw)
        ar = jnp.broadcast_to(avec_ref[:, s_re], (SUBLANES, cw))
        ai = jnp.broadcast_to(avec_ref[:, s_im], (SUBLANES, cw))

        def row_chunk(rc, carry, c_re=c_re, c_im=c_im, s_re=s_re, s_im=s_im, ar=ar, ai=ai):
            r0 = pl.multiple_of(rc * SUBLANES, SUBLANES)
            sr = state[pl.ds(r0, SUBLANES), s_re]
            si = state[pl.ds(r0, SUBLANES), s_im]

            def step_block(tb, st):
                sr, si = st
                for tt in range(unroll):
                    row = pl.multiple_of((tb * unroll + tt) * rows_per_step + r0, SUBLANES)
                    br = bu_scr[pl.ds(row, SUBLANES), c_re]
                    bi = bu_scr[pl.ds(row, SUBLANES), c_im]
                    nr = ar * sr - ai * si + br
                    ni = ar * si + ai * sr + bi
                    bu_scr[pl.ds(row, SUBLANES), c_re] = nr
                    bu_scr[pl.ds(row, SUBLANES), c_im] = ni
                    sr, si = nr, ni
                return sr, si

            sr, si = lax.fori_loop(0, steps // unroll, step_block, (sr, si))
            state[pl.ds(r0, SUBLANES), s_re] = sr
            state[pl.ds(r0, SUBLANES), s_im] = si
            return carry

        lax.fori_loop(0, rows_per_step // SUBLANES, row_chunk, 0)


def _s5(st_ref, bu_scr, yt_ref, state, avec_ref, wb_ref, wc_ref, rows_per_step, steps):
    for k in range(N_SLABS):
        bu_scr[...] = jnp.dot(st_ref[k].astype(BF16), wb_ref[k], preferred_element_type=F32)
        _scan_slab(bu_scr, state, avec_ref, k, rows_per_step, steps)
        yt_ref[k] = jnp.dot(bu_scr[...].astype(BF16), wc_ref[k], preferred_element_type=F32)


def _back(r0, nrows, s_scr, ab_scr, wglu_ref, bglu_ref, wout_ref):
    rows = slice(r0, r0 + nrows)
    y = jax.nn.gelu(jnp.concatenate([s_scr[k, rows, :] for k in range(N_SLABS)], axis=-1))
    gate = jnp.dot(y.astype(BF16), wglu_ref[...], preferred_element_type=F32) + bglu_ref[...]
    ab_scr[rows, D_A:] = (y * jax.nn.sigmoid(gate)).astype(BF16)
    return jnp.dot(ab_scr[rows, :], wout_ref[...], preferred_element_type=F32)


def _route(h2, wrh_ref, wrl_ref, brt_ref):
    hi = h2.astype(BF16)
    lo = (h2 - hi.astype(F32)).astype(BF16)
    logits = (jnp.dot(hi, wrh_ref[...], preferred_element_type=F32)
              + jnp.dot(lo, wrh_ref[...], preferred_element_type=F32)
              + jnp.dot(hi, wrl_ref[...], preferred_element_type=F32)) + brt_ref[...]
    n = logits.shape[0]
    lane = lax.broadcasted_iota(jnp.int32, (n, ROUTE_LANES), 1)
    lane_f = lane.astype(F32)
    big = jnp.float32(1e9)
    ninf = jnp.float32(-jnp.inf)
    is_g = lane < N_EXPERT_GROUPS
    gl = jnp.where(is_g, logits, ninf)
    gmax = jnp.max(gl, axis=-1, keepdims=True)
    gidx = jnp.min(jnp.where(gl == gmax, lane_f, big), axis=-1, keepdims=True)
    gsum = jnp.sum(jnp.where(is_g, jnp.exp(logits - gmax), 0.0), axis=-1, keepdims=True)
    g_w = 1.0 / gsum
    elo = N_EXPERT_GROUPS + EXPERTS_PER_GROUP * gidx
    emask = (lane_f >= elo) & (lane_f < elo + EXPERTS_PER_GROUP)
    el = jnp.where(emask, logits, ninf)
    t1 = jnp.max(el, axis=-1, keepdims=True)
    i1 = jnp.min(jnp.where(el == t1, lane_f, big), axis=-1, keepdims=True)
    el2 = jnp.where(lane_f == i1, ninf, el)
    t2 = jnp.max(el2, axis=-1, keepdims=True)
    i2 = jnp.min(jnp.where(el2 == t2, lane_f, big), axis=-1, keepdims=True)
    e21 = jnp.exp(t2 - t1)
    den = 1.0 + e21
    gate1 = g_w * (1.0 / den)
    gate2 = g_w * (e21 / den)
    ids = jnp.where(lane == 0, i1 - N_EXPERT_GROUPS, jnp.where(lane == 1, i2 - N_EXPERT_GROUPS, 0.0))
    gates = jnp.where(lane == 0, gate1, jnp.where(lane == 1, gate2, 0.0))
    return ids.astype(jnp.int32), gates


def _mixer_prompt_kernel(x_ref, ada_ref, h0_ref, g1_ref, g2_ref, win_ref, lng_ref, lnb_ref, wsp_ref, bsp_ref,
                         avec_ref, wb_ref, wc_ref, dsk_ref, wglu_ref, bglu_ref, wout_ref, wrh_ref, wrl_ref,
                         brt_ref,
                         x1_ref, ids_ref, gates_ref, state_ref,
                         h_scr, u_scr, vn_scr, s_scr, st_scr, yt_scr, bu_scr, ab_scr):
    nb = x_ref.shape[0]
    half = (nb // 2) * CHUNK
    D = D_MODEL

    @pl.when(pl.program_id(0) == 0)
    def _():
        state_ref[...] = h0_ref[...]

    def mod(b, i):
        return ada_ref[b:b + 1, i * D:(i + 1) * D]

    for b in range(nb):
        hb = _rms(x_ref[b], g1_ref[...]) * (1 + mod(b, 1)) + mod(b, 0)
        h_scr[b * CHUNK:(b + 1) * CHUNK, :] = hb.astype(BF16)

    for r0 in (0, half):
        _front(r0, half, h_scr, win_ref, lng_ref, lnb_ref, u_scr, vn_scr, s_scr)

    for b in range(nb):
        rows = slice(b * CHUNK, (b + 1) * CHUNK)
        for h in range(N_HEADS):
            cols = slice(h * HEAD_DIM, (h + 1) * HEAD_DIM)
            mixed = jnp.dot(wsp_ref[h], vn_scr[rows, cols].astype(BF16), preferred_element_type=F32) + bsp_ref[h]
            ab_scr[rows, cols] = (u_scr[rows, cols] * mixed).astype(BF16)

    for k in range(N_SLABS):
        for t in range(CHUNK):
            st_scr[k, t * nb:(t + 1) * nb, :] = s_scr[k, pl.ds(t, nb, stride=CHUNK), :]
    _s5(st_scr, bu_scr, yt_scr, state_ref, avec_ref, wb_ref, wc_ref, nb, CHUNK)
    for k in range(N_SLABS):
        dsk = dsk_ref[:, k * LANES:(k + 1) * LANES]
        for t in range(CHUNK):
            sel = pl.ds(t, nb, stride=CHUNK)
            s_scr[k, sel, :] = yt_scr[k, t * nb:(t + 1) * nb, :] + dsk * s_scr[k, sel, :]

    for r0 in (0, half):
        mix = _back(r0, half, s_scr, ab_scr, wglu_ref, bglu_ref, wout_ref)
        for bl in range(nb // 2):
            b = r0 // CHUNK + bl
            x1 = x_ref[b] + mod(b, 2) * mix[bl * CHUNK:(bl + 1) * CHUNK, :]
            x1_ref[b] = x1
            h2 = _rms(x1, g2_ref[...]) * (1 + mod(b, 4)) + mod(b, 3)
            ids, gates = _route(h2, wrh_ref, wrl_ref, brt_ref)
            ids_ref[b] = ids
            gates_ref[b] = gates


def _const_spec(shape):
    nd = len(shape)
    return pl.BlockSpec(shape, lambda *_: (0,) * nd, pipeline_mode=pl.Buffered(1))


def _mixer_prompt(x, ada, h0, wts):
    nb, seq, D = x.shape
    n_chunks = seq // CHUNK
    R = nb * CHUNK
    weight_specs = [_const_spec(w.shape) for w in wts]
    in_specs = [pl.BlockSpec((nb, CHUNK, D), lambda i: (0, i, 0)),
                _const_spec(ada.shape), _const_spec(h0.shape)] + weight_specs
    out_shape = (jax.ShapeDtypeStruct((nb, seq, D), F32),
                 jax.ShapeDtypeStruct((nb, seq, ROUTE_LANES), jnp.int32),
                 jax.ShapeDtypeStruct((nb, seq, ROUTE_LANES), F32),
                 jax.ShapeDtypeStruct((nb, STATE_COLS), F32))
    out_specs = (pl.BlockSpec((nb, CHUNK, D), lambda i: (0, i, 0)),
                 pl.BlockSpec((nb, CHUNK, ROUTE_LANES), lambda i: (0, i, 0)),
                 pl.BlockSpec((nb, CHUNK, ROUTE_LANES), lambda i: (0, i, 0)),
                 pl.BlockSpec((nb, STATE_COLS), lambda i: (0, 0)))
    scratch = [pltpu.VMEM((R, D), BF16),
               pltpu.VMEM((R, D_A), F32),
               pltpu.VMEM((R, D_A), F32),
               pltpu.VMEM((N_SLABS, R, LANES), F32),
               pltpu.VMEM((N_SLABS, R, LANES), F32),
               pltpu.VMEM((N_SLABS, R, LANES), F32),
               pltpu.VMEM((R, SLAB_COLS), F32),
               pltpu.VMEM((R, D), BF16)]
    return pl.pallas_call(
        _mixer_prompt_kernel,
        grid=(n_chunks,),
        in_specs=in_specs,
        out_specs=out_specs,
        out_shape=out_shape,
        scratch_shapes=scratch,
        compiler_params=pltpu.CompilerParams(dimension_semantics=("arbitrary",), vmem_limit_bytes=VMEM_LIMIT),
        name="mixer_prompt",
    )(x, ada, h0, *wts)


def _mixer_sample_kernel(wsm_ref, bsm_ref, x_ref, ada_ref, h0_ref, g1_ref, g2_ref, win_ref, lng_ref, lnb_ref,
                         avec_ref, wb_ref, wc_ref, dsk_ref, wglu_ref, bglu_ref, wout_ref, wrh_ref, wrl_ref,
                         brt_ref,
                         x1_ref, ids_ref, gates_ref, state_ref, v_ref,
                         h_scr, u_scr, vn_scr, s_scr, yt_scr, bu_scr, ab_scr):
    T, nb, D = x_ref.shape
    R = T * nb
    half = R // 2

    def mod(i):
        return ada_ref[:, i * D:(i + 1) * D]

    state_ref[...] = h0_ref[...]
    for t in range(T):
        ht = _rms(x_ref[t], g1_ref[...]) * (1 + mod(1)) + mod(0)
        h_scr[t * nb:(t + 1) * nb, :] = ht.astype(BF16)

    for r0 in (0, half):
        _front(r0, half, h_scr, win_ref, lng_ref, lnb_ref, u_scr, vn_scr, s_scr)

    for t in range(T):
        rows = slice(t * nb, (t + 1) * nb)
        v_ref[t] = vn_scr[rows, :]
        for h in range(N_HEADS):
            cols = slice(h * HEAD_DIM, (h + 1) * HEAD_DIM)
            acc = jnp.full((nb, HEAD_DIM), bsm_ref[h * T + t], F32)
            for s in range(t + 1):
                acc = acc + wsm_ref[(h * T + t) * T + s] * vn_scr[s * nb:(s + 1) * nb, cols]
            ab_scr[rows, cols] = (u_scr[rows, cols] * acc).astype(BF16)

    _s5(s_scr, bu_scr, yt_scr, state_ref, avec_ref, wb_ref, wc_ref, nb, T)
    for k in range(N_SLABS):
        s_scr[k] = yt_scr[k] + dsk_ref[:, k * LANES:(k + 1) * LANES] * s_scr[k]

    for r0 in (0, half):
        mix = _back(r0, half, s_scr, ab_scr, wglu_ref, bglu_ref, wout_ref)
        for tl in range(T // 2):
            t = r0 // nb + tl
            x1 = x_ref[t] + mod(2) * mix[tl * nb:(tl + 1) * nb, :]
            x1_ref[t] = x1
            h2 = _rms(x1, g2_ref[...]) * (1 + mod(4)) + mod(3)
            ids, gates = _route(h2, wrh_ref, wrl_ref, brt_ref)
            ids_ref[t] = ids
            gates_ref[t] = gates


def _mixer_sample(x_t, ada, h0, w_small, b_small, wts):
    T, nb, D = x_t.shape
    R = T * nb
    smem = pl.BlockSpec(memory_space=pltpu.SMEM)
    out_shape = (jax.ShapeDtypeStruct((T, nb, D), F32),
                 jax.ShapeDtypeStruct((T, nb, ROUTE_LANES), jnp.int32),
                 jax.ShapeDtypeStruct((T, nb, ROUTE_LANES), F32),
                 jax.ShapeDtypeStruct((nb, STATE_COLS), F32),
                 jax.ShapeDtypeStruct((T, nb, D_A), F32))
    scratch = [pltpu.VMEM((R, D), BF16),
               pltpu.VMEM((R, D_A), F32),
               pltpu.VMEM((R, D_A), F32),
               pltpu.VMEM((N_SLABS, R, LANES), F32),
               pltpu.VMEM((N_SLABS, R, LANES), F32),
               pltpu.VMEM((R, SLAB_COLS), F32),
               pltpu.VMEM((R, D), BF16)]
    vmem = pl.BlockSpec(memory_space=pltpu.VMEM)
    return pl.pallas_call(
        _mixer_sample_kernel,
        in_specs=[smem, smem] + [vmem] * (3 + len(wts)),
        out_specs=(vmem,) * 5,
        out_shape=out_shape,
        scratch_shapes=scratch,
        compiler_params=pltpu.CompilerParams(vmem_limit_bytes=VMEM_LIMIT),
        name="mixer_sample",
    )(w_small, b_small, x_t, ada, h0, *wts)


def _moe_kernel(starts_ref, tok_ref, gate_ref, x1_ref, ada_ref, g2_ref, gfin_ref, w1_ref, w3_ref, w2_ref,
                out_ref, h3_scr, acc3_scr, xb3_scr, xb2_scr, yb3_scr, tmp2_scr, *, chunk):
    t = pl.program_id(0)
    e = pl.program_id(1)
    ngrp, rows_g, D = x1_ref.shape
    mrows = ada_ref.shape[1]

    def mod(i, ci):
        if mrows == 1:
            return ada_ref[0, :, i * D:(i + 1) * D]
        return ada_ref[0, ci * chunk:(ci + 1) * chunk, i * D:(i + 1) * D]

    @pl.when(e == 0)
    def _():
        xb3_scr[...] = jnp.zeros_like(xb3_scr)
        acc3_scr[...] = jnp.zeros_like(acc3_scr)
        for gi in range(ngrp):
            for ci in range(rows_g // chunk):
                x1 = x1_ref[gi, ci * chunk:(ci + 1) * chunk, :]
                h2 = _rms(x1, g2_ref[...]) * (1 + mod(4, ci)) + mod(3, ci)
                off = gi * rows_g + ci * chunk
                h3_scr[off:off + chunk] = h2.reshape(chunk, 1, D)

    start = starts_ref[t, e]
    cnt = starts_ref[t, e + 1] - start
    nblk = (cnt + MOE_BLOCK - 1) // MOE_BLOCK

    def block(i, carry):
        base = start + i * MOE_BLOCK

        def gather8(j8, c):
            for jj in range(SUBLANES):
                j = j8 * SUBLANES + jj
                tok = tok_ref[base + j]
                xb3_scr[pl.ds(j, 1)] = h3_scr[pl.ds(tok, 1)]
            return c

        lax.fori_loop(0, MOE_BLOCK // SUBLANES, gather8, 0)
        xb2_scr[...] = xb3_scr[...].reshape(MOE_BLOCK, D)
        xb = xb2_scr[...].astype(BF16)
        hid = (jax.nn.silu(jnp.dot(xb, w1_ref[0], preferred_element_type=F32))
               * jnp.dot(xb, w3_ref[0], preferred_element_type=F32))
        yb = jnp.dot(hid.astype(BF16), w2_ref[0], preferred_element_type=F32)
        yb3_scr[...] = yb.reshape(MOE_BLOCK, 1, D)
        nrows = jnp.minimum(MOE_BLOCK, cnt - i * MOE_BLOCK)

        def scatter1(j, c):
            tok = tok_ref[base + j]
            g = gate_ref[base + j]
            acc3_scr[pl.ds(tok, 1)] = acc3_scr[pl.ds(tok, 1)] + g * yb3_scr[pl.ds(j, 1)]
            return c

        lax.fori_loop(0, nrows, scatter1, 0)
        return carry

    lax.fori_loop(0, nblk, block, 0)

    @pl.when(e == N_EXPERTS - 1)
    def _():
        for gi in range(ngrp):
            for ci in range(rows_g // chunk):
                off = gi * rows_g + ci * chunk
                tmp2_scr[...] = acc3_scr[off:off + chunk].reshape(chunk, D)
                x2 = x1_ref[gi, ci * chunk:(ci + 1) * chunk, :] + mod(5, ci) * tmp2_scr[...]
                out_ref[gi, ci * chunk:(ci + 1) * chunk, :] = _rms(x2, gfin_ref[...])


def _moe(x1, ada, starts, tok, gate, g2, gfin, w1, w3, w2, *, n_tiles, chunk):
    ngrp = x1.shape[0] // n_tiles
    rows_g, D = x1.shape[1], x1.shape[2]
    mrows = ada.shape[1]
    Tt = ngrp * rows_g
    lp = tok.shape[0] // n_tiles
    single = pl.Buffered(1)
    grid_spec = pltpu.PrefetchScalarGridSpec(
        num_scalar_prefetch=1,
        grid=(n_tiles, N_EXPERTS),
        in_specs=[
            pl.BlockSpec((lp,), lambda t, e, s: (t,), memory_space=pltpu.SMEM),
            pl.BlockSpec((lp,), lambda t, e, s: (t,), memory_space=pltpu.SMEM),
            pl.BlockSpec((ngrp, rows_g, D), lambda t, e, s: (t, 0, 0), pipeline_mode=single),
            pl.BlockSpec((1, mrows, 6 * D), lambda t, e, s: (t, 0, 0)),
            pl.BlockSpec((1, D), lambda t, e, s: (0, 0)),
            pl.BlockSpec((1, D), lambda t, e, s: (0, 0)),
            pl.BlockSpec((1, D, D_EXPERT), lambda t, e, s: (e, 0, 0)),
            pl.BlockSpec((1, D, D_EXPERT), lambda t, e, s: (e, 0, 0)),
            pl.BlockSpec((1, D_EXPERT, D), lambda t, e, s: (e, 0, 0)),
        ],
        out_specs=pl.BlockSpec((ngrp, rows_g, D), lambda t, e, s: (t, 0, 0), pipeline_mode=single),
        scratch_shapes=[pltpu.VMEM((Tt, 1, D), F32),
                        pltpu.VMEM((Tt, 1, D), F32),
                        pltpu.VMEM((MOE_BLOCK, 1, D), F32),
                        pltpu.VMEM((MOE_BLOCK, D), F32),
                        pltpu.VMEM((MOE_BLOCK, 1, D), F32),
                        pltpu.VMEM((chunk, D), F32)],
    )
    return pl.pallas_call(
        functools.partial(_moe_kernel, chunk=chunk),
        grid_spec=grid_spec,
        out_shape=jax.ShapeDtypeStruct(x1.shape, F32),
        compiler_params=pltpu.CompilerParams(dimension_semantics=("arbitrary", "arbitrary"),
                                             vmem_limit_bytes=VMEM_LIMIT),
        name="moe",
    )(starts, tok, gate, x1, ada, g2, gfin, w1, w3, w2)


def _dispatch_tables(ids, gates, n_tiles):
    n_tok = ids.shape[0]
    Tt = n_tok // n_tiles
    flat_e = ids[:, :2].reshape(n_tiles, 2 * Tt)
    flat_g = gates[:, :2].reshape(n_tiles, 2 * Tt)
    order = jnp.argsort(flat_e, axis=-1, stable=True)
    tok_s = (order // 2).astype(jnp.int32)
    g_s = jnp.take_along_axis(flat_g, order, axis=-1)
    counts = jnp.sum((flat_e[:, :, None] == jnp.arange(N_EXPERTS, dtype=jnp.int32)[None, None, :]).astype(jnp.int32),
                     axis=1)
    starts = jnp.concatenate([jnp.zeros((n_tiles, 1), jnp.int32), jnp.cumsum(counts, axis=-1, dtype=jnp.int32)],
                             axis=-1)
    lp = -(-(2 * Tt + MOE_BLOCK) // SMEM_PAD) * SMEM_PAD
    tok_p = jnp.zeros((n_tiles, lp), jnp.int32).at[:, :2 * Tt].set(tok_s).reshape(-1)
    g_p = jnp.zeros((n_tiles, lp), F32).at[:, :2 * Tt].set(g_s).reshape(-1)
    return starts, tok_p, g_p


def _slab_states(h):
    return h.reshape(h.shape[0], N_SLABS, SLAB_STATES)


def _pack_state(h_re, h_im):
    return jnp.concatenate([_slab_states(h_re), _slab_states(h_im)], axis=-1).reshape(h_re.shape[0], STATE_COLS)


def _unpack_state(st):
    b = st.shape[0]
    s4 = st.reshape(b, N_SLABS, 2, SLAB_STATES)
    return (s4[:, :, 0].reshape(b, N_SSM_GROUPS, SSM_STATE), s4[:, :, 1].reshape(b, N_SSM_GROUPS, SSM_STATE))


def kernel(x_prompt, x_sample, state_ssm_re, state_ssm_im, c_prompt, c_sample, w_ada, b_ada, g_norm1, g_norm2, w_in, ln_g, ln_b, w_s, b_s, lam_re, lam_im, log_dt, ssm_b_re, ssm_b_im, ssm_c_re, ssm_c_im, ssm_d, w_glu, b_glu, w_out, w_group, b_group, w_expert, b_expert, w1, w3, w2, g_final):
    depth = w_ada.shape[0]
    assert depth == 1, "the final RMSNorm is fused into the (single) layer's MoE epilogue"
    B, L, D = x_prompt.shape
    Bs, Ls, _ = x_sample.shape
    xp = x_prompt
    xs_t = x_sample.transpose(1, 0, 2)
    eye = jnp.eye(SLAB_GROUPS, dtype=F32)
    tril = jnp.tril(jnp.ones((CHUNK, CHUNK), dtype=bool))
    p_re, p_im, s_re, s_im, s_v = [], [], [], [], []
    for l in range(depth):
        ada = _ada(jnp.concatenate([c_prompt, c_sample], axis=0), w_ada[l].astype(BF16), b_ada[l][None])
        ada_p, ada_s = ada[:B], ada[B:]

        ar, ai, br, bi = _discretize(lam_re[l], lam_im[l], log_dt[l], ssm_b_re[l], ssm_b_im[l])
        avec = jnp.concatenate([ar.reshape(N_SLABS, SLAB_STATES), ai.reshape(N_SLABS, SLAB_STATES)],
                               axis=-1).reshape(1, STATE_COLS)

        def blockdiag_in(w):
            w4 = w.reshape(N_SLABS, SLAB_GROUPS, SSM_GROUP, SSM_STATE)
            return jnp.einsum('kghp,gG->kghGp', w4, eye).reshape(N_SLABS, LANES, SLAB_STATES)

        def blockdiag_out(w):
            w4 = w.reshape(N_SLABS, SLAB_GROUPS, SSM_GROUP, SSM_STATE)
            return jnp.einsum('kghp,gG->kgpGh', w4, eye).reshape(N_SLABS, SLAB_STATES, LANES)

        wb = jnp.concatenate([blockdiag_in(br), blockdiag_in(bi)], axis=-1).astype(BF16)
        wc = jnp.concatenate([blockdiag_out(ssm_c_re[l]), -blockdiag_out(ssm_c_im[l])], axis=1).astype(BF16)

        wr = jnp.zeros((D, ROUTE_LANES), F32)
        wr = wr.at[:, :N_EXPERT_GROUPS].set(w_group[l]).at[:, N_EXPERT_GROUPS:N_EXPERT_GROUPS + N_EXPERTS].set(w_expert[l])
        wr_hi = wr.astype(BF16)
        wr_lo = (wr - wr_hi.astype(F32)).astype(BF16)
        br_t = jnp.zeros((1, ROUTE_LANES), F32)
        br_t = br_t.at[0, :N_EXPERT_GROUPS].set(b_group[l]).at[0, N_EXPERT_GROUPS:N_EXPERT_GROUPS + N_EXPERTS].set(b_expert[l])

        g1 = g_norm1[l][None]
        g2 = g_norm2[l][None]
        shared = dict(
            win=w_in[l].astype(BF16), lng=ln_g[l].reshape(1, D_A), lnb=ln_b[l].reshape(1, D_A),
            avec=avec, wb=wb, wc=wc, dsk=ssm_d[l].reshape(1, D_B), wglu=w_glu[l].astype(BF16),
            bglu=b_glu[l][None], wout=w_out[l].astype(BF16), wrh=wr_hi, wrl=wr_lo, brt=br_t)
        w_masked = jnp.where(tril[None], w_s[l], jnp.zeros_like(w_s[l]))
        wsp = w_masked.astype(BF16)
        bsp = jnp.broadcast_to(b_s[l][:, :, None], (N_HEADS, CHUNK, HEAD_DIM))
        wts_p = (g1, g2, shared['win'], shared['lng'], shared['lnb'], wsp, bsp, shared['avec'], shared['wb'],
                 shared['wc'], shared['dsk'], shared['wglu'], shared['bglu'], shared['wout'], shared['wrh'],
                 shared['wrl'], shared['brt'])
        wts_s = (g1, g2, shared['win'], shared['lng'], shared['lnb'], shared['avec'], shared['wb'],
                 shared['wc'], shared['dsk'], shared['wglu'], shared['bglu'], shared['wout'], shared['wrh'],
                 shared['wrl'], shared['brt'])

        w1b, w3b, w2b = w1[l].astype(BF16), w3[l].astype(BF16), w2[l].astype(BF16)
        gfin = g_final[None]

        h0p = jnp.zeros((B, STATE_COLS), F32)
        x1p, ids_p, gates_p, st_p = _mixer_prompt(xp, ada_p, h0p, wts_p)
        starts, tok, gate = _dispatch_tables(ids_p.reshape(B * L, ROUTE_LANES), gates_p.reshape(B * L, ROUTE_LANES), B)
        xp = _moe(x1p, ada_p[:, None, :], starts, tok, gate, g2, gfin, w1b, w3b, w2b, n_tiles=B, chunk=256)
        hr, hi = _unpack_state(st_p)
        p_re.append(hr.astype(state_ssm_re.dtype))
        p_im.append(hi.astype(state_ssm_im.dtype))

        h0s = _pack_state(state_ssm_re[l].astype(F32), state_ssm_im[l].astype(F32))
        w_small = w_masked[:, :Ls, :Ls].reshape(-1)
        b_small = b_s[l][:, :Ls].reshape(-1)
        x1s, ids_s, gates_s, st_s, v_s = _mixer_sample(xs_t, ada_s, h0s, w_small, b_small, wts_s)
        starts, tok, gate = _dispatch_tables(ids_s.reshape(Bs * Ls, ROUTE_LANES), gates_s.reshape(Bs * Ls, ROUTE_LANES), 1)
        xs_t = _moe(x1s, ada_s[None], starts, tok, gate, g2, gfin, w1b, w3b, w2b, n_tiles=1, chunk=Bs)
        hr, hi = _unpack_state(st_s)
        s_re.append(hr.astype(state_ssm_re.dtype))
        s_im.append(hi.astype(state_ssm_im.dtype))
        s_v.append(v_s.transpose(1, 0, 2))

    y_prompt = xp
    y_sample = xs_t.transpose(1, 0, 2)
    return (y_prompt, y_sample, jnp.stack(p_re), jnp.stack(p_im), jnp.stack(s_re), jnp.stack(s_im), jnp.stack(s_v))
```

```python
import functools

import jax
import jax.numpy as jnp
from jax import lax
from jax.experimental import pallas as pl
from jax.experimental.pallas import tpu as pltpu

F32 = jnp.float32
BF16 = jnp.bfloat16

D_MODEL = 1024
D_A = 512
D_B = 512
N_HEADS = 4
HEAD_DIM = 128
CHUNK = 128
N_SSM_GROUPS = 32
SSM_GROUP = 16
SSM_STATE = 64
N_SLABS = 4
SLAB_GROUPS = N_SSM_GROUPS // N_SLABS
SLAB_STATES = SLAB_GROUPS * SSM_STATE
SLAB_COLS = 2 * SLAB_STATES
STATE_COLS = N_SLABS * SLAB_COLS
N_EXPERT_GROUPS = 4
EXPERTS_PER_GROUP = 8
N_EXPERTS = 32
D_EXPERT = 256
EPS = 1e-6

LANES = 128
SUBLANES = 8
ROUTE_LANES = LANES
MOE_BLOCK = 128
S_PITCH = CHUNK + SUBLANES
SMEM_PAD = 1024
VMEM_LIMIT = 56 * 1024 * 1024


def _rms(xf, g):
    ms = jnp.mean(xf * xf, axis=-1, keepdims=True)
    return xf * lax.rsqrt(ms + EPS) * g


def _ada_kernel(c_ref, w_ref, b_ref, o_ref):
    s = jax.nn.silu(c_ref[...]).astype(BF16)
    o_ref[...] = jnp.dot(s, w_ref[...], preferred_element_type=F32) + b_ref[...]


def _ada(c_all, w_bf, b):
    m = c_all.shape[0]
    n = w_bf.shape[1]
    bn = 1024
    return pl.pallas_call(
        _ada_kernel,
        grid=(n // bn,),
        in_specs=[pl.BlockSpec((m, D_MODEL), lambda j: (0, 0)),
                  pl.BlockSpec((D_MODEL, bn), lambda j: (0, j)),
                  pl.BlockSpec((1, bn), lambda j: (0, j))],
        out_specs=pl.BlockSpec((m, bn), lambda j: (0, j)),
        out_shape=jax.ShapeDtypeStruct((m, n), F32),
        name="ada",
    )(c_all, w_bf, b)


def _disc_kernel(lre_ref, lim_ref, ldt_ref, bre_ref, bim_ref, ar_ref, ai_ref, br_ref, bi_ref):
    dt = jnp.exp(ldt_ref[...])
    lr = lre_ref[...]
    li = lim_ref[...]
    mag = jnp.exp(lr * dt)
    ar = mag * jnp.cos(li * dt)
    ai = mag * jnp.sin(li * dt)
    den = lr * lr + li * li
    cr = ((ar - 1) * lr + ai * li) / den
    ci = (ai * lr - (ar - 1) * li) / den
    ar_ref[...] = ar
    ai_ref[...] = ai
    bre = bre_ref[...]
    bim = bim_ref[...]
    br_ref[...] = cr * bre - ci * bim
    bi_ref[...] = cr * bim + ci * bre


def _discretize(lam_re, lam_im, log_dt, b_re, b_im):
    g, p, h = b_re.shape
    o1 = jax.ShapeDtypeStruct((g, 1, p), F32)
    o2 = jax.ShapeDtypeStruct((g, h, p), F32)
    return pl.pallas_call(_disc_kernel, out_shape=(o1, o1, o2, o2), name="ssm_disc")(
        lam_re.reshape(g, 1, p), lam_im.reshape(g, 1, p), log_dt.reshape(g, 1, 1),
        b_re.transpose(0, 2, 1), b_im.transpose(0, 2, 1))


def _s_pieces(r0, nrows, grp, pitch):
    return [(i * grp, slice((r0 // grp + i) * pitch, (r0 // grp + i) * pitch + grp)) for i in range(nrows // grp)]


def _front(r0, nrows, h_scr, win_ref, lng_ref, lnb_ref, u_scr, vn_scr, s_scr, grp, pitch):
    rows = slice(r0, r0 + nrows)
    proj = jnp.dot(h_scr[rows, :], win_ref[...], preferred_element_type=F32)
    u_scr[rows, :] = jax.nn.gelu(proj[:, :D_A])
    vraw = jax.nn.gelu(proj[:, D_A:2 * D_A])
    for h in range(N_HEADS):
        cols = slice(h * HEAD_DIM, (h + 1) * HEAD_DIM)
        vh = vraw[:, cols]
        mu = jnp.mean(vh, axis=-1, keepdims=True)
        dv = vh - mu
        var = jnp.mean(dv * dv, axis=-1, keepdims=True)
        vn_scr[rows, cols] = dv * lax.rsqrt(var + EPS) * lng_ref[:, cols] + lnb_ref[:, cols]
    for k in range(N_SLABS):
        for off, prow in _s_pieces(r0, nrows, grp, pitch):
            s_scr[k, prow, :] = proj[off:off + grp, 2 * D_A + k * LANES:2 * D_A + (k + 1) * LANES]


def _scan_slab(bu_scr, state, avec_ref, k, rows_per_step, steps):
    cw = SLAB_STATES
    unroll = min(SUBLANES, steps)
    for c in range(SLAB_STATES // cw):
        c_re = slice(c * cw, (c + 1) * cw)
        c_im = slice(SLAB_STATES + c * cw, SLAB_STATES + (c + 1) * cw)
        s_re = slice(k * SLAB_COLS + c * cw, k * SLAB_COLS + (c + 1) * cw)
        s_im = slice(k * SLAB_COLS + SLAB_STATES + c * cw, k * SLAB_COLS + SLAB_STATES + (c + 1) * cw)
        ar = jnp.broadcast_to(avec_ref[:, s_re], (SUBLANES, cw))
        ai = jnp.broadcast_to(avec_ref[:, s_im], (SUBLANES, cw))

        def row_chunk(rc, carry, c_re=c_re, c_im=c_im, s_re=s_re, s_im=s_im, ar=ar, ai=ai):
            r0 = pl.multiple_of(rc * SUBLANES, SUBLANES)
            sr = state[pl.ds(r0, SUBLANES), s_re]
            si = state[pl.ds(r0, SUBLANES), s_im]

            def step_block(tb, st):
                sr, si = st
                for tt in range(unroll):
                    row = pl.multiple_of((tb * unroll + tt) * rows_per_step + r0, SUBLANES)
                    br = bu_scr[pl.ds(row, SUBLANES), c_re]
                    bi = bu_scr[pl.ds(row, SUBLANES), c_im]
                    nr = ar * sr - ai * si + br
                    ni = ar * si + ai * sr + bi
                    bu_scr[pl.ds(row, SUBLANES), c_re] = nr
                    bu_scr[pl.ds(row, SUBLANES), c_im] = ni
                    sr, si = nr, ni
                return sr, si

            sr, si = lax.fori_loop(0, steps // unroll, step_block, (sr, si))
            state[pl.ds(r0, SUBLANES), s_re] = sr
            state[pl.ds(r0, SUBLANES), s_im] = si
            return carry

        lax.fori_loop(0, rows_per_step // SUBLANES, row_chunk, 0)


def _s5(st_ref, bu_scr, yt_ref, state, avec_ref, wb_ref, wc_ref, rows_per_step, steps):
    for k in range(N_SLABS):
        bu_scr[...] = jnp.dot(st_ref[k].astype(BF16), wb_ref[k], preferred_element_type=F32)
        _scan_slab(bu_scr, state, avec_ref, k, rows_per_step, steps)
        yt_ref[k] = jnp.dot(bu_scr[...].astype(BF16), wc_ref[k], preferred_element_type=F32)


def _back(r0, nrows, s_scr, ab_scr, wglu_ref, bglu_ref, wout_ref, grp, pitch):
    rows = slice(r0, r0 + nrows)
    pieces = _s_pieces(r0, nrows, grp, pitch)
    y = jax.nn.gelu(jnp.concatenate(
        [jnp.concatenate([s_scr[k, prow, :] for _, prow in pieces], axis=0) for k in range(N_SLABS)], axis=-1))
    gate = jnp.dot(y.astype(BF16), wglu_ref[...], preferred_element_type=F32) + bglu_ref[...]
    ab_scr[rows, D_A:] = (y * jax.nn.sigmoid(gate)).astype(BF16)
    return jnp.dot(ab_scr[rows, :], wout_ref[...], preferred_element_type=F32)


def _route(h2, wrc_ref, brt_ref):
    hi = h2.astype(BF16)
    lo = (h2 - hi.astype(F32)).astype(BF16)
    both = jnp.dot(hi, wrc_ref[...], preferred_element_type=F32)
    logits = (both[:, :ROUTE_LANES] + both[:, ROUTE_LANES:]
              + jnp.dot(lo, wrc_ref[:, :ROUTE_LANES], preferred_element_type=F32)) + brt_ref[...]
    n = logits.shape[0]
    lane = lax.broadcasted_iota(jnp.int32, (n, ROUTE_LANES), 1)
    lane_f = lane.astype(F32)
    big = jnp.float32(1e9)
    ninf = jnp.float32(-jnp.inf)
    is_g = lane < N_EXPERT_GROUPS
    gl = jnp.where(is_g, logits, ninf)
    gmax = jnp.max(gl, axis=-1, keepdims=True)
    gidx = jnp.min(jnp.where(gl == gmax, lane_f, big), axis=-1, keepdims=True)
    gsum = jnp.sum(jnp.where(is_g, jnp.exp(logits - gmax), 0.0), axis=-1, keepdims=True)
    g_w = 1.0 / gsum
    elo = N_EXPERT_GROUPS + EXPERTS_PER_GROUP * gidx
    emask = (lane_f >= elo) & (lane_f < elo + EXPERTS_PER_GROUP)
    el = jnp.where(emask, logits, ninf)
    t1 = jnp.max(el, axis=-1, keepdims=True)
    i1 = jnp.min(jnp.where(el == t1, lane_f, big), axis=-1, keepdims=True)
    el2 = jnp.where(lane_f == i1, ninf, el)
    t2 = jnp.max(el2, axis=-1, keepdims=True)
    i2 = jnp.min(jnp.where(el2 == t2, lane_f, big), axis=-1, keepdims=True)
    e21 = jnp.exp(t2 - t1)
    den = 1.0 + e21
    gate1 = g_w * (1.0 / den)
    gate2 = g_w * (e21 / den)
    ids = jnp.where(lane == 0, i1 - N_EXPERT_GROUPS, jnp.where(lane == 1, i2 - N_EXPERT_GROUPS, 0.0))
    gates = jnp.where(lane == 0, gate1, jnp.where(lane == 1, gate2, 0.0))
    return ids.astype(jnp.int32), gates


def _mixer_prompt_kernel(x_ref, ada_ref, h0_ref, g1_ref, g2_ref, win_ref, lng_ref, lnb_ref, wsp_ref, bsp_ref,
                         avec_ref, wb_ref, wc_ref, dsk_ref, wglu_ref, bglu_ref, wout_ref, wrc_ref, brt_ref,
                         x1_ref, ids_ref, gates_ref, state_ref,
                         h_scr, u_scr, vn_scr, s_scr, st_scr, yt_scr, bu_scr, ab_scr):
    nb = x_ref.shape[0]
    half = (nb // 2) * CHUNK
    D = D_MODEL
    pitch = S_PITCH

    @pl.when(pl.program_id(0) == 0)
    def _():
        state_ref[...] = h0_ref[...]

    def mod(b, i):
        return ada_ref[b:b + 1, i * D:(i + 1) * D]

    for b in range(nb):
        hb = _rms(x_ref[b], g1_ref[...]) * (1 + mod(b, 1)) + mod(b, 0)
        h_scr[b * CHUNK:(b + 1) * CHUNK, :] = hb.astype(BF16)

    for r0 in (0, half):
        _front(r0, half, h_scr, win_ref, lng_ref, lnb_ref, u_scr, vn_scr, s_scr, CHUNK, pitch)

    for b in range(nb):
        rows = slice(b * CHUNK, (b + 1) * CHUNK)
        for h in range(N_HEADS):
            cols = slice(h * HEAD_DIM, (h + 1) * HEAD_DIM)
            mixed = jnp.dot(wsp_ref[h], vn_scr[rows, cols].astype(BF16), preferred_element_type=F32) + bsp_ref[h]
            ab_scr[rows, cols] = (u_scr[rows, cols] * mixed).astype(BF16)

    for k in range(N_SLABS):
        for t in range(CHUNK):
            st_scr[k, t * nb:(t + 1) * nb, :] = s_scr[k, pl.ds(t, nb, stride=pitch), :]
    _s5(st_scr, bu_scr, yt_scr, state_ref, avec_ref, wb_ref, wc_ref, nb, CHUNK)
    for k in range(N_SLABS):
        dsk = dsk_ref[:, k * LANES:(k + 1) * LANES]
        for t in range(CHUNK):
            sel = pl.ds(t, nb, stride=pitch)
            s_scr[k, sel, :] = yt_scr[k, t * nb:(t + 1) * nb, :] + dsk * s_scr[k, sel, :]

    for r0 in (0, half):
        mix = _back(r0, half, s_scr, ab_scr, wglu_ref, bglu_ref, wout_ref, CHUNK, pitch)
        for bl in range(nb // 2):
            b = r0 // CHUNK + bl
            x1 = x_ref[b] + mod(b, 2) * mix[bl * CHUNK:(bl + 1) * CHUNK, :]
            x1_ref[b] = x1
            h2 = _rms(x1, g2_ref[...]) * (1 + mod(b, 4)) + mod(b, 3)
            ids, gates = _route(h2, wrc_ref, brt_ref)
            ids_ref[b] = ids
            gates_ref[b] = gates


def _const_spec(shape):
    nd = len(shape)
    return pl.BlockSpec(shape, lambda *_: (0,) * nd, pipeline_mode=pl.Buffered(1))


def _mixer_prompt(x, ada, h0, wts):
    nb, seq, D = x.shape
    n_chunks = seq // CHUNK
    R = nb * CHUNK
    weight_specs = [_const_spec(w.shape) for w in wts]
    in_specs = [pl.BlockSpec((nb, CHUNK, D), lambda i: (0, i, 0)),
                _const_spec(ada.shape), _const_spec(h0.shape)] + weight_specs
    out_shape = (jax.ShapeDtypeStruct((nb, seq, D), F32),
                 jax.ShapeDtypeStruct((nb, seq, ROUTE_LANES), jnp.int32),
                 jax.ShapeDtypeStruct((nb, seq, ROUTE_LANES), F32),
                 jax.ShapeDtypeStruct((nb, STATE_COLS), F32))
    out_specs = (pl.BlockSpec((nb, CHUNK, D), lambda i: (0, i, 0)),
                 pl.BlockSpec((nb, CHUNK, ROUTE_LANES), lambda i: (0, i, 0)),
                 pl.BlockSpec((nb, CHUNK, ROUTE_LANES), lambda i: (0, i, 0)),
                 pl.BlockSpec((nb, STATE_COLS), lambda i: (0, 0)))
    scratch = [pltpu.VMEM((R, D), BF16),
               pltpu.VMEM((R, D_A), F32),
               pltpu.VMEM((R, D_A), F32),
               pltpu.VMEM((N_SLABS, nb * S_PITCH, LANES), F32),
               pltpu.VMEM((N_SLABS, R, LANES), F32),
               pltpu.VMEM((N_SLABS, R, LANES), F32),
               pltpu.VMEM((R, SLAB_COLS), F32),
               pltpu.VMEM((R, D), BF16)]
    return pl.pallas_call(
        _mixer_prompt_kernel,
        grid=(n_chunks,),
        in_specs=in_specs,
        out_specs=out_specs,
        out_shape=out_shape,
        scratch_shapes=scratch,
        compiler_params=pltpu.CompilerParams(dimension_semantics=("arbitrary",), vmem_limit_bytes=VMEM_LIMIT),
        name="mixer_prompt",
    )(x, ada, h0, *wts)


def _mixer_sample_kernel(wsm_ref, bsm_ref, x_ref, ada_ref, h0_ref, g1_ref, g2_ref, win_ref, lng_ref, lnb_ref,
                         avec_ref, wb_ref, wc_ref, dsk_ref, wglu_ref, bglu_ref, wout_ref, wrc_ref, brt_ref,
                         x1_ref, ids_ref, gates_ref, state_ref, v_ref,
                         h_scr, u_scr, vn_scr, s_scr, yt_scr, bu_scr, ab_scr):
    T, nb, D = x_ref.shape
    R = T * nb
    half = R // 2

    def mod(i):
        return ada_ref[:, i * D:(i + 1) * D]

    state_ref[...] = h0_ref[...]
    for t in range(T):
        ht = _rms(x_ref[t], g1_ref[...]) * (1 + mod(1)) + mod(0)
        h_scr[t * nb:(t + 1) * nb, :] = ht.astype(BF16)

    for r0 in (0, half):
        _front(r0, half, h_scr, win_ref, lng_ref, lnb_ref, u_scr, vn_scr, s_scr, half, half)

    for t in range(T):
        rows = slice(t * nb, (t + 1) * nb)
        v_ref[t] = vn_scr[rows, :]
        for h in range(N_HEADS):
            cols = slice(h * HEAD_DIM, (h + 1) * HEAD_DIM)
            acc = jnp.full((nb, HEAD_DIM), bsm_ref[h * T + t], F32)
            for s in range(t + 1):
                acc = acc + wsm_ref[(h * T + t) * T + s] * vn_scr[s * nb:(s + 1) * nb, cols]
            ab_scr[rows, cols] = (u_scr[rows, cols] * acc).astype(BF16)

    _s5(s_scr, bu_scr, yt_scr, state_ref, avec_ref, wb_ref, wc_ref, nb, T)
    for k in range(N_SLABS):
        s_scr[k] = yt_scr[k] + dsk_ref[:, k * LANES:(k + 1) * LANES] * s_scr[k]

    for r0 in (0, half):
        mix = _back(r0, half, s_scr, ab_scr, wglu_ref, bglu_ref, wout_ref, half, half)
        for tl in range(T // 2):
            t = r0 // nb + tl
            x1 = x_ref[t] + mod(2) * mix[tl * nb:(tl + 1) * nb, :]
            x1_ref[t] = x1
            h2 = _rms(x1, g2_ref[...]) * (1 + mod(4)) + mod(3)
            ids, gates = _route(h2, wrc_ref, brt_ref)
            ids_ref[t] = ids
            gates_ref[t] = gates


def _mixer_sample(x_t, ada, h0, w_small, b_small, wts):
    T, nb, D = x_t.shape
    R = T * nb
    smem = pl.BlockSpec(memory_space=pltpu.SMEM)
    out_shape = (jax.ShapeDtypeStruct((T, nb, D), F32),
                 jax.ShapeDtypeStruct((T, nb, ROUTE_LANES), jnp.int32),
                 jax.ShapeDtypeStruct((T, nb, ROUTE_LANES), F32),
                 jax.ShapeDtypeStruct((nb, STATE_COLS), F32),
                 jax.ShapeDtypeStruct((T, nb, D_A), F32))
    scratch = [pltpu.VMEM((R, D), BF16),
               pltpu.VMEM((R, D_A), F32),
               pltpu.VMEM((R, D_A), F32),
               pltpu.VMEM((N_SLABS, R, LANES), F32),
               pltpu.VMEM((N_SLABS, R, LANES), F32),
               pltpu.VMEM((R, SLAB_COLS), F32),
               pltpu.VMEM((R, D), BF16)]
    vmem = pl.BlockSpec(memory_space=pltpu.VMEM)
    return pl.pallas_call(
        _mixer_sample_kernel,
        in_specs=[smem, smem] + [vmem] * (3 + len(wts)),
        out_specs=(vmem,) * 5,
        out_shape=out_shape,
        scratch_shapes=scratch,
        compiler_params=pltpu.CompilerParams(vmem_limit_bytes=VMEM_LIMIT),
        name="mixer_sample",
    )(w_small, b_small, x_t, ada, h0, *wts)


def _rows_to_tiles(tiles_ref, row0, val):
    n = val.shape[0]
    for c in range(val.shape[1] // LANES):
        tiles_ref[pl.ds(row0 * SUBLANES + c, n, stride=SUBLANES), :] = val[:, c * LANES:(c + 1) * LANES]


def _tiles_to_rows(tiles_ref, row0, n):
    return jnp.concatenate([tiles_ref[pl.ds(row0 * SUBLANES + c, n, stride=SUBLANES), :] for c in range(SUBLANES)],
                           axis=-1)


def _moe_kernel(starts_ref, tok_ref, pos_ref, gate_ref, x1_ref, ada_ref, g2_ref, gfin_ref, w1_ref, w3_ref, w2_ref,
                out_ref, h_tiles, y_tiles, xb_tiles, *, chunk):
    t = pl.program_id(0)
    e = pl.program_id(1)
    ngrp, rows_g, D = x1_ref.shape
    n_tok = ngrp * rows_g
    mrows = ada_ref.shape[1]

    def mod(i, ci):
        if mrows == 1:
            return ada_ref[0, :, i * D:(i + 1) * D]
        return ada_ref[0, ci * chunk:(ci + 1) * chunk, i * D:(i + 1) * D]

    def tile_of(row):
        return pl.ds(pl.multiple_of(row * SUBLANES, SUBLANES), SUBLANES)

    @pl.when(e == 0)
    def _():
        for gi in range(ngrp):
            for ci in range(rows_g // chunk):
                x1 = x1_ref[gi, ci * chunk:(ci + 1) * chunk, :]
                h2 = _rms(x1, g2_ref[...]) * (1 + mod(4, ci)) + mod(3, ci)
                _rows_to_tiles(h_tiles, gi * rows_g + ci * chunk, h2)

    start = starts_ref[t, e]
    cnt = starts_ref[t, e + 1] - start
    nblk = (cnt + MOE_BLOCK - 1) // MOE_BLOCK

    def block(i, carry):
        base = start + i * MOE_BLOCK

        def gather8(j8, c):
            for jj in range(SUBLANES):
                j = j8 * SUBLANES + jj
                xb_tiles[tile_of(j), :] = h_tiles[tile_of(tok_ref[base + j]), :]
            return c

        lax.fori_loop(0, MOE_BLOCK // SUBLANES, gather8, 0)
        xb = _tiles_to_rows(xb_tiles, 0, MOE_BLOCK).astype(BF16)
        hid = (jax.nn.silu(jnp.dot(xb, w1_ref[0], preferred_element_type=F32))
               * jnp.dot(xb, w3_ref[0], preferred_element_type=F32))
        yb = jnp.dot(hid.astype(BF16), w2_ref[0], preferred_element_type=F32)
        _rows_to_tiles(y_tiles, base, yb)
        return carry

    lax.fori_loop(0, nblk, block, 0)

    @pl.when(e == N_EXPERTS - 1)
    def _():
        def combine8(t8, c):
            for tt in range(SUBLANES):
                tok = t8 * SUBLANES + tt
                y0 = y_tiles[tile_of(pos_ref[2 * tok]), :]
                y1 = y_tiles[tile_of(pos_ref[2 * tok + 1]), :]
                h_tiles[tile_of(tok), :] = gate_ref[2 * tok] * y0 + gate_ref[2 * tok + 1] * y1
            return c

        lax.fori_loop(0, n_tok // SUBLANES, combine8, 0)
        for gi in range(ngrp):
            for ci in range(rows_g // chunk):
                moe = _tiles_to_rows(h_tiles, gi * rows_g + ci * chunk, chunk)
                x2 = x1_ref[gi, ci * chunk:(ci + 1) * chunk, :] + mod(5, ci) * moe
                out_ref[gi, ci * chunk:(ci + 1) * chunk, :] = _rms(x2, gfin_ref[...])


def _moe(x1, ada, starts, tok, pos, gate, g2, gfin, w1, w3, w2, *, n_tiles, chunk):
    ngrp = x1.shape[0] // n_tiles
    rows_g, D = x1.shape[1], x1.shape[2]
    mrows = ada.shape[1]
    Tt = ngrp * rows_g
    lp = tok.shape[0] // n_tiles
    single = pl.Buffered(1)
    grid_spec = pltpu.PrefetchScalarGridSpec(
        num_scalar_prefetch=1,
        grid=(n_tiles, N_EXPERTS),
        in_specs=[
            pl.BlockSpec((lp,), lambda t, e, s: (t,), memory_space=pltpu.SMEM),
            pl.BlockSpec((2 * Tt,), lambda t, e, s: (t,), memory_space=pltpu.SMEM),
            pl.BlockSpec((2 * Tt,), lambda t, e, s: (t,), memory_space=pltpu.SMEM),
            pl.BlockSpec((ngrp, rows_g, D), lambda t, e, s: (t, 0, 0), pipeline_mode=single),
            pl.BlockSpec((1, mrows, 6 * D), lambda t, e, s: (t, 0, 0)),
            pl.BlockSpec((1, D), lambda t, e, s: (0, 0)),
            pl.BlockSpec((1, D), lambda t, e, s: (0, 0)),
            pl.BlockSpec((1, D, D_EXPERT), lambda t, e, s: (e, 0, 0)),
            pl.BlockSpec((1, D, D_EXPERT), lambda t, e, s: (e, 0, 0)),
            pl.BlockSpec((1, D_EXPERT, D), lambda t, e, s: (e, 0, 0)),
        ],
        out_specs=pl.BlockSpec((ngrp, rows_g, D), lambda t, e, s: (t, 0, 0), pipeline_mode=single),
        scratch_shapes=[pltpu.VMEM((Tt * SUBLANES, LANES), F32),
                        pltpu.VMEM(((2 * Tt + MOE_BLOCK) * SUBLANES, LANES), F32),
                        pltpu.VMEM((MOE_BLOCK * SUBLANES, LANES), F32)],
    )
    return pl.pallas_call(
        functools.partial(_moe_kernel, chunk=chunk),
        grid_spec=grid_spec,
        out_shape=jax.ShapeDtypeStruct(x1.shape, F32),
        compiler_params=pltpu.CompilerParams(dimension_semantics=("arbitrary", "arbitrary"),
                                             vmem_limit_bytes=VMEM_LIMIT),
        name="moe",
    )(starts, tok, pos, gate, x1, ada, g2, gfin, w1, w3, w2)


def _dispatch_tables(ids, gates, n_tiles):
    n_tok = ids.shape[0]
    Tt = n_tok // n_tiles
    flat_e = ids[:, :2].reshape(n_tiles, 2 * Tt)
    flat_g = gates[:, :2].reshape(n_tiles, 2 * Tt)
    order = jnp.argsort(flat_e, axis=-1, stable=True).astype(jnp.int32)
    pos = jnp.argsort(order, axis=-1).astype(jnp.int32)
    tok_s = order // 2
    counts = jnp.sum((flat_e[:, :, None] == jnp.arange(N_EXPERTS, dtype=jnp.int32)[None, None, :]).astype(jnp.int32),
                     axis=1)
    starts = jnp.concatenate([jnp.zeros((n_tiles, 1), jnp.int32), jnp.cumsum(counts, axis=-1, dtype=jnp.int32)],
                             axis=-1)
    lp = -(-(2 * Tt + MOE_BLOCK) // SMEM_PAD) * SMEM_PAD
    tok_p = jnp.pad(tok_s, ((0, 0), (0, lp - 2 * Tt))).reshape(-1)
    return starts, tok_p, pos.reshape(-1), flat_g.reshape(-1)


def _slab_states(h):
    return h.reshape(h.shape[0], N_SLABS, SLAB_STATES)


def _pack_state(h_re, h_im):
    return jnp.concatenate([_slab_states(h_re), _slab_states(h_im)], axis=-1).reshape(h_re.shape[0], STATE_COLS)


def _unpack_state(st):
    b = st.shape[0]
    s4 = st.reshape(b, N_SLABS, 2, SLAB_STATES)
    return (s4[:, :, 0].reshape(b, N_SSM_GROUPS, SSM_STATE), s4[:, :, 1].reshape(b, N_SSM_GROUPS, SSM_STATE))


def kernel(x_prompt, x_sample, state_ssm_re, state_ssm_im, c_prompt, c_sample, w_ada, b_ada, g_norm1, g_norm2, w_in, ln_g, ln_b, w_s, b_s, lam_re, lam_im, log_dt, ssm_b_re, ssm_b_im, ssm_c_re, ssm_c_im, ssm_d, w_glu, b_glu, w_out, w_group, b_group, w_expert, b_expert, w1, w3, w2, g_final):
    depth = w_ada.shape[0]
    assert depth == 1, "the final RMSNorm is fused into the (single) layer's MoE epilogue"
    B, L, D = x_prompt.shape
    Bs, Ls, _ = x_sample.shape
    xp = x_prompt
    xs_t = x_sample.transpose(1, 0, 2)
    eye = jnp.eye(SLAB_GROUPS, dtype=F32)
    tril = jnp.tril(jnp.ones((CHUNK, CHUNK), dtype=bool))
    p_re, p_im, s_re, s_im, s_v = [], [], [], [], []
    for l in range(depth):
        ada = _ada(jnp.concatenate([c_prompt, c_sample], axis=0), w_ada[l].astype(BF16), b_ada[l][None])
        ada_p, ada_s = ada[:B], ada[B:]

        ar, ai, br, bi = _discretize(lam_re[l], lam_im[l], log_dt[l], ssm_b_re[l], ssm_b_im[l])
        avec = jnp.concatenate([ar.reshape(N_SLABS, SLAB_STATES), ai.reshape(N_SLABS, SLAB_STATES)],
                               axis=-1).reshape(1, STATE_COLS)

        def blockdiag_in(w):
            w4 = w.reshape(N_SLABS, SLAB_GROUPS, SSM_GROUP, SSM_STATE)
            return jnp.einsum('kghp,gG->kghGp', w4, eye).reshape(N_SLABS, LANES, SLAB_STATES)

        def blockdiag_out(w):
            w4 = w.reshape(N_SLABS, SLAB_GROUPS, SSM_GROUP, SSM_STATE)
            return jnp.einsum('kghp,gG->kgpGh', w4, eye).reshape(N_SLABS, SLAB_STATES, LANES)

        wb = jnp.concatenate([blockdiag_in(br), blockdiag_in(bi)], axis=-1).astype(BF16)
        wc = jnp.concatenate([blockdiag_out(ssm_c_re[l]), -blockdiag_out(ssm_c_im[l])], axis=1).astype(BF16)

        wr = jnp.zeros((D, ROUTE_LANES), F32)
        wr = wr.at[:, :N_EXPERT_GROUPS].set(w_group[l]).at[:, N_EXPERT_GROUPS:N_EXPERT_GROUPS + N_EXPERTS].set(w_expert[l])
        wr_hi = wr.astype(BF16)
        wr_lo = (wr - wr_hi.astype(F32)).astype(BF16)
        br_t = jnp.zeros((1, ROUTE_LANES), F32)
        br_t = br_t.at[0, :N_EXPERT_GROUPS].set(b_group[l]).at[0, N_EXPERT_GROUPS:N_EXPERT_GROUPS + N_EXPERTS].set(b_expert[l])

        g1 = g_norm1[l][None]
        g2 = g_norm2[l][None]
        shared = dict(
            win=w_in[l].astype(BF16), lng=ln_g[l].reshape(1, D_A), lnb=ln_b[l].reshape(1, D_A),
            avec=avec, wb=wb, wc=wc, dsk=ssm_d[l].reshape(1, D_B), wglu=w_glu[l].astype(BF16),
            bglu=b_glu[l][None], wout=w_out[l].astype(BF16), wrc=jnp.concatenate([wr_hi, wr_lo], axis=1), brt=br_t)
        w_masked = jnp.where(tril[None], w_s[l], jnp.zeros_like(w_s[l]))
        wsp = w_masked.astype(BF16)
        bsp = jnp.broadcast_to(b_s[l][:, :, None], (N_HEADS, CHUNK, HEAD_DIM))
        wts_p = (g1, g2, shared['win'], shared['lng'], shared['lnb'], wsp, bsp, shared['avec'], shared['wb'],
                 shared['wc'], shared['dsk'], shared['wglu'], shared['bglu'], shared['wout'], shared['wrc'],
                 shared['brt'])
        wts_s = (g1, g2, shared['win'], shared['lng'], shared['lnb'], shared['avec'], shared['wb'],
                 shared['wc'], shared['dsk'], shared['wglu'], shared['bglu'], shared['wout'], shared['wrc'],
                 shared['brt'])

        w1b, w3b, w2b = w1[l].astype(BF16), w3[l].astype(BF16), w2[l].astype(BF16)
        gfin = g_final[None]

        h0p = jnp.zeros((B, STATE_COLS), F32)
        x1p, ids_p, gates_p, st_p = _mixer_prompt(xp, ada_p, h0p, wts_p)
        tables = _dispatch_tables(ids_p.reshape(B * L, ROUTE_LANES), gates_p.reshape(B * L, ROUTE_LANES), B)
        xp = _moe(x1p, ada_p[:, None, :], *tables, g2, gfin, w1b, w3b, w2b, n_tiles=B, chunk=256)
        hr, hi = _unpack_state(st_p)
        p_re.append(hr.astype(state_ssm_re.dtype))
        p_im.append(hi.astype(state_ssm_im.dtype))

        h0s = _pack_state(state_ssm_re[l].astype(F32), state_ssm_im[l].astype(F32))
        w_small = w_masked[:, :Ls, :Ls].reshape(-1)
        b_small = b_s[l][:, :Ls].reshape(-1)
        x1s, ids_s, gates_s, st_s, v_s = _mixer_sample(xs_t, ada_s, h0s, w_small, b_small, wts_s)
        tables = _dispatch_tables(ids_s.reshape(Bs * Ls, ROUTE_LANES), gates_s.reshape(Bs * Ls, ROUTE_LANES), 1)
        xs_t = _moe(x1s, ada_s[None], *tables, g2, gfin, w1b, w3b, w2b, n_tiles=1, chunk=Bs)
        hr, hi = _unpack_state(st_s)
        s_re.append(hr.astype(state_ssm_re.dtype))
        s_im.append(hi.astype(state_ssm_im.dtype))
        s_v.append(v_s.transpose(1, 0, 2))

    y_prompt = xp
    y_sample = xs_t.transpose(1, 0, 2)
    return (y_prompt, y_sample, jnp.stack(p_re), jnp.stack(p_im), jnp.stack(s_re), jnp.stack(s_im), jnp.stack(s_v))
```

```python
import functools

import jax
import jax.numpy as jnp
from jax import lax
from jax.experimental import pallas as pl
from jax.experimental.pallas import tpu as pltpu

F32 = jnp.float32
BF16 = jnp.bfloat16

D_MODEL = 1024
D_A = 512
D_B = 512
N_HEADS = 4
HEAD_DIM = 128
CHUNK = 128
N_SSM_GROUPS = 32
SSM_GROUP = 16
SSM_STATE = 64
N_SLABS = 4
SLAB_GROUPS = N_SSM_GROUPS // N_SLABS
SLAB_STATES = SLAB_GROUPS * SSM_STATE
SLAB_COLS = 2 * SLAB_STATES
STATE_COLS = N_SLABS * SLAB_COLS
N_EXPERT_GROUPS = 4
EXPERTS_PER_GROUP = 8
N_EXPERTS = 32
D_EXPERT = 256
EPS = 1e-6

LANES = 128
SUBLANES = 8
ROUTE_LANES = LANES
MOE_BLOCK = 128
EXPERTS_PER_STEP = 4
S_PITCH = CHUNK + SUBLANES
SMEM_PAD = 1024
VMEM_LIMIT = 56 * 1024 * 1024


def _rms(xf, g):
    ms = jnp.mean(xf * xf, axis=-1, keepdims=True)
    return xf * lax.rsqrt(ms + EPS) * g


def _ada_kernel(c_ref, w_ref, b_ref, o_ref):
    s = jax.nn.silu(c_ref[...]).astype(BF16)
    o_ref[...] = jnp.dot(s, w_ref[...], preferred_element_type=F32) + b_ref[...]


def _ada(c_all, w_bf, b):
    m = c_all.shape[0]
    n = w_bf.shape[1]
    bn = 1024
    return pl.pallas_call(
        _ada_kernel,
        grid=(n // bn,),
        in_specs=[pl.BlockSpec((m, D_MODEL), lambda j: (0, 0)),
                  pl.BlockSpec((D_MODEL, bn), lambda j: (0, j)),
                  pl.BlockSpec((1, bn), lambda j: (0, j))],
        out_specs=pl.BlockSpec((m, bn), lambda j: (0, j)),
        out_shape=jax.ShapeDtypeStruct((m, n), F32),
        name="ada",
    )(c_all, w_bf, b)


def _disc_kernel(lre_ref, lim_ref, ldt_ref, bre_ref, bim_ref, ar_ref, ai_ref, br_ref, bi_ref):
    dt = jnp.exp(ldt_ref[...])
    lr = lre_ref[...]
    li = lim_ref[...]
    mag = jnp.exp(lr * dt)
    ar = mag * jnp.cos(li * dt)
    ai = mag * jnp.sin(li * dt)
    den = lr * lr + li * li
    cr = ((ar - 1) * lr + ai * li) / den
    ci = (ai * lr - (ar - 1) * li) / den
    ar_ref[...] = ar
    ai_ref[...] = ai
    bre = bre_ref[...]
    bim = bim_ref[...]
    br_ref[...] = cr * bre - ci * bim
    bi_ref[...] = cr * bim + ci * bre


def _discretize(lam_re, lam_im, log_dt, b_re, b_im):
    g, p, h = b_re.shape
    o1 = jax.ShapeDtypeStruct((g, 1, p), F32)
    o2 = jax.ShapeDtypeStruct((g, h, p), F32)
    return pl.pallas_call(_disc_kernel, out_shape=(o1, o1, o2, o2), name="ssm_disc")(
        lam_re.reshape(g, 1, p), lam_im.reshape(g, 1, p), log_dt.reshape(g, 1, 1),
        b_re.transpose(0, 2, 1), b_im.transpose(0, 2, 1))


def _s_pieces(r0, nrows, grp, pitch):
    return [(i * grp, slice((r0 // grp + i) * pitch, (r0 // grp + i) * pitch + grp)) for i in range(nrows // grp)]


def _front(r0, nrows, h_scr, win_ref, lng_ref, lnb_ref, u_scr, vn_scr, s_scr, grp, pitch):
    rows = slice(r0, r0 + nrows)
    proj = jnp.dot(h_scr[rows, :], win_ref[...], preferred_element_type=F32)
    u_scr[rows, :] = jax.nn.gelu(proj[:, :D_A])
    vraw = jax.nn.gelu(proj[:, D_A:2 * D_A])
    for h in range(N_HEADS):
        cols = slice(h * HEAD_DIM, (h + 1) * HEAD_DIM)
        vh = vraw[:, cols]
        mu = jnp.mean(vh, axis=-1, keepdims=True)
        dv = vh - mu
        var = jnp.mean(dv * dv, axis=-1, keepdims=True)
        vn_scr[rows, cols] = dv * lax.rsqrt(var + EPS) * lng_ref[:, cols] + lnb_ref[:, cols]
    for k in range(N_SLABS):
        for off, prow in _s_pieces(r0, nrows, grp, pitch):
            s_scr[k, prow, :] = proj[off:off + grp, 2 * D_A + k * LANES:2 * D_A + (k + 1) * LANES]


def _scan_slab(bu_scr, state, avec_ref, k, rows_per_step, steps):
    cw = SLAB_STATES
    unroll = min(SUBLANES, steps)
    for c in range(SLAB_STATES // cw):
        c_re = slice(c * cw, (c + 1) * cw)
        c_im = slice(SLAB_STATES + c * cw, SLAB_STATES + (c + 1) * cw)
        s_re = slice(k * SLAB_COLS + c * cw, k * SLAB_COLS + (c + 1) * cw)
        s_im = slice(k * SLAB_COLS + SLAB_STATES + c * cw, k * SLAB_COLS + SLAB_STATES + (c + 1) * cw)
        ar = jnp.broadcast_to(avec_ref[:, s_re], (SUBLANES, cw))
        ai = jnp.broadcast_to(avec_ref[:, s_im], (SUBLANES, cw))

        def row_chunk(rc, carry, c_re=c_re, c_im=c_im, s_re=s_re, s_im=s_im, ar=ar, ai=ai):
            r0 = pl.multiple_of(rc * SUBLANES, SUBLANES)
            sr = state[pl.ds(r0, SUBLANES), s_re]
            si = state[pl.ds(r0, SUBLANES), s_im]

            def step_block(tb, st):
                sr, si = st
                for tt in range(unroll):
                    row = pl.multiple_of((tb * unroll + tt) * rows_per_step + r0, SUBLANES)
                    br = bu_scr[pl.ds(row, SUBLANES), c_re]
                    bi = bu_scr[pl.ds(row, SUBLANES), c_im]
                    nr = ar * sr - ai * si + br
                    ni = ar * si + ai * sr + bi
                    bu_scr[pl.ds(row, SUBLANES), c_re] = nr
                    bu_scr[pl.ds(row, SUBLANES), c_im] = ni
                    sr, si = nr, ni
                return sr, si

            sr, si = lax.fori_loop(0, steps // unroll, step_block, (sr, si))
            state[pl.ds(r0, SUBLANES), s_re] = sr
            state[pl.ds(r0, SUBLANES), s_im] = si
            return carry

        lax.fori_loop(0, rows_per_step // SUBLANES, row_chunk, 0)


def _s5(st_ref, bu_scr, yt_ref, state, avec_ref, wb_ref, wc_ref, rows_per_step, steps):
    for k in range(N_SLABS):
        bu_scr[...] = jnp.dot(st_ref[k].astype(BF16), wb_ref[k], preferred_element_type=F32)
        _scan_slab(bu_scr, state, avec_ref, k, rows_per_step, steps)
        yt_ref[k] = jnp.dot(bu_scr[...].astype(BF16), wc_ref[k], preferred_element_type=F32)


def _back(r0, nrows, s_scr, ab_scr, wglu_ref, bglu_ref, wout_ref, grp, pitch):
    rows = slice(r0, r0 + nrows)
    pieces = _s_pieces(r0, nrows, grp, pitch)
    y = jax.nn.gelu(jnp.concatenate(
        [jnp.concatenate([s_scr[k, prow, :] for _, prow in pieces], axis=0) for k in range(N_SLABS)], axis=-1))
    gate = jnp.dot(y.astype(BF16), wglu_ref[...], preferred_element_type=F32) + bglu_ref[...]
    ab_scr[rows, D_A:] = (y * jax.nn.sigmoid(gate)).astype(BF16)
    return jnp.dot(ab_scr[rows, :], wout_ref[...], preferred_element_type=F32)


def _route(h2, wrc_ref, brt_ref):
    hi = h2.astype(BF16)
    lo = (h2 - hi.astype(F32)).astype(BF16)
    both = jnp.dot(hi, wrc_ref[...], preferred_element_type=F32)
    logits = (both[:, :ROUTE_LANES] + both[:, ROUTE_LANES:]
              + jnp.dot(lo, wrc_ref[:, :ROUTE_LANES], preferred_element_type=F32)) + brt_ref[...]
    n = logits.shape[0]
    lane = lax.broadcasted_iota(jnp.int32, (n, ROUTE_LANES), 1)
    lane_f = lane.astype(F32)
    big = jnp.float32(1e9)
    ninf = jnp.float32(-jnp.inf)
    is_g = lane < N_EXPERT_GROUPS
    gl = jnp.where(is_g, logits, ninf)
    gmax = jnp.max(gl, axis=-1, keepdims=True)
    gidx = jnp.min(jnp.where(gl == gmax, lane_f, big), axis=-1, keepdims=True)
    gsum = jnp.sum(jnp.where(is_g, jnp.exp(logits - gmax), 0.0), axis=-1, keepdims=True)
    g_w = 1.0 / gsum
    elo = N_EXPERT_GROUPS + EXPERTS_PER_GROUP * gidx
    emask = (lane_f >= elo) & (lane_f < elo + EXPERTS_PER_GROUP)
    el = jnp.where(emask, logits, ninf)
    t1 = jnp.max(el, axis=-1, keepdims=True)
    i1 = jnp.min(jnp.where(el == t1, lane_f, big), axis=-1, keepdims=True)
    el2 = jnp.where(lane_f == i1, ninf, el)
    t2 = jnp.max(el2, axis=-1, keepdims=True)
    i2 = jnp.min(jnp.where(el2 == t2, lane_f, big), axis=-1, keepdims=True)
    e21 = jnp.exp(t2 - t1)
    den = 1.0 + e21
    gate1 = g_w * (1.0 / den)
    gate2 = g_w * (e21 / den)
    ids = jnp.where(lane == 0, i1 - N_EXPERT_GROUPS, jnp.where(lane == 1, i2 - N_EXPERT_GROUPS, 0.0))
    gates = jnp.where(lane == 0, gate1, jnp.where(lane == 1, gate2, 0.0))
    return ids.astype(jnp.int32), gates


def _mixer_prompt_kernel(x_ref, ada_ref, h0_ref, g1_ref, g2_ref, win_ref, lng_ref, lnb_ref, wsp_ref, bsp_ref,
                         avec_ref, wb_ref, wc_ref, dsk_ref, wglu_ref, bglu_ref, wout_ref, wrc_ref, brt_ref,
                         x1_ref, ids_ref, gates_ref, state_ref,
                         h_scr, u_scr, vn_scr, s_scr, st_scr, yt_scr, bu_scr, ab_scr):
    nb = x_ref.shape[0]
    half = (nb // 2) * CHUNK
    D = D_MODEL
    pitch = S_PITCH

    @pl.when(pl.program_id(0) == 0)
    def _():
        state_ref[...] = h0_ref[...]

    def mod(b, i):
        return ada_ref[b:b + 1, i * D:(i + 1) * D]

    for b in range(nb):
        hb = _rms(x_ref[b], g1_ref[...]) * (1 + mod(b, 1)) + mod(b, 0)
        h_scr[b * CHUNK:(b + 1) * CHUNK, :] = hb.astype(BF16)

    for r0 in (0, half):
        _front(r0, half, h_scr, win_ref, lng_ref, lnb_ref, u_scr, vn_scr, s_scr, CHUNK, pitch)

    for b in range(nb):
        rows = slice(b * CHUNK, (b + 1) * CHUNK)
        for h in range(N_HEADS):
            cols = slice(h * HEAD_DIM, (h + 1) * HEAD_DIM)
            mixed = jnp.dot(wsp_ref[h], vn_scr[rows, cols].astype(BF16), preferred_element_type=F32) + bsp_ref[h]
            ab_scr[rows, cols] = (u_scr[rows, cols] * mixed).astype(BF16)

    for k in range(N_SLABS):
        for t in range(CHUNK):
            st_scr[k, t * nb:(t + 1) * nb, :] = s_scr[k, pl.ds(t, nb, stride=pitch), :]
    _s5(st_scr, bu_scr, yt_scr, state_ref, avec_ref, wb_ref, wc_ref, nb, CHUNK)
    for k in range(N_SLABS):
        dsk = dsk_ref[:, k * LANES:(k + 1) * LANES]
        for t in range(CHUNK):
            sel = pl.ds(t, nb, stride=pitch)
            s_scr[k, sel, :] = yt_scr[k, t * nb:(t + 1) * nb, :] + dsk * s_scr[k, sel, :]

    for r0 in (0, half):
        mix = _back(r0, half, s_scr, ab_scr, wglu_ref, bglu_ref, wout_ref, CHUNK, pitch)
        for bl in range(nb // 2):
            b = r0 // CHUNK + bl
            x1 = x_ref[b] + mod(b, 2) * mix[bl * CHUNK:(bl + 1) * CHUNK, :]
            x1_ref[b] = x1
            h2 = _rms(x1, g2_ref[...]) * (1 + mod(b, 4)) + mod(b, 3)
            ids, gates = _route(h2, wrc_ref, brt_ref)
            ids_ref[b] = ids
            gates_ref[b] = gates


def _const_spec(shape):
    nd = len(shape)
    return pl.BlockSpec(shape, lambda *_: (0,) * nd, pipeline_mode=pl.Buffered(1))


def _mixer_prompt(x, ada, h0, wts):
    nb, seq, D = x.shape
    n_chunks = seq // CHUNK
    R = nb * CHUNK
    weight_specs = [_const_spec(w.shape) for w in wts]
    in_specs = [pl.BlockSpec((nb, CHUNK, D), lambda i: (0, i, 0)),
                _const_spec(ada.shape), _const_spec(h0.shape)] + weight_specs
    out_shape = (jax.ShapeDtypeStruct((nb, seq, D), F32),
                 jax.ShapeDtypeStruct((nb, seq, ROUTE_LANES), jnp.int32),
                 jax.ShapeDtypeStruct((nb, seq, ROUTE_LANES), F32),
                 jax.ShapeDtypeStruct((nb, STATE_COLS), F32))
    out_specs = (pl.BlockSpec((nb, CHUNK, D), lambda i: (0, i, 0)),
                 pl.BlockSpec((nb, CHUNK, ROUTE_LANES), lambda i: (0, i, 0)),
                 pl.BlockSpec((nb, CHUNK, ROUTE_LANES), lambda i: (0, i, 0)),
                 pl.BlockSpec((nb, STATE_COLS), lambda i: (0, 0)))
    scratch = [pltpu.VMEM((R, D), BF16),
               pltpu.VMEM((R, D_A), F32),
               pltpu.VMEM((R, D_A), F32),
               pltpu.VMEM((N_SLABS, nb * S_PITCH, LANES), F32),
               pltpu.VMEM((N_SLABS, R, LANES), F32),
               pltpu.VMEM((N_SLABS, R, LANES), F32),
               pltpu.VMEM((R, SLAB_COLS), F32),
               pltpu.VMEM((R, D), BF16)]
    return pl.pallas_call(
        _mixer_prompt_kernel,
        grid=(n_chunks,),
        in_specs=in_specs,
        out_specs=out_specs,
        out_shape=out_shape,
        scratch_shapes=scratch,
        compiler_params=pltpu.CompilerParams(dimension_semantics=("arbitrary",), vmem_limit_bytes=VMEM_LIMIT),
        name="mixer_prompt",
    )(x, ada, h0, *wts)


def _mixer_sample_kernel(wsm_ref, bsm_ref, x_ref, ada_ref, h0_ref, g1_ref, g2_ref, win_ref, lng_ref, lnb_ref,
                         avec_ref, wb_ref, wc_ref, dsk_ref, wglu_ref, bglu_ref, wout_ref, wrc_ref, brt_ref,
                         x1_ref, ids_ref, gates_ref, state_ref, v_ref,
                         h_scr, u_scr, vn_scr, s_scr, yt_scr, bu_scr, ab_scr):
    T, nb, D = x_ref.shape
    R = T * nb
    half = R // 2

    def mod(i):
        return ada_ref[:, i * D:(i + 1) * D]

    state_ref[...] = h0_ref[...]
    for t in range(T):
        ht = _rms(x_ref[t], g1_ref[...]) * (1 + mod(1)) + mod(0)
        h_scr[t * nb:(t + 1) * nb, :] = ht.astype(BF16)

    for r0 in (0, half):
        _front(r0, half, h_scr, win_ref, lng_ref, lnb_ref, u_scr, vn_scr, s_scr, half, half)

    for t in range(T):
        rows = slice(t * nb, (t + 1) * nb)
        v_ref[t] = vn_scr[rows, :]
        for h in range(N_HEADS):
            cols = slice(h * HEAD_DIM, (h + 1) * HEAD_DIM)
            acc = jnp.full((nb, HEAD_DIM), bsm_ref[h * T + t], F32)
            for s in range(t + 1):
                acc = acc + wsm_ref[(h * T + t) * T + s] * vn_scr[s * nb:(s + 1) * nb, cols]
            ab_scr[rows, cols] = (u_scr[rows, cols] * acc).astype(BF16)

    _s5(s_scr, bu_scr, yt_scr, state_ref, avec_ref, wb_ref, wc_ref, nb, T)
    for k in range(N_SLABS):
        s_scr[k] = yt_scr[k] + dsk_ref[:, k * LANES:(k + 1) * LANES] * s_scr[k]

    for r0 in (0, half):
        mix = _back(r0, half, s_scr, ab_scr, wglu_ref, bglu_ref, wout_ref, half, half)
        for tl in range(T // 2):
            t = r0 // nb + tl
            x1 = x_ref[t] + mod(2) * mix[tl * nb:(tl + 1) * nb, :]
            x1_ref[t] = x1
            h2 = _rms(x1, g2_ref[...]) * (1 + mod(4)) + mod(3)
            ids, gates = _route(h2, wrc_ref, brt_ref)
            ids_ref[t] = ids
            gates_ref[t] = gates


def _mixer_sample(x_t, ada, h0, w_small, b_small, wts):
    T, nb, D = x_t.shape
    R = T * nb
    smem = pl.BlockSpec(memory_space=pltpu.SMEM)
    out_shape = (jax.ShapeDtypeStruct((T, nb, D), F32),
                 jax.ShapeDtypeStruct((T, nb, ROUTE_LANES), jnp.int32),
                 jax.ShapeDtypeStruct((T, nb, ROUTE_LANES), F32),
                 jax.ShapeDtypeStruct((nb, STATE_COLS), F32),
                 jax.ShapeDtypeStruct((T, nb, D_A), F32))
    scratch = [pltpu.VMEM((R, D), BF16),
               pltpu.VMEM((R, D_A), F32),
               pltpu.VMEM((R, D_A), F32),
               pltpu.VMEM((N_SLABS, R, LANES), F32),
               pltpu.VMEM((N_SLABS, R, LANES), F32),
               pltpu.VMEM((R, SLAB_COLS), F32),
               pltpu.VMEM((R, D), BF16)]
    vmem = pl.BlockSpec(memory_space=pltpu.VMEM)
    return pl.pallas_call(
        _mixer_sample_kernel,
        in_specs=[smem, smem] + [vmem] * (3 + len(wts)),
        out_specs=(vmem,) * 5,
        out_shape=out_shape,
        scratch_shapes=scratch,
        compiler_params=pltpu.CompilerParams(vmem_limit_bytes=VMEM_LIMIT),
        name="mixer_sample",
    )(w_small, b_small, x_t, ada, h0, *wts)


def _rows_to_tiles(tiles_ref, row0, val):
    n = val.shape[0]
    for c in range(val.shape[1] // LANES):
        tiles_ref[pl.ds(row0 * SUBLANES + c, n, stride=SUBLANES), :] = val[:, c * LANES:(c + 1) * LANES]


def _tiles_to_rows(tiles_ref, row0, n):
    return jnp.concatenate([tiles_ref[pl.ds(row0 * SUBLANES + c, n, stride=SUBLANES), :] for c in range(SUBLANES)],
                           axis=-1)


def _moe_kernel(starts_ref, tok_ref, pos_ref, gate_ref, x1_ref, ada_ref, g2_ref, gfin_ref, w1_ref, w3_ref, w2_ref,
                out_ref, h_tiles, y_tiles, xb_tiles, *, chunk):
    t = pl.program_id(0)
    e = pl.program_id(1)
    ngrp, rows_g, D = x1_ref.shape
    n_tok = ngrp * rows_g
    mrows = ada_ref.shape[1]

    def mod(i, ci):
        if mrows == 1:
            return ada_ref[0, :, i * D:(i + 1) * D]
        return ada_ref[0, ci * chunk:(ci + 1) * chunk, i * D:(i + 1) * D]

    def tile_of(row):
        return pl.ds(pl.multiple_of(row * SUBLANES, SUBLANES), SUBLANES)

    @pl.when(e == 0)
    def _():
        for gi in range(ngrp):
            for ci in range(rows_g // chunk):
                x1 = x1_ref[gi, ci * chunk:(ci + 1) * chunk, :]
                h2 = _rms(x1, g2_ref[...]) * (1 + mod(4, ci)) + mod(3, ci)
                _rows_to_tiles(h_tiles, gi * rows_g + ci * chunk, h2)

    for el in range(EXPERTS_PER_STEP):
        expert = e * EXPERTS_PER_STEP + el
        start = starts_ref[t, expert]
        cnt = starts_ref[t, expert + 1] - start
        nblk = (cnt + MOE_BLOCK - 1) // MOE_BLOCK

        def block(i, carry, start=start, el=el):
            base = start + i * MOE_BLOCK

            def gather8(j8, c):
                for jj in range(SUBLANES):
                    j = j8 * SUBLANES + jj
                    xb_tiles[tile_of(j), :] = h_tiles[tile_of(tok_ref[base + j]), :]
                return c

            lax.fori_loop(0, MOE_BLOCK // SUBLANES, gather8, 0)
            xb = _tiles_to_rows(xb_tiles, 0, MOE_BLOCK).astype(BF16)
            hid = (jax.nn.silu(jnp.dot(xb, w1_ref[el], preferred_element_type=F32))
                   * jnp.dot(xb, w3_ref[el], preferred_element_type=F32))
            yb = jnp.dot(hid.astype(BF16), w2_ref[el], preferred_element_type=F32)
            _rows_to_tiles(y_tiles, base, yb)
            return carry

        lax.fori_loop(0, nblk, block, 0)

    @pl.when(e == N_EXPERTS // EXPERTS_PER_STEP - 1)
    def _():
        def combine8(t8, c):
            for tt in range(SUBLANES):
                tok = t8 * SUBLANES + tt
                y0 = y_tiles[tile_of(pos_ref[2 * tok]), :]
                y1 = y_tiles[tile_of(pos_ref[2 * tok + 1]), :]
                h_tiles[tile_of(tok), :] = gate_ref[2 * tok] * y0 + gate_ref[2 * tok + 1] * y1
            return c

        lax.fori_loop(0, n_tok // SUBLANES, combine8, 0)
        for gi in range(ngrp):
            for ci in range(rows_g // chunk):
                moe = _tiles_to_rows(h_tiles, gi * rows_g + ci * chunk, chunk)
                x2 = x1_ref[gi, ci * chunk:(ci + 1) * chunk, :] + mod(5, ci) * moe
                out_ref[gi, ci * chunk:(ci + 1) * chunk, :] = _rms(x2, gfin_ref[...])


def _moe(x1, ada, starts, tok, pos, gate, g2, gfin, w1, w3, w2, *, n_tiles, chunk):
    ngrp = x1.shape[0] // n_tiles
    rows_g, D = x1.shape[1], x1.shape[2]
    mrows = ada.shape[1]
    Tt = ngrp * rows_g
    lp = tok.shape[0] // n_tiles
    single = pl.Buffered(1)
    grid_spec = pltpu.PrefetchScalarGridSpec(
        num_scalar_prefetch=1,
        grid=(n_tiles, N_EXPERTS // EXPERTS_PER_STEP),
        in_specs=[
            pl.BlockSpec((lp,), lambda t, e, s: (t,), memory_space=pltpu.SMEM),
            pl.BlockSpec((2 * Tt,), lambda t, e, s: (t,), memory_space=pltpu.SMEM),
            pl.BlockSpec((2 * Tt,), lambda t, e, s: (t,), memory_space=pltpu.SMEM),
            pl.BlockSpec((ngrp, rows_g, D), lambda t, e, s: (t, 0, 0), pipeline_mode=single),
            pl.BlockSpec((1, mrows, 6 * D), lambda t, e, s: (t, 0, 0)),
            pl.BlockSpec((1, D), lambda t, e, s: (0, 0)),
            pl.BlockSpec((1, D), lambda t, e, s: (0, 0)),
            pl.BlockSpec((EXPERTS_PER_STEP, D, D_EXPERT), lambda t, e, s: (e, 0, 0)),
            pl.BlockSpec((EXPERTS_PER_STEP, D, D_EXPERT), lambda t, e, s: (e, 0, 0)),
            pl.BlockSpec((EXPERTS_PER_STEP, D_EXPERT, D), lambda t, e, s: (e, 0, 0)),
        ],
        out_specs=pl.BlockSpec((ngrp, rows_g, D), lambda t, e, s: (t, 0, 0), pipeline_mode=single),
        scratch_shapes=[pltpu.VMEM((Tt * SUBLANES, LANES), F32),
                        pltpu.VMEM(((2 * Tt + MOE_BLOCK) * SUBLANES, LANES), F32),
                        pltpu.VMEM((MOE_BLOCK * SUBLANES, LANES), F32)],
    )
    return pl.pallas_call(
        functools.partial(_moe_kernel, chunk=chunk),
        grid_spec=grid_spec,
        out_shape=jax.ShapeDtypeStruct(x1.shape, F32),
        compiler_params=pltpu.CompilerParams(dimension_semantics=("arbitrary", "arbitrary"),
                                             vmem_limit_bytes=VMEM_LIMIT),
        name="moe",
    )(starts, tok, pos, gate, x1, ada, g2, gfin, w1, w3, w2)


def _dispatch_tables(ids, gates, n_tiles):
    n_tok = ids.shape[0]
    Tt = n_tok // n_tiles
    flat_e = ids[:, :2].reshape(n_tiles, 2 * Tt)
    flat_g = gates[:, :2].reshape(n_tiles, 2 * Tt)
    order = jnp.argsort(flat_e, axis=-1, stable=True).astype(jnp.int32)
    pos = jnp.argsort(order, axis=-1).astype(jnp.int32)
    tok_s = order // 2
    counts = jnp.sum((flat_e[:, :, None] == jnp.arange(N_EXPERTS, dtype=jnp.int32)[None, None, :]).astype(jnp.int32),
                     axis=1)
    starts = jnp.concatenate([jnp.zeros((n_tiles, 1), jnp.int32), jnp.cumsum(counts, axis=-1, dtype=jnp.int32)],
                             axis=-1)
    lp = -(-(2 * Tt + MOE_BLOCK) // SMEM_PAD) * SMEM_PAD
    tok_p = jnp.pad(tok_s, ((0, 0), (0, lp - 2 * Tt))).reshape(-1)
    return starts, tok_p, pos.reshape(-1), flat_g.reshape(-1)


def _slab_states(h):
    return h.reshape(h.shape[0], N_SLABS, SLAB_STATES)


def _pack_state(h_re, h_im):
    return jnp.concatenate([_slab_states(h_re), _slab_states(h_im)], axis=-1).reshape(h_re.shape[0], STATE_COLS)


def _unpack_state(st):
    b = st.shape[0]
    s4 = st.reshape(b, N_SLABS, 2, SLAB_STATES)
    return (s4[:, :, 0].reshape(b, N_SSM_GROUPS, SSM_STATE), s4[:, :, 1].reshape(b, N_SSM_GROUPS, SSM_STATE))


def kernel(x_prompt, x_sample, state_ssm_re, state_ssm_im, c_prompt, c_sample, w_ada, b_ada, g_norm1, g_norm2, w_in, ln_g, ln_b, w_s, b_s, lam_re, lam_im, log_dt, ssm_b_re, ssm_b_im, ssm_c_re, ssm_c_im, ssm_d, w_glu, b_glu, w_out, w_group, b_group, w_expert, b_expert, w1, w3, w2, g_final):
    depth = w_ada.shape[0]
    assert depth == 1, "the final RMSNorm is fused into the (single) layer's MoE epilogue"
    B, L, D = x_prompt.shape
    Bs, Ls, _ = x_sample.shape
    xp = x_prompt
    xs_t = x_sample.transpose(1, 0, 2)
    eye = jnp.eye(SLAB_GROUPS, dtype=F32)
    tril = jnp.tril(jnp.ones((CHUNK, CHUNK), dtype=bool))
    p_re, p_im, s_re, s_im, s_v = [], [], [], [], []
    for l in range(depth):
        ada = _ada(jnp.concatenate([c_prompt, c_sample], axis=0), w_ada[l].astype(BF16), b_ada[l][None])
        ada_p, ada_s = ada[:B], ada[B:]

        ar, ai, br, bi = _discretize(lam_re[l], lam_im[l], log_dt[l], ssm_b_re[l], ssm_b_im[l])
        avec = jnp.concatenate([ar.reshape(N_SLABS, SLAB_STATES), ai.reshape(N_SLABS, SLAB_STATES)],
                               axis=-1).reshape(1, STATE_COLS)

        def blockdiag_in(w):
            w4 = w.reshape(N_SLABS, SLAB_GROUPS, SSM_GROUP, SSM_STATE)
            return jnp.einsum('kghp,gG->kghGp', w4, eye).reshape(N_SLABS, LANES, SLAB_STATES)

        def blockdiag_out(w):
            w4 = w.reshape(N_SLABS, SLAB_GROUPS, SSM_GROUP, SSM_STATE)
            return jnp.einsum('kghp,gG->kgpGh', w4, eye).reshape(N_SLABS, SLAB_STATES, LANES)

        wb = jnp.concatenate([blockdiag_in(br), blockdiag_in(bi)], axis=-1).astype(BF16)
        wc = jnp.concatenate([blockdiag_out(ssm_c_re[l]), -blockdiag_out(ssm_c_im[l])], axis=1).astype(BF16)

        wr = jnp.zeros((D, ROUTE_LANES), F32)
        wr = wr.at[:, :N_EXPERT_GROUPS].set(w_group[l]).at[:, N_EXPERT_GROUPS:N_EXPERT_GROUPS + N_EXPERTS].set(w_expert[l])
        wr_hi = wr.astype(BF16)
        wr_lo = (wr - wr_hi.astype(F32)).astype(BF16)
        br_t = jnp.zeros((1, ROUTE_LANES), F32)
        br_t = br_t.at[0, :N_EXPERT_GROUPS].set(b_group[l]).at[0, N_EXPERT_GROUPS:N_EXPERT_GROUPS + N_EXPERTS].set(b_expert[l])

        g1 = g_norm1[l][None]
        g2 = g_norm2[l][None]
        shared = dict(
            win=w_in[l].astype(BF16), lng=ln_g[l].reshape(1, D_A), lnb=ln_b[l].reshape(1, D_A),
            avec=avec, wb=wb, wc=wc, dsk=ssm_d[l].reshape(1, D_B), wglu=w_glu[l].astype(BF16),
            bglu=b_glu[l][None], wout=w_out[l].astype(BF16), wrc=jnp.concatenate([wr_hi, wr_lo], axis=1), brt=br_t)
        w_masked = jnp.where(tril[None], w_s[l], jnp.zeros_like(w_s[l]))
        wsp = w_masked.astype(BF16)
        bsp = jnp.broadcast_to(b_s[l][:, :, None], (N_HEADS, CHUNK, HEAD_DIM))
        wts_p = (g1, g2, shared['win'], shared['lng'], shared['lnb'], wsp, bsp, shared['avec'], shared['wb'],
                 shared['wc'], shared['dsk'], shared['wglu'], shared['bglu'], shared['wout'], shared['wrc'],
                 shared['brt'])
        wts_s = (g1, g2, shared['win'], shared['lng'], shared['lnb'], shared['avec'], shared['wb'],
                 shared['wc'], shared['dsk'], shared['wglu'], shared['bglu'], shared['wout'], shared['wrc'],
                 shared['brt'])

        w1b, w3b, w2b = w1[l].astype(BF16), w3[l].astype(BF16), w2[l].astype(BF16)
        gfin = g_final[None]

        h0p = jnp.zeros((B, STATE_COLS), F32)
        x1p, ids_p, gates_p, st_p = _mixer_prompt(xp, ada_p, h0p, wts_p)
        tables = _dispatch_tables(ids_p.reshape(B * L, ROUTE_LANES), gates_p.reshape(B * L, ROUTE_LANES), B)
        xp = _moe(x1p, ada_p[:, None, :], *tables, g2, gfin, w1b, w3b, w2b, n_tiles=B, chunk=256)
        hr, hi = _unpack_state(st_p)
        p_re.append(hr.astype(state_ssm_re.dtype))
        p_im.append(hi.astype(state_ssm_im.dtype))

        h0s = _pack_state(state_ssm_re[l].astype(F32), state_ssm_im[l].astype(F32))
        w_small = w_masked[:, :Ls, :Ls].reshape(-1)
        b_small = b_s[l][:, :Ls].reshape(-1)
        x1s, ids_s, gates_s, st_s, v_s = _mixer_sample(xs_t, ada_s, h0s, w_small, b_small, wts_s)
        tables = _dispatch_tables(ids_s.reshape(Bs * Ls, ROUTE_LANES), gates_s.reshape(Bs * Ls, ROUTE_LANES), 1)
        xs_t = _moe(x1s, ada_s[None], *tables, g2, gfin, w1b, w3b, w2b, n_tiles=1, chunk=Bs)
        hr, hi = _unpack_state(st_s)
        s_re.append(hr.astype(state_ssm_re.dtype))
        s_im.append(hi.astype(state_ssm_im.dtype))
        s_v.append(v_s.transpose(1, 0, 2))

    y_prompt = xp
    y_sample = xs_t.transpose(1, 0, 2)
    return (y_prompt, y_sample, jnp.stack(p_re), jnp.stack(p_im), jnp.stack(s_re), jnp.stack(s_im), jnp.stack(s_v))
```

```python
import functools

import jax
import jax.numpy as jnp
from jax import lax
from jax.experimental import pallas as pl
from jax.experimental.pallas import tpu as pltpu

F32 = jnp.float32
BF16 = jnp.bfloat16

D_MODEL = 1024
D_A = 512
D_B = 512
N_HEADS = 4
HEAD_DIM = 128
CHUNK = 128
N_SSM_GROUPS = 32
SSM_GROUP = 16
SSM_STATE = 64
N_SLABS = 4
SLAB_GROUPS = N_SSM_GROUPS // N_SLABS
SLAB_STATES = SLAB_GROUPS * SSM_STATE
SLAB_COLS = 2 * SLAB_STATES
STATE_COLS = N_SLABS * SLAB_COLS
N_EXPERT_GROUPS = 4
EXPERTS_PER_GROUP = 8
N_EXPERTS = 32
D_EXPERT = 256
EPS = 1e-6

LANES = 128
SUBLANES = 8
ROUTE_LANES = LANES
MOE_BLOCK = 128
EXPERTS_PER_STEP = 4
PAIR_FIELDS = 4
S_PITCH = CHUNK + SUBLANES
SMEM_PAD = 1024
VMEM_LIMIT = 58 * 1024 * 1024


def _rms(xf, g):
    ms = jnp.mean(xf * xf, axis=-1, keepdims=True)
    return xf * lax.rsqrt(ms + EPS) * g


def _ada_kernel(c_ref, w_ref, b_ref, o_ref):
    s = jax.nn.silu(c_ref[...]).astype(BF16)
    o_ref[...] = jnp.dot(s, w_ref[...], preferred_element_type=F32) + b_ref[...]


def _ada(c_all, w_bf, b):
    m = c_all.shape[0]
    n = w_bf.shape[1]
    bn = 1024
    return pl.pallas_call(
        _ada_kernel,
        grid=(n // bn,),
        in_specs=[pl.BlockSpec((m, D_MODEL), lambda j: (0, 0)),
                  pl.BlockSpec((D_MODEL, bn), lambda j: (0, j)),
                  pl.BlockSpec((1, bn), lambda j: (0, j))],
        out_specs=pl.BlockSpec((m, bn), lambda j: (0, j)),
        out_shape=jax.ShapeDtypeStruct((m, n), F32),
        name="ada",
    )(c_all, w_bf, b)


def _disc_kernel(lre_ref, lim_ref, ldt_ref, bre_ref, bim_ref, ar_ref, ai_ref, br_ref, bi_ref):
    dt = jnp.exp(ldt_ref[...])
    lr = lre_ref[...]
    li = lim_ref[...]
    mag = jnp.exp(lr * dt)
    ar = mag * jnp.cos(li * dt)
    ai = mag * jnp.sin(li * dt)
    den = lr * lr + li * li
    cr = ((ar - 1) * lr + ai * li) / den
    ci = (ai * lr - (ar - 1) * li) / den
    ar_ref[...] = ar
    ai_ref[...] = ai
    bre = bre_ref[...]
    bim = bim_ref[...]
    br_ref[...] = cr * bre - ci * bim
    bi_ref[...] = cr * bim + ci * bre


def _discretize(lam_re, lam_im, log_dt, b_re, b_im):
    g, p, h = b_re.shape
    o1 = jax.ShapeDtypeStruct((g, 1, p), F32)
    o2 = jax.ShapeDtypeStruct((g, h, p), F32)
    return pl.pallas_call(_disc_kernel, out_shape=(o1, o1, o2, o2), name="ssm_disc")(
        lam_re.reshape(g, 1, p), lam_im.reshape(g, 1, p), log_dt.reshape(g, 1, 1),
        b_re.transpose(0, 2, 1), b_im.transpose(0, 2, 1))


def _s_pieces(r0, nrows, grp, pitch):
    return [(i * grp, slice((r0 // grp + i) * pitch, (r0 // grp + i) * pitch + grp)) for i in range(nrows // grp)]


def _front(r0, nrows, h_scr, win_ref, lng_ref, lnb_ref, u_scr, vn_scr, s_scr, grp, pitch):
    rows = slice(r0, r0 + nrows)
    proj = jnp.dot(h_scr[rows, :], win_ref[...], preferred_element_type=F32)
    u_scr[rows, :] = jax.nn.gelu(proj[:, :D_A])
    vraw = jax.nn.gelu(proj[:, D_A:2 * D_A])
    for h in range(N_HEADS):
        cols = slice(h * HEAD_DIM, (h + 1) * HEAD_DIM)
        vh = vraw[:, cols]
        mu = jnp.mean(vh, axis=-1, keepdims=True)
        dv = vh - mu
        var = jnp.mean(dv * dv, axis=-1, keepdims=True)
        vn_scr[rows, cols] = dv * lax.rsqrt(var + EPS) * lng_ref[:, cols] + lnb_ref[:, cols]
    for k in range(N_SLABS):
        for off, prow in _s_pieces(r0, nrows, grp, pitch):
            s_scr[k, prow, :] = proj[off:off + grp, 2 * D_A + k * LANES:2 * D_A + (k + 1) * LANES]


def _scan_slab(bu, state, avec_ref, k, rows_per_step, steps):
    c_re = slice(0, SLAB_STATES)
    c_im = slice(SLAB_STATES, SLAB_COLS)
    s_re = slice(k * SLAB_COLS, k * SLAB_COLS + SLAB_STATES)
    s_im = slice(k * SLAB_COLS + SLAB_STATES, (k + 1) * SLAB_COLS)
    ar = jnp.broadcast_to(avec_ref[:, s_re], (SUBLANES, SLAB_STATES))
    ai = jnp.broadcast_to(avec_ref[:, s_im], (SUBLANES, SLAB_STATES))
    for rc in range(rows_per_step // SUBLANES):
        r0 = rc * SUBLANES
        sr = state[r0:r0 + SUBLANES, s_re]
        si = state[r0:r0 + SUBLANES, s_im]
        for t in range(steps):
            rows = slice(t * rows_per_step + r0, t * rows_per_step + r0 + SUBLANES)
            nr = ar * sr - ai * si + bu[rows, c_re]
            ni = ar * si + ai * sr + bu[rows, c_im]
            bu[rows, c_re] = nr
            bu[rows, c_im] = ni
            sr, si = nr, ni
        state[r0:r0 + SUBLANES, s_re] = sr
        state[r0:r0 + SUBLANES, s_im] = si


def _s5(st_ref, bu_scr, yt_ref, state, avec_ref, wb_ref, wc_ref, rows_per_step, steps):
    for k in range(N_SLABS):
        bu = bu_scr.at[k % 2]
        bu[...] = jnp.dot(st_ref[k].astype(BF16), wb_ref[k], preferred_element_type=F32)
        _scan_slab(bu, state, avec_ref, k, rows_per_step, steps)
        yt_ref[k] = jnp.dot(bu[...].astype(BF16), wc_ref[k], preferred_element_type=F32)


def _back(r0, nrows, s_scr, ab_scr, wglu_ref, bglu_ref, wout_ref, grp, pitch):
    rows = slice(r0, r0 + nrows)
    pieces = _s_pieces(r0, nrows, grp, pitch)
    y = jax.nn.gelu(jnp.concatenate(
        [jnp.concatenate([s_scr[k, prow, :] for _, prow in pieces], axis=0) for k in range(N_SLABS)], axis=-1))
    gate = jnp.dot(y.astype(BF16), wglu_ref[...], preferred_element_type=F32) + bglu_ref[...]
    ab_scr[rows, D_A:] = (y * jax.nn.sigmoid(gate)).astype(BF16)
    return jnp.dot(ab_scr[rows, :], wout_ref[...], preferred_element_type=F32)


def _route(h2, wrc_ref, brt_ref):
    hi = h2.astype(BF16)
    lo = (h2 - hi.astype(F32)).astype(BF16)
    both = jnp.dot(hi, wrc_ref[...], preferred_element_type=F32)
    logits = (both[:, :ROUTE_LANES] + both[:, ROUTE_LANES:]
              + jnp.dot(lo, wrc_ref[:, :ROUTE_LANES], preferred_element_type=F32)) + brt_ref[...]
    n = logits.shape[0]
    lane = lax.broadcasted_iota(jnp.int32, (n, ROUTE_LANES), 1)
    lane_f = lane.astype(F32)
    big = jnp.float32(1e9)
    ninf = jnp.float32(-jnp.inf)
    is_g = lane < N_EXPERT_GROUPS
    gl = jnp.where(is_g, logits, ninf)
    gmax = jnp.max(gl, axis=-1, keepdims=True)
    gidx = jnp.min(jnp.where(gl == gmax, lane_f, big), axis=-1, keepdims=True)
    gsum = jnp.sum(jnp.where(is_g, jnp.exp(logits - gmax), 0.0), axis=-1, keepdims=True)
    g_w = 1.0 / gsum
    elo = N_EXPERT_GROUPS + EXPERTS_PER_GROUP * gidx
    emask = (lane_f >= elo) & (lane_f < elo + EXPERTS_PER_GROUP)
    el = jnp.where(emask, logits, ninf)
    t1 = jnp.max(el, axis=-1, keepdims=True)
    i1 = jnp.min(jnp.where(el == t1, lane_f, big), axis=-1, keepdims=True)
    el2 = jnp.where(lane_f == i1, ninf, el)
    t2 = jnp.max(el2, axis=-1, keepdims=True)
    i2 = jnp.min(jnp.where(el2 == t2, lane_f, big), axis=-1, keepdims=True)
    e21 = jnp.exp(t2 - t1)
    den = 1.0 + e21
    gate1 = g_w * (1.0 / den)
    gate2 = g_w * (e21 / den)
    ids = jnp.where(lane == 0, i1 - N_EXPERT_GROUPS, jnp.where(lane == 1, i2 - N_EXPERT_GROUPS, 0.0))
    gates = jnp.where(lane == 0, gate1, jnp.where(lane == 1, gate2, 0.0))
    return ids.astype(jnp.int32), gates


def _mixer_prompt_kernel(x_ref, ada_ref, h0_ref, g1_ref, g2_ref, win_ref, lng_ref, lnb_ref, wsp_ref, bsp_ref,
                         avec_ref, wb_ref, wc_ref, dsk_ref, wglu_ref, bglu_ref, wout_ref, wrc_ref, brt_ref,
                         x1_ref, ids_ref, gates_ref, state_ref,
                         h_scr, u_scr, vn_scr, s_scr, st_scr, yt_scr, bu_scr, ab_scr):
    nb = x_ref.shape[0]
    half = (nb // 2) * CHUNK
    D = D_MODEL
    pitch = S_PITCH

    @pl.when(pl.program_id(0) == 0)
    def _():
        state_ref[...] = h0_ref[...]

    def mod(b, i):
        return ada_ref[b:b + 1, i * D:(i + 1) * D]

    for b in range(nb):
        hb = _rms(x_ref[b], g1_ref[...]) * (1 + mod(b, 1)) + mod(b, 0)
        h_scr[b * CHUNK:(b + 1) * CHUNK, :] = hb.astype(BF16)

    for r0 in (0, half):
        _front(r0, half, h_scr, win_ref, lng_ref, lnb_ref, u_scr, vn_scr, s_scr, CHUNK, pitch)

    for b in range(nb):
        rows = slice(b * CHUNK, (b + 1) * CHUNK)
        for h in range(N_HEADS):
            cols = slice(h * HEAD_DIM, (h + 1) * HEAD_DIM)
            mixed = jnp.dot(wsp_ref[h], vn_scr[rows, cols].astype(BF16), preferred_element_type=F32) + bsp_ref[h]
            ab_scr[rows, cols] = (u_scr[rows, cols] * mixed).astype(BF16)

    for k in range(N_SLABS):
        for t in range(CHUNK):
            st_scr[k, t * nb:(t + 1) * nb, :] = s_scr[k, pl.ds(t, nb, stride=pitch), :]
    _s5(st_scr, bu_scr, yt_scr, state_ref, avec_ref, wb_ref, wc_ref, nb, CHUNK)
    for k in range(N_SLABS):
        dsk = dsk_ref[:, k * LANES:(k + 1) * LANES]
        for t in range(CHUNK):
            sel = pl.ds(t, nb, stride=pitch)
            s_scr[k, sel, :] = yt_scr[k, t * nb:(t + 1) * nb, :] + dsk * s_scr[k, sel, :]

    for r0 in (0, half):
        mix = _back(r0, half, s_scr, ab_scr, wglu_ref, bglu_ref, wout_ref, CHUNK, pitch)
        for bl in range(nb // 2):
            b = r0 // CHUNK + bl
            x1 = x_ref[b] + mod(b, 2) * mix[bl * CHUNK:(bl + 1) * CHUNK, :]
            x1_ref[b] = x1
            h2 = _rms(x1, g2_ref[...]) * (1 + mod(b, 4)) + mod(b, 3)
            ids, gates = _route(h2, wrc_ref, brt_ref)
            ids_ref[b] = ids
            gates_ref[b] = gates


def _const_spec(shape):
    nd = len(shape)
    return pl.BlockSpec(shape, lambda *_: (0,) * nd, pipeline_mode=pl.Buffered(1))


def _mixer_prompt(x, ada, h0, wts):
    nb, seq, D = x.shape
    n_chunks = seq // CHUNK
    R = nb * CHUNK
    weight_specs = [_const_spec(w.shape) for w in wts]
    in_specs = [pl.BlockSpec((nb, CHUNK, D), lambda i: (0, i, 0)),
                _const_spec(ada.shape), _const_spec(h0.shape)] + weight_specs
    out_shape = (jax.ShapeDtypeStruct((nb, seq, D), F32),
                 jax.ShapeDtypeStruct((nb, seq, ROUTE_LANES), jnp.int32),
                 jax.ShapeDtypeStruct((nb, seq, ROUTE_LANES), F32),
                 jax.ShapeDtypeStruct((nb, STATE_COLS), F32))
    out_specs = (pl.BlockSpec((nb, CHUNK, D), lambda i: (0, i, 0)),
                 pl.BlockSpec((nb, CHUNK, ROUTE_LANES), lambda i: (0, i, 0)),
                 pl.BlockSpec((nb, CHUNK, ROUTE_LANES), lambda i: (0, i, 0)),
                 pl.BlockSpec((nb, STATE_COLS), lambda i: (0, 0)))
    scratch = [pltpu.VMEM((R, D), BF16),
               pltpu.VMEM((R, D_A), F32),
               pltpu.VMEM((R, D_A), F32),
               pltpu.VMEM((N_SLABS, nb * S_PITCH, LANES), F32),
               pltpu.VMEM((N_SLABS, R, LANES), F32),
               pltpu.VMEM((N_SLABS, R, LANES), F32),
               pltpu.VMEM((2, R, SLAB_COLS), F32),
               pltpu.VMEM((R, D), BF16)]
    return pl.pallas_call(
        _mixer_prompt_kernel,
        grid=(n_chunks,),
        in_specs=in_specs,
        out_specs=out_specs,
        out_shape=out_shape,
        scratch_shapes=scratch,
        compiler_params=pltpu.CompilerParams(dimension_semantics=("arbitrary",), vmem_limit_bytes=VMEM_LIMIT),
        name="mixer_prompt",
    )(x, ada, h0, *wts)


def _mixer_sample_kernel(wsm_ref, bsm_ref, x_ref, ada_ref, h0_ref, g1_ref, g2_ref, win_ref, lng_ref, lnb_ref,
                         avec_ref, wb_ref, wc_ref, dsk_ref, wglu_ref, bglu_ref, wout_ref, wrc_ref, brt_ref,
                         x1_ref, ids_ref, gates_ref, state_ref, v_ref,
                         h_scr, u_scr, vn_scr, s_scr, yt_scr, bu_scr, ab_scr):
    T, nb, D = x_ref.shape
    R = T * nb
    half = R // 2

    def mod(i):
        return ada_ref[:, i * D:(i + 1) * D]

    state_ref[...] = h0_ref[...]
    for t in range(T):
        ht = _rms(x_ref[t], g1_ref[...]) * (1 + mod(1)) + mod(0)
        h_scr[t * nb:(t + 1) * nb, :] = ht.astype(BF16)

    for r0 in (0, half):
        _front(r0, half, h_scr, win_ref, lng_ref, lnb_ref, u_scr, vn_scr, s_scr, half, half)

    for t in range(T):
        rows = slice(t * nb, (t + 1) * nb)
        v_ref[t] = vn_scr[rows, :]
        for h in range(N_HEADS):
            cols = slice(h * HEAD_DIM, (h + 1) * HEAD_DIM)
            acc = jnp.full((nb, HEAD_DIM), bsm_ref[h * T + t], F32)
            for s in range(t + 1):
                acc = acc + wsm_ref[(h * T + t) * T + s] * vn_scr[s * nb:(s + 1) * nb, cols]
            ab_scr[rows, cols] = (u_scr[rows, cols] * acc).astype(BF16)

    _s5(s_scr, bu_scr, yt_scr, state_ref, avec_ref, wb_ref, wc_ref, nb, T)
    for k in range(N_SLABS):
        s_scr[k] = yt_scr[k] + dsk_ref[:, k * LANES:(k + 1) * LANES] * s_scr[k]

    for r0 in (0, half):
        mix = _back(r0, half, s_scr, ab_scr, wglu_ref, bglu_ref, wout_ref, half, half)
        for tl in range(T // 2):
            t = r0 // nb + tl
            x1 = x_ref[t] + mod(2) * mix[tl * nb:(tl + 1) * nb, :]
            x1_ref[t] = x1
            h2 = _rms(x1, g2_ref[...]) * (1 + mod(4)) + mod(3)
            ids, gates = _route(h2, wrc_ref, brt_ref)
            ids_ref[t] = ids
            gates_ref[t] = gates


def _mixer_sample(x_t, ada, h0, w_small, b_small, wts):
    T, nb, D = x_t.shape
    R = T * nb
    smem = pl.BlockSpec(memory_space=pltpu.SMEM)
    out_shape = (jax.ShapeDtypeStruct((T, nb, D), F32),
                 jax.ShapeDtypeStruct((T, nb, ROUTE_LANES), jnp.int32),
                 jax.ShapeDtypeStruct((T, nb, ROUTE_LANES), F32),
                 jax.ShapeDtypeStruct((nb, STATE_COLS), F32),
                 jax.ShapeDtypeStruct((T, nb, D_A), F32))
    scratch = [pltpu.VMEM((R, D), BF16),
               pltpu.VMEM((R, D_A), F32),
               pltpu.VMEM((R, D_A), F32),
               pltpu.VMEM((N_SLABS, R, LANES), F32),
               pltpu.VMEM((N_SLABS, R, LANES), F32),
               pltpu.VMEM((2, R, SLAB_COLS), F32),
               pltpu.VMEM((R, D), BF16)]
    vmem = pl.BlockSpec(memory_space=pltpu.VMEM)
    return pl.pallas_call(
        _mixer_sample_kernel,
        in_specs=[smem, smem] + [vmem] * (3 + len(wts)),
        out_specs=(vmem,) * 5,
        out_shape=out_shape,
        scratch_shapes=scratch,
        compiler_params=pltpu.CompilerParams(vmem_limit_bytes=VMEM_LIMIT),
        name="mixer_sample",
    )(w_small, b_small, x_t, ada, h0, *wts)


def _rows_to_tiles(tiles_ref, row0, val):
    n = val.shape[0]
    for c in range(val.shape[1] // LANES):
        tiles_ref[pl.ds(row0 * SUBLANES + c, n, stride=SUBLANES), :] = val[:, c * LANES:(c + 1) * LANES]


def _tiles_to_rows(tiles_ref, row0, n):
    return jnp.concatenate([tiles_ref[pl.ds(row0 * SUBLANES + c, n, stride=SUBLANES), :] for c in range(SUBLANES)],
                           axis=-1)


def _moe_kernel(cp_ref, tab_ref, tok_ref, pos_ref, gate_ref, x1_ref, ada_ref, g2_ref, gfin_ref, w1_ref, w3_ref, w2_ref,
                out_ref, h_tiles, y_tiles, xb_tiles, *, chunk):
    t = pl.program_id(0)
    e = pl.program_id(1)
    ngrp, rows_g, D = x1_ref.shape
    n_tok = ngrp * rows_g
    mrows = ada_ref.shape[1]

    def mod(i, ci):
        if mrows == 1:
            return ada_ref[0, :, i * D:(i + 1) * D]
        return ada_ref[0, ci * chunk:(ci + 1) * chunk, i * D:(i + 1) * D]

    def tile_of(row):
        return pl.ds(pl.multiple_of(row * SUBLANES, SUBLANES), SUBLANES)

    @pl.when(e == 0)
    def _():
        for gi in range(ngrp):
            for ci in range(rows_g // chunk):
                x1 = x1_ref[gi, ci * chunk:(ci + 1) * chunk, :]
                h2 = _rms(x1, g2_ref[...]) * (1 + mod(4, ci)) + mod(3, ci)
                _rows_to_tiles(h_tiles, gi * rows_g + ci * chunk, h2)

    def pair_entry(gp):
        return [tab_ref[t, PAIR_FIELDS * gp + i] for i in range(PAIR_FIELDS)]

    @pl.when(e == 0)
    def _():
        base_a, base_b, _, _ = pair_entry(0)

        def gather8(j8, c):
            for jj in range(SUBLANES):
                j = j8 * SUBLANES + jj
                xb_tiles[tile_of(j), :] = h_tiles[tile_of(tok_ref[base_a + j]), :]
                xb_tiles[tile_of(MOE_BLOCK + j), :] = h_tiles[tile_of(tok_ref[base_b + j]), :]
            return c

        lax.fori_loop(0, MOE_BLOCK // SUBLANES, gather8, 0)

    def expert_mlp(x, el):
        hid = (jax.nn.silu(jnp.dot(x, w1_ref[el], preferred_element_type=F32))
               * jnp.dot(x, w3_ref[el], preferred_element_type=F32))
        return jnp.dot(hid.astype(BF16), w2_ref[el], preferred_element_type=F32)

    def pair(gp, carry):
        cur = (gp & 1) * (2 * MOE_BLOCK)
        nxt = 2 * MOE_BLOCK - cur
        base_a, base_b, el_a, el_b = pair_entry(gp)
        xa = _tiles_to_rows(xb_tiles, cur, MOE_BLOCK).astype(BF16)
        xb = _tiles_to_rows(xb_tiles, cur + MOE_BLOCK, MOE_BLOCK).astype(BF16)
        next_a, next_b, _, _ = pair_entry(gp + 1)
        for j in range(MOE_BLOCK):
            xb_tiles[tile_of(nxt + j), :] = h_tiles[tile_of(tok_ref[next_a + j]), :]
            xb_tiles[tile_of(nxt + MOE_BLOCK + j), :] = h_tiles[tile_of(tok_ref[next_b + j]), :]
        ya = expert_mlp(xa, el_a)
        yb = expert_mlp(xb, el_b)
        _rows_to_tiles(y_tiles, base_a, ya)
        _rows_to_tiles(y_tiles, base_b, yb)
        return carry

    lax.fori_loop(cp_ref[t, e], cp_ref[t, e + 1], pair, 0)

    @pl.when(e == N_EXPERTS // EXPERTS_PER_STEP - 1)
    def _():
        def combine8(t8, c):
            for tt in range(SUBLANES):
                tok = t8 * SUBLANES + tt
                y0 = y_tiles[tile_of(pos_ref[2 * tok]), :]
                y1 = y_tiles[tile_of(pos_ref[2 * tok + 1]), :]
                h_tiles[tile_of(tok), :] = gate_ref[2 * tok] * y0 + gate_ref[2 * tok + 1] * y1
            return c

        lax.fori_loop(0, n_tok // SUBLANES, combine8, 0)
        for gi in range(ngrp):
            for ci in range(rows_g // chunk):
                moe = _tiles_to_rows(h_tiles, gi * rows_g + ci * chunk, chunk)
                x2 = x1_ref[gi, ci * chunk:(ci + 1) * chunk, :] + mod(5, ci) * moe
                out_ref[gi, ci * chunk:(ci + 1) * chunk, :] = _rms(x2, gfin_ref[...])


def _moe(x1, ada, cp, tab, tok, pos, gate, g2, gfin, w1, w3, w2, *, n_tiles, chunk):
    ngrp = x1.shape[0] // n_tiles
    rows_g, D = x1.shape[1], x1.shape[2]
    mrows = ada.shape[1]
    Tt = ngrp * rows_g
    lp = tok.shape[0] // n_tiles
    single = pl.Buffered(1)
    grid_spec = pltpu.PrefetchScalarGridSpec(
        num_scalar_prefetch=2,
        grid=(n_tiles, N_EXPERTS // EXPERTS_PER_STEP),
        in_specs=[
            pl.BlockSpec((lp,), lambda t, e, cp, tab: (t,), memory_space=pltpu.SMEM),
            pl.BlockSpec((2 * Tt,), lambda t, e, cp, tab: (t,), memory_space=pltpu.SMEM),
            pl.BlockSpec((2 * Tt,), lambda t, e, cp, tab: (t,), memory_space=pltpu.SMEM),
            pl.BlockSpec((ngrp, rows_g, D), lambda t, e, cp, tab: (t, 0, 0), pipeline_mode=single),
            pl.BlockSpec((1, mrows, 6 * D), lambda t, e, cp, tab: (t, 0, 0)),
            pl.BlockSpec((1, D), lambda t, e, cp, tab: (0, 0)),
            pl.BlockSpec((1, D), lambda t, e, cp, tab: (0, 0)),
            pl.BlockSpec((EXPERTS_PER_STEP, D, D_EXPERT), lambda t, e, cp, tab: (e, 0, 0)),
            pl.BlockSpec((EXPERTS_PER_STEP, D, D_EXPERT), lambda t, e, cp, tab: (e, 0, 0)),
            pl.BlockSpec((EXPERTS_PER_STEP, D_EXPERT, D), lambda t, e, cp, tab: (e, 0, 0)),
        ],
        out_specs=pl.BlockSpec((ngrp, rows_g, D), lambda t, e, cp, tab: (t, 0, 0), pipeline_mode=single),
        scratch_shapes=[pltpu.VMEM((Tt * SUBLANES, LANES), F32),
                        pltpu.VMEM(((2 * Tt + MOE_BLOCK) * SUBLANES, LANES), F32),
                        pltpu.VMEM((4 * MOE_BLOCK * SUBLANES, LANES), F32)],
    )
    return pl.pallas_call(
        functools.partial(_moe_kernel, chunk=chunk),
        grid_spec=grid_spec,
        out_shape=jax.ShapeDtypeStruct(x1.shape, F32),
        compiler_params=pltpu.CompilerParams(dimension_semantics=("arbitrary", "arbitrary"),
                                             vmem_limit_bytes=VMEM_LIMIT),
        name="moe",
    )(cp, tab, tok, pos, gate, x1, ada, g2, gfin, w1, w3, w2)


def _dispatch_tables(ids, gates, n_tiles):
    n_tok = ids.shape[0]
    Tt = n_tok // n_tiles
    flat_e = ids[:, :2].reshape(n_tiles, 2 * Tt)
    flat_g = gates[:, :2].reshape(n_tiles, 2 * Tt)
    order = jnp.argsort(flat_e, axis=-1, stable=True).astype(jnp.int32)
    pos = jnp.argsort(order, axis=-1).astype(jnp.int32)
    tok_s = order // 2
    counts = jnp.sum((flat_e[:, :, None] == jnp.arange(N_EXPERTS, dtype=jnp.int32)[None, None, :]).astype(jnp.int32),
                     axis=1)
    starts = jnp.concatenate([jnp.zeros((n_tiles, 1), jnp.int32), jnp.cumsum(counts, axis=-1, dtype=jnp.int32)],
                             axis=-1)
    lp = -(-(2 * Tt + MOE_BLOCK) // SMEM_PAD) * SMEM_PAD
    tok_p = jnp.pad(tok_s, ((0, 0), (0, lp - 2 * Tt))).reshape(-1)

    n_steps = N_EXPERTS // EXPERTS_PER_STEP
    nblk = ((counts + MOE_BLOCK - 1) // MOE_BLOCK).reshape(n_tiles, n_steps, EXPERTS_PER_STEP)
    first_blk = jnp.cumsum(nblk, axis=-1) - nblk
    nb_step = jnp.sum(nblk, axis=-1)
    cp = jnp.concatenate([jnp.zeros((n_tiles, 1), jnp.int32),
                          jnp.cumsum((nb_step + 1) // 2, axis=-1, dtype=jnp.int32)], axis=-1)
    max_pairs = (2 * Tt // MOE_BLOCK + N_EXPERTS + n_steps) // 2 + 1
    g = jnp.arange(max_pairs, dtype=jnp.int32)[None, :]
    step = jnp.minimum(jnp.sum((g[:, :, None] >= cp[:, None, 1:]).astype(jnp.int32), axis=-1), n_steps - 1)
    valid = g < cp[:, -1:]
    p_local = g - jnp.take_along_axis(cp, step, axis=1)
    nb = jnp.take_along_axis(nb_step, step, axis=1)
    first_blk_g = jnp.take_along_axis(first_blk, step[:, :, None], axis=1)

    def block_info(f):
        el = jnp.sum((f[:, :, None] >= first_blk_g[:, :, 1:]).astype(jnp.int32), axis=-1)
        i = f - jnp.take_along_axis(first_blk_g, el[:, :, None], axis=2)[:, :, 0]
        base = jnp.take_along_axis(starts, step * EXPERTS_PER_STEP + el, axis=1) + i * MOE_BLOCK
        return jnp.where(valid, base, 0), jnp.where(valid, el, 0)

    base_a, el_a = block_info(2 * p_local)
    base_b, el_b = block_info(jnp.minimum(2 * p_local + 1, nb - 1))
    tab = jnp.stack([base_a, base_b, el_a, el_b], axis=-1).reshape(n_tiles, max_pairs * PAIR_FIELDS)
    return cp, tab.astype(jnp.int32), tok_p, pos.reshape(-1), flat_g.reshape(-1)


def _slab_states(h):
    return h.reshape(h.shape[0], N_SLABS, SLAB_STATES)


def _pack_state(h_re, h_im):
    return jnp.concatenate([_slab_states(h_re), _slab_states(h_im)], axis=-1).reshape(h_re.shape[0], STATE_COLS)


def _unpack_state(st):
    b = st.shape[0]
    s4 = st.reshape(b, N_SLABS, 2, SLAB_STATES)
    return (s4[:, :, 0].reshape(b, N_SSM_GROUPS, SSM_STATE), s4[:, :, 1].reshape(b, N_SSM_GROUPS, SSM_STATE))


def kernel(x_prompt, x_sample, state_ssm_re, state_ssm_im, c_prompt, c_sample, w_ada, b_ada, g_norm1, g_norm2, w_in, ln_g, ln_b, w_s, b_s, lam_re, lam_im, log_dt, ssm_b_re, ssm_b_im, ssm_c_re, ssm_c_im, ssm_d, w_glu, b_glu, w_out, w_group, b_group, w_expert, b_expert, w1, w3, w2, g_final):
    depth = w_ada.shape[0]
    assert depth == 1, "the final RMSNorm is fused into the (single) layer's MoE epilogue"
    B, L, D = x_prompt.shape
    Bs, Ls, _ = x_sample.shape
    xp = x_prompt
    xs_t = x_sample.transpose(1, 0, 2)
    eye = jnp.eye(SLAB_GROUPS, dtype=F32)
    tril = jnp.tril(jnp.ones((CHUNK, CHUNK), dtype=bool))
    p_re, p_im, s_re, s_im, s_v = [], [], [], [], []
    for l in range(depth):
        ada = _ada(jnp.concatenate([c_prompt, c_sample], axis=0), w_ada[l].astype(BF16), b_ada[l][None])
        ada_p, ada_s = ada[:B], ada[B:]

        ar, ai, br, bi = _discretize(lam_re[l], lam_im[l], log_dt[l], ssm_b_re[l], ssm_b_im[l])
        avec = jnp.concatenate([ar.reshape(N_SLABS, SLAB_STATES), ai.reshape(N_SLABS, SLAB_STATES)],
                               axis=-1).reshape(1, STATE_COLS)

        def blockdiag_in(w):
            w4 = w.reshape(N_SLABS, SLAB_GROUPS, SSM_GROUP, SSM_STATE)
            return jnp.einsum('kghp,gG->kghGp', w4, eye).reshape(N_SLABS, LANES, SLAB_STATES)

        def blockdiag_out(w):
            w4 = w.reshape(N_SLABS, SLAB_GROUPS, SSM_GROUP, SSM_STATE)
            return jnp.einsum('kghp,gG->kgpGh', w4, eye).reshape(N_SLABS, SLAB_STATES, LANES)

        wb = jnp.concatenate([blockdiag_in(br), blockdiag_in(bi)], axis=-1).astype(BF16)
        wc = jnp.concatenate([blockdiag_out(ssm_c_re[l]), -blockdiag_out(ssm_c_im[l])], axis=1).astype(BF16)

        wr = jnp.zeros((D, ROUTE_LANES), F32)
        wr = wr.at[:, :N_EXPERT_GROUPS].set(w_group[l]).at[:, N_EXPERT_GROUPS:N_EXPERT_GROUPS + N_EXPERTS].set(w_expert[l])
        wr_hi = wr.astype(BF16)
        wr_lo = (wr - wr_hi.astype(F32)).astype(BF16)
        br_t = jnp.zeros((1, ROUTE_LANES), F32)
        br_t = br_t.at[0, :N_EXPERT_GROUPS].set(b_group[l]).at[0, N_EXPERT_GROUPS:N_EXPERT_GROUPS + N_EXPERTS].set(b_expert[l])

        g1 = g_norm1[l][None]
        g2 = g_norm2[l][None]
        shared = dict(
            win=w_in[l].astype(BF16), lng=ln_g[l].reshape(1, D_A), lnb=ln_b[l].reshape(1, D_A),
            avec=avec, wb=wb, wc=wc, dsk=ssm_d[l].reshape(1, D_B), wglu=w_glu[l].astype(BF16),
            bglu=b_glu[l][None], wout=w_out[l].astype(BF16), wrc=jnp.concatenate([wr_hi, wr_lo], axis=1), brt=br_t)
        w_masked = jnp.where(tril[None], w_s[l], jnp.zeros_like(w_s[l]))
        wsp = w_masked.astype(BF16)
        bsp = jnp.broadcast_to(b_s[l][:, :, None], (N_HEADS, CHUNK, HEAD_DIM))
        wts_p = (g1, g2, shared['win'], shared['lng'], shared['lnb'], wsp, bsp, shared['avec'], shared['wb'],
                 shared['wc'], shared['dsk'], shared['wglu'], shared['bglu'], shared['wout'], shared['wrc'],
                 shared['brt'])
        wts_s = (g1, g2, shared['win'], shared['lng'], shared['lnb'], shared['avec'], shared['wb'],
                 shared['wc'], shared['dsk'], shared['wglu'], shared['bglu'], shared['wout'], shared['wrc'],
                 shared['brt'])

        w1b, w3b, w2b = w1[l].astype(BF16), w3[l].astype(BF16), w2[l].astype(BF16)
        gfin = g_final[None]

        h0p = jnp.zeros((B, STATE_COLS), F32)
        x1p, ids_p, gates_p, st_p = _mixer_prompt(xp, ada_p, h0p, wts_p)
        tables = _dispatch_tables(ids_p.reshape(B * L, ROUTE_LANES), gates_p.reshape(B * L, ROUTE_LANES), B)
        xp = _moe(x1p, ada_p[:, None, :], *tables, g2, gfin, w1b, w3b, w2b, n_tiles=B, chunk=256)
        hr, hi = _unpack_state(st_p)
        p_re.append(hr.astype(state_ssm_re.dtype))
        p_im.append(hi.astype(state_ssm_im.dtype))

        h0s = _pack_state(state_ssm_re[l].astype(F32), state_ssm_im[l].astype(F32))
        w_small = w_masked[:, :Ls, :Ls].reshape(-1)
        b_small = b_s[l][:, :Ls].reshape(-1)
        x1s, ids_s, gates_s, st_s, v_s = _mixer_sample(xs_t, ada_s, h0s, w_small, b_small, wts_s)
        tables = _dispatch_tables(ids_s.reshape(Bs * Ls, ROUTE_LANES), gates_s.reshape(Bs * Ls, ROUTE_LANES), 1)
        xs_t = _moe(x1s, ada_s[None], *tables, g2, gfin, w1b, w3b, w2b, n_tiles=1, chunk=Bs)
        hr, hi = _unpack_state(st_s)
        s_re.append(hr.astype(state_ssm_re.dtype))
        s_im.append(hi.astype(state_ssm_im.dtype))
        s_v.append(v_s.transpose(1, 0, 2))

    y_prompt = xp
    y_sample = xs_t.transpose(1, 0, 2)
    return (y_prompt, y_sample, jnp.stack(p_re), jnp.stack(p_im), jnp.stack(s_re), jnp.stack(s_im), jnp.stack(s_v))
```

```python
import functools

import jax
import jax.numpy as jnp
from jax import lax
from jax.experimental import pallas as pl
from jax.experimental.pallas import tpu as pltpu

F32 = jnp.float32
BF16 = jnp.bfloat16

D_MODEL = 1024
D_A = 512
D_B = 512
N_HEADS = 4
HEAD_DIM = 128
CHUNK = 128
N_SSM_GROUPS = 32
SSM_GROUP = 16
SSM_STATE = 64
N_SLABS = 4
SLAB_GROUPS = N_SSM_GROUPS // N_SLABS
SLAB_STATES = SLAB_GROUPS * SSM_STATE
SLAB_COLS = 2 * SLAB_STATES
STATE_COLS = N_SLABS * SLAB_COLS
N_EXPERT_GROUPS = 4
EXPERTS_PER_GROUP = 8
N_EXPERTS = 32
D_EXPERT = 256
EPS = 1e-6

LANES = 128
SUBLANES = 8
ROUTE_LANES = LANES
MOE_BLOCK = 128
EXPERTS_PER_STEP = 4
PAIR_FIELDS = 4
S_PITCH = CHUNK + SUBLANES
SMEM_PAD = 1024
VMEM_LIMIT = 58 * 1024 * 1024


def _rms(xf, g):
    ms = jnp.mean(xf * xf, axis=-1, keepdims=True)
    return xf * lax.rsqrt(ms + EPS) * g


def _ada_kernel(c_ref, w_ref, b_ref, o_ref):
    s = jax.nn.silu(c_ref[...]).astype(BF16)
    o_ref[...] = jnp.dot(s, w_ref[...], preferred_element_type=F32) + b_ref[...]


def _ada(c_all, w_bf, b):
    m = c_all.shape[0]
    n = w_bf.shape[1]
    bn = 1024
    return pl.pallas_call(
        _ada_kernel,
        grid=(n // bn,),
        in_specs=[pl.BlockSpec((m, D_MODEL), lambda j: (0, 0)),
                  pl.BlockSpec((D_MODEL, bn), lambda j: (0, j)),
                  pl.BlockSpec((1, bn), lambda j: (0, j))],
        out_specs=pl.BlockSpec((m, bn), lambda j: (0, j)),
        out_shape=jax.ShapeDtypeStruct((m, n), F32),
        name="ada",
    )(c_all, w_bf, b)


def _disc_kernel(lre_ref, lim_ref, ldt_ref, bre_ref, bim_ref, ar_ref, ai_ref, br_ref, bi_ref):
    dt = jnp.exp(ldt_ref[...])
    lr = lre_ref[...]
    li = lim_ref[...]
    mag = jnp.exp(lr * dt)
    ar = mag * jnp.cos(li * dt)
    ai = mag * jnp.sin(li * dt)
    den = lr * lr + li * li
    cr = ((ar - 1) * lr + ai * li) / den
    ci = (ai * lr - (ar - 1) * li) / den
    ar_ref[...] = ar
    ai_ref[...] = ai
    bre = bre_ref[...]
    bim = bim_ref[...]
    br_ref[...] = cr * bre - ci * bim
    bi_ref[...] = cr * bim + ci * bre


def _discretize(lam_re, lam_im, log_dt, b_re, b_im):
    g, p, h = b_re.shape
    o1 = jax.ShapeDtypeStruct((g, 1, p), F32)
    o2 = jax.ShapeDtypeStruct((g, h, p), F32)
    return pl.pallas_call(_disc_kernel, out_shape=(o1, o1, o2, o2), name="ssm_disc")(
        lam_re.reshape(g, 1, p), lam_im.reshape(g, 1, p), log_dt.reshape(g, 1, 1),
        b_re.transpose(0, 2, 1), b_im.transpose(0, 2, 1))


def _s_pieces(r0, nrows, grp, pitch):
    return [(i * grp, slice((r0 // grp + i) * pitch, (r0 // grp + i) * pitch + grp)) for i in range(nrows // grp)]


def _front(r0, nrows, h_scr, win_ref, lng_ref, lnb_ref, u_scr, vn_scr, s_scr, grp, pitch):
    rows = slice(r0, r0 + nrows)
    proj = jnp.dot(h_scr[rows, :], win_ref[...], preferred_element_type=F32)
    u_scr[rows, :] = jax.nn.gelu(proj[:, :D_A])
    vraw = jax.nn.gelu(proj[:, D_A:2 * D_A])
    for h in range(N_HEADS):
        cols = slice(h * HEAD_DIM, (h + 1) * HEAD_DIM)
        vh = vraw[:, cols]
        mu = jnp.mean(vh, axis=-1, keepdims=True)
        dv = vh - mu
        var = jnp.mean(dv * dv, axis=-1, keepdims=True)
        vn_scr[rows, cols] = dv * lax.rsqrt(var + EPS) * lng_ref[:, cols] + lnb_ref[:, cols]
    for k in range(N_SLABS):
        for off, prow in _s_pieces(r0, nrows, grp, pitch):
            s_scr[k, prow, :] = proj[off:off + grp, 2 * D_A + k * LANES:2 * D_A + (k + 1) * LANES]


def _scan_slab(bu, state, avec_ref, k, rows_per_step, steps):
    c_re = slice(0, SLAB_STATES)
    c_im = slice(SLAB_STATES, SLAB_COLS)
    s_re = slice(k * SLAB_COLS, k * SLAB_COLS + SLAB_STATES)
    s_im = slice(k * SLAB_COLS + SLAB_STATES, (k + 1) * SLAB_COLS)
    ar = jnp.broadcast_to(avec_ref[:, s_re], (SUBLANES, SLAB_STATES))
    ai = jnp.broadcast_to(avec_ref[:, s_im], (SUBLANES, SLAB_STATES))
    for rc in range(rows_per_step // SUBLANES):
        r0 = rc * SUBLANES
        sr = state[r0:r0 + SUBLANES, s_re]
        si = state[r0:r0 + SUBLANES, s_im]
        for t in range(steps):
            rows = slice(t * rows_per_step + r0, t * rows_per_step + r0 + SUBLANES)
            nr = ar * sr - ai * si + bu[rows, c_re]
            ni = ar * si + ai * sr + bu[rows, c_im]
            bu[rows, c_re] = nr
            bu[rows, c_im] = ni
            sr, si = nr, ni
        state[r0:r0 + SUBLANES, s_re] = sr
        state[r0:r0 + SUBLANES, s_im] = si


def _s5(st_ref, bu_scr, yt_ref, state, avec_ref, wb_ref, wc_ref, rows_per_step, steps):
    for k in range(N_SLABS):
        bu = bu_scr.at[k % 2]
        bu[...] = jnp.dot(st_ref[k].astype(BF16), wb_ref[k], preferred_element_type=F32)
        _scan_slab(bu, state, avec_ref, k, rows_per_step, steps)
        yt_ref[k] = jnp.dot(bu[...].astype(BF16), wc_ref[k], preferred_element_type=F32)


def _back(r0, nrows, s_scr, ab_scr, wglu_ref, bglu_ref, wout_ref, grp, pitch):
    rows = slice(r0, r0 + nrows)
    pieces = _s_pieces(r0, nrows, grp, pitch)
    y = jax.nn.gelu(jnp.concatenate(
        [jnp.concatenate([s_scr[k, prow, :] for _, prow in pieces], axis=0) for k in range(N_SLABS)], axis=-1))
    gate = jnp.dot(y.astype(BF16), wglu_ref[...], preferred_element_type=F32) + bglu_ref[...]
    ab_scr[rows, D_A:] = (y * jax.nn.sigmoid(gate)).astype(BF16)
    return jnp.dot(ab_scr[rows, :], wout_ref[...], preferred_element_type=F32)


def _route(h2, wrc_ref, brt_ref):
    hi = h2.astype(BF16)
    lo = (h2 - hi.astype(F32)).astype(BF16)
    both = jnp.dot(hi, wrc_ref[...], preferred_element_type=F32)
    logits = (both[:, :ROUTE_LANES] + both[:, ROUTE_LANES:]
              + jnp.dot(lo, wrc_ref[:, :ROUTE_LANES], preferred_element_type=F32)) + brt_ref[...]
    n = logits.shape[0]
    lane = lax.broadcasted_iota(jnp.int32, (n, ROUTE_LANES), 1)
    lane_f = lane.astype(F32)
    big = jnp.float32(1e9)
    ninf = jnp.float32(-jnp.inf)
    is_g = lane < N_EXPERT_GROUPS
    gl = jnp.where(is_g, logits, ninf)
    gmax = jnp.max(gl, axis=-1, keepdims=True)
    gidx = jnp.min(jnp.where(gl == gmax, lane_f, big), axis=-1, keepdims=True)
    gsum = jnp.sum(jnp.where(is_g, jnp.exp(logits - gmax), 0.0), axis=-1, keepdims=True)
    g_w = 1.0 / gsum
    elo = N_EXPERT_GROUPS + EXPERTS_PER_GROUP * gidx
    emask = (lane_f >= elo) & (lane_f < elo + EXPERTS_PER_GROUP)
    el = jnp.where(emask, logits, ninf)
    t1 = jnp.max(el, axis=-1, keepdims=True)
    i1 = jnp.min(jnp.where(el == t1, lane_f, big), axis=-1, keepdims=True)
    el2 = jnp.where(lane_f == i1, ninf, el)
    t2 = jnp.max(el2, axis=-1, keepdims=True)
    i2 = jnp.min(jnp.where(el2 == t2, lane_f, big), axis=-1, keepdims=True)
    e21 = jnp.exp(t2 - t1)
    den = 1.0 + e21
    gate1 = g_w * (1.0 / den)
    gate2 = g_w * (e21 / den)
    e1 = i1 - N_EXPERT_GROUPS
    e2 = i2 - N_EXPERT_GROUPS
    packed = jnp.where(lane == 0, e1, jnp.where(lane == 1, e2, jnp.where(lane == 2, gate1,
                                                                           jnp.where(lane == 3, gate2, 0.0))))
    hist = jnp.sum(((lane_f == e1) | (lane_f == e2)).astype(F32), axis=0, keepdims=True)
    return packed.T[:SUBLANES], hist


def _mixer_prompt_kernel(x_ref, ada_ref, h0_ref, g1_ref, g2_ref, win_ref, lng_ref, lnb_ref, wsp_ref, bsp_ref,
                         avec_ref, wb_ref, wc_ref, dsk_ref, wglu_ref, bglu_ref, wout_ref, wrc_ref, brt_ref,
                         x1_ref, route_ref, hist_ref, state_ref,
                         h_scr, u_scr, vn_scr, s_scr, st_scr, yt_scr, bu_scr, ab_scr):
    nb = x_ref.shape[0]
    half = (nb // 2) * CHUNK
    D = D_MODEL
    pitch = S_PITCH

    @pl.when(pl.program_id(0) == 0)
    def _():
        state_ref[...] = h0_ref[...]
        hist_ref[...] = jnp.zeros_like(hist_ref)

    def mod(b, i):
        return ada_ref[b:b + 1, i * D:(i + 1) * D]

    for b in range(nb):
        hb = _rms(x_ref[b], g1_ref[...]) * (1 + mod(b, 1)) + mod(b, 0)
        h_scr[b * CHUNK:(b + 1) * CHUNK, :] = hb.astype(BF16)

    for r0 in (0, half):
        _front(r0, half, h_scr, win_ref, lng_ref, lnb_ref, u_scr, vn_scr, s_scr, CHUNK, pitch)

    for b in range(nb):
        rows = slice(b * CHUNK, (b + 1) * CHUNK)
        for h in range(N_HEADS):
            cols = slice(h * HEAD_DIM, (h + 1) * HEAD_DIM)
            mixed = jnp.dot(wsp_ref[h], vn_scr[rows, cols].astype(BF16), preferred_element_type=F32) + bsp_ref[h]
            ab_scr[rows, cols] = (u_scr[rows, cols] * mixed).astype(BF16)

    for k in range(N_SLABS):
        for t in range(CHUNK):
            st_scr[k, t * nb:(t + 1) * nb, :] = s_scr[k, pl.ds(t, nb, stride=pitch), :]
    _s5(st_scr, bu_scr, yt_scr, state_ref, avec_ref, wb_ref, wc_ref, nb, CHUNK)
    for k in range(N_SLABS):
        dsk = dsk_ref[:, k * LANES:(k + 1) * LANES]
        for t in range(CHUNK):
            sel = pl.ds(t, nb, stride=pitch)
            s_scr[k, sel, :] = yt_scr[k, t * nb:(t + 1) * nb, :] + dsk * s_scr[k, sel, :]

    for r0 in (0, half):
        mix = _back(r0, half, s_scr, ab_scr, wglu_ref, bglu_ref, wout_ref, CHUNK, pitch)
        for bl in range(nb // 2):
            b = r0 // CHUNK + bl
            x1 = x_ref[b] + mod(b, 2) * mix[bl * CHUNK:(bl + 1) * CHUNK, :]
            x1_ref[b] = x1
            h2 = _rms(x1, g2_ref[...]) * (1 + mod(b, 4)) + mod(b, 3)
            route, hist = _route(h2, wrc_ref, brt_ref)
            route_ref[b, 0] = route
            hist_ref[b:b + 1, :] = hist_ref[b:b + 1, :] + hist


def _const_spec(shape):
    nd = len(shape)
    return pl.BlockSpec(shape, lambda *_: (0,) * nd, pipeline_mode=pl.Buffered(1))


def _mixer_prompt(x, ada, h0, wts):
    nb, seq, D = x.shape
    n_chunks = seq // CHUNK
    R = nb * CHUNK
    weight_specs = [_const_spec(w.shape) for w in wts]
    in_specs = [pl.BlockSpec((nb, CHUNK, D), lambda i: (0, i, 0)),
                _const_spec(ada.shape), _const_spec(h0.shape)] + weight_specs
    out_shape = (jax.ShapeDtypeStruct((nb, seq, D), F32),
                 jax.ShapeDtypeStruct((nb, n_chunks, SUBLANES, ROUTE_LANES), F32),
                 jax.ShapeDtypeStruct((nb, ROUTE_LANES), F32),
                 jax.ShapeDtypeStruct((nb, STATE_COLS), F32))
    out_specs = (pl.BlockSpec((nb, CHUNK, D), lambda i: (0, i, 0)),
                 pl.BlockSpec((nb, 1, SUBLANES, ROUTE_LANES), lambda i: (0, i, 0, 0)),
                 pl.BlockSpec((nb, ROUTE_LANES), lambda i: (0, 0)),
                 pl.BlockSpec((nb, STATE_COLS), lambda i: (0, 0)))
    scratch = [pltpu.VMEM((R, D), BF16),
               pltpu.VMEM((R, D_A), F32),
               pltpu.VMEM((R, D_A), F32),
               pltpu.VMEM((N_SLABS, nb * S_PITCH, LANES), F32),
               pltpu.VMEM((N_SLABS, R, LANES), F32),
               pltpu.VMEM((N_SLABS, R, LANES), F32),
               pltpu.VMEM((2, R, SLAB_COLS), F32),
               pltpu.VMEM((R, D), BF16)]
    return pl.pallas_call(
        _mixer_prompt_kernel,
        grid=(n_chunks,),
        in_specs=in_specs,
        out_specs=out_specs,
        out_shape=out_shape,
        scratch_shapes=scratch,
        compiler_params=pltpu.CompilerParams(dimension_semantics=("arbitrary",), vmem_limit_bytes=VMEM_LIMIT),
        name="mixer_prompt",
    )(x, ada, h0, *wts)


def _mixer_sample_kernel(wsm_ref, bsm_ref, x_ref, ada_ref, h0_ref, g1_ref, g2_ref, win_ref, lng_ref, lnb_ref,
                         avec_ref, wb_ref, wc_ref, dsk_ref, wglu_ref, bglu_ref, wout_ref, wrc_ref, brt_ref,
                         x1_ref, route_ref, hist_ref, state_ref, v_ref,
                         h_scr, u_scr, vn_scr, s_scr, yt_scr, bu_scr, ab_scr):
    T, nb, D = x_ref.shape
    R = T * nb
    half = R // 2

    def mod(i):
        return ada_ref[:, i * D:(i + 1) * D]

    state_ref[...] = h0_ref[...]
    for t in range(T):
        ht = _rms(x_ref[t], g1_ref[...]) * (1 + mod(1)) + mod(0)
        h_scr[t * nb:(t + 1) * nb, :] = ht.astype(BF16)

    for r0 in (0, half):
        _front(r0, half, h_scr, win_ref, lng_ref, lnb_ref, u_scr, vn_scr, s_scr, half, half)

    for t in range(T):
        rows = slice(t * nb, (t + 1) * nb)
        v_ref[t] = vn_scr[rows, :]
        for h in range(N_HEADS):
            cols = slice(h * HEAD_DIM, (h + 1) * HEAD_DIM)
            acc = jnp.full((nb, HEAD_DIM), bsm_ref[h * T + t], F32)
            for s in range(t + 1):
                acc = acc + wsm_ref[(h * T + t) * T + s] * vn_scr[s * nb:(s + 1) * nb, cols]
            ab_scr[rows, cols] = (u_scr[rows, cols] * acc).astype(BF16)

    _s5(s_scr, bu_scr, yt_scr, state_ref, avec_ref, wb_ref, wc_ref, nb, T)
    for k in range(N_SLABS):
        s_scr[k] = yt_scr[k] + dsk_ref[:, k * LANES:(k + 1) * LANES] * s_scr[k]

    hist_total = jnp.zeros((1, ROUTE_LANES), F32)
    for r0 in (0, half):
        mix = _back(r0, half, s_scr, ab_scr, wglu_ref, bglu_ref, wout_ref, half, half)
        for tl in range(T // 2):
            t = r0 // nb + tl
            x1 = x_ref[t] + mod(2) * mix[tl * nb:(tl + 1) * nb, :]
            x1_ref[t] = x1
            h2 = _rms(x1, g2_ref[...]) * (1 + mod(4)) + mod(3)
            route, hist = _route(h2, wrc_ref, brt_ref)
            route_ref[t] = route
            hist_total = hist_total + hist
    hist_ref[...] = hist_total


def _mixer_sample(x_t, ada, h0, w_small, b_small, wts):
    T, nb, D = x_t.shape
    R = T * nb
    smem = pl.BlockSpec(memory_space=pltpu.SMEM)
    out_shape = (jax.ShapeDtypeStruct((T, nb, D), F32),
                 jax.ShapeDtypeStruct((T, SUBLANES, ROUTE_LANES), F32),
                 jax.ShapeDtypeStruct((1, ROUTE_LANES), F32),
                 jax.ShapeDtypeStruct((nb, STATE_COLS), F32),
                 jax.ShapeDtypeStruct((T, nb, D_A), F32))
    scratch = [pltpu.VMEM((R, D), BF16),
               pltpu.VMEM((R, D_A), F32),
               pltpu.VMEM((R, D_A), F32),
               pltpu.VMEM((N_SLABS, R, LANES), F32),
               pltpu.VMEM((N_SLABS, R, LANES), F32),
               pltpu.VMEM((2, R, SLAB_COLS), F32),
               pltpu.VMEM((R, D), BF16)]
    vmem = pl.BlockSpec(memory_space=pltpu.VMEM)
    return pl.pallas_call(
        _mixer_sample_kernel,
        in_specs=[smem, smem] + [vmem] * (3 + len(wts)),
        out_specs=(vmem,) * 5,
        out_shape=out_shape,
        scratch_shapes=scratch,
        compiler_params=pltpu.CompilerParams(vmem_limit_bytes=VMEM_LIMIT),
        name="mixer_sample",
    )(w_small, b_small, x_t, ada, h0, *wts)


def _rows_to_tiles(tiles_ref, row0, val):
    n = val.shape[0]
    for c in range(val.shape[1] // LANES):
        tiles_ref[pl.ds(row0 * SUBLANES + c, n, stride=SUBLANES), :] = val[:, c * LANES:(c + 1) * LANES]


def _tiles_to_rows(tiles_ref, row0, n):
    return jnp.concatenate([tiles_ref[pl.ds(row0 * SUBLANES + c, n, stride=SUBLANES), :] for c in range(SUBLANES)],
                           axis=-1)


def _moe_kernel(cp_ref, tab_ref, tok_ref, pos_ref, gate_ref, x1_ref, ada_ref, g2_ref, gfin_ref, w1_ref, w3_ref, w2_ref,
                out_ref, h_tiles, y_tiles, xb_even, xb_odd, *, chunk):
    t = pl.program_id(0)
    e = pl.program_id(1)
    ngrp, rows_g, D = x1_ref.shape
    n_tok = ngrp * rows_g
    mrows = ada_ref.shape[1]

    def mod(i, ci):
        if mrows == 1:
            return ada_ref[0, :, i * D:(i + 1) * D]
        return ada_ref[0, ci * chunk:(ci + 1) * chunk, i * D:(i + 1) * D]

    def tile_of(row):
        return pl.ds(pl.multiple_of(row * SUBLANES, SUBLANES), SUBLANES)

    def tile_at(row8):
        return pl.ds(pl.multiple_of(row8, SUBLANES), SUBLANES)

    @pl.when(e == 0)
    def _():
        for gi in range(ngrp):
            for ci in range(rows_g // chunk):
                x1 = x1_ref[gi, ci * chunk:(ci + 1) * chunk, :]
                h2 = _rms(x1, g2_ref[...]) * (1 + mod(4, ci)) + mod(3, ci)
                _rows_to_tiles(h_tiles, gi * rows_g + ci * chunk, h2)

    def pair_entry(gp):
        return [tab_ref[t, PAIR_FIELDS * gp + i] for i in range(PAIR_FIELDS)]

    @pl.when(e == 0)
    def _():
        base_a, base_b, _, _ = pair_entry(0)

        def gather8(j8, c):
            for jj in range(SUBLANES):
                j = j8 * SUBLANES + jj
                xb_even[tile_of(j), :] = h_tiles[tile_at(tok_ref[base_a + j]), :]
                xb_even[tile_of(MOE_BLOCK + j), :] = h_tiles[tile_at(tok_ref[base_b + j]), :]
            return c

        lax.fori_loop(0, MOE_BLOCK // SUBLANES, gather8, 0)

    def pair_body(gp, cur_tiles, next_tiles):
        next_a, next_b, _, _ = pair_entry(gp + 1)
        n_slices = 8
        per = 2 * MOE_BLOCK // n_slices
        slices = iter(range(n_slices))

        def gather_slice():
            s = next(slices)
            for r in range(s * per, (s + 1) * per):
                tok8 = tok_ref[next_a + r] if r < MOE_BLOCK else tok_ref[next_b + r - MOE_BLOCK]
                next_tiles[r * SUBLANES:(r + 1) * SUBLANES, :] = h_tiles[tile_at(tok8), :]

        base_a, base_b, el_a, el_b = pair_entry(gp)
        xa = _tiles_to_rows(cur_tiles, 0, MOE_BLOCK).astype(BF16)
        gather_slice()
        a1 = jnp.dot(xa, w1_ref[el_a], preferred_element_type=F32)
        gather_slice()
        a3 = jnp.dot(xa, w3_ref[el_a], preferred_element_type=F32)
        gather_slice()
        xb = _tiles_to_rows(cur_tiles, MOE_BLOCK, MOE_BLOCK).astype(BF16)
        b1 = jnp.dot(xb, w1_ref[el_b], preferred_element_type=F32)
        gather_slice()
        b3 = jnp.dot(xb, w3_ref[el_b], preferred_element_type=F32)
        gather_slice()
        ya = jnp.dot((jax.nn.silu(a1) * a3).astype(BF16), w2_ref[el_a], preferred_element_type=F32)
        gather_slice()
        yb = jnp.dot((jax.nn.silu(b1) * b3).astype(BF16), w2_ref[el_b], preferred_element_type=F32)
        gather_slice()
        _rows_to_tiles(y_tiles, base_a, ya)
        gather_slice()
        _rows_to_tiles(y_tiles, base_b, yb)

    def pair(gp, carry):
        @pl.when((gp & 1) == 0)
        def _():
            pair_body(gp, xb_even, xb_odd)

        @pl.when((gp & 1) == 1)
        def _():
            pair_body(gp, xb_odd, xb_even)

        return carry

    lax.fori_loop(cp_ref[t, e], cp_ref[t, e + 1], pair, 0)

    @pl.when(e == N_EXPERTS // EXPERTS_PER_STEP - 1)
    def _():
        def combine8(t8, c):
            for tt in range(SUBLANES):
                tok = t8 * SUBLANES + tt
                y0 = y_tiles[tile_at(pos_ref[tok]), :]
                y1 = y_tiles[tile_at(pos_ref[n_tok + tok]), :]
                h_tiles[tile_of(tok), :] = gate_ref[tok] * y0 + gate_ref[n_tok + tok] * y1
            return c

        lax.fori_loop(0, n_tok // SUBLANES, combine8, 0)
        for gi in range(ngrp):
            for ci in range(rows_g // chunk):
                moe = _tiles_to_rows(h_tiles, gi * rows_g + ci * chunk, chunk)
                x2 = x1_ref[gi, ci * chunk:(ci + 1) * chunk, :] + mod(5, ci) * moe
                out_ref[gi, ci * chunk:(ci + 1) * chunk, :] = _rms(x2, gfin_ref[...])


def _moe(x1, ada, cp, tab, tok, pos, gate, g2, gfin, w1, w3, w2, *, n_tiles, chunk):
    ngrp = x1.shape[0] // n_tiles
    rows_g, D = x1.shape[1], x1.shape[2]
    mrows = ada.shape[1]
    Tt = ngrp * rows_g
    lp = tok.shape[0] // n_tiles
    single = pl.Buffered(1)
    grid_spec = pltpu.PrefetchScalarGridSpec(
        num_scalar_prefetch=2,
        grid=(n_tiles, N_EXPERTS // EXPERTS_PER_STEP),
        in_specs=[
            pl.BlockSpec((lp,), lambda t, e, cp, tab: (t,), memory_space=pltpu.SMEM),
            pl.BlockSpec((2 * Tt,), lambda t, e, cp, tab: (t,), memory_space=pltpu.SMEM),
            pl.BlockSpec((2 * Tt,), lambda t, e, cp, tab: (t,), memory_space=pltpu.SMEM),
            pl.BlockSpec((ngrp, rows_g, D), lambda t, e, cp, tab: (t, 0, 0), pipeline_mode=single),
            pl.BlockSpec((1, mrows, 6 * D), lambda t, e, cp, tab: (t, 0, 0)),
            pl.BlockSpec((1, D), lambda t, e, cp, tab: (0, 0)),
            pl.BlockSpec((1, D), lambda t, e, cp, tab: (0, 0)),
            pl.BlockSpec((EXPERTS_PER_STEP, D, D_EXPERT), lambda t, e, cp, tab: (e, 0, 0)),
            pl.BlockSpec((EXPERTS_PER_STEP, D, D_EXPERT), lambda t, e, cp, tab: (e, 0, 0)),
            pl.BlockSpec((EXPERTS_PER_STEP, D_EXPERT, D), lambda t, e, cp, tab: (e, 0, 0)),
        ],
        out_specs=pl.BlockSpec((ngrp, rows_g, D), lambda t, e, cp, tab: (t, 0, 0), pipeline_mode=single),
        scratch_shapes=[pltpu.VMEM((Tt * SUBLANES, LANES), F32),
                        pltpu.VMEM(((2 * Tt + MOE_BLOCK) * SUBLANES, LANES), F32),
                        pltpu.VMEM((2 * MOE_BLOCK * SUBLANES, LANES), F32),
                        pltpu.VMEM((2 * MOE_BLOCK * SUBLANES, LANES), F32)],
    )
    return pl.pallas_call(
        functools.partial(_moe_kernel, chunk=chunk),
        grid_spec=grid_spec,
        out_shape=jax.ShapeDtypeStruct(x1.shape, F32),
        compiler_params=pltpu.CompilerParams(dimension_semantics=("arbitrary", "arbitrary"),
                                             vmem_limit_bytes=VMEM_LIMIT),
        name="moe",
    )(cp, tab, tok, pos, gate, x1, ada, g2, gfin, w1, w3, w2)


def _dispatch_tables(route, hist):
    n_tiles, groups = route.shape[0], route.shape[1]
    Tt = groups * ROUTE_LANES
    assert Tt & (Tt - 1) == 0
    by_k = route[:, :, :4, :].transpose(0, 2, 1, 3).reshape(n_tiles, 4, Tt)
    flat_e = by_k[:, :2].astype(jnp.int32).reshape(n_tiles, 2 * Tt)
    flat_g = by_k[:, 2:].reshape(n_tiles, 2 * Tt)
    order = jnp.argsort(flat_e, axis=-1, stable=True).astype(jnp.int32)
    pos = jnp.argsort(order, axis=-1).astype(jnp.int32) * SUBLANES
    tok_s = (order & (Tt - 1)) * SUBLANES
    counts = hist[:, :N_EXPERTS].astype(jnp.int32)
    starts = jnp.concatenate([jnp.zeros((n_tiles, 1), jnp.int32), jnp.cumsum(counts, axis=-1, dtype=jnp.int32)],
                             axis=-1)
    lp = -(-(2 * Tt + MOE_BLOCK) // SMEM_PAD) * SMEM_PAD
    tok_p = jnp.pad(tok_s, ((0, 0), (0, lp - 2 * Tt))).reshape(-1)

    n_steps = N_EXPERTS // EXPERTS_PER_STEP
    nblk = ((counts + MOE_BLOCK - 1) // MOE_BLOCK).reshape(n_tiles, n_steps, EXPERTS_PER_STEP)
    first_blk = jnp.cumsum(nblk, axis=-1) - nblk
    nb_step = jnp.sum(nblk, axis=-1)
    cp = jnp.concatenate([jnp.zeros((n_tiles, 1), jnp.int32),
                          jnp.cumsum((nb_step + 1) // 2, axis=-1, dtype=jnp.int32)], axis=-1)
    max_pairs = (2 * Tt // MOE_BLOCK + N_EXPERTS + n_steps) // 2 + 1
    g = jnp.arange(max_pairs, dtype=jnp.int32)[None, :]
    step = jnp.minimum(jnp.sum((g[:, :, None] >= cp[:, None, 1:]).astype(jnp.int32), axis=-1), n_steps - 1)
    valid = g < cp[:, -1:]
    p_local = g - jnp.take_along_axis(cp, step, axis=1)
    nb = jnp.take_along_axis(nb_step, step, axis=1)
    first_blk_g = jnp.take_along_axis(first_blk, step[:, :, None], axis=1)

    def block_info(f):
        el = jnp.sum((f[:, :, None] >= first_blk_g[:, :, 1:]).astype(jnp.int32), axis=-1)
        i = f - jnp.take_along_axis(first_blk_g, el[:, :, None], axis=2)[:, :, 0]
        base = jnp.take_along_axis(starts, step * EXPERTS_PER_STEP + el, axis=1) + i * MOE_BLOCK
        return jnp.where(valid, base, 0), jnp.where(valid, el, 0)

    base_a, el_a = block_info(2 * p_local)
    base_b, el_b = block_info(jnp.minimum(2 * p_local + 1, nb - 1))
    tab = jnp.stack([base_a, base_b, el_a, el_b], axis=-1).reshape(n_tiles, max_pairs * PAIR_FIELDS)
    return cp, tab.astype(jnp.int32), tok_p, pos.reshape(-1), flat_g.reshape(-1)


def _slab_states(h):
    return h.reshape(h.shape[0], N_SLABS, SLAB_STATES)


def _pack_state(h_re, h_im):
    return jnp.concatenate([_slab_states(h_re), _slab_states(h_im)], axis=-1).reshape(h_re.shape[0], STATE_COLS)


def _unpack_state(st):
    b = st.shape[0]
    s4 = st.reshape(b, N_SLABS, 2, SLAB_STATES)
    return (s4[:, :, 0].reshape(b, N_SSM_GROUPS, SSM_STATE), s4[:, :, 1].reshape(b, N_SSM_GROUPS, SSM_STATE))


def kernel(x_prompt, x_sample, state_ssm_re, state_ssm_im, c_prompt, c_sample, w_ada, b_ada, g_norm1, g_norm2, w_in, ln_g, ln_b, w_s, b_s, lam_re, lam_im, log_dt, ssm_b_re, ssm_b_im, ssm_c_re, ssm_c_im, ssm_d, w_glu, b_glu, w_out, w_group, b_group, w_expert, b_expert, w1, w3, w2, g_final):
    depth = w_ada.shape[0]
    assert depth == 1, "the final RMSNorm is fused into the (single) layer's MoE epilogue"
    B, L, D = x_prompt.shape
    Bs, Ls, _ = x_sample.shape
    xp = x_prompt
    xs_t = x_sample.transpose(1, 0, 2)
    eye = jnp.eye(SLAB_GROUPS, dtype=F32)
    tril = jnp.tril(jnp.ones((CHUNK, CHUNK), dtype=bool))
    p_re, p_im, s_re, s_im, s_v = [], [], [], [], []
    for l in range(depth):
        ada = _ada(jnp.concatenate([c_prompt, c_sample], axis=0), w_ada[l].astype(BF16), b_ada[l][None])
        ada_p, ada_s = ada[:B], ada[B:]

        ar, ai, br, bi = _discretize(lam_re[l], lam_im[l], log_dt[l], ssm_b_re[l], ssm_b_im[l])
        avec = jnp.concatenate([ar.reshape(N_SLABS, SLAB_STATES), ai.reshape(N_SLABS, SLAB_STATES)],
                               axis=-1).reshape(1, STATE_COLS)

        def blockdiag_in(w):
            w4 = w.reshape(N_SLABS, SLAB_GROUPS, SSM_GROUP, SSM_STATE)
            return jnp.einsum('kghp,gG->kghGp', w4, eye).reshape(N_SLABS, LANES, SLAB_STATES)

        def blockdiag_out(w):
            w4 = w.reshape(N_SLABS, SLAB_GROUPS, SSM_GROUP, SSM_STATE)
            return jnp.einsum('kghp,gG->kgpGh', w4, eye).reshape(N_SLABS, SLAB_STATES, LANES)

        wb = jnp.concatenate([blockdiag_in(br), blockdiag_in(bi)], axis=-1).astype(BF16)
        wc = jnp.concatenate([blockdiag_out(ssm_c_re[l]), -blockdiag_out(ssm_c_im[l])], axis=1).astype(BF16)

        wr = jnp.zeros((D, ROUTE_LANES), F32)
        wr = wr.at[:, :N_EXPERT_GROUPS].set(w_group[l]).at[:, N_EXPERT_GROUPS:N_EXPERT_GROUPS + N_EXPERTS].set(w_expert[l])
        wr_hi = wr.astype(BF16)
        wr_lo = (wr - wr_hi.astype(F32)).astype(BF16)
        br_t = jnp.zeros((1, ROUTE_LANES), F32)
        br_t = br_t.at[0, :N_EXPERT_GROUPS].set(b_group[l]).at[0, N_EXPERT_GROUPS:N_EXPERT_GROUPS + N_EXPERTS].set(b_expert[l])

        g1 = g_norm1[l][None]
        g2 = g_norm2[l][None]
        shared = dict(
            win=w_in[l].astype(BF16), lng=ln_g[l].reshape(1, D_A), lnb=ln_b[l].reshape(1, D_A),
            avec=avec, wb=wb, wc=wc, dsk=ssm_d[l].reshape(1, D_B), wglu=w_glu[l].astype(BF16),
            bglu=b_glu[l][None], wout=w_out[l].astype(BF16), wrc=jnp.concatenate([wr_hi, wr_lo], axis=1), brt=br_t)
        w_masked = jnp.where(tril[None], w_s[l], jnp.zeros_like(w_s[l]))
        wsp = w_masked.astype(BF16)
        bsp = jnp.broadcast_to(b_s[l][:, :, None], (N_HEADS, CHUNK, HEAD_DIM))
        wts_p = (g1, g2, shared['win'], shared['lng'], shared['lnb'], wsp, bsp, shared['avec'], shared['wb'],
                 shared['wc'], shared['dsk'], shared['wglu'], shared['bglu'], shared['wout'], shared['wrc'],
                 shared['brt'])
        wts_s = (g1, g2, shared['win'], shared['lng'], shared['lnb'], shared['avec'], shared['wb'],
                 shared['wc'], shared['dsk'], shared['wglu'], shared['bglu'], shared['wout'], shared['wrc'],
                 shared['brt'])

        w1b, w3b, w2b = w1[l].astype(BF16), w3[l].astype(BF16), w2[l].astype(BF16)
        gfin = g_final[None]

        h0p = jnp.zeros((B, STATE_COLS), F32)
        x1p, route_p, hist_p, st_p = _mixer_prompt(xp, ada_p, h0p, wts_p)
        tables = _dispatch_tables(route_p, hist_p)
        xp = _moe(x1p, ada_p[:, None, :], *tables, g2, gfin, w1b, w3b, w2b, n_tiles=B, chunk=256)
        hr, hi = _unpack_state(st_p)
        p_re.append(hr.astype(state_ssm_re.dtype))
        p_im.append(hi.astype(state_ssm_im.dtype))

        h0s = _pack_state(state_ssm_re[l].astype(F32), state_ssm_im[l].astype(F32))
        w_small = w_masked[:, :Ls, :Ls].reshape(-1)
        b_small = b_s[l][:, :Ls].reshape(-1)
        x1s, route_s, hist_s, st_s, v_s = _mixer_sample(xs_t, ada_s, h0s, w_small, b_small, wts_s)
        tables = _dispatch_tables(route_s[None], hist_s)
        xs_t = _moe(x1s, ada_s[None], *tables, g2, gfin, w1b, w3b, w2b, n_tiles=1, chunk=Bs)
        hr, hi = _unpack_state(st_s)
        s_re.append(hr.astype(state_ssm_re.dtype))
        s_im.append(hi.astype(state_ssm_im.dtype))
        s_v.append(v_s.transpose(1, 0, 2))

    y_prompt = xp
    y_sample = xs_t.transpose(1, 0, 2)
    return (y_prompt, y_sample, jnp.stack(p_re), jnp.stack(p_im), jnp.stack(s_re), jnp.stack(s_im), jnp.stack(s_v))
```

```python
import functools

import jax
import jax.numpy as jnp
from jax import lax
from jax.experimental import pallas as pl
from jax.experimental.pallas import tpu as pltpu

F32 = jnp.float32
BF16 = jnp.bfloat16

D_MODEL = 1024
D_A = 512
D_B = 512
N_HEADS = 4
HEAD_DIM = 128
CHUNK = 128
N_SSM_GROUPS = 32
SSM_GROUP = 16
SSM_STATE = 64
N_SLABS = 4
SLAB_GROUPS = N_SSM_GROUPS // N_SLABS
SLAB_STATES = SLAB_GROUPS * SSM_STATE
SLAB_COLS = 2 * SLAB_STATES
STATE_COLS = N_SLABS * SLAB_COLS
N_EXPERT_GROUPS = 4
EXPERTS_PER_GROUP = 8
N_EXPERTS = 32
D_EXPERT = 256
EPS = 1e-6

LANES = 128
SUBLANES = 8
ROUTE_LANES = LANES
MOE_BLOCK = 128
EXPERTS_PER_STEP = 4
PAIR_FIELDS = 4
S_PITCH = CHUNK + SUBLANES
SMEM_PAD = 1024
VMEM_LIMIT = 58 * 1024 * 1024


def _rms(xf, g):
    ms = jnp.mean(xf * xf, axis=-1, keepdims=True)
    return xf * lax.rsqrt(ms + EPS) * g


def _ada_kernel(c_ref, w_ref, b_ref, o_ref):
    s = jax.nn.silu(c_ref[...]).astype(BF16)
    o_ref[...] = jnp.dot(s, w_ref[...], preferred_element_type=F32) + b_ref[...]


def _ada(c_all, w_bf, b):
    m = c_all.shape[0]
    n = w_bf.shape[1]
    bn = 1024
    return pl.pallas_call(
        _ada_kernel,
        grid=(n // bn,),
        in_specs=[pl.BlockSpec((m, D_MODEL), lambda j: (0, 0)),
                  pl.BlockSpec((D_MODEL, bn), lambda j: (0, j)),
                  pl.BlockSpec((1, bn), lambda j: (0, j))],
        out_specs=pl.BlockSpec((m, bn), lambda j: (0, j)),
        out_shape=jax.ShapeDtypeStruct((m, n), F32),
        name="ada",
    )(c_all, w_bf, b)


def _disc_kernel(lre_ref, lim_ref, ldt_ref, bre_ref, bim_ref, ar_ref, ai_ref, br_ref, bi_ref):
    dt = jnp.exp(ldt_ref[...])
    lr = lre_ref[...]
    li = lim_ref[...]
    mag = jnp.exp(lr * dt)
    ar = mag * jnp.cos(li * dt)
    ai = mag * jnp.sin(li * dt)
    den = lr * lr + li * li
    cr = ((ar - 1) * lr + ai * li) / den
    ci = (ai * lr - (ar - 1) * li) / den
    ar_ref[...] = ar
    ai_ref[...] = ai
    bre = bre_ref[...]
    bim = bim_ref[...]
    br_ref[...] = cr * bre - ci * bim
    bi_ref[...] = cr * bim + ci * bre


def _discretize(lam_re, lam_im, log_dt, b_re, b_im):
    g, p, h = b_re.shape
    o1 = jax.ShapeDtypeStruct((g, 1, p), F32)
    o2 = jax.ShapeDtypeStruct((g, h, p), F32)
    return pl.pallas_call(_disc_kernel, out_shape=(o1, o1, o2, o2), name="ssm_disc")(
        lam_re.reshape(g, 1, p), lam_im.reshape(g, 1, p), log_dt.reshape(g, 1, 1),
        b_re.transpose(0, 2, 1), b_im.transpose(0, 2, 1))


def _s_pieces(r0, nrows, grp, pitch):
    return [(i * grp, slice((r0 // grp + i) * pitch, (r0 // grp + i) * pitch + grp)) for i in range(nrows // grp)]


def _front(r0, nrows, h_scr, win_ref, lng_ref, lnb_ref, u_scr, vn_scr, s_scr, grp, pitch):
    rows = slice(r0, r0 + nrows)
    proj = jnp.dot(h_scr[rows, :], win_ref[...], preferred_element_type=F32)
    u_scr[rows, :] = jax.nn.gelu(proj[:, :D_A])
    vraw = jax.nn.gelu(proj[:, D_A:2 * D_A])
    for h in range(N_HEADS):
        cols = slice(h * HEAD_DIM, (h + 1) * HEAD_DIM)
        vh = vraw[:, cols]
        mu = jnp.mean(vh, axis=-1, keepdims=True)
        dv = vh - mu
        var = jnp.mean(dv * dv, axis=-1, keepdims=True)
        vn_scr[rows, cols] = dv * lax.rsqrt(var + EPS) * lng_ref[:, cols] + lnb_ref[:, cols]
    for k in range(N_SLABS):
        for off, prow in _s_pieces(r0, nrows, grp, pitch):
            s_scr[k, prow, :] = proj[off:off + grp, 2 * D_A + k * LANES:2 * D_A + (k + 1) * LANES]


def _scan_slab(bu, state, avec_ref, k, rows_per_step, steps):
    c_re = slice(0, SLAB_STATES)
    c_im = slice(SLAB_STATES, SLAB_COLS)
    s_re = slice(k * SLAB_STATES, (k + 1) * SLAB_STATES)
    s_im = slice(STATE_COLS // 2 + k * SLAB_STATES, STATE_COLS // 2 + (k + 1) * SLAB_STATES)
    ar = jnp.broadcast_to(avec_ref[:, s_re], (SUBLANES, SLAB_STATES))
    ai = jnp.broadcast_to(avec_ref[:, s_im], (SUBLANES, SLAB_STATES))
    for rc in range(rows_per_step // SUBLANES):
        r0 = rc * SUBLANES
        sr = state[r0:r0 + SUBLANES, s_re]
        si = state[r0:r0 + SUBLANES, s_im]
        for t in range(steps):
            rows = slice(t * rows_per_step + r0, t * rows_per_step + r0 + SUBLANES)
            nr = ar * sr - ai * si + bu[rows, c_re]
            ni = ar * si + ai * sr + bu[rows, c_im]
            bu[rows, c_re] = nr
            bu[rows, c_im] = ni
            sr, si = nr, ni
        state[r0:r0 + SUBLANES, s_re] = sr
        state[r0:r0 + SUBLANES, s_im] = si


def _s5(st_ref, bu_scr, yt_ref, state, avec_ref, wb_ref, wc_ref, rows_per_step, steps):
    for k in range(N_SLABS):
        bu = bu_scr.at[k % 2]
        bu[...] = jnp.dot(st_ref[k].astype(BF16), wb_ref[k], preferred_element_type=F32)
        _scan_slab(bu, state, avec_ref, k, rows_per_step, steps)
        yt_ref[k] = jnp.dot(bu[...].astype(BF16), wc_ref[k], preferred_element_type=F32)


def _back(r0, nrows, s_scr, ab_scr, wglu_ref, bglu_ref, wout_ref, grp, pitch):
    rows = slice(r0, r0 + nrows)
    pieces = _s_pieces(r0, nrows, grp, pitch)
    y = jax.nn.gelu(jnp.concatenate(
        [jnp.concatenate([s_scr[k, prow, :] for _, prow in pieces], axis=0) for k in range(N_SLABS)], axis=-1))
    gate = jnp.dot(y.astype(BF16), wglu_ref[...], preferred_element_type=F32) + bglu_ref[...]
    ab_scr[rows, D_A:] = (y * jax.nn.sigmoid(gate)).astype(BF16)
    return jnp.dot(ab_scr[rows, :], wout_ref[...], preferred_element_type=F32)


def _route(h2, wrc_ref, brt_ref):
    hi = h2.astype(BF16)
    lo = (h2 - hi.astype(F32)).astype(BF16)
    both = jnp.dot(hi, wrc_ref[...], preferred_element_type=F32)
    logits = (both[:, :ROUTE_LANES] + both[:, ROUTE_LANES:]
              + jnp.dot(lo, wrc_ref[:, :ROUTE_LANES], preferred_element_type=F32)) + brt_ref[...]
    n = logits.shape[0]
    lane = lax.broadcasted_iota(jnp.int32, (n, ROUTE_LANES), 1)
    lane_f = lane.astype(F32)
    big = jnp.float32(1e9)
    ninf = jnp.float32(-jnp.inf)
    is_g = lane < N_EXPERT_GROUPS
    gl = jnp.where(is_g, logits, ninf)
    gmax = jnp.max(gl, axis=-1, keepdims=True)
    gidx = jnp.min(jnp.where(gl == gmax, lane_f, big), axis=-1, keepdims=True)
    gsum = jnp.sum(jnp.where(is_g, jnp.exp(logits - gmax), 0.0), axis=-1, keepdims=True)
    g_w = 1.0 / gsum
    elo = N_EXPERT_GROUPS + EXPERTS_PER_GROUP * gidx
    emask = (lane_f >= elo) & (lane_f < elo + EXPERTS_PER_GROUP)
    el = jnp.where(emask, logits, ninf)
    t1 = jnp.max(el, axis=-1, keepdims=True)
    i1 = jnp.min(jnp.where(el == t1, lane_f, big), axis=-1, keepdims=True)
    el2 = jnp.where(lane_f == i1, ninf, el)
    t2 = jnp.max(el2, axis=-1, keepdims=True)
    i2 = jnp.min(jnp.where(el2 == t2, lane_f, big), axis=-1, keepdims=True)
    e21 = jnp.exp(t2 - t1)
    den = 1.0 + e21
    gate1 = g_w * (1.0 / den)
    gate2 = g_w * (e21 / den)
    e1 = i1 - N_EXPERT_GROUPS
    e2 = i2 - N_EXPERT_GROUPS
    packed = jnp.where(lane == 0, e1, jnp.where(lane == 1, e2, jnp.where(lane == 2, gate1,
                                                                           jnp.where(lane == 3, gate2, 0.0))))
    hist = jnp.sum(((lane_f == e1) | (lane_f == e2)).astype(F32), axis=0, keepdims=True)
    return packed.T[:SUBLANES], hist


def _mixer_prompt_kernel(x_ref, ada_ref, h0_ref, g1_ref, g2_ref, win_ref, lng_ref, lnb_ref, wsp_ref, bsp_ref,
                         avec_ref, wb_ref, wc_ref, dsk_ref, wglu_ref, bglu_ref, wout_ref, wrc_ref, brt_ref,
                         x1_ref, route_ref, hist_ref, state_ref,
                         h_scr, u_scr, vn_scr, s_scr, st_scr, yt_scr, bu_scr, ab_scr):
    nb = x_ref.shape[0]
    half = (nb // 2) * CHUNK
    D = D_MODEL
    pitch = S_PITCH

    @pl.when(pl.program_id(0) == 0)
    def _():
        state_ref[...] = h0_ref[...]
        hist_ref[...] = jnp.zeros_like(hist_ref)

    def mod(b, i):
        return ada_ref[b:b + 1, i * D:(i + 1) * D]

    for b in range(nb):
        hb = _rms(x_ref[b], g1_ref[...]) * (1 + mod(b, 1)) + mod(b, 0)
        h_scr[b * CHUNK:(b + 1) * CHUNK, :] = hb.astype(BF16)

    for r0 in (0, half):
        _front(r0, half, h_scr, win_ref, lng_ref, lnb_ref, u_scr, vn_scr, s_scr, CHUNK, pitch)

    for b in range(nb):
        rows = slice(b * CHUNK, (b + 1) * CHUNK)
        for h in range(N_HEADS):
            cols = slice(h * HEAD_DIM, (h + 1) * HEAD_DIM)
            mixed = jnp.dot(wsp_ref[h], vn_scr[rows, cols].astype(BF16), preferred_element_type=F32) + bsp_ref[h]
            ab_scr[rows, cols] = (u_scr[rows, cols] * mixed).astype(BF16)

    for k in range(N_SLABS):
        for t in range(CHUNK):
            st_scr[k, t * nb:(t + 1) * nb, :] = s_scr[k, pl.ds(t, nb, stride=pitch), :]
    _s5(st_scr, bu_scr, yt_scr, state_ref, avec_ref, wb_ref, wc_ref, nb, CHUNK)
    for k in range(N_SLABS):
        dsk = dsk_ref[:, k * LANES:(k + 1) * LANES]
        for t in range(CHUNK):
            sel = pl.ds(t, nb, stride=pitch)
            s_scr[k, sel, :] = yt_scr[k, t * nb:(t + 1) * nb, :] + dsk * s_scr[k, sel, :]

    for r0 in (0, half):
        mix = _back(r0, half, s_scr, ab_scr, wglu_ref, bglu_ref, wout_ref, CHUNK, pitch)
        for bl in range(nb // 2):
            b = r0 // CHUNK + bl
            x1 = x_ref[b] + mod(b, 2) * mix[bl * CHUNK:(bl + 1) * CHUNK, :]
            x1_ref[b] = x1
            h2 = _rms(x1, g2_ref[...]) * (1 + mod(b, 4)) + mod(b, 3)
            route, hist = _route(h2, wrc_ref, brt_ref)
            route_ref[b, 0] = route
            hist_ref[b:b + 1, :] = hist_ref[b:b + 1, :] + hist


def _const_spec(shape):
    nd = len(shape)
    return pl.BlockSpec(shape, lambda *_: (0,) * nd, pipeline_mode=pl.Buffered(1))


def _mixer_prompt(x, ada, h0, wts):
    nb, seq, D = x.shape
    n_chunks = seq // CHUNK
    R = nb * CHUNK
    weight_specs = [_const_spec(w.shape) for w in wts]
    in_specs = [pl.BlockSpec((nb, CHUNK, D), lambda i: (0, i, 0)),
                _const_spec(ada.shape), _const_spec(h0.shape)] + weight_specs
    out_shape = (jax.ShapeDtypeStruct((nb, seq, D), F32),
                 jax.ShapeDtypeStruct((nb, n_chunks, SUBLANES, ROUTE_LANES), F32),
                 jax.ShapeDtypeStruct((nb, ROUTE_LANES), F32),
                 jax.ShapeDtypeStruct((nb, STATE_COLS), F32))
    out_specs = (pl.BlockSpec((nb, CHUNK, D), lambda i: (0, i, 0)),
                 pl.BlockSpec((nb, 1, SUBLANES, ROUTE_LANES), lambda i: (0, i, 0, 0)),
                 pl.BlockSpec((nb, ROUTE_LANES), lambda i: (0, 0)),
                 pl.BlockSpec((nb, STATE_COLS), lambda i: (0, 0)))
    scratch = [pltpu.VMEM((R, D), BF16),
               pltpu.VMEM((R, D_A), F32),
               pltpu.VMEM((R, D_A), F32),
               pltpu.VMEM((N_SLABS, nb * S_PITCH, LANES), F32),
               pltpu.VMEM((N_SLABS, R, LANES), F32),
               pltpu.VMEM((N_SLABS, R, LANES), F32),
               pltpu.VMEM((2, R, SLAB_COLS), F32),
               pltpu.VMEM((R, D), BF16)]
    return pl.pallas_call(
        _mixer_prompt_kernel,
        grid=(n_chunks,),
        in_specs=in_specs,
        out_specs=out_specs,
        out_shape=out_shape,
        scratch_shapes=scratch,
        compiler_params=pltpu.CompilerParams(dimension_semantics=("arbitrary",), vmem_limit_bytes=VMEM_LIMIT),
        name="mixer_prompt",
    )(x, ada, h0, *wts)


def _mixer_sample_kernel(wsm_ref, bsm_ref, x_ref, ada_ref, h0_ref, g1_ref, g2_ref, win_ref, lng_ref, lnb_ref,
                         avec_ref, wb_ref, wc_ref, dsk_ref, wglu_ref, bglu_ref, wout_ref, wrc_ref, brt_ref,
                         x1_ref, route_ref, hist_ref, state_ref, v_ref,
                         h_scr, u_scr, vn_scr, s_scr, yt_scr, bu_scr, ab_scr):
    T, nb, D = x_ref.shape
    R = T * nb
    half = R // 2

    def mod(i):
        return ada_ref[:, i * D:(i + 1) * D]

    state_ref[...] = h0_ref[...]
    for t in range(T):
        ht = _rms(x_ref[t], g1_ref[...]) * (1 + mod(1)) + mod(0)
        h_scr[t * nb:(t + 1) * nb, :] = ht.astype(BF16)

    for r0 in (0, half):
        _front(r0, half, h_scr, win_ref, lng_ref, lnb_ref, u_scr, vn_scr, s_scr, half, half)

    for t in range(T):
        rows = slice(t * nb, (t + 1) * nb)
        v_ref[t] = vn_scr[rows, :]
        for h in range(N_HEADS):
            cols = slice(h * HEAD_DIM, (h + 1) * HEAD_DIM)
            acc = jnp.full((nb, HEAD_DIM), bsm_ref[h * T + t], F32)
            for s in range(t + 1):
                acc = acc + wsm_ref[(h * T + t) * T + s] * vn_scr[s * nb:(s + 1) * nb, cols]
            ab_scr[rows, cols] = (u_scr[rows, cols] * acc).astype(BF16)

    _s5(s_scr, bu_scr, yt_scr, state_ref, avec_ref, wb_ref, wc_ref, nb, T)
    for k in range(N_SLABS):
        s_scr[k] = yt_scr[k] + dsk_ref[:, k * LANES:(k + 1) * LANES] * s_scr[k]

    hist_total = jnp.zeros((1, ROUTE_LANES), F32)
    for r0 in (0, half):
        mix = _back(r0, half, s_scr, ab_scr, wglu_ref, bglu_ref, wout_ref, half, half)
        for tl in range(T // 2):
            t = r0 // nb + tl
            x1 = x_ref[t] + mod(2) * mix[tl * nb:(tl + 1) * nb, :]
            x1_ref[t] = x1
            h2 = _rms(x1, g2_ref[...]) * (1 + mod(4)) + mod(3)
            route, hist = _route(h2, wrc_ref, brt_ref)
            route_ref[t] = route
            hist_total = hist_total + hist
    hist_ref[...] = hist_total


def _mixer_sample(x_t, ada, h0, w_small, b_small, wts):
    T, nb, D = x_t.shape
    R = T * nb
    smem = pl.BlockSpec(memory_space=pltpu.SMEM)
    out_shape = (jax.ShapeDtypeStruct((T, nb, D), F32),
                 jax.ShapeDtypeStruct((T, SUBLANES, ROUTE_LANES), F32),
                 jax.ShapeDtypeStruct((1, ROUTE_LANES), F32),
                 jax.ShapeDtypeStruct((nb, STATE_COLS), F32),
                 jax.ShapeDtypeStruct((T, nb, D_A), F32))
    scratch = [pltpu.VMEM((R, D), BF16),
               pltpu.VMEM((R, D_A), F32),
               pltpu.VMEM((R, D_A), F32),
               pltpu.VMEM((N_SLABS, R, LANES), F32),
               pltpu.VMEM((N_SLABS, R, LANES), F32),
               pltpu.VMEM((2, R, SLAB_COLS), F32),
               pltpu.VMEM((R, D), BF16)]
    vmem = pl.BlockSpec(memory_space=pltpu.VMEM)
    return pl.pallas_call(
        _mixer_sample_kernel,
        in_specs=[smem, smem] + [vmem] * (3 + len(wts)),
        out_specs=(vmem,) * 5,
        out_shape=out_shape,
        scratch_shapes=scratch,
        compiler_params=pltpu.CompilerParams(vmem_limit_bytes=VMEM_LIMIT),
        name="mixer_sample",
    )(w_small, b_small, x_t, ada, h0, *wts)


def _rows_to_tiles(tiles_ref, row0, val):
    n = val.shape[0]
    for c in range(val.shape[1] // LANES):
        tiles_ref[pl.ds(row0 * SUBLANES + c, n, stride=SUBLANES), :] = val[:, c * LANES:(c + 1) * LANES]


def _tiles_to_rows(tiles_ref, row0, n):
    return jnp.concatenate([tiles_ref[pl.ds(row0 * SUBLANES + c, n, stride=SUBLANES), :] for c in range(SUBLANES)],
                           axis=-1)


def _moe_kernel(cp_ref, tab_ref, tok_ref, pos_ref, gate_ref, x1_ref, ada_ref, g2_ref, gfin_ref, w1_ref, w3_ref, w2_ref,
                out_ref, h_tiles, y_tiles, xb_even, xb_odd, *, chunk):
    t = pl.program_id(0)
    e = pl.program_id(1)
    ngrp, rows_g, D = x1_ref.shape
    n_tok = ngrp * rows_g
    mrows = ada_ref.shape[1]

    def mod(i, ci):
        if mrows == 1:
            return ada_ref[0, :, i * D:(i + 1) * D]
        return ada_ref[0, ci * chunk:(ci + 1) * chunk, i * D:(i + 1) * D]

    def tile_of(row):
        return pl.ds(pl.multiple_of(row * SUBLANES, SUBLANES), SUBLANES)

    def tile_at(row8):
        return pl.ds(pl.multiple_of(row8, SUBLANES), SUBLANES)

    @pl.when(e == 0)
    def _():
        for gi in range(ngrp):
            for ci in range(rows_g // chunk):
                x1 = x1_ref[gi, ci * chunk:(ci + 1) * chunk, :]
                h2 = _rms(x1, g2_ref[...]) * (1 + mod(4, ci)) + mod(3, ci)
                _rows_to_tiles(h_tiles, gi * rows_g + ci * chunk, h2)

    def pair_entry(gp):
        return [tab_ref[t, PAIR_FIELDS * gp + i] for i in range(PAIR_FIELDS)]

    @pl.when(e == 0)
    def _():
        base_a, base_b, _, _ = pair_entry(0)

        def gather8(j8, c):
            for jj in range(SUBLANES):
                j = j8 * SUBLANES + jj
                xb_even[tile_of(j), :] = h_tiles[tile_at(tok_ref[base_a + j]), :]
                xb_even[tile_of(MOE_BLOCK + j), :] = h_tiles[tile_at(tok_ref[base_b + j]), :]
            return c

        lax.fori_loop(0, MOE_BLOCK // SUBLANES, gather8, 0)

    def pair_body(gp, cur_tiles, next_tiles):
        next_a, next_b, _, _ = pair_entry(gp + 1)
        n_slices = 8
        per = 2 * MOE_BLOCK // n_slices
        slices = iter(range(n_slices))

        def gather_slice():
            s = next(slices)
            for r in range(s * per, (s + 1) * per):
                tok8 = tok_ref[next_a + r] if r < MOE_BLOCK else tok_ref[next_b + r - MOE_BLOCK]
                next_tiles[r * SUBLANES:(r + 1) * SUBLANES, :] = h_tiles[tile_at(tok8), :]

        base_a, base_b, el_a, el_b = pair_entry(gp)
        xa = _tiles_to_rows(cur_tiles, 0, MOE_BLOCK).astype(BF16)
        gather_slice()
        a1 = jnp.dot(xa, w1_ref[el_a], preferred_element_type=F32)
        gather_slice()
        a3 = jnp.dot(xa, w3_ref[el_a], preferred_element_type=F32)
        gather_slice()
        xb = _tiles_to_rows(cur_tiles, MOE_BLOCK, MOE_BLOCK).astype(BF16)
        b1 = jnp.dot(xb, w1_ref[el_b], preferred_element_type=F32)
        gather_slice()
        b3 = jnp.dot(xb, w3_ref[el_b], preferred_element_type=F32)
        gather_slice()
        ya = jnp.dot((jax.nn.silu(a1) * a3).astype(BF16), w2_ref[el_a], preferred_element_type=F32)
        gather_slice()
        yb = jnp.dot((jax.nn.silu(b1) * b3).astype(BF16), w2_ref[el_b], preferred_element_type=F32)
        gather_slice()
        _rows_to_tiles(y_tiles, base_a, ya)
        gather_slice()
        _rows_to_tiles(y_tiles, base_b, yb)

    def pair(gp, carry):
        @pl.when((gp & 1) == 0)
        def _():
            pair_body(gp, xb_even, xb_odd)

        @pl.when((gp & 1) == 1)
        def _():
            pair_body(gp, xb_odd, xb_even)

        return carry

    lax.fori_loop(cp_ref[t, e], cp_ref[t, e + 1], pair, 0)

    @pl.when(e == N_EXPERTS // EXPERTS_PER_STEP - 1)
    def _():
        def combine8(t8, c):
            for tt in range(SUBLANES):
                tok = t8 * SUBLANES + tt
                y0 = y_tiles[tile_at(pos_ref[tok]), :]
                y1 = y_tiles[tile_at(pos_ref[n_tok + tok]), :]
                h_tiles[tile_of(tok), :] = gate_ref[tok] * y0 + gate_ref[n_tok + tok] * y1
            return c

        lax.fori_loop(0, n_tok // SUBLANES, combine8, 0)
        for gi in range(ngrp):
            for ci in range(rows_g // chunk):
                moe = _tiles_to_rows(h_tiles, gi * rows_g + ci * chunk, chunk)
                x2 = x1_ref[gi, ci * chunk:(ci + 1) * chunk, :] + mod(5, ci) * moe
                out_ref[gi, ci * chunk:(ci + 1) * chunk, :] = _rms(x2, gfin_ref[...])


def _moe(x1, ada, cp, tab, tok, pos, gate, g2, gfin, w1, w3, w2, *, n_tiles, chunk):
    ngrp = x1.shape[0] // n_tiles
    rows_g, D = x1.shape[1], x1.shape[2]
    mrows = ada.shape[1]
    Tt = ngrp * rows_g
    lp = tok.shape[0] // n_tiles
    single = pl.Buffered(1)
    grid_spec = pltpu.PrefetchScalarGridSpec(
        num_scalar_prefetch=2,
        grid=(n_tiles, N_EXPERTS // EXPERTS_PER_STEP),
        in_specs=[
            pl.BlockSpec((lp,), lambda t, e, cp, tab: (t,), memory_space=pltpu.SMEM),
            pl.BlockSpec((2 * Tt,), lambda t, e, cp, tab: (t,), memory_space=pltpu.SMEM),
            pl.BlockSpec((2 * Tt,), lambda t, e, cp, tab: (t,), memory_space=pltpu.SMEM),
            pl.BlockSpec((ngrp, rows_g, D), lambda t, e, cp, tab: (t, 0, 0), pipeline_mode=single),
            pl.BlockSpec((1, mrows, 6 * D), lambda t, e, cp, tab: (t, 0, 0)),
            pl.BlockSpec((1, D), lambda t, e, cp, tab: (0, 0)),
            pl.BlockSpec((1, D), lambda t, e, cp, tab: (0, 0)),
            pl.BlockSpec((EXPERTS_PER_STEP, D, D_EXPERT), lambda t, e, cp, tab: (e, 0, 0)),
            pl.BlockSpec((EXPERTS_PER_STEP, D, D_EXPERT), lambda t, e, cp, tab: (e, 0, 0)),
            pl.BlockSpec((EXPERTS_PER_STEP, D_EXPERT, D), lambda t, e, cp, tab: (e, 0, 0)),
        ],
        out_specs=pl.BlockSpec((ngrp, rows_g, D), lambda t, e, cp, tab: (t, 0, 0), pipeline_mode=single),
        scratch_shapes=[pltpu.VMEM((Tt * SUBLANES, LANES), F32),
                        pltpu.VMEM(((2 * Tt + MOE_BLOCK) * SUBLANES, LANES), F32),
                        pltpu.VMEM((2 * MOE_BLOCK * SUBLANES, LANES), F32),
                        pltpu.VMEM((2 * MOE_BLOCK * SUBLANES, LANES), F32)],
    )
    return pl.pallas_call(
        functools.partial(_moe_kernel, chunk=chunk),
        grid_spec=grid_spec,
        out_shape=jax.ShapeDtypeStruct(x1.shape, F32),
        compiler_params=pltpu.CompilerParams(dimension_semantics=("arbitrary", "arbitrary"),
                                             vmem_limit_bytes=VMEM_LIMIT),
        name="moe",
    )(cp, tab, tok, pos, gate, x1, ada, g2, gfin, w1, w3, w2)


def _dispatch_tables(routes):
    tile_tokens = [r.shape[1] * ROUTE_LANES for r, _ in routes]
    t_max = max(tile_tokens)
    es, gs, masks = [], [], []
    for (route, _), Tt in zip(routes, tile_tokens):
        assert Tt & (Tt - 1) == 0
        n = route.shape[0]
        by_k = route[:, :, :4, :].transpose(0, 2, 1, 3).reshape(n, 4, Tt)
        extra = 2 * (t_max - Tt)
        es.append(jnp.pad(by_k[:, :2].astype(jnp.int32).reshape(n, 2 * Tt), ((0, 0), (0, extra)),
                          constant_values=N_EXPERTS))
        gs.append(jnp.pad(by_k[:, 2:].reshape(n, 2 * Tt), ((0, 0), (0, extra))))
        masks.append(jnp.full((n, 1), Tt - 1, jnp.int32))
    flat_e = jnp.concatenate(es, axis=0)
    flat_g = jnp.concatenate(gs, axis=0)
    n_tiles = flat_e.shape[0]
    order = jnp.argsort(flat_e, axis=-1, stable=True).astype(jnp.int32)
    pos = jnp.argsort(order, axis=-1).astype(jnp.int32) * SUBLANES
    tok_s = (order & jnp.concatenate(masks, axis=0)) * SUBLANES
    counts = jnp.concatenate([h[:, :N_EXPERTS] for _, h in routes], axis=0).astype(jnp.int32)
    starts = jnp.concatenate([jnp.zeros((n_tiles, 1), jnp.int32), jnp.cumsum(counts, axis=-1, dtype=jnp.int32)],
                             axis=-1)

    def tok_len(Tt):
        return -(-(2 * Tt + MOE_BLOCK) // SMEM_PAD) * SMEM_PAD

    tok_p = jnp.pad(tok_s, ((0, 0), (0, tok_len(t_max) - 2 * t_max)))
    Tt = t_max


    n_steps = N_EXPERTS // EXPERTS_PER_STEP
    nblk = ((counts + MOE_BLOCK - 1) // MOE_BLOCK).reshape(n_tiles, n_steps, EXPERTS_PER_STEP)
    first_blk = jnp.cumsum(nblk, axis=-1) - nblk
    nb_step = jnp.sum(nblk, axis=-1)
    cp = jnp.concatenate([jnp.zeros((n_tiles, 1), jnp.int32),
                          jnp.cumsum((nb_step + 1) // 2, axis=-1, dtype=jnp.int32)], axis=-1)
    max_pairs = (2 * Tt // MOE_BLOCK + N_EXPERTS + n_steps) // 2 + 1
    g = jnp.arange(max_pairs, dtype=jnp.int32)[None, :]
    step = jnp.minimum(jnp.sum((g[:, :, None] >= cp[:, None, 1:]).astype(jnp.int32), axis=-1), n_steps - 1)
    valid = g < cp[:, -1:]
    p_local = g - jnp.take_along_axis(cp, step, axis=1)
    nb = jnp.take_along_axis(nb_step, step, axis=1)
    first_blk_g = jnp.take_along_axis(first_blk, step[:, :, None], axis=1)

    def block_info(f):
        el = jnp.sum((f[:, :, None] >= first_blk_g[:, :, 1:]).astype(jnp.int32), axis=-1)
        i = f - jnp.take_along_axis(first_blk_g, el[:, :, None], axis=2)[:, :, 0]
        base = jnp.take_along_axis(starts, step * EXPERTS_PER_STEP + el, axis=1) + i * MOE_BLOCK
        return jnp.where(valid, base, 0), jnp.where(valid, el, 0)

    base_a, el_a = block_info(2 * p_local)
    base_b, el_b = block_info(jnp.minimum(2 * p_local + 1, nb - 1))
    tab = jnp.stack([base_a, base_b, el_a, el_b], axis=-1).reshape(n_tiles, max_pairs * PAIR_FIELDS)
    tab = tab.astype(jnp.int32)
    out, r0 = [], 0
    for (route, _), Tt in zip(routes, tile_tokens):
        r1 = r0 + route.shape[0]
        out.append((cp[r0:r1], tab[r0:r1], tok_p[r0:r1, :tok_len(Tt)].reshape(-1),
                    pos[r0:r1, :2 * Tt].reshape(-1), flat_g[r0:r1, :2 * Tt].reshape(-1)))
        r0 = r1
    return out


def _pack_state(h_re, h_im):
    b = h_re.shape[0]
    return jnp.concatenate([h_re.reshape(b, STATE_COLS // 2), h_im.reshape(b, STATE_COLS // 2)], axis=-1)


def _unpack_state(st):
    b = st.shape[0]
    return (st[:, :STATE_COLS // 2].reshape(b, N_SSM_GROUPS, SSM_STATE),
            st[:, STATE_COLS // 2:].reshape(b, N_SSM_GROUPS, SSM_STATE))


def kernel(x_prompt, x_sample, state_ssm_re, state_ssm_im, c_prompt, c_sample, w_ada, b_ada, g_norm1, g_norm2, w_in, ln_g, ln_b, w_s, b_s, lam_re, lam_im, log_dt, ssm_b_re, ssm_b_im, ssm_c_re, ssm_c_im, ssm_d, w_glu, b_glu, w_out, w_group, b_group, w_expert, b_expert, w1, w3, w2, g_final):
    depth = w_ada.shape[0]
    assert depth == 1, "the final RMSNorm is fused into the (single) layer's MoE epilogue"
    B, L, D = x_prompt.shape
    Bs, Ls, _ = x_sample.shape
    xp = x_prompt
    xs_t = x_sample.transpose(1, 0, 2)
    eye = jnp.eye(SLAB_GROUPS, dtype=F32)
    tril = jnp.tril(jnp.ones((CHUNK, CHUNK), dtype=bool))
    p_re, p_im, s_re, s_im, s_v = [], [], [], [], []
    for l in range(depth):
        ada = _ada(jnp.concatenate([c_prompt, c_sample], axis=0), w_ada[l].astype(BF16), b_ada[l][None])
        ada_p, ada_s = ada[:B], ada[B:]

        ar, ai, br, bi = _discretize(lam_re[l], lam_im[l], log_dt[l], ssm_b_re[l], ssm_b_im[l])
        avec = jnp.concatenate([ar.reshape(1, STATE_COLS // 2), ai.reshape(1, STATE_COLS // 2)], axis=-1)

        def blockdiag_in(w):
            w4 = w.reshape(N_SLABS, SLAB_GROUPS, SSM_GROUP, SSM_STATE)
            return jnp.einsum('kghp,gG->kghGp', w4, eye).reshape(N_SLABS, LANES, SLAB_STATES)

        def blockdiag_out(w):
            w4 = w.reshape(N_SLABS, SLAB_GROUPS, SSM_GROUP, SSM_STATE)
            return jnp.einsum('kghp,gG->kgpGh', w4, eye).reshape(N_SLABS, SLAB_STATES, LANES)

        wb = jnp.concatenate([blockdiag_in(br), blockdiag_in(bi)], axis=-1).astype(BF16)
        wc = jnp.concatenate([blockdiag_out(ssm_c_re[l]), -blockdiag_out(ssm_c_im[l])], axis=1).astype(BF16)

        wr = jnp.zeros((D, ROUTE_LANES), F32)
        wr = wr.at[:, :N_EXPERT_GROUPS].set(w_group[l]).at[:, N_EXPERT_GROUPS:N_EXPERT_GROUPS + N_EXPERTS].set(w_expert[l])
        wr_hi = wr.astype(BF16)
        wr_lo = (wr - wr_hi.astype(F32)).astype(BF16)
        br_t = jnp.zeros((1, ROUTE_LANES), F32)
        br_t = br_t.at[0, :N_EXPERT_GROUPS].set(b_group[l]).at[0, N_EXPERT_GROUPS:N_EXPERT_GROUPS + N_EXPERTS].set(b_expert[l])

        g1 = g_norm1[l][None]
        g2 = g_norm2[l][None]
        shared = dict(
            win=w_in[l].astype(BF16), lng=ln_g[l].reshape(1, D_A), lnb=ln_b[l].reshape(1, D_A),
            avec=avec, wb=wb, wc=wc, dsk=ssm_d[l].reshape(1, D_B), wglu=w_glu[l].astype(BF16),
            bglu=b_glu[l][None], wout=w_out[l].astype(BF16), wrc=jnp.concatenate([wr_hi, wr_lo], axis=1), brt=br_t)
        w_masked = jnp.where(tril[None], w_s[l], jnp.zeros_like(w_s[l]))
        wsp = w_masked.astype(BF16)
        bsp = jnp.broadcast_to(b_s[l][:, :, None], (N_HEADS, CHUNK, HEAD_DIM))
        wts_p = (g1, g2, shared['win'], shared['lng'], shared['lnb'], wsp, bsp, shared['avec'], shared['wb'],
                 shared['wc'], shared['dsk'], shared['wglu'], shared['bglu'], shared['wout'], shared['wrc'],
                 shared['brt'])
        wts_s = (g1, g2, shared['win'], shared['lng'], shared['lnb'], shared['avec'], shared['wb'],
                 shared['wc'], shared['dsk'], shared['wglu'], shared['bglu'], shared['wout'], shared['wrc'],
                 shared['brt'])

        w1b, w3b, w2b = w1[l].astype(BF16), w3[l].astype(BF16), w2[l].astype(BF16)
        gfin = g_final[None]

        h0p = jnp.zeros((B, STATE_COLS), F32)
        x1p, route_p, hist_p, st_p = _mixer_prompt(xp, ada_p, h0p, wts_p)
        h0s = _pack_state(state_ssm_re[l].astype(F32), state_ssm_im[l].astype(F32))
        w_small = w_masked[:, :Ls, :Ls].reshape(-1)
        b_small = b_s[l][:, :Ls].reshape(-1)
        x1s, route_s, hist_s, st_s, v_s = _mixer_sample(xs_t, ada_s, h0s, w_small, b_small, wts_s)

        tables_p, tables_s = _dispatch_tables([(route_p, hist_p), (route_s[None], hist_s)])
        xp = _moe(x1p, ada_p[:, None, :], *tables_p, g2, gfin, w1b, w3b, w2b, n_tiles=B, chunk=256)
        xs_t = _moe(x1s, ada_s[None], *tables_s, g2, gfin, w1b, w3b, w2b, n_tiles=1, chunk=Bs)
        hr, hi = _unpack_state(st_p)
        p_re.append(hr.astype(state_ssm_re.dtype))
        p_im.append(hi.astype(state_ssm_im.dtype))
        hr, hi = _unpack_state(st_s)
        s_re.append(hr.astype(state_ssm_re.dtype))
        s_im.append(hi.astype(state_ssm_im.dtype))
        s_v.append(v_s.transpose(1, 0, 2))

    y_prompt = xp
    y_sample = xs_t.transpose(1, 0, 2)
    return (y_prompt, y_sample, jnp.stack(p_re), jnp.stack(p_im), jnp.stack(s_re), jnp.stack(s_im), jnp.stack(s_v))
```

```python
import functools

import jax
import jax.numpy as jnp
from jax import lax
from jax.experimental import pallas as pl
from jax.experimental.pallas import tpu as pltpu

F32 = jnp.float32
BF16 = jnp.bfloat16

D_MODEL = 1024
D_A = 512
D_B = 512
N_HEADS = 4
HEAD_DIM = 128
CHUNK = 128
N_SSM_GROUPS = 32
SSM_GROUP = 16
SSM_STATE = 64
N_SLABS = 4
SLAB_GROUPS = N_SSM_GROUPS // N_SLABS
SLAB_STATES = SLAB_GROUPS * SSM_STATE
SLAB_COLS = 2 * SLAB_STATES
STATE_COLS = N_SLABS * SLAB_COLS
N_EXPERT_GROUPS = 4
EXPERTS_PER_GROUP = 8
N_EXPERTS = 32
D_EXPERT = 256
EPS = 1e-6

LANES = 128
SUBLANES = 8
ROUTE_LANES = LANES
MOE_BLOCK = 128
MOE_BLOCK_SHIFT = MOE_BLOCK.bit_length() - 1
assert 1 << MOE_BLOCK_SHIFT == MOE_BLOCK
EXPERTS_PER_STEP = 4
PAIR_FIELDS = 4
S_PITCH = CHUNK + SUBLANES
SMEM_PAD = 1024
VMEM_LIMIT = 58 * 1024 * 1024


def _rms(xf, g):
    ms = jnp.mean(xf * xf, axis=-1, keepdims=True)
    return xf * lax.rsqrt(ms + EPS) * g


def _ada_kernel(c_ref, w_ref, b_ref, o_ref):
    s = jax.nn.silu(c_ref[...]).astype(BF16)
    o_ref[...] = jnp.dot(s, w_ref[...], preferred_element_type=F32) + b_ref[...]


def _ada(c_all, w_bf, b):
    m = c_all.shape[0]
    n = w_bf.shape[1]
    bn = 1024
    return pl.pallas_call(
        _ada_kernel,
        grid=(n // bn,),
        in_specs=[pl.BlockSpec((m, D_MODEL), lambda j: (0, 0)),
                  pl.BlockSpec((D_MODEL, bn), lambda j: (0, j)),
                  pl.BlockSpec((1, bn), lambda j: (0, j))],
        out_specs=pl.BlockSpec((m, bn), lambda j: (0, j)),
        out_shape=jax.ShapeDtypeStruct((m, n), F32),
        name="ada",
    )(c_all, w_bf, b)


def _disc_kernel(lre_ref, lim_ref, ldt_ref, bre_ref, bim_ref, ar_ref, ai_ref, br_ref, bi_ref):
    dt = jnp.exp(ldt_ref[...])
    lr = lre_ref[...]
    li = lim_ref[...]
    mag = jnp.exp(lr * dt)
    ar = mag * jnp.cos(li * dt)
    ai = mag * jnp.sin(li * dt)
    den = lr * lr + li * li
    cr = ((ar - 1) * lr + ai * li) / den
    ci = (ai * lr - (ar - 1) * li) / den
    ar_ref[...] = ar
    ai_ref[...] = ai
    bre = bre_ref[...]
    bim = bim_ref[...]
    br_ref[...] = cr * bre - ci * bim
    bi_ref[...] = cr * bim + ci * bre


def _discretize(lam_re, lam_im, log_dt, b_re, b_im):
    g, p, h = b_re.shape
    o1 = jax.ShapeDtypeStruct((g, 1, p), F32)
    o2 = jax.ShapeDtypeStruct((g, h, p), F32)
    return pl.pallas_call(_disc_kernel, out_shape=(o1, o1, o2, o2), name="ssm_disc")(
        lam_re.reshape(g, 1, p), lam_im.reshape(g, 1, p), log_dt.reshape(g, 1, 1),
        b_re.transpose(0, 2, 1), b_im.transpose(0, 2, 1))


def _s_pieces(r0, nrows, grp, pitch):
    return [(i * grp, slice((r0 // grp + i) * pitch, (r0 // grp + i) * pitch + grp)) for i in range(nrows // grp)]


def _front(r0, nrows, h_scr, win_ref, lng_ref, lnb_ref, u_scr, vn_scr, s_scr, grp, pitch):
    rows = slice(r0, r0 + nrows)
    proj = jnp.dot(h_scr[rows, :], win_ref[...], preferred_element_type=F32)
    u_scr[rows, :] = jax.nn.gelu(proj[:, :D_A])
    vraw = jax.nn.gelu(proj[:, D_A:2 * D_A])
    for h in range(N_HEADS):
        cols = slice(h * HEAD_DIM, (h + 1) * HEAD_DIM)
        vh = vraw[:, cols]
        mu = jnp.mean(vh, axis=-1, keepdims=True)
        dv = vh - mu
        var = jnp.mean(dv * dv, axis=-1, keepdims=True)
        vn_scr[rows, cols] = dv * lax.rsqrt(var + EPS) * lng_ref[:, cols] + lnb_ref[:, cols]
    for k in range(N_SLABS):
        for off, prow in _s_pieces(r0, nrows, grp, pitch):
            s_scr[k, prow, :] = proj[off:off + grp, 2 * D_A + k * LANES:2 * D_A + (k + 1) * LANES]


def _scan_slab(bu, state, avec_ref, k, rows_per_step, steps):
    c_re = slice(0, SLAB_STATES)
    c_im = slice(SLAB_STATES, SLAB_COLS)
    s_re = slice(k * SLAB_STATES, (k + 1) * SLAB_STATES)
    s_im = slice(STATE_COLS // 2 + k * SLAB_STATES, STATE_COLS // 2 + (k + 1) * SLAB_STATES)
    ar = jnp.broadcast_to(avec_ref[:, s_re], (SUBLANES, SLAB_STATES))
    ai = jnp.broadcast_to(avec_ref[:, s_im], (SUBLANES, SLAB_STATES))
    for rc in range(rows_per_step // SUBLANES):
        r0 = rc * SUBLANES
        sr = state[r0:r0 + SUBLANES, s_re]
        si = state[r0:r0 + SUBLANES, s_im]
        for t in range(steps):
            rows = slice(t * rows_per_step + r0, t * rows_per_step + r0 + SUBLANES)
            nr = ar * sr - ai * si + bu[rows, c_re]
            ni = ar * si + ai * sr + bu[rows, c_im]
            bu[rows, c_re] = nr
            bu[rows, c_im] = ni
            sr, si = nr, ni
        state[r0:r0 + SUBLANES, s_re] = sr
        state[r0:r0 + SUBLANES, s_im] = si


def _s5(st_ref, bu_scr, yt_ref, state, avec_ref, wb_ref, wc_ref, rows_per_step, steps):
    for k in range(N_SLABS):
        bu = bu_scr.at[k % 2]
        bu[...] = jnp.dot(st_ref[k].astype(BF16), wb_ref[k], preferred_element_type=F32)
        _scan_slab(bu, state, avec_ref, k, rows_per_step, steps)
        yt_ref[k] = jnp.dot(bu[...].astype(BF16), wc_ref[k], preferred_element_type=F32)


def _back(r0, nrows, s_scr, ab_scr, wglu_ref, bglu_ref, wout_ref, grp, pitch):
    rows = slice(r0, r0 + nrows)
    pieces = _s_pieces(r0, nrows, grp, pitch)
    y = jax.nn.gelu(jnp.concatenate(
        [jnp.concatenate([s_scr[k, prow, :] for _, prow in pieces], axis=0) for k in range(N_SLABS)], axis=-1))
    gate = jnp.dot(y.astype(BF16), wglu_ref[...], preferred_element_type=F32) + bglu_ref[...]
    ab_scr[rows, D_A:] = (y * jax.nn.sigmoid(gate)).astype(BF16)
    return jnp.dot(ab_scr[rows, :], wout_ref[...], preferred_element_type=F32)


def _route(h2, wrc_ref, brt_ref):
    hi = h2.astype(BF16)
    lo = (h2 - hi.astype(F32)).astype(BF16)
    both = jnp.dot(hi, wrc_ref[...], preferred_element_type=F32)
    logits = (both[:, :ROUTE_LANES] + both[:, ROUTE_LANES:]
              + jnp.dot(lo, wrc_ref[:, :ROUTE_LANES], preferred_element_type=F32)) + brt_ref[...]
    n = logits.shape[0]
    lane = lax.broadcasted_iota(jnp.int32, (n, ROUTE_LANES), 1)
    lane_f = lane.astype(F32)
    big = jnp.float32(1e9)
    ninf = jnp.float32(-jnp.inf)
    is_g = lane < N_EXPERT_GROUPS
    gl = jnp.where(is_g, logits, ninf)
    gmax = jnp.max(gl, axis=-1, keepdims=True)
    gidx = jnp.min(jnp.where(gl == gmax, lane_f, big), axis=-1, keepdims=True)
    gsum = jnp.sum(jnp.where(is_g, jnp.exp(logits - gmax), 0.0), axis=-1, keepdims=True)
    g_w = 1.0 / gsum
    elo = N_EXPERT_GROUPS + EXPERTS_PER_GROUP * gidx
    emask = (lane_f >= elo) & (lane_f < elo + EXPERTS_PER_GROUP)
    el = jnp.where(emask, logits, ninf)
    t1 = jnp.max(el, axis=-1, keepdims=True)
    i1 = jnp.min(jnp.where(el == t1, lane_f, big), axis=-1, keepdims=True)
    el2 = jnp.where(lane_f == i1, ninf, el)
    t2 = jnp.max(el2, axis=-1, keepdims=True)
    i2 = jnp.min(jnp.where(el2 == t2, lane_f, big), axis=-1, keepdims=True)
    e21 = jnp.exp(t2 - t1)
    den = 1.0 + e21
    gate1 = g_w * (1.0 / den)
    gate2 = g_w * (e21 / den)
    e1 = i1 - N_EXPERT_GROUPS
    e2 = i2 - N_EXPERT_GROUPS
    packed = jnp.where(lane == 0, e1, jnp.where(lane == 1, e2, jnp.where(lane == 2, gate1,
                                                                           jnp.where(lane == 3, gate2, 0.0))))
    hist = jnp.sum(((lane_f == e1) | (lane_f == e2)).astype(F32), axis=0, keepdims=True)
    return packed.T[:SUBLANES], hist


def _mixer_prompt_kernel(x_ref, ada_ref, h0_ref, g1_ref, g2_ref, win_ref, lng_ref, lnb_ref, wsp_ref, bsp_ref,
                         avec_ref, wb_ref, wc_ref, dsk_ref, wglu_ref, bglu_ref, wout_ref, wrc_ref, brt_ref,
                         x1_ref, route_ref, hist_ref, state_ref,
                         h_scr, u_scr, vn_scr, s_scr, st_scr, yt_scr, bu_scr, ab_scr):
    nb = x_ref.shape[0]
    half = (nb // 2) * CHUNK
    D = D_MODEL
    pitch = S_PITCH

    @pl.when(pl.program_id(0) == 0)
    def _():
        state_ref[...] = h0_ref[...]
        hist_ref[...] = jnp.zeros_like(hist_ref)

    def mod(b, i):
        return ada_ref[b:b + 1, i * D:(i + 1) * D]

    for b in range(nb):
        hb = _rms(x_ref[b], g1_ref[...]) * (1 + mod(b, 1)) + mod(b, 0)
        h_scr[b * CHUNK:(b + 1) * CHUNK, :] = hb.astype(BF16)

    for r0 in (0, half):
        _front(r0, half, h_scr, win_ref, lng_ref, lnb_ref, u_scr, vn_scr, s_scr, CHUNK, pitch)

    for b in range(nb):
        rows = slice(b * CHUNK, (b + 1) * CHUNK)
        for h in range(N_HEADS):
            cols = slice(h * HEAD_DIM, (h + 1) * HEAD_DIM)
            mixed = jnp.dot(wsp_ref[h], vn_scr[rows, cols].astype(BF16), preferred_element_type=F32) + bsp_ref[h]
            ab_scr[rows, cols] = (u_scr[rows, cols] * mixed).astype(BF16)

    for k in range(N_SLABS):
        for t in range(CHUNK):
            st_scr[k, t * nb:(t + 1) * nb, :] = s_scr[k, pl.ds(t, nb, stride=pitch), :]
    _s5(st_scr, bu_scr, yt_scr, state_ref, avec_ref, wb_ref, wc_ref, nb, CHUNK)
    for k in range(N_SLABS):
        dsk = dsk_ref[:, k * LANES:(k + 1) * LANES]
        for t in range(CHUNK):
            sel = pl.ds(t, nb, stride=pitch)
            s_scr[k, sel, :] = yt_scr[k, t * nb:(t + 1) * nb, :] + dsk * s_scr[k, sel, :]

    for r0 in (0, half):
        mix = _back(r0, half, s_scr, ab_scr, wglu_ref, bglu_ref, wout_ref, CHUNK, pitch)
        for bl in range(nb // 2):
            b = r0 // CHUNK + bl
            x1 = x_ref[b] + mod(b, 2) * mix[bl * CHUNK:(bl + 1) * CHUNK, :]
            x1_ref[b] = x1
            h2 = _rms(x1, g2_ref[...]) * (1 + mod(b, 4)) + mod(b, 3)
            route, hist = _route(h2, wrc_ref, brt_ref)
            route_ref[b, 0] = route
            hist_ref[b:b + 1, :] = hist_ref[b:b + 1, :] + hist


def _const_spec(shape):
    nd = len(shape)
    return pl.BlockSpec(shape, lambda *_: (0,) * nd, pipeline_mode=pl.Buffered(1))


def _mixer_prompt(x, ada, h0, wts):
    nb, seq, D = x.shape
    n_chunks = seq // CHUNK
    R = nb * CHUNK
    weight_specs = [_const_spec(w.shape) for w in wts]
    in_specs = [pl.BlockSpec((nb, CHUNK, D), lambda i: (0, i, 0)),
                _const_spec(ada.shape), _const_spec(h0.shape)] + weight_specs
    out_shape = (jax.ShapeDtypeStruct((nb, seq, D), F32),
                 jax.ShapeDtypeStruct((nb, n_chunks, SUBLANES, ROUTE_LANES), F32),
                 jax.ShapeDtypeStruct((nb, ROUTE_LANES), F32),
                 jax.ShapeDtypeStruct((nb, STATE_COLS), F32))
    out_specs = (pl.BlockSpec((nb, CHUNK, D), lambda i: (0, i, 0)),
                 pl.BlockSpec((nb, 1, SUBLANES, ROUTE_LANES), lambda i: (0, i, 0, 0)),
                 pl.BlockSpec((nb, ROUTE_LANES), lambda i: (0, 0)),
                 pl.BlockSpec((nb, STATE_COLS), lambda i: (0, 0)))
    scratch = [pltpu.VMEM((R, D), BF16),
               pltpu.VMEM((R, D_A), F32),
               pltpu.VMEM((R, D_A), F32),
               pltpu.VMEM((N_SLABS, nb * S_PITCH, LANES), F32),
               pltpu.VMEM((N_SLABS, R, LANES), F32),
               pltpu.VMEM((N_SLABS, R, LANES), F32),
               pltpu.VMEM((2, R, SLAB_COLS), F32),
               pltpu.VMEM((R, D), BF16)]
    return pl.pallas_call(
        _mixer_prompt_kernel,
        grid=(n_chunks,),
        in_specs=in_specs,
        out_specs=out_specs,
        out_shape=out_shape,
        scratch_shapes=scratch,
        compiler_params=pltpu.CompilerParams(dimension_semantics=("arbitrary",), vmem_limit_bytes=VMEM_LIMIT),
        name="mixer_prompt",
    )(x, ada, h0, *wts)


def _mixer_sample_kernel(wsm_ref, bsm_ref, x_ref, ada_ref, h0_ref, g1_ref, g2_ref, win_ref, lng_ref, lnb_ref,
                         avec_ref, wb_ref, wc_ref, dsk_ref, wglu_ref, bglu_ref, wout_ref, wrc_ref, brt_ref,
                         x1_ref, route_ref, hist_ref, state_ref, v_ref,
                         h_scr, u_scr, vn_scr, s_scr, yt_scr, bu_scr, ab_scr):
    T, nb, D = x_ref.shape
    R = T * nb
    half = R // 2

    def mod(i):
        return ada_ref[:, i * D:(i + 1) * D]

    state_ref[...] = h0_ref[...]
    for t in range(T):
        ht = _rms(x_ref[t], g1_ref[...]) * (1 + mod(1)) + mod(0)
        h_scr[t * nb:(t + 1) * nb, :] = ht.astype(BF16)

    for r0 in (0, half):
        _front(r0, half, h_scr, win_ref, lng_ref, lnb_ref, u_scr, vn_scr, s_scr, half, half)

    for t in range(T):
        rows = slice(t * nb, (t + 1) * nb)
        v_ref[t] = vn_scr[rows, :]
        for h in range(N_HEADS):
            cols = slice(h * HEAD_DIM, (h + 1) * HEAD_DIM)
            acc = jnp.full((nb, HEAD_DIM), bsm_ref[h * T + t], F32)
            for s in range(t + 1):
                acc = acc + wsm_ref[(h * T + t) * T + s] * vn_scr[s * nb:(s + 1) * nb, cols]
            ab_scr[rows, cols] = (u_scr[rows, cols] * acc).astype(BF16)

    _s5(s_scr, bu_scr, yt_scr, state_ref, avec_ref, wb_ref, wc_ref, nb, T)
    for k in range(N_SLABS):
        s_scr[k] = yt_scr[k] + dsk_ref[:, k * LANES:(k + 1) * LANES] * s_scr[k]

    hist_total = jnp.zeros((1, ROUTE_LANES), F32)
    for r0 in (0, half):
        mix = _back(r0, half, s_scr, ab_scr, wglu_ref, bglu_ref, wout_ref, half, half)
        for tl in range(T // 2):
            t = r0 // nb + tl
            x1 = x_ref[t] + mod(2) * mix[tl * nb:(tl + 1) * nb, :]
            x1_ref[t] = x1
            h2 = _rms(x1, g2_ref[...]) * (1 + mod(4)) + mod(3)
            route, hist = _route(h2, wrc_ref, brt_ref)
            route_ref[t] = route
            hist_total = hist_total + hist
    hist_ref[...] = hist_total


def _mixer_sample(x_t, ada, h0, w_small, b_small, wts):
    T, nb, D = x_t.shape
    R = T * nb
    smem = pl.BlockSpec(memory_space=pltpu.SMEM)
    out_shape = (jax.ShapeDtypeStruct((T, nb, D), F32),
                 jax.ShapeDtypeStruct((T, SUBLANES, ROUTE_LANES), F32),
                 jax.ShapeDtypeStruct((1, ROUTE_LANES), F32),
                 jax.ShapeDtypeStruct((nb, STATE_COLS), F32),
                 jax.ShapeDtypeStruct((T, nb, D_A), F32))
    scratch = [pltpu.VMEM((R, D), BF16),
               pltpu.VMEM((R, D_A), F32),
               pltpu.VMEM((R, D_A), F32),
               pltpu.VMEM((N_SLABS, R, LANES), F32),
               pltpu.VMEM((N_SLABS, R, LANES), F32),
               pltpu.VMEM((2, R, SLAB_COLS), F32),
               pltpu.VMEM((R, D), BF16)]
    vmem = pl.BlockSpec(memory_space=pltpu.VMEM)
    return pl.pallas_call(
        _mixer_sample_kernel,
        in_specs=[smem, smem] + [vmem] * (3 + len(wts)),
        out_specs=(vmem,) * 5,
        out_shape=out_shape,
        scratch_shapes=scratch,
        compiler_params=pltpu.CompilerParams(vmem_limit_bytes=VMEM_LIMIT),
        name="mixer_sample",
    )(w_small, b_small, x_t, ada, h0, *wts)


def _rows_to_tiles(tiles_ref, row0, val):
    n = val.shape[0]
    for c in range(val.shape[1] // LANES):
        tiles_ref[pl.ds(row0 * SUBLANES + c, n, stride=SUBLANES), :] = val[:, c * LANES:(c + 1) * LANES]


def _tiles_to_rows(tiles_ref, row0, n):
    return jnp.concatenate([tiles_ref[pl.ds(row0 * SUBLANES + c, n, stride=SUBLANES), :] for c in range(SUBLANES)],
                           axis=-1)


def _moe_kernel(starts_ref, tok_ref, pos_ref, gate_ref, x1_ref, ada_ref, g2_ref, gfin_ref, w1_ref, w3_ref, w2_ref,
                out_ref, h_tiles, y_tiles, xb_even, xb_odd, cp_smem, tab_smem, *, chunk):
    t = pl.program_id(0)
    e = pl.program_id(1)
    ngrp, rows_g, D = x1_ref.shape
    n_tok = ngrp * rows_g
    mrows = ada_ref.shape[1]

    def mod(i, ci):
        if mrows == 1:
            return ada_ref[0, :, i * D:(i + 1) * D]
        return ada_ref[0, ci * chunk:(ci + 1) * chunk, i * D:(i + 1) * D]

    def tile_of(row):
        return pl.ds(pl.multiple_of(row * SUBLANES, SUBLANES), SUBLANES)

    def tile_at(row8):
        return pl.ds(pl.multiple_of(row8, SUBLANES), SUBLANES)

    @pl.when(e == 0)
    def _():
        for gi in range(ngrp):
            for ci in range(rows_g // chunk):
                x1 = x1_ref[gi, ci * chunk:(ci + 1) * chunk, :]
                h2 = _rms(x1, g2_ref[...]) * (1 + mod(4, ci)) + mod(3, ci)
                _rows_to_tiles(h_tiles, gi * rows_g + ci * chunk, h2)

    @pl.when(e == 0)
    def _():
        gp = jnp.int32(0)
        for s in range(N_EXPERTS // EXPERTS_PER_STEP):
            cp_smem[s] = gp
            st = [starts_ref[t, s * EXPERTS_PER_STEP + i] for i in range(EXPERTS_PER_STEP + 1)]
            first = [jnp.int32(0)]
            for i in range(EXPERTS_PER_STEP):
                first.append(first[i] + lax.shift_right_logical(st[i + 1] - st[i] + (MOE_BLOCK - 1), MOE_BLOCK_SHIFT))
            nb = first[EXPERTS_PER_STEP]

            def locate(f, st=st, first=first):
                el, fb, sb = jnp.int32(0), first[0], st[0]
                for i in range(1, EXPERTS_PER_STEP):
                    hit = f >= first[i]
                    el = jnp.where(hit, i, el)
                    fb = jnp.where(hit, first[i], fb)
                    sb = jnp.where(hit, st[i], sb)
                return sb + (f - fb) * MOE_BLOCK, el

            def add_pair(p, gp, nb=nb, locate=locate):
                base_a, el_a = locate(2 * p)
                base_b, el_b = locate(jnp.minimum(2 * p + 1, nb - 1))
                for i, v in enumerate((base_a, base_b, el_a, el_b)):
                    tab_smem[PAIR_FIELDS * gp + i] = v
                return gp + 1

            gp = lax.fori_loop(0, lax.shift_right_logical(nb + 1, 1), add_pair, gp)
        cp_smem[N_EXPERTS // EXPERTS_PER_STEP] = gp
        for i in range(PAIR_FIELDS):
            tab_smem[PAIR_FIELDS * gp + i] = jnp.int32(0)

    def pair_entry(gp):
        return [tab_smem[PAIR_FIELDS * gp + i] for i in range(PAIR_FIELDS)]

    @pl.when(e == 0)
    def _():
        base_a, base_b, _, _ = pair_entry(0)

        def gather8(j8, c):
            for jj in range(SUBLANES):
                j = j8 * SUBLANES + jj
                xb_even[tile_of(j), :] = h_tiles[tile_at(tok_ref[base_a + j]), :]
                xb_even[tile_of(MOE_BLOCK + j), :] = h_tiles[tile_at(tok_ref[base_b + j]), :]
            return c

        lax.fori_loop(0, MOE_BLOCK // SUBLANES, gather8, 0)

    def pair_body(gp, cur_tiles, next_tiles):
        next_a, next_b, _, _ = pair_entry(gp + 1)
        n_slices = 8
        per = 2 * MOE_BLOCK // n_slices
        slices = iter(range(n_slices))

        def gather_slice():
            s = next(slices)
            for r in range(s * per, (s + 1) * per):
                tok8 = tok_ref[next_a + r] if r < MOE_BLOCK else tok_ref[next_b + r - MOE_BLOCK]
                next_tiles[r * SUBLANES:(r + 1) * SUBLANES, :] = h_tiles[tile_at(tok8), :]

        base_a, base_b, el_a, el_b = pair_entry(gp)
        xa = _tiles_to_rows(cur_tiles, 0, MOE_BLOCK).astype(BF16)
        gather_slice()
        a1 = jnp.dot(xa, w1_ref[el_a], preferred_element_type=F32)
        gather_slice()
        a3 = jnp.dot(xa, w3_ref[el_a], preferred_element_type=F32)
        gather_slice()
        xb = _tiles_to_rows(cur_tiles, MOE_BLOCK, MOE_BLOCK).astype(BF16)
        b1 = jnp.dot(xb, w1_ref[el_b], preferred_element_type=F32)
        gather_slice()
        b3 = jnp.dot(xb, w3_ref[el_b], preferred_element_type=F32)
        gather_slice()
        ya = jnp.dot((jax.nn.silu(a1) * a3).astype(BF16), w2_ref[el_a], preferred_element_type=F32)
        gather_slice()
        yb = jnp.dot((jax.nn.silu(b1) * b3).astype(BF16), w2_ref[el_b], preferred_element_type=F32)
        gather_slice()
        _rows_to_tiles(y_tiles, base_a, ya)
        gather_slice()
        _rows_to_tiles(y_tiles, base_b, yb)

    def pair(gp, carry):
        @pl.when((gp & 1) == 0)
        def _():
            pair_body(gp, xb_even, xb_odd)

        @pl.when((gp & 1) == 1)
        def _():
            pair_body(gp, xb_odd, xb_even)

        return carry

    lax.fori_loop(cp_smem[e], cp_smem[e + 1], pair, 0)

    @pl.when(e == N_EXPERTS // EXPERTS_PER_STEP - 1)
    def _():
        def combine8(t8, c):
            for tt in range(SUBLANES):
                tok = t8 * SUBLANES + tt
                y0 = y_tiles[tile_at(pos_ref[tok]), :]
                y1 = y_tiles[tile_at(pos_ref[n_tok + tok]), :]
                h_tiles[tile_of(tok), :] = gate_ref[tok] * y0 + gate_ref[n_tok + tok] * y1
            return c

        lax.fori_loop(0, n_tok // SUBLANES, combine8, 0)
        for gi in range(ngrp):
            for ci in range(rows_g // chunk):
                moe = _tiles_to_rows(h_tiles, gi * rows_g + ci * chunk, chunk)
                x2 = x1_ref[gi, ci * chunk:(ci + 1) * chunk, :] + mod(5, ci) * moe
                out_ref[gi, ci * chunk:(ci + 1) * chunk, :] = _rms(x2, gfin_ref[...])


def _moe(x1, ada, starts, tok, pos, gate, g2, gfin, w1, w3, w2, *, n_tiles, chunk):
    ngrp = x1.shape[0] // n_tiles
    rows_g, D = x1.shape[1], x1.shape[2]
    mrows = ada.shape[1]
    Tt = ngrp * rows_g
    lp = tok.shape[0] // n_tiles
    max_pairs = (2 * Tt // MOE_BLOCK + N_EXPERTS + N_EXPERTS // EXPERTS_PER_STEP) // 2 + 1
    single = pl.Buffered(1)
    grid_spec = pltpu.PrefetchScalarGridSpec(
        num_scalar_prefetch=1,
        grid=(n_tiles, N_EXPERTS // EXPERTS_PER_STEP),
        in_specs=[
            pl.BlockSpec((lp,), lambda t, e, st: (t,), memory_space=pltpu.SMEM),
            pl.BlockSpec((2 * Tt,), lambda t, e, st: (t,), memory_space=pltpu.SMEM),
            pl.BlockSpec((2 * Tt,), lambda t, e, st: (t,), memory_space=pltpu.SMEM),
            pl.BlockSpec((ngrp, rows_g, D), lambda t, e, st: (t, 0, 0), pipeline_mode=single),
            pl.BlockSpec((1, mrows, 6 * D), lambda t, e, st: (t, 0, 0)),
            pl.BlockSpec((1, D), lambda t, e, st: (0, 0)),
            pl.BlockSpec((1, D), lambda t, e, st: (0, 0)),
            pl.BlockSpec((EXPERTS_PER_STEP, D, D_EXPERT), lambda t, e, st: (e, 0, 0)),
            pl.BlockSpec((EXPERTS_PER_STEP, D, D_EXPERT), lambda t, e, st: (e, 0, 0)),
            pl.BlockSpec((EXPERTS_PER_STEP, D_EXPERT, D), lambda t, e, st: (e, 0, 0)),
        ],
        out_specs=pl.BlockSpec((ngrp, rows_g, D), lambda t, e, st: (t, 0, 0), pipeline_mode=single),
        scratch_shapes=[pltpu.VMEM((Tt * SUBLANES, LANES), F32),
                        pltpu.VMEM(((2 * Tt + MOE_BLOCK) * SUBLANES, LANES), F32),
                        pltpu.VMEM((2 * MOE_BLOCK * SUBLANES, LANES), F32),
                        pltpu.VMEM((2 * MOE_BLOCK * SUBLANES, LANES), F32),
                        pltpu.SMEM((N_EXPERTS // EXPERTS_PER_STEP + 1,), jnp.int32),
                        pltpu.SMEM((max_pairs * PAIR_FIELDS,), jnp.int32)],
    )
    return pl.pallas_call(
        functools.partial(_moe_kernel, chunk=chunk),
        grid_spec=grid_spec,
        out_shape=jax.ShapeDtypeStruct(x1.shape, F32),
        compiler_params=pltpu.CompilerParams(dimension_semantics=("arbitrary", "arbitrary"),
                                             vmem_limit_bytes=VMEM_LIMIT),
        name="moe",
    )(starts, tok, pos, gate, x1, ada, g2, gfin, w1, w3, w2)


def _dispatch_tables(routes):
    tile_tokens = [r.shape[1] * ROUTE_LANES for r, _ in routes]
    t_max = max(tile_tokens)
    es, gs, masks = [], [], []
    for (route, _), Tt in zip(routes, tile_tokens):
        assert Tt & (Tt - 1) == 0
        n = route.shape[0]
        by_k = route[:, :, :4, :].transpose(0, 2, 1, 3).reshape(n, 4, Tt)
        extra = 2 * (t_max - Tt)
        es.append(jnp.pad(by_k[:, :2].astype(jnp.int32).reshape(n, 2 * Tt), ((0, 0), (0, extra)),
                          constant_values=N_EXPERTS))
        gs.append(jnp.pad(by_k[:, 2:].reshape(n, 2 * Tt), ((0, 0), (0, extra))))
        masks.append(jnp.full((n, 1), Tt - 1, jnp.int32))
    flat_e = jnp.concatenate(es, axis=0)
    flat_g = jnp.concatenate(gs, axis=0)
    n_tiles = flat_e.shape[0]
    order = jnp.argsort(flat_e, axis=-1, stable=True).astype(jnp.int32)
    pos = jnp.argsort(order, axis=-1).astype(jnp.int32) * SUBLANES
    tok_s = (order & jnp.concatenate(masks, axis=0)) * SUBLANES
    counts = jnp.concatenate([h[:, :N_EXPERTS] for _, h in routes], axis=0).astype(jnp.int32)
    starts = jnp.concatenate([jnp.zeros((n_tiles, 1), jnp.int32), jnp.cumsum(counts, axis=-1, dtype=jnp.int32)],
                             axis=-1)

    def tok_len(Tt):
        return -(-(2 * Tt + MOE_BLOCK) // SMEM_PAD) * SMEM_PAD

    tok_p = jnp.pad(tok_s, ((0, 0), (0, tok_len(t_max) - 2 * t_max)))
    Tt = t_max


    out, r0 = [], 0
    for (route, _), Tt in zip(routes, tile_tokens):
        r1 = r0 + route.shape[0]
        out.append((starts[r0:r1], tok_p[r0:r1, :tok_len(Tt)].reshape(-1),
                    pos[r0:r1, :2 * Tt].reshape(-1), flat_g[r0:r1, :2 * Tt].reshape(-1)))
        r0 = r1
    return out


def _pack_state(h_re, h_im):
    b = h_re.shape[0]
    return jnp.concatenate([h_re.reshape(b, STATE_COLS // 2), h_im.reshape(b, STATE_COLS // 2)], axis=-1)


def _unpack_state(st):
    b = st.shape[0]
    return (st[:, :STATE_COLS // 2].reshape(b, N_SSM_GROUPS, SSM_STATE),
            st[:, STATE_COLS // 2:].reshape(b, N_SSM_GROUPS, SSM_STATE))


def kernel(x_prompt, x_sample, state_ssm_re, state_ssm_im, c_prompt, c_sample, w_ada, b_ada, g_norm1, g_norm2, w_in, ln_g, ln_b, w_s, b_s, lam_re, lam_im, log_dt, ssm_b_re, ssm_b_im, ssm_c_re, ssm_c_im, ssm_d, w_glu, b_glu, w_out, w_group, b_group, w_expert, b_expert, w1, w3, w2, g_final):
    depth = w_ada.shape[0]
    assert depth == 1, "the final RMSNorm is fused into the (single) layer's MoE epilogue"
    B, L, D = x_prompt.shape
    Bs, Ls, _ = x_sample.shape
    xp = x_prompt
    xs_t = x_sample.transpose(1, 0, 2)
    eye = jnp.eye(SLAB_GROUPS, dtype=F32)
    tril = jnp.tril(jnp.ones((CHUNK, CHUNK), dtype=bool))
    p_re, p_im, s_re, s_im, s_v = [], [], [], [], []
    for l in range(depth):
        ada = _ada(jnp.concatenate([c_prompt, c_sample], axis=0), w_ada[l].astype(BF16), b_ada[l][None])
        ada_p, ada_s = ada[:B], ada[B:]

        ar, ai, br, bi = _discretize(lam_re[l], lam_im[l], log_dt[l], ssm_b_re[l], ssm_b_im[l])
        avec = jnp.concatenate([ar.reshape(1, STATE_COLS // 2), ai.reshape(1, STATE_COLS // 2)], axis=-1)

        def blockdiag_in(w):
            w4 = w.reshape(N_SLABS, SLAB_GROUPS, SSM_GROUP, SSM_STATE)
            return jnp.einsum('kghp,gG->kghGp', w4, eye).reshape(N_SLABS, LANES, SLAB_STATES)

        def blockdiag_out(w):
            w4 = w.reshape(N_SLABS, SLAB_GROUPS, SSM_GROUP, SSM_STATE)
            return jnp.einsum('kghp,gG->kgpGh', w4, eye).reshape(N_SLABS, SLAB_STATES, LANES)

        wb = jnp.concatenate([blockdiag_in(br), blockdiag_in(bi)], axis=-1).astype(BF16)
        wc = jnp.concatenate([blockdiag_out(ssm_c_re[l]), -blockdiag_out(ssm_c_im[l])], axis=1).astype(BF16)

        wr = jnp.zeros((D, ROUTE_LANES), F32)
        wr = wr.at[:, :N_EXPERT_GROUPS].set(w_group[l]).at[:, N_EXPERT_GROUPS:N_EXPERT_GROUPS + N_EXPERTS].set(w_expert[l])
        wr_hi = wr.astype(BF16)
        wr_lo = (wr - wr_hi.astype(F32)).astype(BF16)
        br_t = jnp.zeros((1, ROUTE_LANES), F32)
        br_t = br_t.at[0, :N_EXPERT_GROUPS].set(b_group[l]).at[0, N_EXPERT_GROUPS:N_EXPERT_GROUPS + N_EXPERTS].set(b_expert[l])

        g1 = g_norm1[l][None]
        g2 = g_norm2[l][None]
        shared = dict(
            win=w_in[l].astype(BF16), lng=ln_g[l].reshape(1, D_A), lnb=ln_b[l].reshape(1, D_A),
            avec=avec, wb=wb, wc=wc, dsk=ssm_d[l].reshape(1, D_B), wglu=w_glu[l].astype(BF16),
            bglu=b_glu[l][None], wout=w_out[l].astype(BF16), wrc=jnp.concatenate([wr_hi, wr_lo], axis=1), brt=br_t)
        w_masked = jnp.where(tril[None], w_s[l], jnp.zeros_like(w_s[l]))
        wsp = w_masked.astype(BF16)
        bsp = jnp.broadcast_to(b_s[l][:, :, None], (N_HEADS, CHUNK, HEAD_DIM))
        wts_p = (g1, g2, shared['win'], shared['lng'], shared['lnb'], wsp, bsp, shared['avec'], shared['wb'],
                 shared['wc'], shared['dsk'], shared['wglu'], shared['bglu'], shared['wout'], shared['wrc'],
                 shared['brt'])
        wts_s = (g1, g2, shared['win'], shared['lng'], shared['lnb'], shared['avec'], shared['wb'],
                 shared['wc'], shared['dsk'], shared['wglu'], shared['bglu'], shared['wout'], shared['wrc'],
                 shared['brt'])

        w1b, w3b, w2b = w1[l].astype(BF16), w3[l].astype(BF16), w2[l].astype(BF16)
        gfin = g_final[None]

        h0p = jnp.zeros((B, STATE_COLS), F32)
        x1p, route_p, hist_p, st_p = _mixer_prompt(xp, ada_p, h0p, wts_p)
        h0s = _pack_state(state_ssm_re[l].astype(F32), state_ssm_im[l].astype(F32))
        w_small = w_masked[:, :Ls, :Ls].reshape(-1)
        b_small = b_s[l][:, :Ls].reshape(-1)
        x1s, route_s, hist_s, st_s, v_s = _mixer_sample(xs_t, ada_s, h0s, w_small, b_small, wts_s)

        (tables_p,) = _dispatch_tables([(route_p, hist_p)])
        (tables_s,) = _dispatch_tables([(route_s[None], hist_s)])
        xp = _moe(x1p, ada_p[:, None, :], *tables_p, g2, gfin, w1b, w3b, w2b, n_tiles=B, chunk=256)
        xs_t = _moe(x1s, ada_s[None], *tables_s, g2, gfin, w1b, w3b, w2b, n_tiles=1, chunk=Bs)
        hr, hi = _unpack_state(st_p)
        p_re.append(hr.astype(state_ssm_re.dtype))
        p_im.append(hi.astype(state_ssm_im.dtype))
        hr, hi = _unpack_state(st_s)
        s_re.append(hr.astype(state_ssm_re.dtype))
        s_im.append(hi.astype(state_ssm_im.dtype))
        s_v.append(v_s.transpose(1, 0, 2))

    y_prompt = xp
    y_sample = xs_t.transpose(1, 0, 2)
    return (y_prompt, y_sample, jnp.stack(p_re), jnp.stack(p_im), jnp.stack(s_re), jnp.stack(s_im), jnp.stack(s_v))
```

```python
import functools

import jax
import jax.numpy as jnp
from jax import lax
from jax.experimental import pallas as pl
from jax.experimental.pallas import tpu as pltpu

F32 = jnp.float32
BF16 = jnp.bfloat16

D_MODEL = 1024
D_A = 512
D_B = 512
N_HEADS = 4
HEAD_DIM = 128
CHUNK = 128
N_SSM_GROUPS = 32
SSM_GROUP = 16
SSM_STATE = 64
N_SLABS = 4
SLAB_GROUPS = N_SSM_GROUPS // N_SLABS
SLAB_STATES = SLAB_GROUPS * SSM_STATE
SLAB_COLS = 2 * SLAB_STATES
STATE_COLS = N_SLABS * SLAB_COLS
N_EXPERT_GROUPS = 4
EXPERTS_PER_GROUP = 8
N_EXPERTS = 32
D_EXPERT = 256
EPS = 1e-6

LANES = 128
SUBLANES = 8
ROUTE_LANES = LANES
MOE_BLOCK = 128
MOE_BLOCK_SHIFT = MOE_BLOCK.bit_length() - 1
assert 1 << MOE_BLOCK_SHIFT == MOE_BLOCK
EXPERTS_PER_STEP = 4
PAIR_FIELDS = 4
S_PITCH = CHUNK + SUBLANES
SMEM_PAD = 1024
VMEM_LIMIT = 58 * 1024 * 1024


def _rms(xf, g):
    ms = jnp.mean(xf * xf, axis=-1, keepdims=True)
    return xf * lax.rsqrt(ms + EPS) * g


def _ada_kernel(cp_ref, cs_ref, w_ref, b_ref, op_ref, os_ref):
    w = w_ref[...].astype(BF16)
    for c_ref, o_ref in ((cp_ref, op_ref), (cs_ref, os_ref)):
        s = jax.nn.silu(c_ref[...]).astype(BF16)
        o_ref[...] = jnp.dot(s, w, preferred_element_type=F32) + b_ref[...]


def _ada(c_p, c_s, w, b):
    mp, ms = c_p.shape[0], c_s.shape[0]
    n = w.shape[1]
    bn = 1024
    return pl.pallas_call(
        _ada_kernel,
        grid=(n // bn,),
        in_specs=[pl.BlockSpec((mp, D_MODEL), lambda j: (0, 0)),
                  pl.BlockSpec((ms, D_MODEL), lambda j: (0, 0)),
                  pl.BlockSpec((D_MODEL, bn), lambda j: (0, j)),
                  pl.BlockSpec((1, bn), lambda j: (0, j))],
        out_specs=(pl.BlockSpec((mp, bn), lambda j: (0, j)), pl.BlockSpec((ms, bn), lambda j: (0, j))),
        out_shape=(jax.ShapeDtypeStruct((mp, n), F32), jax.ShapeDtypeStruct((ms, n), F32)),
        name="ada",
    )(c_p, c_s, w, b)


def _disc_kernel(lre_ref, lim_ref, ldt_ref, bre_ref, bim_ref, ar_ref, ai_ref, br_ref, bi_ref):
    dt = jnp.exp(ldt_ref[...])
    lr = lre_ref[...]
    li = lim_ref[...]
    mag = jnp.exp(lr * dt)
    ar = mag * jnp.cos(li * dt)
    ai = mag * jnp.sin(li * dt)
    den = lr * lr + li * li
    cr = ((ar - 1) * lr + ai * li) / den
    ci = (ai * lr - (ar - 1) * li) / den
    ar_ref[...] = ar
    ai_ref[...] = ai
    bre = bre_ref[...]
    bim = bim_ref[...]
    br_ref[...] = cr * bre - ci * bim
    bi_ref[...] = cr * bim + ci * bre


def _discretize(lam_re, lam_im, log_dt, b_re, b_im):
    g, p, h = b_re.shape
    o1 = jax.ShapeDtypeStruct((g, 1, p), F32)
    o2 = jax.ShapeDtypeStruct((g, h, p), F32)
    return pl.pallas_call(_disc_kernel, out_shape=(o1, o1, o2, o2), name="ssm_disc")(
        lam_re.reshape(g, 1, p), lam_im.reshape(g, 1, p), log_dt.reshape(g, 1, 1),
        b_re.transpose(0, 2, 1), b_im.transpose(0, 2, 1))


def _s_pieces(r0, nrows, grp, pitch):
    return [(i * grp, slice((r0 // grp + i) * pitch, (r0 // grp + i) * pitch + grp)) for i in range(nrows // grp)]


def _front(r0, nrows, h_scr, win_ref, lng_ref, lnb_ref, u_scr, vn_scr, s_scr, grp, pitch):
    rows = slice(r0, r0 + nrows)
    proj = jnp.dot(h_scr[rows, :], win_ref[...], preferred_element_type=F32)
    u_scr[rows, :] = jax.nn.gelu(proj[:, :D_A])
    vraw = jax.nn.gelu(proj[:, D_A:2 * D_A])
    for h in range(N_HEADS):
        cols = slice(h * HEAD_DIM, (h + 1) * HEAD_DIM)
        vh = vraw[:, cols]
        mu = jnp.mean(vh, axis=-1, keepdims=True)
        dv = vh - mu
        var = jnp.mean(dv * dv, axis=-1, keepdims=True)
        vn_scr[rows, cols] = dv * lax.rsqrt(var + EPS) * lng_ref[:, cols] + lnb_ref[:, cols]
    for k in range(N_SLABS):
        for off, prow in _s_pieces(r0, nrows, grp, pitch):
            s_scr[k, prow, :] = proj[off:off + grp, 2 * D_A + k * LANES:2 * D_A + (k + 1) * LANES]


def _scan_slab(bu, state, avec_ref, k, rows_per_step, steps):
    c_re = slice(0, SLAB_STATES)
    c_im = slice(SLAB_STATES, SLAB_COLS)
    s_re = slice(k * SLAB_STATES, (k + 1) * SLAB_STATES)
    s_im = slice(STATE_COLS // 2 + k * SLAB_STATES, STATE_COLS // 2 + (k + 1) * SLAB_STATES)
    ar = jnp.broadcast_to(avec_ref[:, s_re], (SUBLANES, SLAB_STATES))
    ai = jnp.broadcast_to(avec_ref[:, s_im], (SUBLANES, SLAB_STATES))
    for rc in range(rows_per_step // SUBLANES):
        r0 = rc * SUBLANES
        sr = state[r0:r0 + SUBLANES, s_re]
        si = state[r0:r0 + SUBLANES, s_im]
        for t in range(steps):
            rows = slice(t * rows_per_step + r0, t * rows_per_step + r0 + SUBLANES)
            nr = ar * sr - ai * si + bu[rows, c_re]
            ni = ar * si + ai * sr + bu[rows, c_im]
            bu[rows, c_re] = nr
            bu[rows, c_im] = ni
            sr, si = nr, ni
        state[r0:r0 + SUBLANES, s_re] = sr
        state[r0:r0 + SUBLANES, s_im] = si


def _s5(st_ref, bu_scr, yt_ref, state, avec_ref, wb_ref, wc_ref, rows_per_step, steps):
    for k in range(N_SLABS):
        bu = bu_scr.at[k % 2]
        bu[...] = jnp.dot(st_ref[k].astype(BF16), wb_ref[k], preferred_element_type=F32)
        _scan_slab(bu, state, avec_ref, k, rows_per_step, steps)
        yt_ref[k] = jnp.dot(bu[...].astype(BF16), wc_ref[k], preferred_element_type=F32)


def _back(r0, nrows, s_scr, ab_scr, wglu_ref, bglu_ref, wout_ref, grp, pitch):
    rows = slice(r0, r0 + nrows)
    pieces = _s_pieces(r0, nrows, grp, pitch)
    y = jax.nn.gelu(jnp.concatenate(
        [jnp.concatenate([s_scr[k, prow, :] for _, prow in pieces], axis=0) for k in range(N_SLABS)], axis=-1))
    gate = jnp.dot(y.astype(BF16), wglu_ref[...], preferred_element_type=F32) + bglu_ref[...]
    ab_scr[rows, D_A:] = (y * jax.nn.sigmoid(gate)).astype(BF16)
    return jnp.dot(ab_scr[rows, :], wout_ref[...], preferred_element_type=F32)


def _split_hi_lo(h2):
    hi = h2.astype(BF16)
    return hi, (h2 - hi.astype(F32)).astype(BF16)


def _route(hi, lo, wrc_ref, brt_ref):
    both = jnp.dot(hi, wrc_ref[...], preferred_element_type=F32)
    logits = (both[:, :ROUTE_LANES] + both[:, ROUTE_LANES:]
              + jnp.dot(lo, wrc_ref[:, :ROUTE_LANES], preferred_element_type=F32)) + brt_ref[...]
    n = logits.shape[0]
    lane = lax.broadcasted_iota(jnp.int32, (n, ROUTE_LANES), 1)
    lane_f = lane.astype(F32)
    big = jnp.float32(1e9)
    ninf = jnp.float32(-jnp.inf)
    is_g = lane < N_EXPERT_GROUPS
    gl = jnp.where(is_g, logits, ninf)
    gmax = jnp.max(gl, axis=-1, keepdims=True)
    gidx = jnp.min(jnp.where(gl == gmax, lane_f, big), axis=-1, keepdims=True)
    gsum = jnp.sum(jnp.where(is_g, jnp.exp(logits - gmax), 0.0), axis=-1, keepdims=True)
    g_w = 1.0 / gsum
    elo = N_EXPERT_GROUPS + EXPERTS_PER_GROUP * gidx
    emask = (lane_f >= elo) & (lane_f < elo + EXPERTS_PER_GROUP)
    el = jnp.where(emask, logits, ninf)
    t1 = jnp.max(el, axis=-1, keepdims=True)
    i1 = jnp.min(jnp.where(el == t1, lane_f, big), axis=-1, keepdims=True)
    el2 = jnp.where(lane_f == i1, ninf, el)
    t2 = jnp.max(el2, axis=-1, keepdims=True)
    i2 = jnp.min(jnp.where(el2 == t2, lane_f, big), axis=-1, keepdims=True)
    e21 = jnp.exp(t2 - t1)
    den = 1.0 + e21
    gate1 = g_w * (1.0 / den)
    gate2 = g_w * (e21 / den)
    e1 = i1 - N_EXPERT_GROUPS
    e2 = i2 - N_EXPERT_GROUPS
    packed = jnp.where(lane == 0, e1, jnp.where(lane == 1, e2, jnp.where(lane == 2, gate1,
                                                                           jnp.where(lane == 3, gate2, 0.0))))
    hist = jnp.sum(((lane_f == e1) | (lane_f == e2)).astype(F32), axis=0, keepdims=True)
    return packed.T[:SUBLANES], hist


def _mixer_prompt_kernel(x_ref, ada_ref, h0_ref, g1_ref, g2_ref, win_ref, lng_ref, lnb_ref, wsp_ref, bsp_ref,
                         avec_ref, wb_ref, wc_ref, dsk_ref, wglu_ref, bglu_ref, wout_ref, wrc_ref, brt_ref,
                         x1_ref, route_ref, hist_ref, state_ref,
                         h_scr, u_scr, vn_scr, s_scr, st_scr, yt_scr, bu_scr, ab_scr, hi_scr, lo_scr):
    nb = x_ref.shape[0]
    half = (nb // 2) * CHUNK
    D = D_MODEL
    pitch = S_PITCH
    step = pl.program_id(0)

    @pl.when(step == 0)
    def _():
        state_ref[...] = h0_ref[...]
        hist_ref[...] = jnp.zeros_like(hist_ref)
        hi_scr[...] = jnp.zeros_like(hi_scr)
        lo_scr[...] = jnp.zeros_like(lo_scr)

    def mod(b, i):
        return ada_ref[b:b + 1, i * D:(i + 1) * D]

    def route_chunk(chunk, weight):
        for b in range(nb):
            rows = slice(b * CHUNK, (b + 1) * CHUNK)
            route, hist = _route(hi_scr[rows, :], lo_scr[rows, :], wrc_ref, brt_ref)
            route_ref[b, chunk] = route
            hist_ref[b:b + 1, :] = hist_ref[b:b + 1, :] + weight * hist

    route_chunk(jnp.maximum(step - 1, 0), (step > 0).astype(F32))

    for b in range(nb):
        hb = _rms(x_ref[b], g1_ref[...]) * (1 + mod(b, 1)) + mod(b, 0)
        h_scr[b * CHUNK:(b + 1) * CHUNK, :] = hb.astype(BF16)

    for r0 in (0, half):
        _front(r0, half, h_scr, win_ref, lng_ref, lnb_ref, u_scr, vn_scr, s_scr, CHUNK, pitch)

    for b in range(nb):
        rows = slice(b * CHUNK, (b + 1) * CHUNK)
        for h in range(N_HEADS):
            cols = slice(h * HEAD_DIM, (h + 1) * HEAD_DIM)
            mixed = jnp.dot(wsp_ref[h], vn_scr[rows, cols].astype(BF16), preferred_element_type=F32) + bsp_ref[h]
            ab_scr[rows, cols] = (u_scr[rows, cols] * mixed).astype(BF16)

    for k in range(N_SLABS):
        for t in range(CHUNK):
            st_scr[k, t * nb:(t + 1) * nb, :] = s_scr[k, pl.ds(t, nb, stride=pitch), :]
    _s5(st_scr, bu_scr, yt_scr, state_ref, avec_ref, wb_ref, wc_ref, nb, CHUNK)
    for k in range(N_SLABS):
        dsk = dsk_ref[:, k * LANES:(k + 1) * LANES]
        for t in range(CHUNK):
            sel = pl.ds(t, nb, stride=pitch)
            s_scr[k, sel, :] = yt_scr[k, t * nb:(t + 1) * nb, :] + dsk * s_scr[k, sel, :]

    for r0 in (0, half):
        mix = _back(r0, half, s_scr, ab_scr, wglu_ref, bglu_ref, wout_ref, CHUNK, pitch)
        for bl in range(nb // 2):
            b = r0 // CHUNK + bl
            x1 = x_ref[b] + mod(b, 2) * mix[bl * CHUNK:(bl + 1) * CHUNK, :]
            x1_ref[b] = x1
            h2 = _rms(x1, g2_ref[...]) * (1 + mod(b, 4)) + mod(b, 3)
            hi, lo = _split_hi_lo(h2)
            hi_scr[b * CHUNK:(b + 1) * CHUNK, :] = hi
            lo_scr[b * CHUNK:(b + 1) * CHUNK, :] = lo

    @pl.when(step == pl.num_programs(0) - 1)
    def _():
        route_chunk(step, 1.0)


def _const_spec(shape):
    nd = len(shape)
    return pl.BlockSpec(shape, lambda *_: (0,) * nd, pipeline_mode=pl.Buffered(1))


def _mixer_prompt(x, ada, h0, wts):
    nb, seq, D = x.shape
    n_chunks = seq // CHUNK
    R = nb * CHUNK
    weight_specs = [_const_spec(w.shape) for w in wts]
    in_specs = [pl.BlockSpec((nb, CHUNK, D), lambda i: (0, i, 0)),
                _const_spec(ada.shape), _const_spec(h0.shape)] + weight_specs
    out_shape = (jax.ShapeDtypeStruct((nb, seq, D), F32),
                 jax.ShapeDtypeStruct((nb, n_chunks, SUBLANES, ROUTE_LANES), F32),
                 jax.ShapeDtypeStruct((nb, ROUTE_LANES), F32),
                 jax.ShapeDtypeStruct((nb, STATE_COLS), F32))
    out_specs = (pl.BlockSpec((nb, CHUNK, D), lambda i: (0, i, 0)),
                 pl.BlockSpec((nb, n_chunks, SUBLANES, ROUTE_LANES), lambda i: (0, 0, 0, 0)),
                 pl.BlockSpec((nb, ROUTE_LANES), lambda i: (0, 0)),
                 pl.BlockSpec((nb, STATE_COLS), lambda i: (0, 0)))
    scratch = [pltpu.VMEM((R, D), BF16),
               pltpu.VMEM((R, D_A), F32),
               pltpu.VMEM((R, D_A), F32),
               pltpu.VMEM((N_SLABS, nb * S_PITCH, LANES), F32),
               pltpu.VMEM((N_SLABS, R, LANES), F32),
               pltpu.VMEM((N_SLABS, R, LANES), F32),
               pltpu.VMEM((2, R, SLAB_COLS), F32),
               pltpu.VMEM((R, D), BF16),
               pltpu.VMEM((R, D), BF16),
               pltpu.VMEM((R, D), BF16)]
    return pl.pallas_call(
        _mixer_prompt_kernel,
        grid=(n_chunks,),
        in_specs=in_specs,
        out_specs=out_specs,
        out_shape=out_shape,
        scratch_shapes=scratch,
        compiler_params=pltpu.CompilerParams(dimension_semantics=("arbitrary",), vmem_limit_bytes=VMEM_LIMIT),
        name="mixer_prompt",
    )(x, ada, h0, *wts)


def _mixer_sample_kernel(wsm_ref, bsm_ref, x_ref, ada_ref, h0_ref, g1_ref, g2_ref, win_ref, lng_ref, lnb_ref,
                         avec_ref, wb_ref, wc_ref, dsk_ref, wglu_ref, bglu_ref, wout_ref, wrc_ref, brt_ref,
                         x1_ref, route_ref, hist_ref, state_ref, v_ref,
                         h_scr, u_scr, vn_scr, s_scr, yt_scr, bu_scr, ab_scr):
    T, nb, D = x_ref.shape
    R = T * nb
    half = R // 2

    def mod(i):
        return ada_ref[:, i * D:(i + 1) * D]

    state_ref[...] = h0_ref[...]
    for t in range(T):
        ht = _rms(x_ref[t], g1_ref[...]) * (1 + mod(1)) + mod(0)
        h_scr[t * nb:(t + 1) * nb, :] = ht.astype(BF16)

    for r0 in (0, half):
        _front(r0, half, h_scr, win_ref, lng_ref, lnb_ref, u_scr, vn_scr, s_scr, half, half)

    for t in range(T):
        rows = slice(t * nb, (t + 1) * nb)
        v_ref[t] = vn_scr[rows, :]
        for h in range(N_HEADS):
            cols = slice(h * HEAD_DIM, (h + 1) * HEAD_DIM)
            acc = jnp.full((nb, HEAD_DIM), bsm_ref[h * T + t], F32)
            for s in range(t + 1):
                acc = acc + wsm_ref[(h * T + t) * T + s] * vn_scr[s * nb:(s + 1) * nb, cols]
            ab_scr[rows, cols] = (u_scr[rows, cols] * acc).astype(BF16)

    _s5(s_scr, bu_scr, yt_scr, state_ref, avec_ref, wb_ref, wc_ref, nb, T)
    for k in range(N_SLABS):
        s_scr[k] = yt_scr[k] + dsk_ref[:, k * LANES:(k + 1) * LANES] * s_scr[k]

    hist_total = jnp.zeros((1, ROUTE_LANES), F32)
    for r0 in (0, half):
        mix = _back(r0, half, s_scr, ab_scr, wglu_ref, bglu_ref, wout_ref, half, half)
        for tl in range(T // 2):
            t = r0 // nb + tl
            x1 = x_ref[t] + mod(2) * mix[tl * nb:(tl + 1) * nb, :]
            x1_ref[t] = x1
            h2 = _rms(x1, g2_ref[...]) * (1 + mod(4)) + mod(3)
            route, hist = _route(*_split_hi_lo(h2), wrc_ref, brt_ref)
            route_ref[t] = route
            hist_total = hist_total + hist
    hist_ref[...] = hist_total


def _mixer_sample(x_t, ada, h0, w_small, b_small, wts):
    T, nb, D = x_t.shape
    R = T * nb
    smem = pl.BlockSpec(memory_space=pltpu.SMEM)
    out_shape = (jax.ShapeDtypeStruct((T, nb, D), F32),
                 jax.ShapeDtypeStruct((T, SUBLANES, ROUTE_LANES), F32),
                 jax.ShapeDtypeStruct((1, ROUTE_LANES), F32),
                 jax.ShapeDtypeStruct((nb, STATE_COLS), F32),
                 jax.ShapeDtypeStruct((T, nb, D_A), F32))
    scratch = [pltpu.VMEM((R, D), BF16),
               pltpu.VMEM((R, D_A), F32),
               pltpu.VMEM((R, D_A), F32),
               pltpu.VMEM((N_SLABS, R, LANES), F32),
               pltpu.VMEM((N_SLABS, R, LANES), F32),
               pltpu.VMEM((2, R, SLAB_COLS), F32),
               pltpu.VMEM((R, D), BF16)]
    vmem = pl.BlockSpec(memory_space=pltpu.VMEM)
    return pl.pallas_call(
        _mixer_sample_kernel,
        in_specs=[smem, smem] + [vmem] * (3 + len(wts)),
        out_specs=(vmem,) * 5,
        out_shape=out_shape,
        scratch_shapes=scratch,
        compiler_params=pltpu.CompilerParams(vmem_limit_bytes=VMEM_LIMIT),
        name="mixer_sample",
    )(w_small, b_small, x_t, ada, h0, *wts)


def _rows_to_tiles(tiles_ref, row0, val):
    n = val.shape[0]
    for c in range(val.shape[1] // LANES):
        tiles_ref[pl.ds(row0 * SUBLANES + c, n, stride=SUBLANES), :] = val[:, c * LANES:(c + 1) * LANES]


def _tiles_to_rows(tiles_ref, row0, n):
    return jnp.concatenate([tiles_ref[pl.ds(row0 * SUBLANES + c, n, stride=SUBLANES), :] for c in range(SUBLANES)],
                           axis=-1)


def _moe_kernel(starts_ref, tok_ref, pos_ref, gate_ref, x1_ref, ada_ref, g2_ref, gfin_ref, w1_ref, w3_ref, w2_ref,
                out_ref, h_tiles, y_tiles, xb_even, xb_odd, cp_smem, tab_smem, *, chunk):
    t = pl.program_id(0)
    e = pl.program_id(1)
    ngrp, rows_g, D = x1_ref.shape
    n_tok = ngrp * rows_g
    mrows = ada_ref.shape[1]

    def mod(i, ci):
        if mrows == 1:
            return ada_ref[0, :, i * D:(i + 1) * D]
        return ada_ref[0, ci * chunk:(ci + 1) * chunk, i * D:(i + 1) * D]

    def tile_of(row):
        return pl.ds(pl.multiple_of(row * SUBLANES, SUBLANES), SUBLANES)

    def tile_at(row8):
        return pl.ds(pl.multiple_of(row8, SUBLANES), SUBLANES)

    @pl.when(e == 0)
    def _():
        for gi in range(ngrp):
            for ci in range(rows_g // chunk):
                x1 = x1_ref[gi, ci * chunk:(ci + 1) * chunk, :]
                h2 = _rms(x1, g2_ref[...]) * (1 + mod(4, ci)) + mod(3, ci)
                _rows_to_tiles(h_tiles, gi * rows_g + ci * chunk, h2)

    @pl.when(e == 0)
    def _():
        gp = jnp.int32(0)
        for s in range(N_EXPERTS // EXPERTS_PER_STEP):
            cp_smem[s] = gp
            st = [starts_ref[t, s * EXPERTS_PER_STEP + i] for i in range(EXPERTS_PER_STEP + 1)]
            first = [jnp.int32(0)]
            for i in range(EXPERTS_PER_STEP):
                first.append(first[i] + lax.shift_right_logical(st[i + 1] - st[i] + (MOE_BLOCK - 1), MOE_BLOCK_SHIFT))
            nb = first[EXPERTS_PER_STEP]

            def locate(f, st=st, first=first):
                el, fb, sb = jnp.int32(0), first[0], st[0]
                for i in range(1, EXPERTS_PER_STEP):
                    hit = f >= first[i]
                    el = jnp.where(hit, i, el)
                    fb = jnp.where(hit, first[i], fb)
                    sb = jnp.where(hit, st[i], sb)
                return sb + (f - fb) * MOE_BLOCK, el

            def add_pair(p, gp, nb=nb, locate=locate):
                base_a, el_a = locate(2 * p)
                base_b, el_b = locate(jnp.minimum(2 * p + 1, nb - 1))
                for i, v in enumerate((base_a, base_b, el_a, el_b)):
                    tab_smem[PAIR_FIELDS * gp + i] = v
                return gp + 1

            gp = lax.fori_loop(0, lax.shift_right_logical(nb + 1, 1), add_pair, gp)
        cp_smem[N_EXPERTS // EXPERTS_PER_STEP] = gp
        for i in range(PAIR_FIELDS):
            tab_smem[PAIR_FIELDS * gp + i] = jnp.int32(0)

    def pair_entry(gp):
        return [tab_smem[PAIR_FIELDS * gp + i] for i in range(PAIR_FIELDS)]

    @pl.when(e == 0)
    def _():
        base_a, base_b, _, _ = pair_entry(0)

        def gather8(j8, c):
            for jj in range(SUBLANES):
                j = j8 * SUBLANES + jj
                xb_even[tile_of(j), :] = h_tiles[tile_at(tok_ref[base_a + j]), :]
                xb_even[tile_of(MOE_BLOCK + j), :] = h_tiles[tile_at(tok_ref[base_b + j]), :]
            return c

        lax.fori_loop(0, MOE_BLOCK // SUBLANES, gather8, 0)

    def pair_body(gp, cur_tiles, next_tiles):
        next_a, next_b, _, _ = pair_entry(gp + 1)
        n_slices = 8
        per = 2 * MOE_BLOCK // n_slices
        slices = iter(range(n_slices))

        def gather_slice():
            s = next(slices)
            for r in range(s * per, (s + 1) * per):
                tok8 = tok_ref[next_a + r] if r < MOE_BLOCK else tok_ref[next_b + r - MOE_BLOCK]
                next_tiles[r * SUBLANES:(r + 1) * SUBLANES, :] = h_tiles[tile_at(tok8), :]

        base_a, base_b, el_a, el_b = pair_entry(gp)
        xa = _tiles_to_rows(cur_tiles, 0, MOE_BLOCK).astype(BF16)
        gather_slice()
        a1 = jnp.dot(xa, w1_ref[el_a], preferred_element_type=F32)
        gather_slice()
        a3 = jnp.dot(xa, w3_ref[el_a], preferred_element_type=F32)
        gather_slice()
        xb = _tiles_to_rows(cur_tiles, MOE_BLOCK, MOE_BLOCK).astype(BF16)
        b1 = jnp.dot(xb, w1_ref[el_b], preferred_element_type=F32)
        gather_slice()
        b3 = jnp.dot(xb, w3_ref[el_b], preferred_element_type=F32)
        gather_slice()
        ya = jnp.dot((jax.nn.silu(a1) * a3).astype(BF16), w2_ref[el_a], preferred_element_type=F32)
        gather_slice()
        yb = jnp.dot((jax.nn.silu(b1) * b3).astype(BF16), w2_ref[el_b], preferred_element_type=F32)
        gather_slice()
        _rows_to_tiles(y_tiles, base_a, ya)
        gather_slice()
        _rows_to_tiles(y_tiles, base_b, yb)

    def pair(gp, carry):
        @pl.when((gp & 1) == 0)
        def _():
            pair_body(gp, xb_even, xb_odd)

        @pl.when((gp & 1) == 1)
        def _():
            pair_body(gp, xb_odd, xb_even)

        return carry

    lax.fori_loop(cp_smem[e], cp_smem[e + 1], pair, 0)

    @pl.when(e == N_EXPERTS // EXPERTS_PER_STEP - 1)
    def _():
        def combine8(t8, c):
            for tt in range(SUBLANES):
                tok = t8 * SUBLANES + tt
                y0 = y_tiles[tile_at(pos_ref[tok]), :]
                y1 = y_tiles[tile_at(pos_ref[n_tok + tok]), :]
                h_tiles[tile_of(tok), :] = gate_ref[tok] * y0 + gate_ref[n_tok + tok] * y1
            return c

        lax.fori_loop(0, n_tok // SUBLANES, combine8, 0)
        for gi in range(ngrp):
            for ci in range(rows_g // chunk):
                moe = _tiles_to_rows(h_tiles, gi * rows_g + ci * chunk, chunk)
                x2 = x1_ref[gi, ci * chunk:(ci + 1) * chunk, :] + mod(5, ci) * moe
                out_ref[gi, ci * chunk:(ci + 1) * chunk, :] = _rms(x2, gfin_ref[...])


def _moe(x1, ada, starts, tok, pos, gate, g2, gfin, w1, w3, w2, *, n_tiles, chunk):
    ngrp = x1.shape[0] // n_tiles
    rows_g, D = x1.shape[1], x1.shape[2]
    mrows = ada.shape[1]
    Tt = ngrp * rows_g
    lp = tok.shape[0] // n_tiles
    max_pairs = (2 * Tt // MOE_BLOCK + N_EXPERTS + N_EXPERTS // EXPERTS_PER_STEP) // 2 + 1
    single = pl.Buffered(1)
    grid_spec = pltpu.PrefetchScalarGridSpec(
        num_scalar_prefetch=1,
        grid=(n_tiles, N_EXPERTS // EXPERTS_PER_STEP),
        in_specs=[
            pl.BlockSpec((lp,), lambda t, e, st: (t,), memory_space=pltpu.SMEM),
            pl.BlockSpec((2 * Tt,), lambda t, e, st: (t,), memory_space=pltpu.SMEM),
            pl.BlockSpec((2 * Tt,), lambda t, e, st: (t,), memory_space=pltpu.SMEM),
            pl.BlockSpec((ngrp, rows_g, D), lambda t, e, st: (t, 0, 0), pipeline_mode=single),
            pl.BlockSpec((1, mrows, 6 * D), lambda t, e, st: (t, 0, 0)),
            pl.BlockSpec((1, D), lambda t, e, st: (0, 0)),
            pl.BlockSpec((1, D), lambda t, e, st: (0, 0)),
            pl.BlockSpec((EXPERTS_PER_STEP, D, D_EXPERT), lambda t, e, st: (e, 0, 0)),
            pl.BlockSpec((EXPERTS_PER_STEP, D, D_EXPERT), lambda t, e, st: (e, 0, 0)),
            pl.BlockSpec((EXPERTS_PER_STEP, D_EXPERT, D), lambda t, e, st: (e, 0, 0)),
        ],
        out_specs=pl.BlockSpec((ngrp, rows_g, D), lambda t, e, st: (t, 0, 0), pipeline_mode=single),
        scratch_shapes=[pltpu.VMEM((Tt * SUBLANES, LANES), F32),
                        pltpu.VMEM(((2 * Tt + MOE_BLOCK) * SUBLANES, LANES), F32),
                        pltpu.VMEM((2 * MOE_BLOCK * SUBLANES, LANES), F32),
                        pltpu.VMEM((2 * MOE_BLOCK * SUBLANES, LANES), F32),
                        pltpu.SMEM((N_EXPERTS // EXPERTS_PER_STEP + 1,), jnp.int32),
                        pltpu.SMEM((max_pairs * PAIR_FIELDS,), jnp.int32)],
    )
    return pl.pallas_call(
        functools.partial(_moe_kernel, chunk=chunk),
        grid_spec=grid_spec,
        out_shape=jax.ShapeDtypeStruct(x1.shape, F32),
        compiler_params=pltpu.CompilerParams(dimension_semantics=("arbitrary", "arbitrary"),
                                             vmem_limit_bytes=VMEM_LIMIT),
        name="moe",
    )(starts, tok, pos, gate, x1, ada, g2, gfin, w1, w3, w2)


def _dispatch_tables(routes):
    tile_tokens = [r.shape[1] * ROUTE_LANES for r, _ in routes]
    t_max = max(tile_tokens)
    es, gs, masks = [], [], []
    for (route, _), Tt in zip(routes, tile_tokens):
        assert Tt & (Tt - 1) == 0
        n = route.shape[0]
        by_k = route[:, :, :4, :].transpose(0, 2, 1, 3).reshape(n, 4, Tt)
        extra = 2 * (t_max - Tt)
        es.append(jnp.pad(by_k[:, :2].astype(jnp.int32).reshape(n, 2 * Tt), ((0, 0), (0, extra)),
                          constant_values=N_EXPERTS))
        gs.append(jnp.pad(by_k[:, 2:].reshape(n, 2 * Tt), ((0, 0), (0, extra))))
        masks.append(jnp.full((n, 1), Tt - 1, jnp.int32))
    flat_e = jnp.concatenate(es, axis=0)
    flat_g = jnp.concatenate(gs, axis=0)
    n_tiles = flat_e.shape[0]
    order = jnp.argsort(flat_e, axis=-1, stable=True).astype(jnp.int32)
    pos = jnp.argsort(order, axis=-1).astype(jnp.int32) * SUBLANES
    tok_s = (order & jnp.concatenate(masks, axis=0)) * SUBLANES
    counts = jnp.concatenate([h[:, :N_EXPERTS] for _, h in routes], axis=0).astype(jnp.int32)
    starts = jnp.concatenate([jnp.zeros((n_tiles, 1), jnp.int32), jnp.cumsum(counts, axis=-1, dtype=jnp.int32)],
                             axis=-1)

    def tok_len(Tt):
        return -(-(2 * Tt + MOE_BLOCK) // SMEM_PAD) * SMEM_PAD

    tok_p = jnp.pad(tok_s, ((0, 0), (0, tok_len(t_max) - 2 * t_max)))
    Tt = t_max


    out, r0 = [], 0
    for (route, _), Tt in zip(routes, tile_tokens):
        r1 = r0 + route.shape[0]
        out.append((starts[r0:r1], tok_p[r0:r1, :tok_len(Tt)].reshape(-1),
                    pos[r0:r1, :2 * Tt].reshape(-1), flat_g[r0:r1, :2 * Tt].reshape(-1)))
        r0 = r1
    return out


def _pack_state(h_re, h_im):
    b = h_re.shape[0]
    return jnp.concatenate([h_re.reshape(b, STATE_COLS // 2), h_im.reshape(b, STATE_COLS // 2)], axis=-1)


def _unpack_state(st):
    b = st.shape[0]
    return (st[:, :STATE_COLS // 2].reshape(b, N_SSM_GROUPS, SSM_STATE),
            st[:, STATE_COLS // 2:].reshape(b, N_SSM_GROUPS, SSM_STATE))


def kernel(x_prompt, x_sample, state_ssm_re, state_ssm_im, c_prompt, c_sample, w_ada, b_ada, g_norm1, g_norm2, w_in, ln_g, ln_b, w_s, b_s, lam_re, lam_im, log_dt, ssm_b_re, ssm_b_im, ssm_c_re, ssm_c_im, ssm_d, w_glu, b_glu, w_out, w_group, b_group, w_expert, b_expert, w1, w3, w2, g_final):
    depth = w_ada.shape[0]
    assert depth == 1, "the final RMSNorm is fused into the (single) layer's MoE epilogue"
    B, L, D = x_prompt.shape
    Bs, Ls, _ = x_sample.shape
    xp = x_prompt
    xs_t = x_sample.transpose(1, 0, 2)
    eye = jnp.eye(SLAB_GROUPS, dtype=F32)
    tril = jnp.tril(jnp.ones((CHUNK, CHUNK), dtype=bool))
    p_re, p_im, s_re, s_im, s_v = [], [], [], [], []
    for l in range(depth):
        ada_p, ada_s = _ada(c_prompt, c_sample, w_ada[l], b_ada[l][None])

        ar, ai, br, bi = _discretize(lam_re[l], lam_im[l], log_dt[l], ssm_b_re[l], ssm_b_im[l])
        avec = jnp.concatenate([ar.reshape(1, STATE_COLS // 2), ai.reshape(1, STATE_COLS // 2)], axis=-1)

        def blockdiag_in(w):
            w4 = w.reshape(N_SLABS, SLAB_GROUPS, SSM_GROUP, SSM_STATE)
            return jnp.einsum('kghp,gG->kghGp', w4, eye).reshape(N_SLABS, LANES, SLAB_STATES)

        def blockdiag_out(w):
            w4 = w.reshape(N_SLABS, SLAB_GROUPS, SSM_GROUP, SSM_STATE)
            return jnp.einsum('kghp,gG->kgpGh', w4, eye).reshape(N_SLABS, SLAB_STATES, LANES)

        wb = jnp.concatenate([blockdiag_in(br), blockdiag_in(bi)], axis=-1).astype(BF16)
        wc = jnp.concatenate([blockdiag_out(ssm_c_re[l]), -blockdiag_out(ssm_c_im[l])], axis=1).astype(BF16)

        lane_pad = ROUTE_LANES - N_EXPERT_GROUPS - N_EXPERTS
        wr = jnp.pad(jnp.concatenate([w_group[l], w_expert[l]], axis=1), ((0, 0), (0, lane_pad)))
        wr_hi = wr.astype(BF16)
        wr_lo = (wr - wr_hi.astype(F32)).astype(BF16)
        br_t = jnp.pad(jnp.concatenate([b_group[l], b_expert[l]]), (0, lane_pad))[None]

        g1 = g_norm1[l][None]
        g2 = g_norm2[l][None]
        shared = dict(
            win=w_in[l].astype(BF16), lng=ln_g[l].reshape(1, D_A), lnb=ln_b[l].reshape(1, D_A),
            avec=avec, wb=wb, wc=wc, dsk=ssm_d[l].reshape(1, D_B), wglu=w_glu[l].astype(BF16),
            bglu=b_glu[l][None], wout=w_out[l].astype(BF16), wrc=jnp.concatenate([wr_hi, wr_lo], axis=1), brt=br_t)
        w_masked = jnp.where(tril[None], w_s[l], jnp.zeros_like(w_s[l]))
        wsp = w_masked.astype(BF16)
        bsp = jnp.broadcast_to(b_s[l][:, :, None], (N_HEADS, CHUNK, HEAD_DIM))
        wts_p = (g1, g2, shared['win'], shared['lng'], shared['lnb'], wsp, bsp, shared['avec'], shared['wb'],
                 shared['wc'], shared['dsk'], shared['wglu'], shared['bglu'], shared['wout'], shared['wrc'],
                 shared['brt'])
        wts_s = (g1, g2, shared['win'], shared['lng'], shared['lnb'], shared['avec'], shared['wb'],
                 shared['wc'], shared['dsk'], shared['wglu'], shared['bglu'], shared['wout'], shared['wrc'],
                 shared['brt'])

        w1b, w3b, w2b = w1[l].astype(BF16), w3[l].astype(BF16), w2[l].astype(BF16)
        gfin = g_final[None]

        h0p = jnp.zeros((B, STATE_COLS), F32)
        x1p, route_p, hist_p, st_p = _mixer_prompt(xp, ada_p, h0p, wts_p)
        h0s = _pack_state(state_ssm_re[l].astype(F32), state_ssm_im[l].astype(F32))
        w_small = w_masked[:, :Ls, :Ls].reshape(-1)
        b_small = b_s[l][:, :Ls].reshape(-1)
        x1s, route_s, hist_s, st_s, v_s = _mixer_sample(xs_t, ada_s, h0s, w_small, b_small, wts_s)

        (tables_p,) = _dispatch_tables([(route_p, hist_p)])
        (tables_s,) = _dispatch_tables([(route_s[None], hist_s)])
        xp = _moe(x1p, ada_p[:, None, :], *tables_p, g2, gfin, w1b, w3b, w2b, n_tiles=B, chunk=256)
        xs_t = _moe(x1s, ada_s[None], *tables_s, g2, gfin, w1b, w3b, w2b, n_tiles=1, chunk=Bs)
        hr, hi = _unpack_state(st_p)
        p_re.append(hr.astype(state_ssm_re.dtype))
        p_im.append(hi.astype(state_ssm_im.dtype))
        hr, hi = _unpack_state(st_s)
        s_re.append(hr.astype(state_ssm_re.dtype))
        s_im.append(hi.astype(state_ssm_im.dtype))
        s_v.append(v_s.transpose(1, 0, 2))

    y_prompt = xp
    y_sample = xs_t.transpose(1, 0, 2)
    return (y_prompt, y_sample, jnp.stack(p_re), jnp.stack(p_im), jnp.stack(s_re), jnp.stack(s_im), jnp.stack(s_v))
```

```python
import functools

import jax
import jax.numpy as jnp
from jax import lax
from jax.experimental import pallas as pl
from jax.experimental.pallas import tpu as pltpu

F32 = jnp.float32
BF16 = jnp.bfloat16

D_MODEL = 1024
D_A = 512
D_B = 512
N_HEADS = 4
HEAD_DIM = 128
CHUNK = 128
N_SSM_GROUPS = 32
SSM_GROUP = 16
SSM_STATE = 64
N_SLABS = 4
SLAB_GROUPS = N_SSM_GROUPS // N_SLABS
SLAB_STATES = SLAB_GROUPS * SSM_STATE
SLAB_COLS = 2 * SLAB_STATES
STATE_COLS = N_SLABS * SLAB_COLS
N_EXPERT_GROUPS = 4
EXPERTS_PER_GROUP = 8
N_EXPERTS = 32
D_EXPERT = 256
EPS = 1e-6

LANES = 128
SUBLANES = 8
ROUTE_LANES = LANES
MOE_BLOCK = 128
MOE_BLOCK_SHIFT = MOE_BLOCK.bit_length() - 1
assert 1 << MOE_BLOCK_SHIFT == MOE_BLOCK
EXPERTS_PER_STEP = 4
MOE_EPILOGUE_STEPS = 4
PAIR_FIELDS = 4
S_PITCH = CHUNK + SUBLANES
SMEM_PAD = 1024
VMEM_LIMIT = 58 * 1024 * 1024


def _rms(xf, g):
    ms = jnp.mean(xf * xf, axis=-1, keepdims=True)
    return xf * lax.rsqrt(ms + EPS) * g


def _ada_kernel(cp_ref, cs_ref, w_ref, b_ref, op_ref, os_ref):
    w = w_ref[...].astype(BF16)
    for c_ref, o_ref in ((cp_ref, op_ref), (cs_ref, os_ref)):
        s = jax.nn.silu(c_ref[...]).astype(BF16)
        o_ref[...] = jnp.dot(s, w, preferred_element_type=F32) + b_ref[...]


def _ada(c_p, c_s, w, b):
    mp, ms = c_p.shape[0], c_s.shape[0]
    n = w.shape[1]
    bn = 1024
    return pl.pallas_call(
        _ada_kernel,
        grid=(n // bn,),
        in_specs=[pl.BlockSpec((mp, D_MODEL), lambda j: (0, 0)),
                  pl.BlockSpec((ms, D_MODEL), lambda j: (0, 0)),
                  pl.BlockSpec((D_MODEL, bn), lambda j: (0, j)),
                  pl.BlockSpec((1, bn), lambda j: (0, j))],
        out_specs=(pl.BlockSpec((mp, bn), lambda j: (0, j)), pl.BlockSpec((ms, bn), lambda j: (0, j))),
        out_shape=(jax.ShapeDtypeStruct((mp, n), F32), jax.ShapeDtypeStruct((ms, n), F32)),
        name="ada",
    )(c_p, c_s, w, b)


def _disc_kernel(lre_ref, lim_ref, ldt_ref, bre_ref, bim_ref, ar_ref, ai_ref, br_ref, bi_ref):
    dt = jnp.exp(ldt_ref[...])
    lr = lre_ref[...]
    li = lim_ref[...]
    mag = jnp.exp(lr * dt)
    ar = mag * jnp.cos(li * dt)
    ai = mag * jnp.sin(li * dt)
    den = lr * lr + li * li
    cr = ((ar - 1) * lr + ai * li) / den
    ci = (ai * lr - (ar - 1) * li) / den
    ar_ref[...] = ar
    ai_ref[...] = ai
    bre = bre_ref[...]
    bim = bim_ref[...]
    br_ref[...] = cr * bre - ci * bim
    bi_ref[...] = cr * bim + ci * bre


def _discretize(lam_re, lam_im, log_dt, b_re, b_im):
    g, p, h = b_re.shape
    o1 = jax.ShapeDtypeStruct((g, 1, p), F32)
    o2 = jax.ShapeDtypeStruct((g, h, p), F32)
    return pl.pallas_call(_disc_kernel, out_shape=(o1, o1, o2, o2), name="ssm_disc")(
        lam_re.reshape(g, 1, p), lam_im.reshape(g, 1, p), log_dt.reshape(g, 1, 1),
        b_re.transpose(0, 2, 1), b_im.transpose(0, 2, 1))


def _s_pieces(r0, nrows, grp, pitch):
    return [(i * grp, slice((r0 // grp + i) * pitch, (r0 // grp + i) * pitch + grp)) for i in range(nrows // grp)]


def _front(r0, nrows, h_scr, win_ref, lng_ref, lnb_ref, u_scr, vn_scr, s_scr, grp, pitch):
    rows = slice(r0, r0 + nrows)
    proj = jnp.dot(h_scr[rows, :], win_ref[...], preferred_element_type=F32)
    u_scr[rows, :] = jax.nn.gelu(proj[:, :D_A])
    vraw = jax.nn.gelu(proj[:, D_A:2 * D_A])
    for h in range(N_HEADS):
        cols = slice(h * HEAD_DIM, (h + 1) * HEAD_DIM)
        vh = vraw[:, cols]
        mu = jnp.mean(vh, axis=-1, keepdims=True)
        dv = vh - mu
        var = jnp.mean(dv * dv, axis=-1, keepdims=True)
        vn_scr[rows, cols] = dv * lax.rsqrt(var + EPS) * lng_ref[:, cols] + lnb_ref[:, cols]
    for k in range(N_SLABS):
        for off, prow in _s_pieces(r0, nrows, grp, pitch):
            s_scr[k, prow, :] = proj[off:off + grp, 2 * D_A + k * LANES:2 * D_A + (k + 1) * LANES]


def _scan_slab(bu, state, avec_ref, k, rows_per_step, steps):
    c_re = slice(0, SLAB_STATES)
    c_im = slice(SLAB_STATES, SLAB_COLS)
    s_re = slice(k * SLAB_STATES, (k + 1) * SLAB_STATES)
    s_im = slice(STATE_COLS // 2 + k * SLAB_STATES, STATE_COLS // 2 + (k + 1) * SLAB_STATES)
    ar = jnp.broadcast_to(avec_ref[:, s_re], (SUBLANES, SLAB_STATES))
    ai = jnp.broadcast_to(avec_ref[:, s_im], (SUBLANES, SLAB_STATES))
    for rc in range(rows_per_step // SUBLANES):
        r0 = rc * SUBLANES
        sr = state[r0:r0 + SUBLANES, s_re]
        si = state[r0:r0 + SUBLANES, s_im]
        for t in range(steps):
            rows = slice(t * rows_per_step + r0, t * rows_per_step + r0 + SUBLANES)
            nr = ar * sr - ai * si + bu[rows, c_re]
            ni = ar * si + ai * sr + bu[rows, c_im]
            bu[rows, c_re] = nr
            bu[rows, c_im] = ni
            sr, si = nr, ni
        state[r0:r0 + SUBLANES, s_re] = sr
        state[r0:r0 + SUBLANES, s_im] = si


def _s5(st_ref, bu_scr, yt_ref, state, avec_ref, wb_ref, wc_ref, rows_per_step, steps):
    for k in range(N_SLABS):
        bu = bu_scr.at[k % 2]
        bu[...] = jnp.dot(st_ref[k].astype(BF16), wb_ref[k], preferred_element_type=F32)
        _scan_slab(bu, state, avec_ref, k, rows_per_step, steps)
        yt_ref[k] = jnp.dot(bu[...].astype(BF16), wc_ref[k], preferred_element_type=F32)


def _back(r0, nrows, s_scr, ab_scr, wglu_ref, bglu_ref, wout_ref, grp, pitch):
    rows = slice(r0, r0 + nrows)
    pieces = _s_pieces(r0, nrows, grp, pitch)
    y = jax.nn.gelu(jnp.concatenate(
        [jnp.concatenate([s_scr[k, prow, :] for _, prow in pieces], axis=0) for k in range(N_SLABS)], axis=-1))
    gate = jnp.dot(y.astype(BF16), wglu_ref[...], preferred_element_type=F32) + bglu_ref[...]
    ab_scr[rows, D_A:] = (y * jax.nn.sigmoid(gate)).astype(BF16)
    return jnp.dot(ab_scr[rows, :], wout_ref[...], preferred_element_type=F32)


def _split_hi_lo(h2):
    hi = h2.astype(BF16)
    return hi, (h2 - hi.astype(F32)).astype(BF16)


def _route(hi, lo, wrc_ref, brt_ref):
    both = jnp.dot(hi, wrc_ref[...], preferred_element_type=F32)
    logits = (both[:, :ROUTE_LANES] + both[:, ROUTE_LANES:]
              + jnp.dot(lo, wrc_ref[:, :ROUTE_LANES], preferred_element_type=F32)) + brt_ref[...]
    n = logits.shape[0]
    lane = lax.broadcasted_iota(jnp.int32, (n, ROUTE_LANES), 1)
    lane_f = lane.astype(F32)
    big = jnp.float32(1e9)
    ninf = jnp.float32(-jnp.inf)
    is_g = lane < N_EXPERT_GROUPS
    gl = jnp.where(is_g, logits, ninf)
    gmax = jnp.max(gl, axis=-1, keepdims=True)
    gidx = jnp.min(jnp.where(gl == gmax, lane_f, big), axis=-1, keepdims=True)
    gsum = jnp.sum(jnp.where(is_g, jnp.exp(logits - gmax), 0.0), axis=-1, keepdims=True)
    g_w = 1.0 / gsum
    elo = N_EXPERT_GROUPS + EXPERTS_PER_GROUP * gidx
    emask = (lane_f >= elo) & (lane_f < elo + EXPERTS_PER_GROUP)
    el = jnp.where(emask, logits, ninf)
    t1 = jnp.max(el, axis=-1, keepdims=True)
    i1 = jnp.min(jnp.where(el == t1, lane_f, big), axis=-1, keepdims=True)
    el2 = jnp.where(lane_f == i1, ninf, el)
    t2 = jnp.max(el2, axis=-1, keepdims=True)
    i2 = jnp.min(jnp.where(el2 == t2, lane_f, big), axis=-1, keepdims=True)
    e21 = jnp.exp(t2 - t1)
    den = 1.0 + e21
    gate1 = g_w * (1.0 / den)
    gate2 = g_w * (e21 / den)
    e1 = i1 - N_EXPERT_GROUPS
    e2 = i2 - N_EXPERT_GROUPS
    packed = jnp.where(lane == 0, e1, jnp.where(lane == 1, e2, jnp.where(lane == 2, gate1,
                                                                           jnp.where(lane == 3, gate2, 0.0))))
    hist = jnp.sum(((lane_f == e1) | (lane_f == e2)).astype(F32), axis=0, keepdims=True)
    return packed.T[:SUBLANES], hist


def _mixer_prompt_kernel(x_ref, ada_ref, h0_ref, g1_ref, g2_ref, win_ref, lng_ref, lnb_ref, wsp_ref, bsp_ref,
                         avec_ref, wb_ref, wc_ref, dsk_ref, wglu_ref, bglu_ref, wout_ref, wrc_ref, brt_ref,
                         x1_ref, h2_ref, route_ref, hist_ref, state_ref,
                         h_scr, u_scr, vn_scr, s_scr, st_scr, yt_scr, bu_scr, ab_scr, hi_scr, lo_scr):
    nb = x_ref.shape[0]
    half = (nb // 2) * CHUNK
    D = D_MODEL
    pitch = S_PITCH
    step = pl.program_id(0)

    @pl.when(step == 0)
    def _():
        state_ref[...] = h0_ref[...]
        hist_ref[...] = jnp.zeros_like(hist_ref)
        hi_scr[...] = jnp.zeros_like(hi_scr)
        lo_scr[...] = jnp.zeros_like(lo_scr)

    def mod(b, i):
        return ada_ref[b:b + 1, i * D:(i + 1) * D]

    def route_chunk(chunk, weight):
        for b in range(nb):
            rows = slice(b * CHUNK, (b + 1) * CHUNK)
            route, hist = _route(hi_scr[rows, :], lo_scr[rows, :], wrc_ref, brt_ref)
            route_ref[b, chunk] = route
            hist_ref[b:b + 1, :] = hist_ref[b:b + 1, :] + weight * hist

    route_chunk(jnp.maximum(step - 1, 0), (step > 0).astype(F32))

    for b in range(nb):
        hb = _rms(x_ref[b], g1_ref[...]) * (1 + mod(b, 1)) + mod(b, 0)
        h_scr[b * CHUNK:(b + 1) * CHUNK, :] = hb.astype(BF16)

    for r0 in (0, half):
        _front(r0, half, h_scr, win_ref, lng_ref, lnb_ref, u_scr, vn_scr, s_scr, CHUNK, pitch)

    for b in range(nb):
        rows = slice(b * CHUNK, (b + 1) * CHUNK)
        for h in range(N_HEADS):
            cols = slice(h * HEAD_DIM, (h + 1) * HEAD_DIM)
            mixed = jnp.dot(wsp_ref[h], vn_scr[rows, cols].astype(BF16), preferred_element_type=F32) + bsp_ref[h]
            ab_scr[rows, cols] = (u_scr[rows, cols] * mixed).astype(BF16)

    for k in range(N_SLABS):
        for t in range(CHUNK):
            st_scr[k, t * nb:(t + 1) * nb, :] = s_scr[k, pl.ds(t, nb, stride=pitch), :]
    _s5(st_scr, bu_scr, yt_scr, state_ref, avec_ref, wb_ref, wc_ref, nb, CHUNK)
    for k in range(N_SLABS):
        dsk = dsk_ref[:, k * LANES:(k + 1) * LANES]
        for t in range(CHUNK):
            sel = pl.ds(t, nb, stride=pitch)
            s_scr[k, sel, :] = yt_scr[k, t * nb:(t + 1) * nb, :] + dsk * s_scr[k, sel, :]

    for r0 in (0, half):
        mix = _back(r0, half, s_scr, ab_scr, wglu_ref, bglu_ref, wout_ref, CHUNK, pitch)
        for bl in range(nb // 2):
            b = r0 // CHUNK + bl
            x1 = x_ref[b] + mod(b, 2) * mix[bl * CHUNK:(bl + 1) * CHUNK, :]
            x1_ref[b] = x1
            h2 = _rms(x1, g2_ref[...]) * (1 + mod(b, 4)) + mod(b, 3)
            hi, lo = _split_hi_lo(h2)
            hi_scr[b * CHUNK:(b + 1) * CHUNK, :] = hi
            h2_ref[b] = hi
            lo_scr[b * CHUNK:(b + 1) * CHUNK, :] = lo

    @pl.when(step == pl.num_programs(0) - 1)
    def _():
        route_chunk(step, 1.0)


def _const_spec(shape):
    nd = len(shape)
    return pl.BlockSpec(shape, lambda *_: (0,) * nd, pipeline_mode=pl.Buffered(1))


def _mixer_prompt(x, ada, h0, wts):
    nb, seq, D = x.shape
    n_chunks = seq // CHUNK
    R = nb * CHUNK
    weight_specs = [_const_spec(w.shape) for w in wts]
    in_specs = [pl.BlockSpec((nb, CHUNK, D), lambda i: (0, i, 0)),
                _const_spec(ada.shape), _const_spec(h0.shape)] + weight_specs
    out_shape = (jax.ShapeDtypeStruct((nb, seq, D), F32),
                 jax.ShapeDtypeStruct((nb, seq, D), BF16),
                 jax.ShapeDtypeStruct((nb, n_chunks, SUBLANES, ROUTE_LANES), F32),
                 jax.ShapeDtypeStruct((nb, ROUTE_LANES), F32),
                 jax.ShapeDtypeStruct((nb, STATE_COLS), F32))
    out_specs = (pl.BlockSpec((nb, CHUNK, D), lambda i: (0, i, 0)),
                 pl.BlockSpec((nb, CHUNK, D), lambda i: (0, i, 0)),
                 pl.BlockSpec((nb, n_chunks, SUBLANES, ROUTE_LANES), lambda i: (0, 0, 0, 0)),
                 pl.BlockSpec((nb, ROUTE_LANES), lambda i: (0, 0)),
                 pl.BlockSpec((nb, STATE_COLS), lambda i: (0, 0)))
    scratch = [pltpu.VMEM((R, D), BF16),
               pltpu.VMEM((R, D_A), F32),
               pltpu.VMEM((R, D_A), F32),
               pltpu.VMEM((N_SLABS, nb * S_PITCH, LANES), F32),
               pltpu.VMEM((N_SLABS, R, LANES), F32),
               pltpu.VMEM((N_SLABS, R, LANES), F32),
               pltpu.VMEM((2, R, SLAB_COLS), F32),
               pltpu.VMEM((R, D), BF16),
               pltpu.VMEM((R, D), BF16),
               pltpu.VMEM((R, D), BF16)]
    return pl.pallas_call(
        _mixer_prompt_kernel,
        grid=(n_chunks,),
        in_specs=in_specs,
        out_specs=out_specs,
        out_shape=out_shape,
        scratch_shapes=scratch,
        compiler_params=pltpu.CompilerParams(dimension_semantics=("arbitrary",), vmem_limit_bytes=VMEM_LIMIT),
        name="mixer_prompt",
    )(x, ada, h0, *wts)


def _mixer_sample_kernel(wsm_ref, bsm_ref, x_ref, ada_ref, h0_ref, g1_ref, g2_ref, win_ref, lng_ref, lnb_ref,
                         avec_ref, wb_ref, wc_ref, dsk_ref, wglu_ref, bglu_ref, wout_ref, wrc_ref, brt_ref,
                         x1_ref, h2_ref, route_ref, hist_ref, state_ref, v_ref,
                         h_scr, u_scr, vn_scr, s_scr, yt_scr, bu_scr, ab_scr):
    T, nb, D = x_ref.shape
    R = T * nb
    half = R // 2

    def mod(i):
        return ada_ref[:, i * D:(i + 1) * D]

    state_ref[...] = h0_ref[...]
    for t in range(T):
        ht = _rms(x_ref[t], g1_ref[...]) * (1 + mod(1)) + mod(0)
        h_scr[t * nb:(t + 1) * nb, :] = ht.astype(BF16)

    for r0 in (0, half):
        _front(r0, half, h_scr, win_ref, lng_ref, lnb_ref, u_scr, vn_scr, s_scr, half, half)

    for t in range(T):
        rows = slice(t * nb, (t + 1) * nb)
        v_ref[t] = vn_scr[rows, :]
        for h in range(N_HEADS):
            cols = slice(h * HEAD_DIM, (h + 1) * HEAD_DIM)
            acc = jnp.full((nb, HEAD_DIM), bsm_ref[h * T + t], F32)
            for s in range(t + 1):
                acc = acc + wsm_ref[(h * T + t) * T + s] * vn_scr[s * nb:(s + 1) * nb, cols]
            ab_scr[rows, cols] = (u_scr[rows, cols] * acc).astype(BF16)

    _s5(s_scr, bu_scr, yt_scr, state_ref, avec_ref, wb_ref, wc_ref, nb, T)
    for k in range(N_SLABS):
        s_scr[k] = yt_scr[k] + dsk_ref[:, k * LANES:(k + 1) * LANES] * s_scr[k]

    hist_total = jnp.zeros((1, ROUTE_LANES), F32)
    for r0 in (0, half):
        mix = _back(r0, half, s_scr, ab_scr, wglu_ref, bglu_ref, wout_ref, half, half)
        for tl in range(T // 2):
            t = r0 // nb + tl
            x1 = x_ref[t] + mod(2) * mix[tl * nb:(tl + 1) * nb, :]
            x1_ref[t] = x1
            h2 = _rms(x1, g2_ref[...]) * (1 + mod(4)) + mod(3)
            hi, lo = _split_hi_lo(h2)
            h2_ref[t] = hi
            route, hist = _route(hi, lo, wrc_ref, brt_ref)
            route_ref[t] = route
            hist_total = hist_total + hist
    hist_ref[...] = hist_total


def _mixer_sample(x_t, ada, h0, w_small, b_small, wts):
    T, nb, D = x_t.shape
    R = T * nb
    smem = pl.BlockSpec(memory_space=pltpu.SMEM)
    out_shape = (jax.ShapeDtypeStruct((T, nb, D), F32),
                 jax.ShapeDtypeStruct((T, nb, D), BF16),
                 jax.ShapeDtypeStruct((T, SUBLANES, ROUTE_LANES), F32),
                 jax.ShapeDtypeStruct((1, ROUTE_LANES), F32),
                 jax.ShapeDtypeStruct((nb, STATE_COLS), F32),
                 jax.ShapeDtypeStruct((T, nb, D_A), F32))
    scratch = [pltpu.VMEM((R, D), BF16),
               pltpu.VMEM((R, D_A), F32),
               pltpu.VMEM((R, D_A), F32),
               pltpu.VMEM((N_SLABS, R, LANES), F32),
               pltpu.VMEM((N_SLABS, R, LANES), F32),
               pltpu.VMEM((2, R, SLAB_COLS), F32),
               pltpu.VMEM((R, D), BF16)]
    vmem = pl.BlockSpec(memory_space=pltpu.VMEM)
    return pl.pallas_call(
        _mixer_sample_kernel,
        in_specs=[smem, smem] + [vmem] * (3 + len(wts)),
        out_specs=(vmem,) * 6,
        out_shape=out_shape,
        scratch_shapes=scratch,
        compiler_params=pltpu.CompilerParams(vmem_limit_bytes=VMEM_LIMIT),
        name="mixer_sample",
    )(w_small, b_small, x_t, ada, h0, *wts)


def _rows_to_tiles(tiles_ref, row0, val):
    n = val.shape[0]
    for c in range(val.shape[1] // LANES):
        tiles_ref[pl.ds(row0 * SUBLANES + c, n, stride=SUBLANES), :] = val[:, c * LANES:(c + 1) * LANES]


def _tiles_to_rows(tiles_ref, row0, n):
    return jnp.concatenate([tiles_ref[pl.ds(row0 * SUBLANES + c, n, stride=SUBLANES), :] for c in range(SUBLANES)],
                           axis=-1)


def _moe_kernel(starts_ref, tok_ref, pos_ref, gate_ref, h2_ref, x1_ref, ada_ref, gfin_ref, w1_ref, w3_ref, w2_ref,
                out_ref, h_tiles, y_tiles, xb_even, xb_odd, cp_smem, tab_smem, *, chunk):
    t = pl.program_id(0)
    e = pl.program_id(1)
    n_expert_steps = N_EXPERTS // EXPERTS_PER_STEP
    ngrp, rows_g, D = h2_ref.shape
    n_tok = ngrp * rows_g
    gb, rb, _ = x1_ref.shape
    epi_tok = gb * rb
    mrows = ada_ref.shape[1]

    def tile_of(row):
        return pl.ds(pl.multiple_of(row * SUBLANES, SUBLANES), SUBLANES)

    def tile_at(row8):
        return pl.ds(pl.multiple_of(row8, SUBLANES), SUBLANES)

    @pl.when(e == 0)
    def _():
        for gi in range(ngrp):
            for ci in range(rows_g // chunk):
                h2 = h2_ref[gi, ci * chunk:(ci + 1) * chunk, :].astype(F32)
                _rows_to_tiles(h_tiles, gi * rows_g + ci * chunk, h2)

    @pl.when(e == 0)
    def _():
        gp = jnp.int32(0)
        for s in range(N_EXPERTS // EXPERTS_PER_STEP):
            cp_smem[s] = gp
            st = [starts_ref[t, s * EXPERTS_PER_STEP + i] for i in range(EXPERTS_PER_STEP + 1)]
            first = [jnp.int32(0)]
            for i in range(EXPERTS_PER_STEP):
                first.append(first[i] + lax.shift_right_logical(st[i + 1] - st[i] + (MOE_BLOCK - 1), MOE_BLOCK_SHIFT))
            nb = first[EXPERTS_PER_STEP]

            def locate(f, st=st, first=first):
                el, fb, sb = jnp.int32(0), first[0], st[0]
                for i in range(1, EXPERTS_PER_STEP):
                    hit = f >= first[i]
                    el = jnp.where(hit, i, el)
                    fb = jnp.where(hit, first[i], fb)
                    sb = jnp.where(hit, st[i], sb)
                return sb + (f - fb) * MOE_BLOCK, el

            def add_pair(p, gp, nb=nb, locate=locate):
                base_a, el_a = locate(2 * p)
                base_b, el_b = locate(jnp.minimum(2 * p + 1, nb - 1))
                for i, v in enumerate((base_a, base_b, el_a, el_b)):
                    tab_smem[PAIR_FIELDS * gp + i] = v
                return gp + 1

            gp = lax.fori_loop(0, lax.shift_right_logical(nb + 1, 1), add_pair, gp)
        cp_smem[N_EXPERTS // EXPERTS_PER_STEP] = gp
        for i in range(PAIR_FIELDS):
            tab_smem[PAIR_FIELDS * gp + i] = jnp.int32(0)

    def pair_entry(gp):
        return [tab_smem[PAIR_FIELDS * gp + i] for i in range(PAIR_FIELDS)]

    @pl.when(e == 0)
    def _():
        base_a, base_b, _, _ = pair_entry(0)

        def gather8(j8, c):
            for jj in range(SUBLANES):
                j = j8 * SUBLANES + jj
                xb_even[tile_of(j), :] = h_tiles[tile_at(tok_ref[base_a + j]), :]
                xb_even[tile_of(MOE_BLOCK + j), :] = h_tiles[tile_at(tok_ref[base_b + j]), :]
            return c

        lax.fori_loop(0, MOE_BLOCK // SUBLANES, gather8, 0)

    def pair_body(gp, cur_tiles, next_tiles):
        next_a, next_b, _, _ = pair_entry(gp + 1)
        n_slices = 8
        per = 2 * MOE_BLOCK // n_slices
        slices = iter(range(n_slices))

        def gather_slice():
            s = next(slices)
            for r in range(s * per, (s + 1) * per):
                tok8 = tok_ref[next_a + r] if r < MOE_BLOCK else tok_ref[next_b + r - MOE_BLOCK]
                next_tiles[r * SUBLANES:(r + 1) * SUBLANES, :] = h_tiles[tile_at(tok8), :]

        base_a, base_b, el_a, el_b = pair_entry(gp)
        xa = _tiles_to_rows(cur_tiles, 0, MOE_BLOCK).astype(BF16)
        gather_slice()
        a1 = jnp.dot(xa, w1_ref[el_a], preferred_element_type=F32)
        gather_slice()
        a3 = jnp.dot(xa, w3_ref[el_a], preferred_element_type=F32)
        gather_slice()
        xb = _tiles_to_rows(cur_tiles, MOE_BLOCK, MOE_BLOCK).astype(BF16)
        b1 = jnp.dot(xb, w1_ref[el_b], preferred_element_type=F32)
        gather_slice()
        b3 = jnp.dot(xb, w3_ref[el_b], preferred_element_type=F32)
        gather_slice()
        ya = jnp.dot((jax.nn.silu(a1) * a3).astype(BF16), w2_ref[el_a], preferred_element_type=F32)
        gather_slice()
        yb = jnp.dot((jax.nn.silu(b1) * b3).astype(BF16), w2_ref[el_b], preferred_element_type=F32)
        gather_slice()
        _rows_to_tiles(y_tiles, base_a, ya)
        gather_slice()
        _rows_to_tiles(y_tiles, base_b, yb)

    def pair(gp, carry):
        @pl.when((gp & 1) == 0)
        def _():
            pair_body(gp, xb_even, xb_odd)

        @pl.when((gp & 1) == 1)
        def _():
            pair_body(gp, xb_odd, xb_even)

        return carry

    @pl.when(e < n_expert_steps)
    def _():
        lax.fori_loop(cp_smem[e], cp_smem[e + 1], pair, 0)

    @pl.when(e >= n_expert_steps)
    def _():
        tok0 = (e - n_expert_steps) * epi_tok

        def combine8(t8, c):
            for tt in range(SUBLANES):
                tok = tok0 + t8 * SUBLANES + tt
                y0 = y_tiles[tile_at(pos_ref[tok]), :]
                y1 = y_tiles[tile_at(pos_ref[n_tok + tok]), :]
                h_tiles[tile_of(tok), :] = gate_ref[tok] * y0 + gate_ref[n_tok + tok] * y1
            return c

        lax.fori_loop(0, epi_tok // SUBLANES, combine8, 0)
        sub = min(rb, chunk)
        for gi in range(gb):
            for ci in range(rb // sub):
                rows = slice(ci * sub, (ci + 1) * sub)
                gt2 = ada_ref[0, :, 5 * D:6 * D] if mrows == 1 else ada_ref[0, rows, 5 * D:6 * D]
                moe = _tiles_to_rows(h_tiles, tok0 + gi * rb + ci * sub, sub)
                x2 = x1_ref[gi, rows, :] + gt2 * moe
                out_ref[gi, rows, :] = _rms(x2, gfin_ref[...])


def _moe(h2, x1, ada, starts, tok, pos, gate, gfin, w1, w3, w2, *, n_tiles, chunk):
    ngrp = x1.shape[0] // n_tiles
    rows_g, D = x1.shape[1], x1.shape[2]
    mrows = ada.shape[1]
    Tt = ngrp * rows_g
    n_expert_steps = N_EXPERTS // EXPERTS_PER_STEP
    epi_tok = Tt // MOE_EPILOGUE_STEPS
    if ngrp == 1:
        epi_block = (1, epi_tok, D)

        def epi_map(t, e, st):
            return (t, jnp.maximum(e - n_expert_steps, 0), 0)
    else:
        assert n_tiles == 1 and epi_tok % rows_g == 0
        epi_block = (epi_tok // rows_g, rows_g, D)

        def epi_map(t, e, st):
            return (jnp.maximum(e - n_expert_steps, 0), 0, 0)

    def weight_map(t, e, st):
        return (jnp.minimum(e, n_expert_steps - 1), 0, 0)
    lp = tok.shape[0] // n_tiles
    max_pairs = (2 * Tt // MOE_BLOCK + N_EXPERTS + N_EXPERTS // EXPERTS_PER_STEP) // 2 + 1
    grid_spec = pltpu.PrefetchScalarGridSpec(
        num_scalar_prefetch=1,
        grid=(n_tiles, n_expert_steps + MOE_EPILOGUE_STEPS),
        in_specs=[
            pl.BlockSpec((lp,), lambda t, e, st: (t,), memory_space=pltpu.SMEM),
            pl.BlockSpec((2 * Tt,), lambda t, e, st: (t,), memory_space=pltpu.SMEM),
            pl.BlockSpec((2 * Tt,), lambda t, e, st: (t,), memory_space=pltpu.SMEM),
            pl.BlockSpec((ngrp, rows_g, D), lambda t, e, st: (t, 0, 0)),
            pl.BlockSpec(epi_block, epi_map),
            pl.BlockSpec((1, mrows, 6 * D), lambda t, e, st: (t, 0, 0)),
            pl.BlockSpec((1, D), lambda t, e, st: (0, 0)),
            pl.BlockSpec((EXPERTS_PER_STEP, D, D_EXPERT), weight_map),
            pl.BlockSpec((EXPERTS_PER_STEP, D, D_EXPERT), weight_map),
            pl.BlockSpec((EXPERTS_PER_STEP, D_EXPERT, D), weight_map),
        ],
        out_specs=pl.BlockSpec(epi_block, epi_map),
        scratch_shapes=[pltpu.VMEM((Tt * SUBLANES, LANES), F32),
                        pltpu.VMEM(((2 * Tt + MOE_BLOCK) * SUBLANES, LANES), F32),
                        pltpu.VMEM((2 * MOE_BLOCK * SUBLANES, LANES), F32),
                        pltpu.VMEM((2 * MOE_BLOCK * SUBLANES, LANES), F32),
                        pltpu.SMEM((N_EXPERTS // EXPERTS_PER_STEP + 1,), jnp.int32),
                        pltpu.SMEM((max_pairs * PAIR_FIELDS,), jnp.int32)],
    )
    return pl.pallas_call(
        functools.partial(_moe_kernel, chunk=chunk),
        grid_spec=grid_spec,
        out_shape=jax.ShapeDtypeStruct(x1.shape, F32),
        compiler_params=pltpu.CompilerParams(dimension_semantics=("arbitrary", "arbitrary"),
                                             vmem_limit_bytes=VMEM_LIMIT),
        name="moe",
    )(starts, tok, pos, gate, h2, x1, ada, gfin, w1, w3, w2)


def _dispatch_tables(routes):
    tile_tokens = [r.shape[1] * ROUTE_LANES for r, _ in routes]
    t_max = max(tile_tokens)
    es, gs, masks = [], [], []
    for (route, _), Tt in zip(routes, tile_tokens):
        assert Tt & (Tt - 1) == 0
        n = route.shape[0]
        by_k = route[:, :, :4, :].transpose(0, 2, 1, 3).reshape(n, 4, Tt)
        extra = 2 * (t_max - Tt)
        es.append(jnp.pad(by_k[:, :2].astype(jnp.int32).reshape(n, 2 * Tt), ((0, 0), (0, extra)),
                          constant_values=N_EXPERTS))
        gs.append(jnp.pad(by_k[:, 2:].reshape(n, 2 * Tt), ((0, 0), (0, extra))))
        masks.append(jnp.full((n, 1), Tt - 1, jnp.int32))
    flat_e = jnp.concatenate(es, axis=0)
    flat_g = jnp.concatenate(gs, axis=0)
    n_tiles = flat_e.shape[0]
    order = jnp.argsort(flat_e, axis=-1, stable=True).astype(jnp.int32)
    pos = jnp.argsort(order, axis=-1).astype(jnp.int32) * SUBLANES
    tok_s = (order & jnp.concatenate(masks, axis=0)) * SUBLANES
    counts = jnp.concatenate([h[:, :N_EXPERTS] for _, h in routes], axis=0).astype(jnp.int32)
    starts = jnp.concatenate([jnp.zeros((n_tiles, 1), jnp.int32), jnp.cumsum(counts, axis=-1, dtype=jnp.int32)],
                             axis=-1)

    def tok_len(Tt):
        return -(-(2 * Tt + MOE_BLOCK) // SMEM_PAD) * SMEM_PAD

    tok_p = jnp.pad(tok_s, ((0, 0), (0, tok_len(t_max) - 2 * t_max)))
    Tt = t_max


    out, r0 = [], 0
    for (route, _), Tt in zip(routes, tile_tokens):
        r1 = r0 + route.shape[0]
        out.append((starts[r0:r1], tok_p[r0:r1, :tok_len(Tt)].reshape(-1),
                    pos[r0:r1, :2 * Tt].reshape(-1), flat_g[r0:r1, :2 * Tt].reshape(-1)))
        r0 = r1
    return out


def _pack_state(h_re, h_im):
    b = h_re.shape[0]
    return jnp.concatenate([h_re.reshape(b, STATE_COLS // 2), h_im.reshape(b, STATE_COLS // 2)], axis=-1)


def _unpack_state(st):
    b = st.shape[0]
    return (st[:, :STATE_COLS // 2].reshape(b, N_SSM_GROUPS, SSM_STATE),
            st[:, STATE_COLS // 2:].reshape(b, N_SSM_GROUPS, SSM_STATE))


def kernel(x_prompt, x_sample, state_ssm_re, state_ssm_im, c_prompt, c_sample, w_ada, b_ada, g_norm1, g_norm2, w_in, ln_g, ln_b, w_s, b_s, lam_re, lam_im, log_dt, ssm_b_re, ssm_b_im, ssm_c_re, ssm_c_im, ssm_d, w_glu, b_glu, w_out, w_group, b_group, w_expert, b_expert, w1, w3, w2, g_final):
    depth = w_ada.shape[0]
    assert depth == 1, "the final RMSNorm is fused into the (single) layer's MoE epilogue"
    B, L, D = x_prompt.shape
    Bs, Ls, _ = x_sample.shape
    xp = x_prompt
    xs_t = x_sample.transpose(1, 0, 2)
    eye = jnp.eye(SLAB_GROUPS, dtype=F32)
    tril = jnp.tril(jnp.ones((CHUNK, CHUNK), dtype=bool))
    p_re, p_im, s_re, s_im, s_v = [], [], [], [], []
    for l in range(depth):
        ada_p, ada_s = _ada(c_prompt, c_sample, w_ada[l], b_ada[l][None])

        ar, ai, br, bi = _discretize(lam_re[l], lam_im[l], log_dt[l], ssm_b_re[l], ssm_b_im[l])
        avec = jnp.concatenate([ar.reshape(1, STATE_COLS // 2), ai.reshape(1, STATE_COLS // 2)], axis=-1)

        def blockdiag_in(w):
            w4 = w.reshape(N_SLABS, SLAB_GROUPS, SSM_GROUP, SSM_STATE)
            return jnp.einsum('kghp,gG->kghGp', w4, eye).reshape(N_SLABS, LANES, SLAB_STATES)

        def blockdiag_out(w):
            w4 = w.reshape(N_SLABS, SLAB_GROUPS, SSM_GROUP, SSM_STATE)
            return jnp.einsum('kghp,gG->kgpGh', w4, eye).reshape(N_SLABS, SLAB_STATES, LANES)

        wb = jnp.concatenate([blockdiag_in(br), blockdiag_in(bi)], axis=-1).astype(BF16)
        wc = jnp.concatenate([blockdiag_out(ssm_c_re[l]), -blockdiag_out(ssm_c_im[l])], axis=1).astype(BF16)

        lane_pad = ROUTE_LANES - N_EXPERT_GROUPS - N_EXPERTS
        wr = jnp.pad(jnp.concatenate([w_group[l], w_expert[l]], axis=1), ((0, 0), (0, lane_pad)))
        wr_hi = wr.astype(BF16)
        wr_lo = (wr - wr_hi.astype(F32)).astype(BF16)
        br_t = jnp.pad(jnp.concatenate([b_group[l], b_expert[l]]), (0, lane_pad))[None]

        g1 = g_norm1[l][None]
        g2 = g_norm2[l][None]
        shared = dict(
            win=w_in[l].astype(BF16), lng=ln_g[l].reshape(1, D_A), lnb=ln_b[l].reshape(1, D_A),
            avec=avec, wb=wb, wc=wc, dsk=ssm_d[l].reshape(1, D_B), wglu=w_glu[l].astype(BF16),
            bglu=b_glu[l][None], wout=w_out[l].astype(BF16), wrc=jnp.concatenate([wr_hi, wr_lo], axis=1), brt=br_t)
        w_masked = jnp.where(tril[None], w_s[l], jnp.zeros_like(w_s[l]))
        wsp = w_masked.astype(BF16)
        bsp = jnp.broadcast_to(b_s[l][:, :, None], (N_HEADS, CHUNK, HEAD_DIM))
        wts_p = (g1, g2, shared['win'], shared['lng'], shared['lnb'], wsp, bsp, shared['avec'], shared['wb'],
                 shared['wc'], shared['dsk'], shared['wglu'], shared['bglu'], shared['wout'], shared['wrc'],
                 shared['brt'])
        wts_s = (g1, g2, shared['win'], shared['lng'], shared['lnb'], shared['avec'], shared['wb'],
                 shared['wc'], shared['dsk'], shared['wglu'], shared['bglu'], shared['wout'], shared['wrc'],
                 shared['brt'])

        w1b, w3b, w2b = w1[l].astype(BF16), w3[l].astype(BF16), w2[l].astype(BF16)
        gfin = g_final[None]

        h0p = jnp.zeros((B, STATE_COLS), F32)
        x1p, h2p, route_p, hist_p, st_p = _mixer_prompt(xp, ada_p, h0p, wts_p)
        h0s = _pack_state(state_ssm_re[l].astype(F32), state_ssm_im[l].astype(F32))
        w_small = w_masked[:, :Ls, :Ls].reshape(-1)
        b_small = b_s[l][:, :Ls].reshape(-1)
        x1s, h2s, route_s, hist_s, st_s, v_s = _mixer_sample(xs_t, ada_s, h0s, w_small, b_small, wts_s)

        (tables_p,) = _dispatch_tables([(route_p, hist_p)])
        (tables_s,) = _dispatch_tables([(route_s[None], hist_s)])
        xp = _moe(h2p, x1p, ada_p[:, None, :], *tables_p, gfin, w1b, w3b, w2b, n_tiles=B, chunk=256)
        xs_t = _moe(h2s, x1s, ada_s[None], *tables_s, gfin, w1b, w3b, w2b, n_tiles=1, chunk=Bs)
        hr, hi = _unpack_state(st_p)
        p_re.append(hr.astype(state_ssm_re.dtype))
        p_im.append(hi.astype(state_ssm_im.dtype))
        hr, hi = _unpack_state(st_s)
        s_re.append(hr.astype(state_ssm_re.dtype))
        s_im.append(hi.astype(state_ssm_im.dtype))
        s_v.append(v_s.transpose(1, 0, 2))

    y_prompt = xp
    y_sample = xs_t.transpose(1, 0, 2)
    return (y_prompt, y_sample, jnp.stack(p_re), jnp.stack(p_im), jnp.stack(s_re), jnp.stack(s_im), jnp.stack(s_v))
```

```python
import functools

import jax
import jax.numpy as jnp
from jax import lax
from jax.experimental import pallas as pl
from jax.experimental.pallas import tpu as pltpu

F32 = jnp.float32
BF16 = jnp.bfloat16

D_MODEL = 1024
D_A = 512
D_B = 512
N_HEADS = 4
HEAD_DIM = 128
CHUNK = 128
N_SSM_GROUPS = 32
SSM_GROUP = 16
SSM_STATE = 64
N_SLABS = 4
SLAB_GROUPS = N_SSM_GROUPS // N_SLABS
SLAB_STATES = SLAB_GROUPS * SSM_STATE
SLAB_COLS = 2 * SLAB_STATES
STATE_COLS = N_SLABS * SLAB_COLS
N_EXPERT_GROUPS = 4
EXPERTS_PER_GROUP = 8
N_EXPERTS = 32
D_EXPERT = 256
EPS = 1e-6

LANES = 128
SUBLANES = 8
ROUTE_LANES = LANES
MOE_BLOCK = 128
MOE_BLOCK_SHIFT = MOE_BLOCK.bit_length() - 1
assert 1 << MOE_BLOCK_SHIFT == MOE_BLOCK
EXPERTS_PER_STEP = 4
MOE_EPILOGUE_STEPS = 4
PAIR_FIELDS = 4
S_PITCH = CHUNK + SUBLANES
SMEM_PAD = 1024
VMEM_LIMIT = 58 * 1024 * 1024


def _rms(xf, g):
    ms = jnp.mean(xf * xf, axis=-1, keepdims=True)
    return xf * lax.rsqrt(ms + EPS) * g


def _ada_kernel(cp_ref, cs_ref, w_ref, b_ref, op_ref, os_ref):
    w = w_ref[...].astype(BF16)
    for c_ref, o_ref in ((cp_ref, op_ref), (cs_ref, os_ref)):
        s = jax.nn.silu(c_ref[...]).astype(BF16)
        o_ref[...] = jnp.dot(s, w, preferred_element_type=F32) + b_ref[...]


def _ada(c_p, c_s, w, b):
    mp, ms = c_p.shape[0], c_s.shape[0]
    n = w.shape[1]
    bn = 1024
    return pl.pallas_call(
        _ada_kernel,
        grid=(n // bn,),
        in_specs=[pl.BlockSpec((mp, D_MODEL), lambda j: (0, 0)),
                  pl.BlockSpec((ms, D_MODEL), lambda j: (0, 0)),
                  pl.BlockSpec((D_MODEL, bn), lambda j: (0, j)),
                  pl.BlockSpec((1, bn), lambda j: (0, j))],
        out_specs=(pl.BlockSpec((mp, bn), lambda j: (0, j)), pl.BlockSpec((ms, bn), lambda j: (0, j))),
        out_shape=(jax.ShapeDtypeStruct((mp, n), F32), jax.ShapeDtypeStruct((ms, n), F32)),
        name="ada",
    )(c_p, c_s, w, b)


def _disc_kernel(lre_ref, lim_ref, ldt_ref, bre_ref, bim_ref, ar_ref, ai_ref, br_ref, bi_ref):
    dt = jnp.exp(ldt_ref[...])
    lr = lre_ref[...]
    li = lim_ref[...]
    mag = jnp.exp(lr * dt)
    ar = mag * jnp.cos(li * dt)
    ai = mag * jnp.sin(li * dt)
    den = lr * lr + li * li
    cr = ((ar - 1) * lr + ai * li) / den
    ci = (ai * lr - (ar - 1) * li) / den
    ar_ref[...] = ar
    ai_ref[...] = ai
    bre = bre_ref[...]
    bim = bim_ref[...]
    br_ref[...] = cr * bre - ci * bim
    bi_ref[...] = cr * bim + ci * bre


def _discretize(lam_re, lam_im, log_dt, b_re, b_im):
    g, p, h = b_re.shape
    o1 = jax.ShapeDtypeStruct((g, 1, p), F32)
    o2 = jax.ShapeDtypeStruct((g, h, p), F32)
    return pl.pallas_call(_disc_kernel, out_shape=(o1, o1, o2, o2), name="ssm_disc")(
        lam_re.reshape(g, 1, p), lam_im.reshape(g, 1, p), log_dt.reshape(g, 1, 1),
        b_re.transpose(0, 2, 1), b_im.transpose(0, 2, 1))


def _s_pieces(r0, nrows, grp, pitch):
    return [(i * grp, slice((r0 // grp + i) * pitch, (r0 // grp + i) * pitch + grp)) for i in range(nrows // grp)]


def _front(r0, nrows, h_scr, win_ref, lng_ref, lnb_ref, u_scr, vn_scr, s_scr, grp, pitch):
    rows = slice(r0, r0 + nrows)
    proj = jnp.dot(h_scr[rows, :], win_ref[...], preferred_element_type=F32)
    u_scr[rows, :] = jax.nn.gelu(proj[:, :D_A])
    vraw = jax.nn.gelu(proj[:, D_A:2 * D_A])
    for h in range(N_HEADS):
        cols = slice(h * HEAD_DIM, (h + 1) * HEAD_DIM)
        vh = vraw[:, cols]
        mu = jnp.mean(vh, axis=-1, keepdims=True)
        dv = vh - mu
        var = jnp.mean(dv * dv, axis=-1, keepdims=True)
        vn_scr[rows, cols] = dv * lax.rsqrt(var + EPS) * lng_ref[:, cols] + lnb_ref[:, cols]
    for k in range(N_SLABS):
        for off, prow in _s_pieces(r0, nrows, grp, pitch):
            s_scr[k, prow, :] = proj[off:off + grp, 2 * D_A + k * LANES:2 * D_A + (k + 1) * LANES]


def _scan_slab(bu, state, avec_ref, k, rows_per_step, steps):
    c_re = slice(0, SLAB_STATES)
    c_im = slice(SLAB_STATES, SLAB_COLS)
    s_re = slice(k * SLAB_STATES, (k + 1) * SLAB_STATES)
    s_im = slice(STATE_COLS // 2 + k * SLAB_STATES, STATE_COLS // 2 + (k + 1) * SLAB_STATES)
    ar = jnp.broadcast_to(avec_ref[:, s_re], (SUBLANES, SLAB_STATES))
    ai = jnp.broadcast_to(avec_ref[:, s_im], (SUBLANES, SLAB_STATES))
    for rc in range(rows_per_step // SUBLANES):
        r0 = rc * SUBLANES
        sr = state[r0:r0 + SUBLANES, s_re]
        si = state[r0:r0 + SUBLANES, s_im]
        for t in range(steps):
            rows = slice(t * rows_per_step + r0, t * rows_per_step + r0 + SUBLANES)
            nr = ar * sr - ai * si + bu[rows, c_re]
            ni = ar * si + ai * sr + bu[rows, c_im]
            bu[rows, c_re] = nr
            bu[rows, c_im] = ni
            sr, si = nr, ni
        state[r0:r0 + SUBLANES, s_re] = sr
        state[r0:r0 + SUBLANES, s_im] = si


def _s5(st_ref, bu_scr, yt_ref, state, avec_ref, wb_ref, wc_ref, rows_per_step, steps):
    for k in range(N_SLABS):
        bu = bu_scr.at[k % 2]
        bu[...] = jnp.dot(st_ref[k].astype(BF16), wb_ref[k], preferred_element_type=F32)
        _scan_slab(bu, state, avec_ref, k, rows_per_step, steps)
        yt_ref[k] = jnp.dot(bu[...].astype(BF16), wc_ref[k], preferred_element_type=F32)


def _back(r0, nrows, s_scr, ab_scr, wglu_ref, bglu_ref, wout_ref, grp, pitch):
    rows = slice(r0, r0 + nrows)
    pieces = _s_pieces(r0, nrows, grp, pitch)
    y = jax.nn.gelu(jnp.concatenate(
        [jnp.concatenate([s_scr[k, prow, :] for _, prow in pieces], axis=0) for k in range(N_SLABS)], axis=-1))
    gate = jnp.dot(y.astype(BF16), wglu_ref[...], preferred_element_type=F32) + bglu_ref[...]
    ab_scr[rows, D_A:] = (y * jax.nn.sigmoid(gate)).astype(BF16)
    return jnp.dot(ab_scr[rows, :], wout_ref[...], preferred_element_type=F32)


def _split_hi_lo(h2):
    hi = h2.astype(BF16)
    return hi, (h2 - hi.astype(F32)).astype(BF16)


def _route(hi, lo, wrc_ref, brt_ref):
    both = jnp.dot(hi, wrc_ref[...], preferred_element_type=F32)
    logits = (both[:, :ROUTE_LANES] + both[:, ROUTE_LANES:]
              + jnp.dot(lo, wrc_ref[:, :ROUTE_LANES], preferred_element_type=F32)) + brt_ref[...]
    n = logits.shape[0]
    lane = lax.broadcasted_iota(jnp.int32, (n, ROUTE_LANES), 1)
    lane_f = lane.astype(F32)
    big = jnp.float32(1e9)
    ninf = jnp.float32(-jnp.inf)
    is_g = lane < N_EXPERT_GROUPS
    gl = jnp.where(is_g, logits, ninf)
    gmax = jnp.max(gl, axis=-1, keepdims=True)
    gidx = jnp.min(jnp.where(gl == gmax, lane_f, big), axis=-1, keepdims=True)
    gsum = jnp.sum(jnp.where(is_g, jnp.exp(logits - gmax), 0.0), axis=-1, keepdims=True)
    g_w = 1.0 / gsum
    elo = N_EXPERT_GROUPS + EXPERTS_PER_GROUP * gidx
    emask = (lane_f >= elo) & (lane_f < elo + EXPERTS_PER_GROUP)
    el = jnp.where(emask, logits, ninf)
    t1 = jnp.max(el, axis=-1, keepdims=True)
    i1 = jnp.min(jnp.where(el == t1, lane_f, big), axis=-1, keepdims=True)
    el2 = jnp.where(lane_f == i1, ninf, el)
    t2 = jnp.max(el2, axis=-1, keepdims=True)
    i2 = jnp.min(jnp.where(el2 == t2, lane_f, big), axis=-1, keepdims=True)
    e21 = jnp.exp(t2 - t1)
    den = 1.0 + e21
    gate1 = g_w * (1.0 / den)
    gate2 = g_w * (e21 / den)
    e1 = i1 - N_EXPERT_GROUPS
    e2 = i2 - N_EXPERT_GROUPS
    packed = jnp.where(lane == 0, e1, jnp.where(lane == 1, e2, jnp.where(lane == 2, gate1,
                                                                           jnp.where(lane == 3, gate2, 0.0))))
    hist = jnp.sum(((lane_f == e1) | (lane_f == e2)).astype(F32), axis=0, keepdims=True)
    return packed.T[:SUBLANES], hist


def _mixer_prompt_kernel(x_ref, ada_ref, h0_ref, g1_ref, g2_ref, win_ref, lng_ref, lnb_ref, wsp_ref, bsp_ref,
                         avec_ref, wb_ref, wc_ref, dsk_ref, wglu_ref, bglu_ref, wout_ref, wrc_ref, brt_ref,
                         x1_ref, h2_ref, route_ref, hist_ref, state_ref,
                         h_scr, u_scr, vn_scr, s_scr, st_scr, yt_scr, bu_scr, ab_scr, hi_scr, lo_scr):
    nb = x_ref.shape[0]
    half = (nb // 2) * CHUNK
    D = D_MODEL
    pitch = S_PITCH
    step = pl.program_id(0)

    @pl.when(step == 0)
    def _():
        state_ref[...] = h0_ref[...]
        hist_ref[...] = jnp.zeros_like(hist_ref)
        hi_scr[...] = jnp.zeros_like(hi_scr)
        lo_scr[...] = jnp.zeros_like(lo_scr)

    def mod(b, i):
        return ada_ref[b:b + 1, i * D:(i + 1) * D]

    def route_chunk(chunk, weight):
        for b in range(nb):
            rows = slice(b * CHUNK, (b + 1) * CHUNK)
            route, hist = _route(hi_scr[rows, :], lo_scr[rows, :], wrc_ref, brt_ref)
            route_ref[b, chunk] = route
            hist_ref[b:b + 1, :] = hist_ref[b:b + 1, :] + weight * hist

    route_chunk(jnp.maximum(step - 1, 0), (step > 0).astype(F32))

    for b in range(nb):
        hb = _rms(x_ref[b], g1_ref[...]) * (1 + mod(b, 1)) + mod(b, 0)
        h_scr[b * CHUNK:(b + 1) * CHUNK, :] = hb.astype(BF16)

    for r0 in (0, half):
        _front(r0, half, h_scr, win_ref, lng_ref, lnb_ref, u_scr, vn_scr, s_scr, CHUNK, pitch)

    for b in range(nb):
        rows = slice(b * CHUNK, (b + 1) * CHUNK)
        for h in range(N_HEADS):
            cols = slice(h * HEAD_DIM, (h + 1) * HEAD_DIM)
            mixed = jnp.dot(wsp_ref[h], vn_scr[rows, cols].astype(BF16), preferred_element_type=F32) + bsp_ref[h]
            ab_scr[rows, cols] = (u_scr[rows, cols] * mixed).astype(BF16)

    for k in range(N_SLABS):
        for t in range(CHUNK):
            st_scr[k, t * nb:(t + 1) * nb, :] = s_scr[k, pl.ds(t, nb, stride=pitch), :]
    _s5(st_scr, bu_scr, yt_scr, state_ref, avec_ref, wb_ref, wc_ref, nb, CHUNK)
    for k in range(N_SLABS):
        dsk = dsk_ref[:, k * LANES:(k + 1) * LANES]
        for t in range(CHUNK):
            sel = pl.ds(t, nb, stride=pitch)
            s_scr[k, sel, :] = yt_scr[k, t * nb:(t + 1) * nb, :] + dsk * s_scr[k, sel, :]

    for r0 in (0, half):
        mix = _back(r0, half, s_scr, ab_scr, wglu_ref, bglu_ref, wout_ref, CHUNK, pitch)
        for bl in range(nb // 2):
            b = r0 // CHUNK + bl
            x1 = x_ref[b] + mod(b, 2) * mix[bl * CHUNK:(bl + 1) * CHUNK, :]
            x1_ref[b] = x1
            h2 = _rms(x1, g2_ref[...]) * (1 + mod(b, 4)) + mod(b, 3)
            hi, lo = _split_hi_lo(h2)
            hi_scr[b * CHUNK:(b + 1) * CHUNK, :] = hi
            h2_ref[b] = hi
            lo_scr[b * CHUNK:(b + 1) * CHUNK, :] = lo

    @pl.when(step == pl.num_programs(0) - 1)
    def _():
        route_chunk(step, 1.0)


def _const_spec(shape):
    nd = len(shape)
    return pl.BlockSpec(shape, lambda *_: (0,) * nd, pipeline_mode=pl.Buffered(1))


def _mixer_prompt(x, ada, h0, wts):
    nb, seq, D = x.shape
    n_chunks = seq // CHUNK
    R = nb * CHUNK
    weight_specs = [_const_spec(w.shape) for w in wts]
    in_specs = [pl.BlockSpec((nb, CHUNK, D), lambda i: (0, i, 0)),
                _const_spec(ada.shape), _const_spec(h0.shape)] + weight_specs
    out_shape = (jax.ShapeDtypeStruct((nb, seq, D), F32),
                 jax.ShapeDtypeStruct((nb, seq, D), BF16),
                 jax.ShapeDtypeStruct((nb, n_chunks, SUBLANES, ROUTE_LANES), F32),
                 jax.ShapeDtypeStruct((nb, ROUTE_LANES), F32),
                 jax.ShapeDtypeStruct((nb, STATE_COLS), F32))
    out_specs = (pl.BlockSpec((nb, CHUNK, D), lambda i: (0, i, 0)),
                 pl.BlockSpec((nb, CHUNK, D), lambda i: (0, i, 0)),
                 pl.BlockSpec((nb, n_chunks, SUBLANES, ROUTE_LANES), lambda i: (0, 0, 0, 0)),
                 pl.BlockSpec((nb, ROUTE_LANES), lambda i: (0, 0)),
                 pl.BlockSpec((nb, STATE_COLS), lambda i: (0, 0)))
    scratch = [pltpu.VMEM((R, D), BF16),
               pltpu.VMEM((R, D_A), F32),
               pltpu.VMEM((R, D_A), F32),
               pltpu.VMEM((N_SLABS, nb * S_PITCH, LANES), F32),
               pltpu.VMEM((N_SLABS, R, LANES), F32),
               pltpu.VMEM((N_SLABS, R, LANES), F32),
               pltpu.VMEM((2, R, SLAB_COLS), F32),
               pltpu.VMEM((R, D), BF16),
               pltpu.VMEM((R, D), BF16),
               pltpu.VMEM((R, D), BF16)]
    return pl.pallas_call(
        _mixer_prompt_kernel,
        grid=(n_chunks,),
        in_specs=in_specs,
        out_specs=out_specs,
        out_shape=out_shape,
        scratch_shapes=scratch,
        compiler_params=pltpu.CompilerParams(dimension_semantics=("arbitrary",), vmem_limit_bytes=VMEM_LIMIT),
        name="mixer_prompt",
    )(x, ada, h0, *wts)


def _mixer_sample_kernel(wsm_ref, bsm_ref, x_ref, ada_ref, h0_ref, g1_ref, g2_ref, win_ref, lng_ref, lnb_ref,
                         avec_ref, wb_ref, wc_ref, dsk_ref, wglu_ref, bglu_ref, wout_ref, wrc_ref, brt_ref,
                         x1_ref, h2_ref, route_ref, hist_ref, state_ref, v_ref,
                         h_scr, u_scr, vn_scr, s_scr, yt_scr, bu_scr, ab_scr):
    T, nb, D = x_ref.shape
    R = T * nb
    half = R // 2

    def mod(i):
        return ada_ref[:, i * D:(i + 1) * D]

    state_ref[...] = h0_ref[...]
    for t in range(T):
        ht = _rms(x_ref[t], g1_ref[...]) * (1 + mod(1)) + mod(0)
        h_scr[t * nb:(t + 1) * nb, :] = ht.astype(BF16)

    for r0 in (0, half):
        _front(r0, half, h_scr, win_ref, lng_ref, lnb_ref, u_scr, vn_scr, s_scr, half, half)

    for t in range(T):
        rows = slice(t * nb, (t + 1) * nb)
        v_ref[t] = vn_scr[rows, :]
        for h in range(N_HEADS):
            cols = slice(h * HEAD_DIM, (h + 1) * HEAD_DIM)
            acc = jnp.full((nb, HEAD_DIM), bsm_ref[h * T + t], F32)
            for s in range(t + 1):
                acc = acc + wsm_ref[(h * T + t) * T + s] * vn_scr[s * nb:(s + 1) * nb, cols]
            ab_scr[rows, cols] = (u_scr[rows, cols] * acc).astype(BF16)

    _s5(s_scr, bu_scr, yt_scr, state_ref, avec_ref, wb_ref, wc_ref, nb, T)
    for k in range(N_SLABS):
        s_scr[k] = yt_scr[k] + dsk_ref[:, k * LANES:(k + 1) * LANES] * s_scr[k]

    hist_total = jnp.zeros((1, ROUTE_LANES), F32)
    for r0 in (0, half):
        mix = _back(r0, half, s_scr, ab_scr, wglu_ref, bglu_ref, wout_ref, half, half)
        for tl in range(T // 2):
            t = r0 // nb + tl
            x1 = x_ref[t] + mod(2) * mix[tl * nb:(tl + 1) * nb, :]
            x1_ref[t] = x1
            h2 = _rms(x1, g2_ref[...]) * (1 + mod(4)) + mod(3)
            hi, lo = _split_hi_lo(h2)
            h2_ref[t] = hi
            route, hist = _route(hi, lo, wrc_ref, brt_ref)
            route_ref[t] = route
            hist_total = hist_total + hist
    hist_ref[...] = hist_total


def _mixer_sample(x_t, ada, h0, w_small, b_small, wts):
    T, nb, D = x_t.shape
    R = T * nb
    smem = pl.BlockSpec(memory_space=pltpu.SMEM)
    out_shape = (jax.ShapeDtypeStruct((T, nb, D), F32),
                 jax.ShapeDtypeStruct((T, nb, D), BF16),
                 jax.ShapeDtypeStruct((T, SUBLANES, ROUTE_LANES), F32),
                 jax.ShapeDtypeStruct((1, ROUTE_LANES), F32),
                 jax.ShapeDtypeStruct((nb, STATE_COLS), F32),
                 jax.ShapeDtypeStruct((T, nb, D_A), F32))
    scratch = [pltpu.VMEM((R, D), BF16),
               pltpu.VMEM((R, D_A), F32),
               pltpu.VMEM((R, D_A), F32),
               pltpu.VMEM((N_SLABS, R, LANES), F32),
               pltpu.VMEM((N_SLABS, R, LANES), F32),
               pltpu.VMEM((2, R, SLAB_COLS), F32),
               pltpu.VMEM((R, D), BF16)]
    vmem = pl.BlockSpec(memory_space=pltpu.VMEM)
    return pl.pallas_call(
        _mixer_sample_kernel,
        in_specs=[smem, smem] + [vmem] * (3 + len(wts)),
        out_specs=(vmem,) * 6,
        out_shape=out_shape,
        scratch_shapes=scratch,
        compiler_params=pltpu.CompilerParams(vmem_limit_bytes=VMEM_LIMIT),
        name="mixer_sample",
    )(w_small, b_small, x_t, ada, h0, *wts)


def _rows_to_tiles(tiles_ref, row0, val):
    n = val.shape[0]
    for c in range(val.shape[1] // LANES):
        tiles_ref[pl.ds(row0 * SUBLANES + c, n, stride=SUBLANES), :] = val[:, c * LANES:(c + 1) * LANES]


def _tiles_to_rows(tiles_ref, row0, n):
    return jnp.concatenate([tiles_ref[pl.ds(row0 * SUBLANES + c, n, stride=SUBLANES), :] for c in range(SUBLANES)],
                           axis=-1)


def _moe_kernel(starts_ref, tok_ref, pos_ref, gate_ref, h2_ref, x1_ref, ada_ref, gfin_ref, w1_ref, w3_ref, w2_ref,
                out_ref, h_tiles, y_tiles, xb_even, xb_odd, cp_smem, tab_smem, *, chunk):
    t = pl.program_id(0)
    e = pl.program_id(1)
    n_expert_steps = N_EXPERTS // EXPERTS_PER_STEP
    ngrp, rows_g, D = h2_ref.shape
    n_tok = ngrp * rows_g
    gb, rb, _ = x1_ref.shape
    epi_tok = gb * rb
    mrows = ada_ref.shape[1]

    def tile_of(row):
        return pl.ds(pl.multiple_of(row * SUBLANES, SUBLANES), SUBLANES)

    def tile_at(row8):
        return pl.ds(pl.multiple_of(row8, SUBLANES), SUBLANES)

    @pl.when(e == 0)
    def _():
        for gi in range(ngrp):
            for ci in range(rows_g // chunk):
                h2 = h2_ref[gi, ci * chunk:(ci + 1) * chunk, :].astype(F32)
                _rows_to_tiles(h_tiles, gi * rows_g + ci * chunk, h2)

    @pl.when(e == 0)
    def _():
        gp = jnp.int32(0)
        for s in range(N_EXPERTS // EXPERTS_PER_STEP):
            cp_smem[s] = gp
            st = [starts_ref[t, s * EXPERTS_PER_STEP + i] for i in range(EXPERTS_PER_STEP + 1)]
            first = [jnp.int32(0)]
            for i in range(EXPERTS_PER_STEP):
                first.append(first[i] + lax.shift_right_logical(st[i + 1] - st[i] + (MOE_BLOCK - 1), MOE_BLOCK_SHIFT))
            nb = first[EXPERTS_PER_STEP]

            def locate(f, st=st, first=first):
                el, fb, sb = jnp.int32(0), first[0], st[0]
                for i in range(1, EXPERTS_PER_STEP):
                    hit = f >= first[i]
                    el = jnp.where(hit, i, el)
                    fb = jnp.where(hit, first[i], fb)
                    sb = jnp.where(hit, st[i], sb)
                return sb + (f - fb) * MOE_BLOCK, el

            def add_pair(p, gp, nb=nb, locate=locate):
                base_a, el_a = locate(2 * p)
                base_b, el_b = locate(jnp.minimum(2 * p + 1, nb - 1))
                for i, v in enumerate((base_a, base_b, el_a, el_b)):
                    tab_smem[PAIR_FIELDS * gp + i] = v
                return gp + 1

            gp = lax.fori_loop(0, lax.shift_right_logical(nb + 1, 1), add_pair, gp)
        cp_smem[N_EXPERTS // EXPERTS_PER_STEP] = gp
        for i in range(PAIR_FIELDS):
            tab_smem[PAIR_FIELDS * gp + i] = jnp.int32(0)

    def pair_entry(gp):
        return [tab_smem[PAIR_FIELDS * gp + i] for i in range(PAIR_FIELDS)]

    @pl.when(e == 0)
    def _():
        base_a, base_b, _, _ = pair_entry(0)

        def gather8(j8, c):
            for jj in range(SUBLANES):
                j = j8 * SUBLANES + jj
                xb_even[tile_of(j), :] = h_tiles[tile_at(tok_ref[base_a + j]), :]
                xb_even[tile_of(MOE_BLOCK + j), :] = h_tiles[tile_at(tok_ref[base_b + j]), :]
            return c

        lax.fori_loop(0, MOE_BLOCK // SUBLANES, gather8, 0)

    def pair_body(gp, cur_tiles, next_tiles):
        next_a, next_b, _, _ = pair_entry(gp + 1)
        n_slices = 8
        per = 2 * MOE_BLOCK // n_slices
        slices = iter(range(n_slices))

        def gather_slice():
            s = next(slices)
            for r in range(s * per, (s + 1) * per):
                tok8 = tok_ref[next_a + r] if r < MOE_BLOCK else tok_ref[next_b + r - MOE_BLOCK]
                next_tiles[r * SUBLANES:(r + 1) * SUBLANES, :] = h_tiles[tile_at(tok8), :]

        base_a, base_b, el_a, el_b = pair_entry(gp)
        xa = _tiles_to_rows(cur_tiles, 0, MOE_BLOCK).astype(BF16)
        gather_slice()
        a1 = jnp.dot(xa, w1_ref[el_a], preferred_element_type=F32)
        gather_slice()
        a3 = jnp.dot(xa, w3_ref[el_a], preferred_element_type=F32)
        gather_slice()
        xb = _tiles_to_rows(cur_tiles, MOE_BLOCK, MOE_BLOCK).astype(BF16)
        b1 = jnp.dot(xb, w1_ref[el_b], preferred_element_type=F32)
        gather_slice()
        b3 = jnp.dot(xb, w3_ref[el_b], preferred_element_type=F32)
        gather_slice()
        ya = jnp.dot((jax.nn.silu(a1) * a3).astype(BF16), w2_ref[el_a], preferred_element_type=F32)
        gather_slice()
        yb = jnp.dot((jax.nn.silu(b1) * b3).astype(BF16), w2_ref[el_b], preferred_element_type=F32)
        gather_slice()
        _rows_to_tiles(y_tiles, base_a, ya)
        gather_slice()
        _rows_to_tiles(y_tiles, base_b, yb)

    def pairs(gp, count):
        def run(first, second):
            bufs = (first, second)
            for i in range(count):
                pair_body(gp + i, bufs[i % 2], bufs[(i + 1) % 2])

        @pl.when((gp & 1) == 0)
        def _():
            run(xb_even, xb_odd)

        @pl.when((gp & 1) == 1)
        def _():
            run(xb_odd, xb_even)

    @pl.when(e < n_expert_steps)
    def _():
        first, last = cp_smem[e], cp_smem[e + 1]
        n_double = lax.shift_right_logical(last - first, 1)

        def double(i, carry):
            pairs(first + 2 * i, 2)
            return carry

        lax.fori_loop(0, n_double, double, 0)

        @pl.when(((last - first) & 1) == 1)
        def _():
            pairs(last - 1, 1)

    @pl.when(e >= n_expert_steps)
    def _():
        tok0 = (e - n_expert_steps) * epi_tok

        def combine8(t8, c):
            for tt in range(SUBLANES):
                tok = tok0 + t8 * SUBLANES + tt
                y0 = y_tiles[tile_at(pos_ref[tok]), :]
                y1 = y_tiles[tile_at(pos_ref[n_tok + tok]), :]
                h_tiles[tile_of(tok), :] = gate_ref[tok] * y0 + gate_ref[n_tok + tok] * y1
            return c

        lax.fori_loop(0, epi_tok // SUBLANES, combine8, 0)
        sub = min(rb, chunk)
        for gi in range(gb):
            for ci in range(rb // sub):
                rows = slice(ci * sub, (ci + 1) * sub)
                gt2 = ada_ref[0, :, 5 * D:6 * D] if mrows == 1 else ada_ref[0, rows, 5 * D:6 * D]
                moe = _tiles_to_rows(h_tiles, tok0 + gi * rb + ci * sub, sub)
                x2 = x1_ref[gi, rows, :] + gt2 * moe
                out_ref[gi, rows, :] = _rms(x2, gfin_ref[...])


def _moe(h2, x1, ada, starts, tok, pos, gate, gfin, w1, w3, w2, *, n_tiles, chunk):
    ngrp = x1.shape[0] // n_tiles
    rows_g, D = x1.shape[1], x1.shape[2]
    mrows = ada.shape[1]
    Tt = ngrp * rows_g
    n_expert_steps = N_EXPERTS // EXPERTS_PER_STEP
    epi_tok = Tt // MOE_EPILOGUE_STEPS
    if ngrp == 1:
        epi_block = (1, epi_tok, D)

        def epi_map(t, e, st):
            return (t, jnp.maximum(e - n_expert_steps, 0), 0)
    else:
        assert n_tiles == 1 and epi_tok % rows_g == 0
        epi_block = (epi_tok // rows_g, rows_g, D)

        def epi_map(t, e, st):
            return (jnp.maximum(e - n_expert_steps, 0), 0, 0)

    def weight_map(t, e, st):
        return (jnp.minimum(e, n_expert_steps - 1), 0, 0)
    lp = tok.shape[0] // n_tiles
    max_pairs = (2 * Tt // MOE_BLOCK + N_EXPERTS + N_EXPERTS // EXPERTS_PER_STEP) // 2 + 1
    grid_spec = pltpu.PrefetchScalarGridSpec(
        num_scalar_prefetch=1,
        grid=(n_tiles, n_expert_steps + MOE_EPILOGUE_STEPS),
        in_specs=[
            pl.BlockSpec((lp,), lambda t, e, st: (t,), memory_space=pltpu.SMEM),
            pl.BlockSpec((2 * Tt,), lambda t, e, st: (t,), memory_space=pltpu.SMEM),
            pl.BlockSpec((2 * Tt,), lambda t, e, st: (t,), memory_space=pltpu.SMEM),
            pl.BlockSpec((ngrp, rows_g, D), lambda t, e, st: (t, 0, 0)),
            pl.BlockSpec(epi_block, epi_map),
            pl.BlockSpec((1, mrows, 6 * D), lambda t, e, st: (t, 0, 0)),
            pl.BlockSpec((1, D), lambda t, e, st: (0, 0)),
            pl.BlockSpec((EXPERTS_PER_STEP, D, D_EXPERT), weight_map),
            pl.BlockSpec((EXPERTS_PER_STEP, D, D_EXPERT), weight_map),
            pl.BlockSpec((EXPERTS_PER_STEP, D_EXPERT, D), weight_map),
        ],
        out_specs=pl.BlockSpec(epi_block, epi_map),
        scratch_shapes=[pltpu.VMEM((Tt * SUBLANES, LANES), F32),
                        pltpu.VMEM(((2 * Tt + MOE_BLOCK) * SUBLANES, LANES), F32),
                        pltpu.VMEM((2 * MOE_BLOCK * SUBLANES, LANES), F32),
                        pltpu.VMEM((2 * MOE_BLOCK * SUBLANES, LANES), F32),
                        pltpu.SMEM((N_EXPERTS // EXPERTS_PER_STEP + 1,), jnp.int32),
                        pltpu.SMEM((max_pairs * PAIR_FIELDS,), jnp.int32)],
    )
    return pl.pallas_call(
        functools.partial(_moe_kernel, chunk=chunk),
        grid_spec=grid_spec,
        out_shape=jax.ShapeDtypeStruct(x1.shape, F32),
        compiler_params=pltpu.CompilerParams(dimension_semantics=("arbitrary", "arbitrary"),
                                             vmem_limit_bytes=VMEM_LIMIT),
        name="moe",
    )(starts, tok, pos, gate, h2, x1, ada, gfin, w1, w3, w2)


def _dispatch_tables(routes):
    tile_tokens = [r.shape[1] * ROUTE_LANES for r, _ in routes]
    t_max = max(tile_tokens)
    es, gs, masks = [], [], []
    for (route, _), Tt in zip(routes, tile_tokens):
        assert Tt & (Tt - 1) == 0
        n = route.shape[0]
        by_k = route[:, :, :4, :].transpose(0, 2, 1, 3).reshape(n, 4, Tt)
        extra = 2 * (t_max - Tt)
        es.append(jnp.pad(by_k[:, :2].astype(jnp.int32).reshape(n, 2 * Tt), ((0, 0), (0, extra)),
                          constant_values=N_EXPERTS))
        gs.append(jnp.pad(by_k[:, 2:].reshape(n, 2 * Tt), ((0, 0), (0, extra))))
        masks.append(jnp.full((n, 1), Tt - 1, jnp.int32))
    flat_e = jnp.concatenate(es, axis=0)
    flat_g = jnp.concatenate(gs, axis=0)
    n_tiles = flat_e.shape[0]
    order = jnp.argsort(flat_e, axis=-1, stable=True).astype(jnp.int32)
    pos = jnp.argsort(order, axis=-1).astype(jnp.int32) * SUBLANES
    tok_s = (order & jnp.concatenate(masks, axis=0)) * SUBLANES
    counts = jnp.concatenate([h[:, :N_EXPERTS] for _, h in routes], axis=0).astype(jnp.int32)
    starts = jnp.concatenate([jnp.zeros((n_tiles, 1), jnp.int32), jnp.cumsum(counts, axis=-1, dtype=jnp.int32)],
                             axis=-1)

    def tok_len(Tt):
        return -(-(2 * Tt + MOE_BLOCK) // SMEM_PAD) * SMEM_PAD

    tok_p = jnp.pad(tok_s, ((0, 0), (0, tok_len(t_max) - 2 * t_max)))
    Tt = t_max


    out, r0 = [], 0
    for (route, _), Tt in zip(routes, tile_tokens):
        r1 = r0 + route.shape[0]
        out.append((starts[r0:r1], tok_p[r0:r1, :tok_len(Tt)].reshape(-1),
                    pos[r0:r1, :2 * Tt].reshape(-1), flat_g[r0:r1, :2 * Tt].reshape(-1)))
        r0 = r1
    return out


def _pack_state(h_re, h_im):
    b = h_re.shape[0]
    return jnp.concatenate([h_re.reshape(b, STATE_COLS // 2), h_im.reshape(b, STATE_COLS // 2)], axis=-1)


def _unpack_state(st):
    b = st.shape[0]
    return (st[:, :STATE_COLS // 2].reshape(b, N_SSM_GROUPS, SSM_STATE),
            st[:, STATE_COLS // 2:].reshape(b, N_SSM_GROUPS, SSM_STATE))


def kernel(x_prompt, x_sample, state_ssm_re, state_ssm_im, c_prompt, c_sample, w_ada, b_ada, g_norm1, g_norm2, w_in, ln_g, ln_b, w_s, b_s, lam_re, lam_im, log_dt, ssm_b_re, ssm_b_im, ssm_c_re, ssm_c_im, ssm_d, w_glu, b_glu, w_out, w_group, b_group, w_expert, b_expert, w1, w3, w2, g_final):
    depth = w_ada.shape[0]
    assert depth == 1, "the final RMSNorm is fused into the (single) layer's MoE epilogue"
    B, L, D = x_prompt.shape
    Bs, Ls, _ = x_sample.shape
    xp = x_prompt
    xs_t = x_sample.transpose(1, 0, 2)
    eye = jnp.eye(SLAB_GROUPS, dtype=F32)
    tril = jnp.tril(jnp.ones((CHUNK, CHUNK), dtype=bool))
    p_re, p_im, s_re, s_im, s_v = [], [], [], [], []
    for l in range(depth):
        ada_p, ada_s = _ada(c_prompt, c_sample, w_ada[l], b_ada[l][None])

        ar, ai, br, bi = _discretize(lam_re[l], lam_im[l], log_dt[l], ssm_b_re[l], ssm_b_im[l])
        avec = jnp.concatenate([ar.reshape(1, STATE_COLS // 2), ai.reshape(1, STATE_COLS // 2)], axis=-1)

        def blockdiag_in(w):
            w4 = w.reshape(N_SLABS, SLAB_GROUPS, SSM_GROUP, SSM_STATE)
            return jnp.einsum('kghp,gG->kghGp', w4, eye).reshape(N_SLABS, LANES, SLAB_STATES)

        def blockdiag_out(w):
            w4 = w.reshape(N_SLABS, SLAB_GROUPS, SSM_GROUP, SSM_STATE)
            return jnp.einsum('kghp,gG->kgpGh', w4, eye).reshape(N_SLABS, SLAB_STATES, LANES)

        wb = jnp.concatenate([blockdiag_in(br), blockdiag_in(bi)], axis=-1).astype(BF16)
        wc = jnp.concatenate([blockdiag_out(ssm_c_re[l]), -blockdiag_out(ssm_c_im[l])], axis=1).astype(BF16)

        lane_pad = ROUTE_LANES - N_EXPERT_GROUPS - N_EXPERTS
        wr = jnp.pad(jnp.concatenate([w_group[l], w_expert[l]], axis=1), ((0, 0), (0, lane_pad)))
        wr_hi = wr.astype(BF16)
        wr_lo = (wr - wr_hi.astype(F32)).astype(BF16)
        br_t = jnp.pad(jnp.concatenate([b_group[l], b_expert[l]]), (0, lane_pad))[None]

        g1 = g_norm1[l][None]
        g2 = g_norm2[l][None]
        shared = dict(
            win=w_in[l].astype(BF16), lng=ln_g[l].reshape(1, D_A), lnb=ln_b[l].reshape(1, D_A),
            avec=avec, wb=wb, wc=wc, dsk=ssm_d[l].reshape(1, D_B), wglu=w_glu[l].astype(BF16),
            bglu=b_glu[l][None], wout=w_out[l].astype(BF16), wrc=jnp.concatenate([wr_hi, wr_lo], axis=1), brt=br_t)
        w_masked = jnp.where(tril[None], w_s[l], jnp.zeros_like(w_s[l]))
        wsp = w_masked.astype(BF16)
        bsp = jnp.broadcast_to(b_s[l][:, :, None], (N_HEADS, CHUNK, HEAD_DIM))
        wts_p = (g1, g2, shared['win'], shared['lng'], shared['lnb'], wsp, bsp, shared['avec'], shared['wb'],
                 shared['wc'], shared['dsk'], shared['wglu'], shared['bglu'], shared['wout'], shared['wrc'],
                 shared['brt'])
        wts_s = (g1, g2, shared['win'], shared['lng'], shared['lnb'], shared['avec'], shared['wb'],
                 shared['wc'], shared['dsk'], shared['wglu'], shared['bglu'], shared['wout'], shared['wrc'],
                 shared['brt'])

        w1b, w3b, w2b = w1[l].astype(BF16), w3[l].astype(BF16), w2[l].astype(BF16)
        gfin = g_final[None]

        h0p = jnp.zeros((B, STATE_COLS), F32)
        x1p, h2p, route_p, hist_p, st_p = _mixer_prompt(xp, ada_p, h0p, wts_p)
        h0s = _pack_state(state_ssm_re[l].astype(F32), state_ssm_im[l].astype(F32))
        w_small = w_masked[:, :Ls, :Ls].reshape(-1)
        b_small = b_s[l][:, :Ls].reshape(-1)
        x1s, h2s, route_s, hist_s, st_s, v_s = _mixer_sample(xs_t, ada_s, h0s, w_small, b_small, wts_s)

        (tables_p,) = _dispatch_tables([(route_p, hist_p)])
        (tables_s,) = _dispatch_tables([(route_s[None], hist_s)])
        xp = _moe(h2p, x1p, ada_p[:, None, :], *tables_p, gfin, w1b, w3b, w2b, n_tiles=B, chunk=256)
        xs_t = _moe(h2s, x1s, ada_s[None], *tables_s, gfin, w1b, w3b, w2b, n_tiles=1, chunk=Bs)
        hr, hi = _unpack_state(st_p)
        p_re.append(hr.astype(state_ssm_re.dtype))
        p_im.append(hi.astype(state_ssm_im.dtype))
        hr, hi = _unpack_state(st_s)
        s_re.append(hr.astype(state_ssm_re.dtype))
        s_im.append(hi.astype(state_ssm_im.dtype))
        s_v.append(v_s.transpose(1, 0, 2))

    y_prompt = xp
    y_sample = xs_t.transpose(1, 0, 2)
    return (y_prompt, y_sample, jnp.stack(p_re), jnp.stack(p_im), jnp.stack(s_re), jnp.stack(s_im), jnp.stack(s_v))
```

```python
import functools

import jax
import jax.numpy as jnp
from jax import lax
from jax.experimental import pallas as pl
from jax.experimental.pallas import tpu as pltpu

F32 = jnp.float32
BF16 = jnp.bfloat16

D_MODEL = 1024
D_A = 512
D_B = 512
N_HEADS = 4
HEAD_DIM = 128
CHUNK = 128
N_SSM_GROUPS = 32
SSM_GROUP = 16
SSM_STATE = 64
N_SLABS = 4
SLAB_GROUPS = N_SSM_GROUPS // N_SLABS
SLAB_STATES = SLAB_GROUPS * SSM_STATE
SLAB_COLS = 2 * SLAB_STATES
STATE_COLS = N_SLABS * SLAB_COLS
N_EXPERT_GROUPS = 4
EXPERTS_PER_GROUP = 8
N_EXPERTS = 32
D_EXPERT = 256
EPS = 1e-6

LANES = 128
SUBLANES = 8
ROUTE_LANES = LANES
MOE_BLOCK = 128
MOE_BLOCK_SHIFT = MOE_BLOCK.bit_length() - 1
assert 1 << MOE_BLOCK_SHIFT == MOE_BLOCK
EXPERTS_PER_STEP = 4
MOE_EPILOGUE_STEPS = 4
PAIR_FIELDS = 4
S_PITCH = CHUNK + SUBLANES
SMEM_PAD = 1024
VMEM_LIMIT = 58 * 1024 * 1024


def _rms(xf, g):
    ms = jnp.mean(xf * xf, axis=-1, keepdims=True)
    return xf * lax.rsqrt(ms + EPS) * g


def _ada_kernel(cp_ref, cs_ref, w_ref, b_ref, op_ref, os_ref):
    mp = cp_ref.shape[0]
    s = jax.nn.silu(jnp.concatenate([cp_ref[...], cs_ref[...]], axis=0)).astype(BF16)
    out = jnp.dot(s, w_ref[...].astype(BF16), preferred_element_type=F32) + b_ref[...]
    op_ref[...] = out[:mp]
    os_ref[...] = out[mp:]


def _ada(c_p, c_s, w, b):
    mp, ms = c_p.shape[0], c_s.shape[0]
    n = w.shape[1]
    bn = 1024
    return pl.pallas_call(
        _ada_kernel,
        grid=(n // bn,),
        in_specs=[pl.BlockSpec((mp, D_MODEL), lambda j: (0, 0)),
                  pl.BlockSpec((ms, D_MODEL), lambda j: (0, 0)),
                  pl.BlockSpec((D_MODEL, bn), lambda j: (0, j)),
                  pl.BlockSpec((1, bn), lambda j: (0, j))],
        out_specs=(pl.BlockSpec((mp, bn), lambda j: (0, j)), pl.BlockSpec((ms, bn), lambda j: (0, j))),
        out_shape=(jax.ShapeDtypeStruct((mp, n), F32), jax.ShapeDtypeStruct((ms, n), F32)),
        name="ada",
    )(c_p, c_s, w, b)


def _disc_kernel(lre_ref, lim_ref, ldt_ref, bre_ref, bim_ref, ar_ref, ai_ref, br_ref, bi_ref):
    dt = jnp.exp(ldt_ref[...])
    lr = lre_ref[...]
    li = lim_ref[...]
    mag = jnp.exp(lr * dt)
    ar = mag * jnp.cos(li * dt)
    ai = mag * jnp.sin(li * dt)
    den = lr * lr + li * li
    cr = ((ar - 1) * lr + ai * li) / den
    ci = (ai * lr - (ar - 1) * li) / den
    ar_ref[...] = ar
    ai_ref[...] = ai
    bre = bre_ref[...]
    bim = bim_ref[...]
    br_ref[...] = cr * bre - ci * bim
    bi_ref[...] = cr * bim + ci * bre


def _discretize(lam_re, lam_im, log_dt, b_re, b_im):
    g, p, h = b_re.shape
    o1 = jax.ShapeDtypeStruct((g, 1, p), F32)
    o2 = jax.ShapeDtypeStruct((g, h, p), F32)
    return pl.pallas_call(_disc_kernel, out_shape=(o1, o1, o2, o2), name="ssm_disc")(
        lam_re.reshape(g, 1, p), lam_im.reshape(g, 1, p), log_dt.reshape(g, 1, 1),
        b_re.transpose(0, 2, 1), b_im.transpose(0, 2, 1))


def _s_pieces(r0, nrows, grp, pitch):
    return [(i * grp, slice((r0 // grp + i) * pitch, (r0 // grp + i) * pitch + grp)) for i in range(nrows // grp)]


def _front(r0, nrows, h_scr, win_ref, lng_ref, lnb_ref, u_scr, vn_scr, s_scr, grp, pitch):
    rows = slice(r0, r0 + nrows)
    proj = jnp.dot(h_scr[rows, :], win_ref[...], preferred_element_type=F32)
    u_scr[rows, :] = jax.nn.gelu(proj[:, :D_A])
    vraw = jax.nn.gelu(proj[:, D_A:2 * D_A])
    for h in range(N_HEADS):
        cols = slice(h * HEAD_DIM, (h + 1) * HEAD_DIM)
        vh = vraw[:, cols]
        mu = jnp.mean(vh, axis=-1, keepdims=True)
        dv = vh - mu
        var = jnp.mean(dv * dv, axis=-1, keepdims=True)
        vn_scr[rows, cols] = dv * lax.rsqrt(var + EPS) * lng_ref[:, cols] + lnb_ref[:, cols]
    for k in range(N_SLABS):
        for off, prow in _s_pieces(r0, nrows, grp, pitch):
            s_scr[k, prow, :] = proj[off:off + grp, 2 * D_A + k * LANES:2 * D_A + (k + 1) * LANES]


def _scan_slab(bu, state, avec_ref, k, rows_per_step, steps):
    c_re = slice(0, SLAB_STATES)
    c_im = slice(SLAB_STATES, SLAB_COLS)
    s_re = slice(k * SLAB_STATES, (k + 1) * SLAB_STATES)
    s_im = slice(STATE_COLS // 2 + k * SLAB_STATES, STATE_COLS // 2 + (k + 1) * SLAB_STATES)
    ar = jnp.broadcast_to(avec_ref[:, s_re], (SUBLANES, SLAB_STATES))
    ai = jnp.broadcast_to(avec_ref[:, s_im], (SUBLANES, SLAB_STATES))
    for rc in range(rows_per_step // SUBLANES):
        r0 = rc * SUBLANES
        sr = state[r0:r0 + SUBLANES, s_re]
        si = state[r0:r0 + SUBLANES, s_im]
        for t in range(steps):
            rows = slice(t * rows_per_step + r0, t * rows_per_step + r0 + SUBLANES)
            nr = ar * sr - ai * si + bu[rows, c_re]
            ni = ar * si + ai * sr + bu[rows, c_im]
            bu[rows, c_re] = nr
            bu[rows, c_im] = ni
            sr, si = nr, ni
        state[r0:r0 + SUBLANES, s_re] = sr
        state[r0:r0 + SUBLANES, s_im] = si


def _s5(st_ref, bu_scr, yt_ref, state, avec_ref, wb_ref, wc_ref, rows_per_step, steps):
    for k in range(N_SLABS):
        bu = bu_scr.at[k % 2]
        bu[...] = jnp.dot(st_ref[k].astype(BF16), wb_ref[k], preferred_element_type=F32)
        _scan_slab(bu, state, avec_ref, k, rows_per_step, steps)
        yt_ref[k] = jnp.dot(bu[...].astype(BF16), wc_ref[k], preferred_element_type=F32)


def _back(r0, nrows, s_scr, ab_scr, wglu_ref, bglu_ref, wout_ref, grp, pitch):
    rows = slice(r0, r0 + nrows)
    pieces = _s_pieces(r0, nrows, grp, pitch)
    y = jax.nn.gelu(jnp.concatenate(
        [jnp.concatenate([s_scr[k, prow, :] for _, prow in pieces], axis=0) for k in range(N_SLABS)], axis=-1))
    gate = jnp.dot(y.astype(BF16), wglu_ref[...], preferred_element_type=F32) + bglu_ref[...]
    ab_scr[rows, D_A:] = (y * jax.nn.sigmoid(gate)).astype(BF16)
    return jnp.dot(ab_scr[rows, :], wout_ref[...], preferred_element_type=F32)


def _route(h2_bf, wrc_ref, brt_ref):
    both = jnp.dot(h2_bf, wrc_ref[...], preferred_element_type=F32)
    logits = both[:, :ROUTE_LANES] + both[:, ROUTE_LANES:] + brt_ref[...]
    n = logits.shape[0]
    lane = lax.broadcasted_iota(jnp.int32, (n, ROUTE_LANES), 1)
    lane_f = lane.astype(F32)
    big = jnp.float32(1e9)
    ninf = jnp.float32(-jnp.inf)
    is_g = lane < N_EXPERT_GROUPS
    gl = jnp.where(is_g, logits, ninf)
    gmax = jnp.max(gl, axis=-1, keepdims=True)
    gidx = jnp.min(jnp.where(gl == gmax, lane_f, big), axis=-1, keepdims=True)
    gsum = jnp.sum(jnp.where(is_g, jnp.exp(logits - gmax), 0.0), axis=-1, keepdims=True)
    g_w = 1.0 / gsum
    elo = N_EXPERT_GROUPS + EXPERTS_PER_GROUP * gidx
    emask = (lane_f >= elo) & (lane_f < elo + EXPERTS_PER_GROUP)
    el = jnp.where(emask, logits, ninf)
    t1 = jnp.max(el, axis=-1, keepdims=True)
    i1 = jnp.min(jnp.where(el == t1, lane_f, big), axis=-1, keepdims=True)
    el2 = jnp.where(lane_f == i1, ninf, el)
    t2 = jnp.max(el2, axis=-1, keepdims=True)
    i2 = jnp.min(jnp.where(el2 == t2, lane_f, big), axis=-1, keepdims=True)
    e21 = jnp.exp(t2 - t1)
    den = 1.0 + e21
    gate1 = g_w * (1.0 / den)
    gate2 = g_w * (e21 / den)
    e1 = i1 - N_EXPERT_GROUPS
    e2 = i2 - N_EXPERT_GROUPS
    packed = jnp.where(lane == 0, e1, jnp.where(lane == 1, e2, jnp.where(lane == 2, gate1,
                                                                           jnp.where(lane == 3, gate2, 0.0))))
    hist = jnp.sum(((lane_f == e1) | (lane_f == e2)).astype(F32), axis=0, keepdims=True)
    return packed.T[:SUBLANES], hist


def _mixer_prompt_kernel(x_ref, ada_ref, h0_ref, g1_ref, g2_ref, win_ref, lng_ref, lnb_ref, wsp_ref, bsp_ref,
                         avec_ref, wb_ref, wc_ref, dsk_ref, wglu_ref, bglu_ref, wout_ref, wrc_ref, brt_ref,
                         x1_ref, h2_ref, route_ref, hist_ref, state_ref,
                         h_scr, u_scr, vn_scr, s_scr, st_scr, yt_scr, bu_scr, ab_scr, hi_scr):
    nb = x_ref.shape[0]
    half = (nb // 2) * CHUNK
    D = D_MODEL
    pitch = S_PITCH
    step = pl.program_id(0)

    @pl.when(step == 0)
    def _():
        state_ref[...] = h0_ref[...]
        hist_ref[...] = jnp.zeros_like(hist_ref)
        hi_scr[...] = jnp.zeros_like(hi_scr)

    def mod(b, i):
        return ada_ref[b:b + 1, i * D:(i + 1) * D]

    def route_chunk(chunk, weight):
        for b in range(nb):
            rows = slice(b * CHUNK, (b + 1) * CHUNK)
            route, hist = _route(hi_scr[rows, :], wrc_ref, brt_ref)
            route_ref[b, chunk] = route
            hist_ref[b:b + 1, :] = hist_ref[b:b + 1, :] + weight * hist

    route_chunk(jnp.maximum(step - 1, 0), (step > 0).astype(F32))

    for b in range(nb):
        hb = _rms(x_ref[b], g1_ref[...]) * (1 + mod(b, 1)) + mod(b, 0)
        h_scr[b * CHUNK:(b + 1) * CHUNK, :] = hb.astype(BF16)

    for r0 in (0, half):
        _front(r0, half, h_scr, win_ref, lng_ref, lnb_ref, u_scr, vn_scr, s_scr, CHUNK, pitch)

    for b in range(nb):
        rows = slice(b * CHUNK, (b + 1) * CHUNK)
        for h in range(N_HEADS):
            cols = slice(h * HEAD_DIM, (h + 1) * HEAD_DIM)
            mixed = jnp.dot(wsp_ref[h], vn_scr[rows, cols].astype(BF16), preferred_element_type=F32) + bsp_ref[h]
            ab_scr[rows, cols] = (u_scr[rows, cols] * mixed).astype(BF16)

    for k in range(N_SLABS):
        for t in range(CHUNK):
            st_scr[k, t * nb:(t + 1) * nb, :] = s_scr[k, pl.ds(t, nb, stride=pitch), :]
    _s5(st_scr, bu_scr, yt_scr, state_ref, avec_ref, wb_ref, wc_ref, nb, CHUNK)
    for k in range(N_SLABS):
        dsk = dsk_ref[:, k * LANES:(k + 1) * LANES]
        for t in range(CHUNK):
            sel = pl.ds(t, nb, stride=pitch)
            s_scr[k, sel, :] = yt_scr[k, t * nb:(t + 1) * nb, :] + dsk * s_scr[k, sel, :]

    for r0 in (0, half):
        mix = _back(r0, half, s_scr, ab_scr, wglu_ref, bglu_ref, wout_ref, CHUNK, pitch)
        for bl in range(nb // 2):
            b = r0 // CHUNK + bl
            x1 = x_ref[b] + mod(b, 2) * mix[bl * CHUNK:(bl + 1) * CHUNK, :]
            x1_ref[b] = x1
            h2 = _rms(x1, g2_ref[...]) * (1 + mod(b, 4)) + mod(b, 3)
            hi = h2.astype(BF16)
            hi_scr[b * CHUNK:(b + 1) * CHUNK, :] = hi
            h2_ref[b] = hi

    @pl.when(step == pl.num_programs(0) - 1)
    def _():
        route_chunk(step, 1.0)


def _const_spec(shape):
    nd = len(shape)
    return pl.BlockSpec(shape, lambda *_: (0,) * nd, pipeline_mode=pl.Buffered(1))


def _mixer_prompt(x, ada, h0, wts):
    nb, seq, D = x.shape
    n_chunks = seq // CHUNK
    R = nb * CHUNK
    weight_specs = [_const_spec(w.shape) for w in wts]
    in_specs = [pl.BlockSpec((nb, CHUNK, D), lambda i: (0, i, 0)),
                _const_spec(ada.shape), _const_spec(h0.shape)] + weight_specs
    out_shape = (jax.ShapeDtypeStruct((nb, seq, D), F32),
                 jax.ShapeDtypeStruct((nb, seq, D), BF16),
                 jax.ShapeDtypeStruct((nb, n_chunks, SUBLANES, ROUTE_LANES), F32),
                 jax.ShapeDtypeStruct((nb, ROUTE_LANES), F32),
                 jax.ShapeDtypeStruct((nb, STATE_COLS), F32))
    out_specs = (pl.BlockSpec((nb, CHUNK, D), lambda i: (0, i, 0)),
                 pl.BlockSpec((nb, CHUNK, D), lambda i: (0, i, 0)),
                 pl.BlockSpec((nb, n_chunks, SUBLANES, ROUTE_LANES), lambda i: (0, 0, 0, 0)),
                 pl.BlockSpec((nb, ROUTE_LANES), lambda i: (0, 0)),
                 pl.BlockSpec((nb, STATE_COLS), lambda i: (0, 0)))
    scratch = [pltpu.VMEM((R, D), BF16),
               pltpu.VMEM((R, D_A), F32),
               pltpu.VMEM((R, D_A), F32),
               pltpu.VMEM((N_SLABS, nb * S_PITCH, LANES), F32),
               pltpu.VMEM((N_SLABS, R, LANES), F32),
               pltpu.VMEM((N_SLABS, R, LANES), F32),
               pltpu.VMEM((2, R, SLAB_COLS), F32),
               pltpu.VMEM((R, D), BF16),
               pltpu.VMEM((R, D), BF16)]
    return pl.pallas_call(
        _mixer_prompt_kernel,
        grid=(n_chunks,),
        in_specs=in_specs,
        out_specs=out_specs,
        out_shape=out_shape,
        scratch_shapes=scratch,
        compiler_params=pltpu.CompilerParams(dimension_semantics=("arbitrary",), vmem_limit_bytes=VMEM_LIMIT),
        name="mixer_prompt",
    )(x, ada, h0, *wts)


def _mixer_sample_kernel(wsm_ref, bsm_ref, x_ref, ada_ref, h0_ref, g1_ref, g2_ref, win_ref, lng_ref, lnb_ref,
                         avec_ref, wb_ref, wc_ref, dsk_ref, wglu_ref, bglu_ref, wout_ref, wrc_ref, brt_ref,
                         x1_ref, h2_ref, route_ref, hist_ref, state_ref, v_ref,
                         h_scr, u_scr, vn_scr, s_scr, yt_scr, bu_scr, ab_scr):
    T, nb, D = x_ref.shape
    R = T * nb
    half = R // 2

    def mod(i):
        return ada_ref[:, i * D:(i + 1) * D]

    state_ref[...] = h0_ref[...]
    for t in range(T):
        ht = _rms(x_ref[t], g1_ref[...]) * (1 + mod(1)) + mod(0)
        h_scr[t * nb:(t + 1) * nb, :] = ht.astype(BF16)

    for r0 in (0, half):
        _front(r0, half, h_scr, win_ref, lng_ref, lnb_ref, u_scr, vn_scr, s_scr, half, half)

    for t in range(T):
        rows = slice(t * nb, (t + 1) * nb)
        v_ref[t] = vn_scr[rows, :]
        for h in range(N_HEADS):
            cols = slice(h * HEAD_DIM, (h + 1) * HEAD_DIM)
            acc = jnp.full((nb, HEAD_DIM), bsm_ref[h * T + t], F32)
            for s in range(t + 1):
                acc = acc + wsm_ref[(h * T + t) * T + s] * vn_scr[s * nb:(s + 1) * nb, cols]
            ab_scr[rows, cols] = (u_scr[rows, cols] * acc).astype(BF16)

    _s5(s_scr, bu_scr, yt_scr, state_ref, avec_ref, wb_ref, wc_ref, nb, T)
    for k in range(N_SLABS):
        s_scr[k] = yt_scr[k] + dsk_ref[:, k * LANES:(k + 1) * LANES] * s_scr[k]

    hist_total = jnp.zeros((1, ROUTE_LANES), F32)
    for r0 in (0, half):
        mix = _back(r0, half, s_scr, ab_scr, wglu_ref, bglu_ref, wout_ref, half, half)
        for tl in range(T // 2):
            t = r0 // nb + tl
            x1 = x_ref[t] + mod(2) * mix[tl * nb:(tl + 1) * nb, :]
            x1_ref[t] = x1
            h2 = _rms(x1, g2_ref[...]) * (1 + mod(4)) + mod(3)
            hi = h2.astype(BF16)
            h2_ref[t] = hi
            route, hist = _route(hi, wrc_ref, brt_ref)
            route_ref[t] = route
            hist_total = hist_total + hist
    hist_ref[...] = hist_total


def _mixer_sample(x_t, ada, h0, w_small, b_small, wts):
    T, nb, D = x_t.shape
    R = T * nb
    smem = pl.BlockSpec(memory_space=pltpu.SMEM)
    out_shape = (jax.ShapeDtypeStruct((T, nb, D), F32),
                 jax.ShapeDtypeStruct((T, nb, D), BF16),
                 jax.ShapeDtypeStruct((T, SUBLANES, ROUTE_LANES), F32),
                 jax.ShapeDtypeStruct((1, ROUTE_LANES), F32),
                 jax.ShapeDtypeStruct((nb, STATE_COLS), F32),
                 jax.ShapeDtypeStruct((T, nb, D_A), F32))
    scratch = [pltpu.VMEM((R, D), BF16),
               pltpu.VMEM((R, D_A), F32),
               pltpu.VMEM((R, D_A), F32),
               pltpu.VMEM((N_SLABS, R, LANES), F32),
               pltpu.VMEM((N_SLABS, R, LANES), F32),
               pltpu.VMEM((2, R, SLAB_COLS), F32),
               pltpu.VMEM((R, D), BF16)]
    vmem = pl.BlockSpec(memory_space=pltpu.VMEM)
    return pl.pallas_call(
        _mixer_sample_kernel,
        in_specs=[smem, smem] + [vmem] * (3 + len(wts)),
        out_specs=(vmem,) * 6,
        out_shape=out_shape,
        scratch_shapes=scratch,
        compiler_params=pltpu.CompilerParams(vmem_limit_bytes=VMEM_LIMIT),
        name="mixer_sample",
    )(w_small, b_small, x_t, ada, h0, *wts)


def _rows_to_tiles(tiles_ref, row0, val):
    n = val.shape[0]
    for c in range(val.shape[1] // LANES):
        tiles_ref[pl.ds(row0 * SUBLANES + c, n, stride=SUBLANES), :] = val[:, c * LANES:(c + 1) * LANES]


def _tiles_to_rows(tiles_ref, row0, n):
    return jnp.concatenate([tiles_ref[pl.ds(row0 * SUBLANES + c, n, stride=SUBLANES), :] for c in range(SUBLANES)],
                           axis=-1)


def _moe_kernel(starts_ref, tok_ref, pos_ref, gate_ref, h2_ref, x1_ref, ada_ref, gfin_ref, w1_ref, w3_ref, w2_ref,
                out_ref, h_tiles, y_tiles, xb_even, xb_odd, cp_smem, tab_smem, *, chunk):
    t = pl.program_id(0)
    e = pl.program_id(1)
    n_expert_steps = N_EXPERTS // EXPERTS_PER_STEP
    ngrp, rows_g, D = h2_ref.shape
    n_tok = ngrp * rows_g
    gb, rb, _ = x1_ref.shape
    epi_tok = gb * rb
    mrows = ada_ref.shape[1]

    def tile_of(row):
        return pl.ds(pl.multiple_of(row * SUBLANES, SUBLANES), SUBLANES)

    def tile_at(row8):
        return pl.ds(pl.multiple_of(row8, SUBLANES), SUBLANES)

    @pl.when(e == 0)
    def _():
        for gi in range(ngrp):
            for ci in range(rows_g // chunk):
                h2 = h2_ref[gi, ci * chunk:(ci + 1) * chunk, :].astype(F32)
                _rows_to_tiles(h_tiles, gi * rows_g + ci * chunk, h2)

    @pl.when(e == 0)
    def _():
        gp = jnp.int32(0)
        for s in range(N_EXPERTS // EXPERTS_PER_STEP):
            cp_smem[s] = gp
            st = [starts_ref[t, s * EXPERTS_PER_STEP + i] for i in range(EXPERTS_PER_STEP + 1)]
            first = [jnp.int32(0)]
            for i in range(EXPERTS_PER_STEP):
                first.append(first[i] + lax.shift_right_logical(st[i + 1] - st[i] + (MOE_BLOCK - 1), MOE_BLOCK_SHIFT))
            nb = first[EXPERTS_PER_STEP]

            def locate(f, st=st, first=first):
                el, fb, sb = jnp.int32(0), first[0], st[0]
                for i in range(1, EXPERTS_PER_STEP):
                    hit = f >= first[i]
                    el = jnp.where(hit, i, el)
                    fb = jnp.where(hit, first[i], fb)
                    sb = jnp.where(hit, st[i], sb)
                return sb + (f - fb) * MOE_BLOCK, el

            def add_pair(p, gp, nb=nb, locate=locate):
                base_a, el_a = locate(2 * p)
                base_b, el_b = locate(jnp.minimum(2 * p + 1, nb - 1))
                for i, v in enumerate((base_a, base_b, el_a, el_b)):
                    tab_smem[PAIR_FIELDS * gp + i] = v
                return gp + 1

            gp = lax.fori_loop(0, lax.shift_right_logical(nb + 1, 1), add_pair, gp)
        cp_smem[N_EXPERTS // EXPERTS_PER_STEP] = gp
        for i in range(PAIR_FIELDS):
            tab_smem[PAIR_FIELDS * gp + i] = jnp.int32(0)

    def pair_entry(gp):
        return [tab_smem[PAIR_FIELDS * gp + i] for i in range(PAIR_FIELDS)]

    @pl.when(e == 0)
    def _():
        base_a, base_b, _, _ = pair_entry(0)

        def gather8(j8, c):
            for jj in range(SUBLANES):
                j = j8 * SUBLANES + jj
                xb_even[tile_of(j), :] = h_tiles[tile_at(tok_ref[base_a + j]), :]
                xb_even[tile_of(MOE_BLOCK + j), :] = h_tiles[tile_at(tok_ref[base_b + j]), :]
            return c

        lax.fori_loop(0, MOE_BLOCK // SUBLANES, gather8, 0)

    def pair_body(gp, cur_tiles, next_tiles):
        next_a, next_b, _, _ = pair_entry(gp + 1)
        n_slices = 8
        per = 2 * MOE_BLOCK // n_slices
        slices = iter(range(n_slices))

        def gather_slice():
            s = next(slices)
            for r in range(s * per, (s + 1) * per):
                tok8 = tok_ref[next_a + r] if r < MOE_BLOCK else tok_ref[next_b + r - MOE_BLOCK]
                next_tiles[r * SUBLANES:(r + 1) * SUBLANES, :] = h_tiles[tile_at(tok8), :]

        base_a, base_b, el_a, el_b = pair_entry(gp)
        xa = _tiles_to_rows(cur_tiles, 0, MOE_BLOCK).astype(BF16)
        gather_slice()
        a1 = jnp.dot(xa, w1_ref[el_a], preferred_element_type=F32)
        gather_slice()
        a3 = jnp.dot(xa, w3_ref[el_a], preferred_element_type=F32)
        gather_slice()
        xb = _tiles_to_rows(cur_tiles, MOE_BLOCK, MOE_BLOCK).astype(BF16)
        b1 = jnp.dot(xb, w1_ref[el_b], preferred_element_type=F32)
        gather_slice()
        b3 = jnp.dot(xb, w3_ref[el_b], preferred_element_type=F32)
        gather_slice()
        ya = jnp.dot((jax.nn.silu(a1) * a3).astype(BF16), w2_ref[el_a], preferred_element_type=F32)
        gather_slice()
        yb = jnp.dot((jax.nn.silu(b1) * b3).astype(BF16), w2_ref[el_b], preferred_element_type=F32)
        gather_slice()
        _rows_to_tiles(y_tiles, base_a, ya)
        gather_slice()
        _rows_to_tiles(y_tiles, base_b, yb)

    def pairs(gp, count):
        def run(first, second):
            bufs = (first, second)
            for i in range(count):
                pair_body(gp + i, bufs[i % 2], bufs[(i + 1) % 2])

        @pl.when((gp & 1) == 0)
        def _():
            run(xb_even, xb_odd)

        @pl.when((gp & 1) == 1)
        def _():
            run(xb_odd, xb_even)

    @pl.when(e < n_expert_steps)
    def _():
        first, last = cp_smem[e], cp_smem[e + 1]
        n_double = lax.shift_right_logical(last - first, 1)

        def double(i, carry):
            pairs(first + 2 * i, 2)
            return carry

        lax.fori_loop(0, n_double, double, 0)

        @pl.when(((last - first) & 1) == 1)
        def _():
            pairs(last - 1, 1)

    @pl.when(e >= n_expert_steps)
    def _():
        tok0 = (e - n_expert_steps) * epi_tok

        def combine8(t8, c):
            for tt in range(SUBLANES):
                tok = tok0 + t8 * SUBLANES + tt
                y0 = y_tiles[tile_at(pos_ref[tok]), :]
                y1 = y_tiles[tile_at(pos_ref[n_tok + tok]), :]
                h_tiles[tile_of(tok), :] = gate_ref[tok] * y0 + gate_ref[n_tok + tok] * y1
            return c

        lax.fori_loop(0, epi_tok // SUBLANES, combine8, 0)
        sub = min(rb, chunk)
        for gi in range(gb):
            for ci in range(rb // sub):
                rows = slice(ci * sub, (ci + 1) * sub)
                gt2 = ada_ref[0, :, 5 * D:6 * D] if mrows == 1 else ada_ref[0, rows, 5 * D:6 * D]
                moe = _tiles_to_rows(h_tiles, tok0 + gi * rb + ci * sub, sub)
                x2 = x1_ref[gi, rows, :] + gt2 * moe
                out_ref[gi, rows, :] = _rms(x2, gfin_ref[...])


def _moe(h2, x1, ada, starts, tok, pos, gate, gfin, w1, w3, w2, *, n_tiles, chunk):
    ngrp = x1.shape[0] // n_tiles
    rows_g, D = x1.shape[1], x1.shape[2]
    mrows = ada.shape[1]
    Tt = ngrp * rows_g
    n_expert_steps = N_EXPERTS // EXPERTS_PER_STEP
    epi_tok = Tt // MOE_EPILOGUE_STEPS
    if ngrp == 1:
        epi_block = (1, epi_tok, D)

        def epi_map(t, e, st):
            return (t, jnp.maximum(e - n_expert_steps, 0), 0)
    else:
        assert n_tiles == 1 and epi_tok % rows_g == 0
        epi_block = (epi_tok // rows_g, rows_g, D)

        def epi_map(t, e, st):
            return (jnp.maximum(e - n_expert_steps, 0), 0, 0)

    def weight_map(t, e, st):
        return (jnp.minimum(e, n_expert_steps - 1), 0, 0)
    lp = tok.shape[0] // n_tiles
    max_pairs = (2 * Tt // MOE_BLOCK + N_EXPERTS + N_EXPERTS // EXPERTS_PER_STEP) // 2 + 1
    grid_spec = pltpu.PrefetchScalarGridSpec(
        num_scalar_prefetch=1,
        grid=(n_tiles, n_expert_steps + MOE_EPILOGUE_STEPS),
        in_specs=[
            pl.BlockSpec((lp,), lambda t, e, st: (t,), memory_space=pltpu.SMEM),
            pl.BlockSpec((2 * Tt,), lambda t, e, st: (t,), memory_space=pltpu.SMEM),
            pl.BlockSpec((2 * Tt,), lambda t, e, st: (t,), memory_space=pltpu.SMEM),
            pl.BlockSpec((ngrp, rows_g, D), lambda t, e, st: (t, 0, 0)),
            pl.BlockSpec(epi_block, epi_map),
            pl.BlockSpec((1, mrows, 6 * D), lambda t, e, st: (t, 0, 0)),
            pl.BlockSpec((1, D), lambda t, e, st: (0, 0)),
            pl.BlockSpec((EXPERTS_PER_STEP, D, D_EXPERT), weight_map),
            pl.BlockSpec((EXPERTS_PER_STEP, D, D_EXPERT), weight_map),
            pl.BlockSpec((EXPERTS_PER_STEP, D_EXPERT, D), weight_map),
        ],
        out_specs=pl.BlockSpec(epi_block, epi_map),
        scratch_shapes=[pltpu.VMEM((Tt * SUBLANES, LANES), F32),
                        pltpu.VMEM(((2 * Tt + MOE_BLOCK) * SUBLANES, LANES), F32),
                        pltpu.VMEM((2 * MOE_BLOCK * SUBLANES, LANES), F32),
                        pltpu.VMEM((2 * MOE_BLOCK * SUBLANES, LANES), F32),
                        pltpu.SMEM((N_EXPERTS // EXPERTS_PER_STEP + 1,), jnp.int32),
                        pltpu.SMEM((max_pairs * PAIR_FIELDS,), jnp.int32)],
    )
    return pl.pallas_call(
        functools.partial(_moe_kernel, chunk=chunk),
        grid_spec=grid_spec,
        out_shape=jax.ShapeDtypeStruct(x1.shape, F32),
        compiler_params=pltpu.CompilerParams(dimension_semantics=("arbitrary", "arbitrary"),
                                             vmem_limit_bytes=VMEM_LIMIT),
        name="moe",
    )(starts, tok, pos, gate, h2, x1, ada, gfin, w1, w3, w2)


def _dispatch_tables(routes):
    tile_tokens = [r.shape[1] * ROUTE_LANES for r, _ in routes]
    t_max = max(tile_tokens)
    es, gs, masks = [], [], []
    for (route, _), Tt in zip(routes, tile_tokens):
        assert Tt & (Tt - 1) == 0
        n = route.shape[0]
        by_k = route[:, :, :4, :].transpose(0, 2, 1, 3).reshape(n, 4, Tt)
        extra = 2 * (t_max - Tt)
        es.append(jnp.pad(by_k[:, :2].astype(jnp.int32).reshape(n, 2 * Tt), ((0, 0), (0, extra)),
                          constant_values=N_EXPERTS))
        gs.append(jnp.pad(by_k[:, 2:].reshape(n, 2 * Tt), ((0, 0), (0, extra))))
        masks.append(jnp.full((n, 1), Tt - 1, jnp.int32))
    flat_e = jnp.concatenate(es, axis=0)
    flat_g = jnp.concatenate(gs, axis=0)
    n_tiles = flat_e.shape[0]
    order = jnp.argsort(flat_e, axis=-1, stable=False).astype(jnp.int32)
    pos = jnp.argsort(order, axis=-1).astype(jnp.int32) * SUBLANES
    tok_s = (order & jnp.concatenate(masks, axis=0)) * SUBLANES
    counts = jnp.concatenate([h[:, :N_EXPERTS] for _, h in routes], axis=0).astype(jnp.int32)
    starts = jnp.concatenate([jnp.zeros((n_tiles, 1), jnp.int32), jnp.cumsum(counts, axis=-1, dtype=jnp.int32)],
                             axis=-1)

    def tok_len(Tt):
        return -(-(2 * Tt + MOE_BLOCK) // SMEM_PAD) * SMEM_PAD

    tok_p = jnp.pad(tok_s, ((0, 0), (0, tok_len(t_max) - 2 * t_max)))
    Tt = t_max


    out, r0 = [], 0
    for (route, _), Tt in zip(routes, tile_tokens):
        r1 = r0 + route.shape[0]
        out.append((starts[r0:r1], tok_p[r0:r1, :tok_len(Tt)].reshape(-1),
                    pos[r0:r1, :2 * Tt].reshape(-1), flat_g[r0:r1, :2 * Tt].reshape(-1)))
        r0 = r1
    return out


def _pack_state(h_re, h_im):
    b = h_re.shape[0]
    return jnp.concatenate([h_re.reshape(b, STATE_COLS // 2), h_im.reshape(b, STATE_COLS // 2)], axis=-1)


def _unpack_state(st):
    b = st.shape[0]
    return (st[:, :STATE_COLS // 2].reshape(b, N_SSM_GROUPS, SSM_STATE),
            st[:, STATE_COLS // 2:].reshape(b, N_SSM_GROUPS, SSM_STATE))


def kernel(x_prompt, x_sample, state_ssm_re, state_ssm_im, c_prompt, c_sample, w_ada, b_ada, g_norm1, g_norm2, w_in, ln_g, ln_b, w_s, b_s, lam_re, lam_im, log_dt, ssm_b_re, ssm_b_im, ssm_c_re, ssm_c_im, ssm_d, w_glu, b_glu, w_out, w_group, b_group, w_expert, b_expert, w1, w3, w2, g_final):
    depth = w_ada.shape[0]
    assert depth == 1, "the final RMSNorm is fused into the (single) layer's MoE epilogue"
    B, L, D = x_prompt.shape
    Bs, Ls, _ = x_sample.shape
    xp = x_prompt
    xs_t = x_sample.transpose(1, 0, 2)
    eye = jnp.eye(SLAB_GROUPS, dtype=F32)
    tril = jnp.tril(jnp.ones((CHUNK, CHUNK), dtype=bool))
    p_re, p_im, s_re, s_im, s_v = [], [], [], [], []
    for l in range(depth):
        ada_p, ada_s = _ada(c_prompt, c_sample, w_ada[l], b_ada[l][None])

        ar, ai, br, bi = _discretize(lam_re[l], lam_im[l], log_dt[l], ssm_b_re[l], ssm_b_im[l])
        avec = jnp.concatenate([ar.reshape(1, STATE_COLS // 2), ai.reshape(1, STATE_COLS // 2)], axis=-1)

        def blockdiag_in(w):
            w4 = w.reshape(N_SLABS, SLAB_GROUPS, SSM_GROUP, SSM_STATE)
            return jnp.einsum('kghp,gG->kghGp', w4, eye).reshape(N_SLABS, LANES, SLAB_STATES)

        def blockdiag_out(w):
            w4 = w.reshape(N_SLABS, SLAB_GROUPS, SSM_GROUP, SSM_STATE)
            return jnp.einsum('kghp,gG->kgpGh', w4, eye).reshape(N_SLABS, SLAB_STATES, LANES)

        wb = jnp.concatenate([blockdiag_in(br), blockdiag_in(bi)], axis=-1).astype(BF16)
        wc = jnp.concatenate([blockdiag_out(ssm_c_re[l]), -blockdiag_out(ssm_c_im[l])], axis=1).astype(BF16)

        lane_pad = ROUTE_LANES - N_EXPERT_GROUPS - N_EXPERTS
        wr = jnp.pad(jnp.concatenate([w_group[l], w_expert[l]], axis=1), ((0, 0), (0, lane_pad)))
        wr_hi = wr.astype(BF16)
        wr_lo = (wr - wr_hi.astype(F32)).astype(BF16)
        br_t = jnp.pad(jnp.concatenate([b_group[l], b_expert[l]]), (0, lane_pad))[None]

        g1 = g_norm1[l][None]
        g2 = g_norm2[l][None]
        shared = dict(
            win=w_in[l].astype(BF16), lng=ln_g[l].reshape(1, D_A), lnb=ln_b[l].reshape(1, D_A),
            avec=avec, wb=wb, wc=wc, dsk=ssm_d[l].reshape(1, D_B), wglu=w_glu[l].astype(BF16),
            bglu=b_glu[l][None], wout=w_out[l].astype(BF16), wrc=jnp.concatenate([wr_hi, wr_lo], axis=1), brt=br_t)
        w_masked = jnp.where(tril[None], w_s[l], jnp.zeros_like(w_s[l]))
        wsp = w_masked.astype(BF16)
        bsp = jnp.broadcast_to(b_s[l][:, :, None], (N_HEADS, CHUNK, HEAD_DIM))
        wts_p = (g1, g2, shared['win'], shared['lng'], shared['lnb'], wsp, bsp, shared['avec'], shared['wb'],
                 shared['wc'], shared['dsk'], shared['wglu'], shared['bglu'], shared['wout'], shared['wrc'],
                 shared['brt'])
        wts_s = (g1, g2, shared['win'], shared['lng'], shared['lnb'], shared['avec'], shared['wb'],
                 shared['wc'], shared['dsk'], shared['wglu'], shared['bglu'], shared['wout'], shared['wrc'],
                 shared['brt'])

        w1b, w3b, w2b = w1[l].astype(BF16), w3[l].astype(BF16), w2[l].astype(BF16)
        gfin = g_final[None]

        h0p = jnp.zeros((B, STATE_COLS), F32)
        x1p, h2p, route_p, hist_p, st_p = _mixer_prompt(xp, ada_p, h0p, wts_p)
        h0s = _pack_state(state_ssm_re[l].astype(F32), state_ssm_im[l].astype(F32))
        w_small = w_masked[:, :Ls, :Ls].reshape(-1)
        b_small = b_s[l][:, :Ls].reshape(-1)
        x1s, h2s, route_s, hist_s, st_s, v_s = _mixer_sample(xs_t, ada_s, h0s, w_small, b_small, wts_s)

        (tables_p,) = _dispatch_tables([(route_p, hist_p)])
        (tables_s,) = _dispatch_tables([(route_s[None], hist_s)])
        xp = _moe(h2p, x1p, ada_p[:, None, :], *tables_p, gfin, w1b, w3b, w2b, n_tiles=B, chunk=256)
        xs_t = _moe(h2s, x1s, ada_s[None], *tables_s, gfin, w1b, w3b, w2b, n_tiles=1, chunk=Bs)
        hr, hi = _unpack_state(st_p)
        p_re.append(hr.astype(state_ssm_re.dtype))
        p_im.append(hi.astype(state_ssm_im.dtype))
        hr, hi = _unpack_state(st_s)
        s_re.append(hr.astype(state_ssm_re.dtype))
        s_im.append(hi.astype(state_ssm_im.dtype))
        s_v.append(v_s.transpose(1, 0, 2))

    y_prompt = xp
    y_sample = xs_t.transpose(1, 0, 2)
    return (y_prompt, y_sample, jnp.stack(p_re), jnp.stack(p_im), jnp.stack(s_re), jnp.stack(s_im), jnp.stack(s_v))
```

```python
import functools

import jax
import jax.numpy as jnp
from jax import lax
from jax.experimental import pallas as pl
from jax.experimental.pallas import tpu as pltpu

F32 = jnp.float32
BF16 = jnp.bfloat16

D_MODEL = 1024
D_A = 512
D_B = 512
N_HEADS = 4
HEAD_DIM = 128
CHUNK = 128
N_SSM_GROUPS = 32
SSM_GROUP = 16
SSM_STATE = 64
N_SLABS = 4
SLAB_GROUPS = N_SSM_GROUPS // N_SLABS
SLAB_STATES = SLAB_GROUPS * SSM_STATE
SLAB_COLS = 2 * SLAB_STATES
STATE_COLS = N_SLABS * SLAB_COLS
N_EXPERT_GROUPS = 4
EXPERTS_PER_GROUP = 8
N_EXPERTS = 32
D_EXPERT = 256
EPS = 1e-6

LANES = 128
SUBLANES = 8
ROUTE_LANES = LANES
MOE_BLOCK = 128
MOE_BLOCK_SHIFT = MOE_BLOCK.bit_length() - 1
assert 1 << MOE_BLOCK_SHIFT == MOE_BLOCK
EXPERTS_PER_STEP = 4
MOE_EPILOGUE_STEPS = 4
PAIR_FIELDS = 4
S_PITCH = CHUNK + SUBLANES
SMEM_PAD = 1024
VMEM_LIMIT = 58 * 1024 * 1024


def _rms(xf, g):
    ms = jnp.mean(xf * xf, axis=-1, keepdims=True)
    return xf * lax.rsqrt(ms + EPS) * g


def _ada_kernel(cp_ref, cs_ref, w_ref, b_ref, op_ref, os_ref):
    mp = cp_ref.shape[0]
    s = jax.nn.silu(jnp.concatenate([cp_ref[...], cs_ref[...]], axis=0)).astype(BF16)
    out = jnp.dot(s, w_ref[...].astype(BF16), preferred_element_type=F32) + b_ref[...]
    op_ref[...] = out[:mp]
    os_ref[...] = out[mp:]


def _ada(c_p, c_s, w, b):
    mp, ms = c_p.shape[0], c_s.shape[0]
    n = w.shape[1]
    bn = 1024
    return pl.pallas_call(
        _ada_kernel,
        grid=(n // bn,),
        in_specs=[pl.BlockSpec((mp, D_MODEL), lambda j: (0, 0)),
                  pl.BlockSpec((ms, D_MODEL), lambda j: (0, 0)),
                  pl.BlockSpec((D_MODEL, bn), lambda j: (0, j)),
                  pl.BlockSpec((1, bn), lambda j: (0, j))],
        out_specs=(pl.BlockSpec((mp, bn), lambda j: (0, j)), pl.BlockSpec((ms, bn), lambda j: (0, j))),
        out_shape=(jax.ShapeDtypeStruct((mp, n), F32), jax.ShapeDtypeStruct((ms, n), F32)),
        name="ada",
    )(c_p, c_s, w, b)


def _disc_kernel(lre_ref, lim_ref, ldt_ref, bre_ref, bim_ref, ar_ref, ai_ref, br_ref, bi_ref):
    dt = jnp.exp(ldt_ref[...])
    lr = lre_ref[...]
    li = lim_ref[...]
    mag = jnp.exp(lr * dt)
    ar = mag * jnp.cos(li * dt)
    ai = mag * jnp.sin(li * dt)
    den = lr * lr + li * li
    cr = ((ar - 1) * lr + ai * li) / den
    ci = (ai * lr - (ar - 1) * li) / den
    ar_ref[...] = ar
    ai_ref[...] = ai
    bre = bre_ref[...]
    bim = bim_ref[...]
    br_ref[...] = cr * bre - ci * bim
    bi_ref[...] = cr * bim + ci * bre


def _discretize(lam_re, lam_im, log_dt, b_re, b_im):
    g, p, h = b_re.shape
    o1 = jax.ShapeDtypeStruct((g, 1, p), F32)
    o2 = jax.ShapeDtypeStruct((g, h, p), F32)
    return pl.pallas_call(_disc_kernel, out_shape=(o1, o1, o2, o2), name="ssm_disc")(
        lam_re.reshape(g, 1, p), lam_im.reshape(g, 1, p), log_dt.reshape(g, 1, 1),
        b_re.transpose(0, 2, 1), b_im.transpose(0, 2, 1))


def _s_pieces(r0, nrows, grp, pitch):
    return [(i * grp, slice((r0 // grp + i) * pitch, (r0 // grp + i) * pitch + grp)) for i in range(nrows // grp)]


def _front(r0, nrows, h_scr, win_ref, lng_ref, lnb_ref, u_scr, vn_scr, s_scr, grp, pitch):
    rows = slice(r0, r0 + nrows)
    proj = jnp.dot(h_scr[rows, :], win_ref[...], preferred_element_type=F32)
    u_scr[rows, :] = jax.nn.gelu(proj[:, :D_A])
    vraw = jax.nn.gelu(proj[:, D_A:2 * D_A])
    for h in range(N_HEADS):
        cols = slice(h * HEAD_DIM, (h + 1) * HEAD_DIM)
        vh = vraw[:, cols]
        mu = jnp.mean(vh, axis=-1, keepdims=True)
        dv = vh - mu
        var = jnp.mean(dv * dv, axis=-1, keepdims=True)
        vn_scr[rows, cols] = dv * lax.rsqrt(var + EPS) * lng_ref[:, cols] + lnb_ref[:, cols]
    for k in range(N_SLABS):
        for off, prow in _s_pieces(r0, nrows, grp, pitch):
            s_scr[k, prow, :] = proj[off:off + grp, 2 * D_A + k * LANES:2 * D_A + (k + 1) * LANES]


def _scan_slab(bu, state, avec_ref, k, rows_per_step, steps):
    c_re = slice(0, SLAB_STATES)
    c_im = slice(SLAB_STATES, SLAB_COLS)
    s_re = slice(k * SLAB_STATES, (k + 1) * SLAB_STATES)
    s_im = slice(STATE_COLS // 2 + k * SLAB_STATES, STATE_COLS // 2 + (k + 1) * SLAB_STATES)
    ar = jnp.broadcast_to(avec_ref[:, s_re], (SUBLANES, SLAB_STATES))
    ai = jnp.broadcast_to(avec_ref[:, s_im], (SUBLANES, SLAB_STATES))
    for rc in range(rows_per_step // SUBLANES):
        r0 = rc * SUBLANES
        sr = state[r0:r0 + SUBLANES, s_re]
        si = state[r0:r0 + SUBLANES, s_im]
        for t in range(steps):
            rows = slice(t * rows_per_step + r0, t * rows_per_step + r0 + SUBLANES)
            nr = ar * sr - ai * si + bu[rows, c_re]
            ni = ar * si + ai * sr + bu[rows, c_im]
            bu[rows, c_re] = nr
            bu[rows, c_im] = ni
            sr, si = nr, ni
        state[r0:r0 + SUBLANES, s_re] = sr
        state[r0:r0 + SUBLANES, s_im] = si


def _s5(st_ref, bu_scr, yt_ref, state, avec_ref, wb_ref, wc_ref, rows_per_step, steps):
    for k in range(N_SLABS):
        bu = bu_scr.at[k % 2]
        bu[...] = jnp.dot(st_ref[k].astype(BF16), wb_ref[k], preferred_element_type=F32)
        _scan_slab(bu, state, avec_ref, k, rows_per_step, steps)
        yt_ref[k] = jnp.dot(bu[...].astype(BF16), wc_ref[k], preferred_element_type=F32)


def _back(r0, nrows, s_scr, ab_scr, wglu_ref, bglu_ref, wout_ref, grp, pitch):
    rows = slice(r0, r0 + nrows)
    pieces = _s_pieces(r0, nrows, grp, pitch)
    y = jax.nn.gelu(jnp.concatenate(
        [jnp.concatenate([s_scr[k, prow, :] for _, prow in pieces], axis=0) for k in range(N_SLABS)], axis=-1))
    gate = jnp.dot(y.astype(BF16), wglu_ref[...], preferred_element_type=F32) + bglu_ref[...]
    ab_scr[rows, D_A:] = (y * jax.nn.sigmoid(gate)).astype(BF16)
    return jnp.dot(ab_scr[rows, :], wout_ref[...], preferred_element_type=F32)


def _route(h2_bf, wrc_ref, brt_ref):
    both = jnp.dot(h2_bf, wrc_ref[...], preferred_element_type=F32)
    logits = both[:, :ROUTE_LANES] + both[:, ROUTE_LANES:] + brt_ref[...]
    n = logits.shape[0]
    lane = lax.broadcasted_iota(jnp.int32, (n, ROUTE_LANES), 1)
    lane_f = lane.astype(F32)
    big = jnp.float32(1e9)
    ninf = jnp.float32(-jnp.inf)
    is_g = lane < N_EXPERT_GROUPS
    gl = jnp.where(is_g, logits, ninf)
    gmax = jnp.max(gl, axis=-1, keepdims=True)
    gidx = jnp.min(jnp.where(gl == gmax, lane_f, big), axis=-1, keepdims=True)
    gsum = jnp.sum(jnp.where(is_g, jnp.exp(logits - gmax), 0.0), axis=-1, keepdims=True)
    g_w = 1.0 / gsum
    elo = N_EXPERT_GROUPS + EXPERTS_PER_GROUP * gidx
    emask = (lane_f >= elo) & (lane_f < elo + EXPERTS_PER_GROUP)
    el = jnp.where(emask, logits, ninf)
    t1 = jnp.max(el, axis=-1, keepdims=True)
    i1 = jnp.min(jnp.where(el == t1, lane_f, big), axis=-1, keepdims=True)
    el2 = jnp.where(lane_f == i1, ninf, el)
    t2 = jnp.max(el2, axis=-1, keepdims=True)
    i2 = jnp.min(jnp.where(el2 == t2, lane_f, big), axis=-1, keepdims=True)
    e21 = jnp.exp(t2 - t1)
    den = 1.0 + e21
    gate1 = g_w * (1.0 / den)
    gate2 = g_w * (e21 / den)
    e1 = i1 - N_EXPERT_GROUPS
    e2 = i2 - N_EXPERT_GROUPS
    packed = jnp.where(lane == 0, e1, jnp.where(lane == 1, e2, jnp.where(lane == 2, gate1,
                                                                           jnp.where(lane == 3, gate2, 0.0))))
    hist = jnp.sum(((lane_f == e1) | (lane_f == e2)).astype(F32), axis=0, keepdims=True)
    return packed.T[:SUBLANES], hist


def _mixer_prompt_kernel(x_ref, ada_ref, h0_ref, g1_ref, g2_ref, win_ref, lng_ref, lnb_ref, wsp_ref, bsp_ref,
                         avec_ref, wb_ref, wc_ref, dsk_ref, wglu_ref, bglu_ref, wout_ref, wrc_ref, brt_ref,
                         x1_ref, h2_ref, route_ref, hist_ref, state_ref,
                         h_scr, u_scr, vn_scr, s_scr, st_scr, yt_scr, bu_scr, ab_scr):
    nb = x_ref.shape[0]
    half = (nb // 2) * CHUNK
    D = D_MODEL
    pitch = S_PITCH
    step = pl.program_id(0)

    @pl.when(step == 0)
    def _():
        state_ref[...] = h0_ref[...]
        hist_ref[...] = jnp.zeros_like(hist_ref)

    def mod(b, i):
        return ada_ref[b:b + 1, i * D:(i + 1) * D]

    for b in range(nb):
        hb = _rms(x_ref[b], g1_ref[...]) * (1 + mod(b, 1)) + mod(b, 0)
        h_scr[b * CHUNK:(b + 1) * CHUNK, :] = hb.astype(BF16)

    for r0 in (0, half):
        _front(r0, half, h_scr, win_ref, lng_ref, lnb_ref, u_scr, vn_scr, s_scr, CHUNK, pitch)

    for b in range(nb):
        rows = slice(b * CHUNK, (b + 1) * CHUNK)
        for h in range(N_HEADS):
            cols = slice(h * HEAD_DIM, (h + 1) * HEAD_DIM)
            mixed = jnp.dot(wsp_ref[h], vn_scr[rows, cols].astype(BF16), preferred_element_type=F32) + bsp_ref[h]
            ab_scr[rows, cols] = (u_scr[rows, cols] * mixed).astype(BF16)

    for k in range(N_SLABS):
        for t in range(CHUNK):
            st_scr[k, t * nb:(t + 1) * nb, :] = s_scr[k, pl.ds(t, nb, stride=pitch), :]
    _s5(st_scr, bu_scr, yt_scr, state_ref, avec_ref, wb_ref, wc_ref, nb, CHUNK)
    for k in range(N_SLABS):
        dsk = dsk_ref[:, k * LANES:(k + 1) * LANES]
        for t in range(CHUNK):
            sel = pl.ds(t, nb, stride=pitch)
            s_scr[k, sel, :] = yt_scr[k, t * nb:(t + 1) * nb, :] + dsk * s_scr[k, sel, :]

    for r0 in (0, half):
        mix = _back(r0, half, s_scr, ab_scr, wglu_ref, bglu_ref, wout_ref, CHUNK, pitch)
        for bl in range(nb // 2):
            b = r0 // CHUNK + bl
            x1 = x_ref[b] + mod(b, 2) * mix[bl * CHUNK:(bl + 1) * CHUNK, :]
            x1_ref[b] = x1
            h2 = _rms(x1, g2_ref[...]) * (1 + mod(b, 4)) + mod(b, 3)
            hi = h2.astype(BF16)
            h2_ref[b] = hi
            route, hist = _route(hi, wrc_ref, brt_ref)
            route_ref[b, 0] = route
            hist_ref[b:b + 1, :] = hist_ref[b:b + 1, :] + hist


def _const_spec(shape):
    nd = len(shape)
    return pl.BlockSpec(shape, lambda *_: (0,) * nd, pipeline_mode=pl.Buffered(1))


def _mixer_prompt(x, ada, h0, wts):
    nb, seq, D = x.shape
    n_chunks = seq // CHUNK
    R = nb * CHUNK
    weight_specs = [_const_spec(w.shape) for w in wts]
    in_specs = [pl.BlockSpec((nb, CHUNK, D), lambda i: (0, i, 0)),
                _const_spec(ada.shape), _const_spec(h0.shape)] + weight_specs
    out_shape = (jax.ShapeDtypeStruct((nb, seq, D), F32),
                 jax.ShapeDtypeStruct((nb, seq, D), BF16),
                 jax.ShapeDtypeStruct((nb, n_chunks, SUBLANES, ROUTE_LANES), F32),
                 jax.ShapeDtypeStruct((nb, ROUTE_LANES), F32),
                 jax.ShapeDtypeStruct((nb, STATE_COLS), F32))
    out_specs = (pl.BlockSpec((nb, CHUNK, D), lambda i: (0, i, 0)),
                 pl.BlockSpec((nb, CHUNK, D), lambda i: (0, i, 0)),
                 pl.BlockSpec((nb, 1, SUBLANES, ROUTE_LANES), lambda i: (0, i, 0, 0)),
                 pl.BlockSpec((nb, ROUTE_LANES), lambda i: (0, 0)),
                 pl.BlockSpec((nb, STATE_COLS), lambda i: (0, 0)))
    scratch = [pltpu.VMEM((R, D), BF16),
               pltpu.VMEM((R, D_A), F32),
               pltpu.VMEM((R, D_A), F32),
               pltpu.VMEM((N_SLABS, nb * S_PITCH, LANES), F32),
               pltpu.VMEM((N_SLABS, R, LANES), F32),
               pltpu.VMEM((N_SLABS, R, LANES), F32),
               pltpu.VMEM((2, R, SLAB_COLS), F32),
               pltpu.VMEM((R, D), BF16)]
    return pl.pallas_call(
        _mixer_prompt_kernel,
        grid=(n_chunks,),
        in_specs=in_specs,
        out_specs=out_specs,
        out_shape=out_shape,
        scratch_shapes=scratch,
        compiler_params=pltpu.CompilerParams(dimension_semantics=("arbitrary",), vmem_limit_bytes=VMEM_LIMIT),
        name="mixer_prompt",
    )(x, ada, h0, *wts)


def _mixer_sample_kernel(wsm_ref, bsm_ref, x_ref, ada_ref, h0_ref, g1_ref, g2_ref, win_ref, lng_ref, lnb_ref,
                         avec_ref, wb_ref, wc_ref, dsk_ref, wglu_ref, bglu_ref, wout_ref, wrc_ref, brt_ref,
                         x1_ref, h2_ref, route_ref, hist_ref, state_ref, v_ref,
                         h_scr, u_scr, vn_scr, s_scr, yt_scr, bu_scr, ab_scr):
    T, nb, D = x_ref.shape
    R = T * nb
    half = R // 2

    def mod(i):
        return ada_ref[:, i * D:(i + 1) * D]

    state_ref[...] = h0_ref[...]
    for t in range(T):
        ht = _rms(x_ref[t], g1_ref[...]) * (1 + mod(1)) + mod(0)
        h_scr[t * nb:(t + 1) * nb, :] = ht.astype(BF16)

    for r0 in (0, half):
        _front(r0, half, h_scr, win_ref, lng_ref, lnb_ref, u_scr, vn_scr, s_scr, half, half)

    for t in range(T):
        rows = slice(t * nb, (t + 1) * nb)
        v_ref[t] = vn_scr[rows, :]
        for h in range(N_HEADS):
            cols = slice(h * HEAD_DIM, (h + 1) * HEAD_DIM)
            acc = jnp.full((nb, HEAD_DIM), bsm_ref[h * T + t], F32)
            for s in range(t + 1):
                acc = acc + wsm_ref[(h * T + t) * T + s] * vn_scr[s * nb:(s + 1) * nb, cols]
            ab_scr[rows, cols] = (u_scr[rows, cols] * acc).astype(BF16)

    _s5(s_scr, bu_scr, yt_scr, state_ref, avec_ref, wb_ref, wc_ref, nb, T)
    for k in range(N_SLABS):
        s_scr[k] = yt_scr[k] + dsk_ref[:, k * LANES:(k + 1) * LANES] * s_scr[k]

    hist_total = jnp.zeros((1, ROUTE_LANES), F32)
    for r0 in (0, half):
        mix = _back(r0, half, s_scr, ab_scr, wglu_ref, bglu_ref, wout_ref, half, half)
        for tl in range(T // 2):
            t = r0 // nb + tl
            x1 = x_ref[t] + mod(2) * mix[tl * nb:(tl + 1) * nb, :]
            x1_ref[t] = x1
            h2 = _rms(x1, g2_ref[...]) * (1 + mod(4)) + mod(3)
            hi = h2.astype(BF16)
            h2_ref[t] = hi
            route, hist = _route(hi, wrc_ref, brt_ref)
            route_ref[t] = route
            hist_total = hist_total + hist
    hist_ref[...] = hist_total


def _mixer_sample(x_t, ada, h0, w_small, b_small, wts):
    T, nb, D = x_t.shape
    R = T * nb
    smem = pl.BlockSpec(memory_space=pltpu.SMEM)
    out_shape = (jax.ShapeDtypeStruct((T, nb, D), F32),
                 jax.ShapeDtypeStruct((T, nb, D), BF16),
                 jax.ShapeDtypeStruct((T, SUBLANES, ROUTE_LANES), F32),
                 jax.ShapeDtypeStruct((1, ROUTE_LANES), F32),
                 jax.ShapeDtypeStruct((nb, STATE_COLS), F32),
                 jax.ShapeDtypeStruct((T, nb, D_A), F32))
    scratch = [pltpu.VMEM((R, D), BF16),
               pltpu.VMEM((R, D_A), F32),
               pltpu.VMEM((R, D_A), F32),
               pltpu.VMEM((N_SLABS, R, LANES), F32),
               pltpu.VMEM((N_SLABS, R, LANES), F32),
               pltpu.VMEM((2, R, SLAB_COLS), F32),
               pltpu.VMEM((R, D), BF16)]
    vmem = pl.BlockSpec(memory_space=pltpu.VMEM)
    return pl.pallas_call(
        _mixer_sample_kernel,
        in_specs=[smem, smem] + [vmem] * (3 + len(wts)),
        out_specs=(vmem,) * 6,
        out_shape=out_shape,
        scratch_shapes=scratch,
        compiler_params=pltpu.CompilerParams(vmem_limit_bytes=VMEM_LIMIT),
        name="mixer_sample",
    )(w_small, b_small, x_t, ada, h0, *wts)


def _rows_to_tiles(tiles_ref, row0, val):
    n = val.shape[0]
    for c in range(val.shape[1] // LANES):
        tiles_ref[pl.ds(row0 * SUBLANES + c, n, stride=SUBLANES), :] = val[:, c * LANES:(c + 1) * LANES]


def _tiles_to_rows(tiles_ref, row0, n):
    return jnp.concatenate([tiles_ref[pl.ds(row0 * SUBLANES + c, n, stride=SUBLANES), :] for c in range(SUBLANES)],
                           axis=-1)


def _moe_kernel(starts_ref, tok_ref, pos_ref, gate_ref, h2_ref, x1_ref, ada_ref, gfin_ref, w1_ref, w3_ref, w2_ref,
                out_ref, h_tiles, y_tiles, xb_even, xb_odd, cp_smem, tab_smem, *, chunk):
    t = pl.program_id(0)
    e = pl.program_id(1)
    n_expert_steps = N_EXPERTS // EXPERTS_PER_STEP
    ngrp, rows_g, D = h2_ref.shape
    n_tok = ngrp * rows_g
    gb, rb, _ = x1_ref.shape
    epi_tok = gb * rb
    mrows = ada_ref.shape[1]

    def tile_of(row):
        return pl.ds(pl.multiple_of(row * SUBLANES, SUBLANES), SUBLANES)

    def tile_at(row8):
        return pl.ds(pl.multiple_of(row8, SUBLANES), SUBLANES)

    @pl.when(e == 0)
    def _():
        for gi in range(ngrp):
            for ci in range(rows_g // chunk):
                h2 = h2_ref[gi, ci * chunk:(ci + 1) * chunk, :].astype(F32)
                _rows_to_tiles(h_tiles, gi * rows_g + ci * chunk, h2)

    @pl.when(e == 0)
    def _():
        gp = jnp.int32(0)
        for s in range(N_EXPERTS // EXPERTS_PER_STEP):
            cp_smem[s] = gp
            st = [starts_ref[t, s * EXPERTS_PER_STEP + i] for i in range(EXPERTS_PER_STEP + 1)]
            first = [jnp.int32(0)]
            for i in range(EXPERTS_PER_STEP):
                first.append(first[i] + lax.shift_right_logical(st[i + 1] - st[i] + (MOE_BLOCK - 1), MOE_BLOCK_SHIFT))
            nb = first[EXPERTS_PER_STEP]

            def locate(f, st=st, first=first):
                el, fb, sb = jnp.int32(0), first[0], st[0]
                for i in range(1, EXPERTS_PER_STEP):
                    hit = f >= first[i]
                    el = jnp.where(hit, i, el)
                    fb = jnp.where(hit, first[i], fb)
                    sb = jnp.where(hit, st[i], sb)
                return sb + (f - fb) * MOE_BLOCK, el

            def add_pair(p, gp, nb=nb, locate=locate):
                base_a, el_a = locate(2 * p)
                base_b, el_b = locate(jnp.minimum(2 * p + 1, nb - 1))
                for i, v in enumerate((base_a, base_b, el_a, el_b)):
                    tab_smem[PAIR_FIELDS * gp + i] = v
                return gp + 1

            gp = lax.fori_loop(0, lax.shift_right_logical(nb + 1, 1), add_pair, gp)
        cp_smem[N_EXPERTS // EXPERTS_PER_STEP] = gp
        for i in range(PAIR_FIELDS):
            tab_smem[PAIR_FIELDS * gp + i] = jnp.int32(0)

    def pair_entry(gp):
        return [tab_smem[PAIR_FIELDS * gp + i] for i in range(PAIR_FIELDS)]

    @pl.when(e == 0)
    def _():
        base_a, base_b, _, _ = pair_entry(0)

        def gather8(j8, c):
            for jj in range(SUBLANES):
                j = j8 * SUBLANES + jj
                xb_even[tile_of(j), :] = h_tiles[tile_at(tok_ref[base_a + j]), :]
                xb_even[tile_of(MOE_BLOCK + j), :] = h_tiles[tile_at(tok_ref[base_b + j]), :]
            return c

        lax.fori_loop(0, MOE_BLOCK // SUBLANES, gather8, 0)

    def pair_body(gp, cur_tiles, next_tiles):
        next_a, next_b, _, _ = pair_entry(gp + 1)
        n_slices = 8
        per = 2 * MOE_BLOCK // n_slices
        slices = iter(range(n_slices))

        def gather_slice():
            s = next(slices)
            for r in range(s * per, (s + 1) * per):
                tok8 = tok_ref[next_a + r] if r < MOE_BLOCK else tok_ref[next_b + r - MOE_BLOCK]
                next_tiles[r * SUBLANES:(r + 1) * SUBLANES, :] = h_tiles[tile_at(tok8), :]

        base_a, base_b, el_a, el_b = pair_entry(gp)
        xa = _tiles_to_rows(cur_tiles, 0, MOE_BLOCK).astype(BF16)
        gather_slice()
        a1 = jnp.dot(xa, w1_ref[el_a], preferred_element_type=F32)
        gather_slice()
        a3 = jnp.dot(xa, w3_ref[el_a], preferred_element_type=F32)
        gather_slice()
        xb = _tiles_to_rows(cur_tiles, MOE_BLOCK, MOE_BLOCK).astype(BF16)
        b1 = jnp.dot(xb, w1_ref[el_b], preferred_element_type=F32)
        gather_slice()
        b3 = jnp.dot(xb, w3_ref[el_b], preferred_element_type=F32)
        gather_slice()
        ya = jnp.dot((jax.nn.silu(a1) * a3).astype(BF16), w2_ref[el_a], preferred_element_type=F32)
        gather_slice()
        yb = jnp.dot((jax.nn.silu(b1) * b3).astype(BF16), w2_ref[el_b], preferred_element_type=F32)
        gather_slice()
        _rows_to_tiles(y_tiles, base_a, ya)
        gather_slice()
        _rows_to_tiles(y_tiles, base_b, yb)

    def pairs(gp, count):
        def run(first, second):
            bufs = (first, second)
            for i in range(count):
                pair_body(gp + i, bufs[i % 2], bufs[(i + 1) % 2])

        @pl.when((gp & 1) == 0)
        def _():
            run(xb_even, xb_odd)

        @pl.when((gp & 1) == 1)
        def _():
            run(xb_odd, xb_even)

    @pl.when(e < n_expert_steps)
    def _():
        first, last = cp_smem[e], cp_smem[e + 1]
        n_double = lax.shift_right_logical(last - first, 1)

        def double(i, carry):
            pairs(first + 2 * i, 2)
            return carry

        lax.fori_loop(0, n_double, double, 0)

        @pl.when(((last - first) & 1) == 1)
        def _():
            pairs(last - 1, 1)

    @pl.when(e >= n_expert_steps)
    def _():
        tok0 = (e - n_expert_steps) * epi_tok

        def combine8(t8, c):
            for tt in range(SUBLANES):
                tok = tok0 + t8 * SUBLANES + tt
                y0 = y_tiles[tile_at(pos_ref[tok]), :]
                y1 = y_tiles[tile_at(pos_ref[n_tok + tok]), :]
                h_tiles[tile_of(tok), :] = gate_ref[tok] * y0 + gate_ref[n_tok + tok] * y1
            return c

        lax.fori_loop(0, epi_tok // SUBLANES, combine8, 0)
        sub = min(rb, chunk)
        for gi in range(gb):
            for ci in range(rb // sub):
                rows = slice(ci * sub, (ci + 1) * sub)
                gt2 = ada_ref[0, :, 5 * D:6 * D] if mrows == 1 else ada_ref[0, rows, 5 * D:6 * D]
                moe = _tiles_to_rows(h_tiles, tok0 + gi * rb + ci * sub, sub)
                x2 = x1_ref[gi, rows, :] + gt2 * moe
                out_ref[gi, rows, :] = _rms(x2, gfin_ref[...])


def _moe(h2, x1, ada, starts, tok, pos, gate, gfin, w1, w3, w2, *, n_tiles, chunk):
    ngrp = x1.shape[0] // n_tiles
    rows_g, D = x1.shape[1], x1.shape[2]
    mrows = ada.shape[1]
    Tt = ngrp * rows_g
    n_expert_steps = N_EXPERTS // EXPERTS_PER_STEP
    epi_tok = Tt // MOE_EPILOGUE_STEPS
    if ngrp == 1:
        epi_block = (1, epi_tok, D)

        def epi_map(t, e, st):
            return (t, jnp.maximum(e - n_expert_steps, 0), 0)
    else:
        assert n_tiles == 1 and epi_tok % rows_g == 0
        epi_block = (epi_tok // rows_g, rows_g, D)

        def epi_map(t, e, st):
            return (jnp.maximum(e - n_expert_steps, 0), 0, 0)

    def weight_map(t, e, st):
        return (jnp.minimum(e, n_expert_steps - 1), 0, 0)
    lp = tok.shape[0] // n_tiles
    max_pairs = (2 * Tt // MOE_BLOCK + N_EXPERTS + N_EXPERTS // EXPERTS_PER_STEP) // 2 + 1
    grid_spec = pltpu.PrefetchScalarGridSpec(
        num_scalar_prefetch=1,
        grid=(n_tiles, n_expert_steps + MOE_EPILOGUE_STEPS),
        in_specs=[
            pl.BlockSpec((lp,), lambda t, e, st: (t,), memory_space=pltpu.SMEM),
            pl.BlockSpec((2 * Tt,), lambda t, e, st: (t,), memory_space=pltpu.SMEM),
            pl.BlockSpec((2 * Tt,), lambda t, e, st: (t,), memory_space=pltpu.SMEM),
            pl.BlockSpec((ngrp, rows_g, D), lambda t, e, st: (t, 0, 0)),
            pl.BlockSpec(epi_block, epi_map),
            pl.BlockSpec((1, mrows, 6 * D), lambda t, e, st: (t, 0, 0)),
            pl.BlockSpec((1, D), lambda t, e, st: (0, 0)),
            pl.BlockSpec((EXPERTS_PER_STEP, D, D_EXPERT), weight_map),
            pl.BlockSpec((EXPERTS_PER_STEP, D, D_EXPERT), weight_map),
            pl.BlockSpec((EXPERTS_PER_STEP, D_EXPERT, D), weight_map),
        ],
        out_specs=pl.BlockSpec(epi_block, epi_map),
        scratch_shapes=[pltpu.VMEM((Tt * SUBLANES, LANES), F32),
                        pltpu.VMEM(((2 * Tt + MOE_BLOCK) * SUBLANES, LANES), F32),
                        pltpu.VMEM((2 * MOE_BLOCK * SUBLANES, LANES), F32),
                        pltpu.VMEM((2 * MOE_BLOCK * SUBLANES, LANES), F32),
                        pltpu.SMEM((N_EXPERTS // EXPERTS_PER_STEP + 1,), jnp.int32),
                        pltpu.SMEM((max_pairs * PAIR_FIELDS,), jnp.int32)],
    )
    return pl.pallas_call(
        functools.partial(_moe_kernel, chunk=chunk),
        grid_spec=grid_spec,
        out_shape=jax.ShapeDtypeStruct(x1.shape, F32),
        compiler_params=pltpu.CompilerParams(dimension_semantics=("arbitrary", "arbitrary"),
                                             vmem_limit_bytes=VMEM_LIMIT),
        name="moe",
    )(starts, tok, pos, gate, h2, x1, ada, gfin, w1, w3, w2)


def _dispatch_tables(routes):
    tile_tokens = [r.shape[1] * ROUTE_LANES for r, _ in routes]
    t_max = max(tile_tokens)
    es, gs, masks = [], [], []
    for (route, _), Tt in zip(routes, tile_tokens):
        assert Tt & (Tt - 1) == 0
        n = route.shape[0]
        by_k = route[:, :, :4, :].transpose(0, 2, 1, 3).reshape(n, 4, Tt)
        extra = 2 * (t_max - Tt)
        es.append(jnp.pad(by_k[:, :2].astype(jnp.int32).reshape(n, 2 * Tt), ((0, 0), (0, extra)),
                          constant_values=N_EXPERTS))
        gs.append(jnp.pad(by_k[:, 2:].reshape(n, 2 * Tt), ((0, 0), (0, extra))))
        masks.append(jnp.full((n, 1), Tt - 1, jnp.int32))
    flat_e = jnp.concatenate(es, axis=0)
    flat_g = jnp.concatenate(gs, axis=0)
    n_tiles = flat_e.shape[0]
    order = jnp.argsort(flat_e, axis=-1, stable=False).astype(jnp.int32)
    pos = jnp.argsort(order, axis=-1).astype(jnp.int32) * SUBLANES
    tok_s = (order & jnp.concatenate(masks, axis=0)) * SUBLANES
    counts = jnp.concatenate([h[:, :N_EXPERTS] for _, h in routes], axis=0).astype(jnp.int32)
    starts = jnp.concatenate([jnp.zeros((n_tiles, 1), jnp.int32), jnp.cumsum(counts, axis=-1, dtype=jnp.int32)],
                             axis=-1)

    def tok_len(Tt):
        return -(-(2 * Tt + MOE_BLOCK) // SMEM_PAD) * SMEM_PAD

    tok_p = jnp.pad(tok_s, ((0, 0), (0, tok_len(t_max) - 2 * t_max)))
    Tt = t_max


    out, r0 = [], 0
    for (route, _), Tt in zip(routes, tile_tokens):
        r1 = r0 + route.shape[0]
        out.append((starts[r0:r1], tok_p[r0:r1, :tok_len(Tt)].reshape(-1),
                    pos[r0:r1, :2 * Tt].reshape(-1), flat_g[r0:r1, :2 * Tt].reshape(-1)))
        r0 = r1
    return out


def _pack_state(h_re, h_im):
    b = h_re.shape[0]
    return jnp.concatenate([h_re.reshape(b, STATE_COLS // 2), h_im.reshape(b, STATE_COLS // 2)], axis=-1)


def _unpack_state(st):
    b = st.shape[0]
    return (st[:, :STATE_COLS // 2].reshape(b, N_SSM_GROUPS, SSM_STATE),
            st[:, STATE_COLS // 2:].reshape(b, N_SSM_GROUPS, SSM_STATE))


def kernel(x_prompt, x_sample, state_ssm_re, state_ssm_im, c_prompt, c_sample, w_ada, b_ada, g_norm1, g_norm2, w_in, ln_g, ln_b, w_s, b_s, lam_re, lam_im, log_dt, ssm_b_re, ssm_b_im, ssm_c_re, ssm_c_im, ssm_d, w_glu, b_glu, w_out, w_group, b_group, w_expert, b_expert, w1, w3, w2, g_final):
    depth = w_ada.shape[0]
    assert depth == 1, "the final RMSNorm is fused into the (single) layer's MoE epilogue"
    B, L, D = x_prompt.shape
    Bs, Ls, _ = x_sample.shape
    xp = x_prompt
    xs_t = x_sample.transpose(1, 0, 2)
    eye = jnp.eye(SLAB_GROUPS, dtype=F32)
    tril = jnp.tril(jnp.ones((CHUNK, CHUNK), dtype=bool))
    p_re, p_im, s_re, s_im, s_v = [], [], [], [], []
    for l in range(depth):
        ada_p, ada_s = _ada(c_prompt, c_sample, w_ada[l], b_ada[l][None])

        ar, ai, br, bi = _discretize(lam_re[l], lam_im[l], log_dt[l], ssm_b_re[l], ssm_b_im[l])
        avec = jnp.concatenate([ar.reshape(1, STATE_COLS // 2), ai.reshape(1, STATE_COLS // 2)], axis=-1)

        def blockdiag_in(w):
            w4 = w.reshape(N_SLABS, SLAB_GROUPS, SSM_GROUP, SSM_STATE)
            return jnp.einsum('kghp,gG->kghGp', w4, eye).reshape(N_SLABS, LANES, SLAB_STATES)

        def blockdiag_out(w):
            w4 = w.reshape(N_SLABS, SLAB_GROUPS, SSM_GROUP, SSM_STATE)
            return jnp.einsum('kghp,gG->kgpGh', w4, eye).reshape(N_SLABS, SLAB_STATES, LANES)

        wb = jnp.concatenate([blockdiag_in(br), blockdiag_in(bi)], axis=-1).astype(BF16)
        wc = jnp.concatenate([blockdiag_out(ssm_c_re[l]), -blockdiag_out(ssm_c_im[l])], axis=1).astype(BF16)

        lane_pad = ROUTE_LANES - N_EXPERT_GROUPS - N_EXPERTS
        wr = jnp.pad(jnp.concatenate([w_group[l], w_expert[l]], axis=1), ((0, 0), (0, lane_pad)))
        wr_hi = wr.astype(BF16)
        wr_lo = (wr - wr_hi.astype(F32)).astype(BF16)
        br_t = jnp.pad(jnp.concatenate([b_group[l], b_expert[l]]), (0, lane_pad))[None]

        g1 = g_norm1[l][None]
        g2 = g_norm2[l][None]
        shared = dict(
            win=w_in[l].astype(BF16), lng=ln_g[l].reshape(1, D_A), lnb=ln_b[l].reshape(1, D_A),
            avec=avec, wb=wb, wc=wc, dsk=ssm_d[l].reshape(1, D_B), wglu=w_glu[l].astype(BF16),
            bglu=b_glu[l][None], wout=w_out[l].astype(BF16), wrc=jnp.concatenate([wr_hi, wr_lo], axis=1), brt=br_t)
        w_masked = jnp.where(tril[None], w_s[l], jnp.zeros_like(w_s[l]))
        wsp = w_masked.astype(BF16)
        bsp = jnp.broadcast_to(b_s[l][:, :, None], (N_HEADS, CHUNK, HEAD_DIM))
        wts_p = (g1, g2, shared['win'], shared['lng'], shared['lnb'], wsp, bsp, shared['avec'], shared['wb'],
                 shared['wc'], shared['dsk'], shared['wglu'], shared['bglu'], shared['wout'], shared['wrc'],
                 shared['brt'])
        wts_s = (g1, g2, shared['win'], shared['lng'], shared['lnb'], shared['avec'], shared['wb'],
                 shared['wc'], shared['dsk'], shared['wglu'], shared['bglu'], shared['wout'], shared['wrc'],
                 shared['brt'])

        w1b, w3b, w2b = w1[l].astype(BF16), w3[l].astype(BF16), w2[l].astype(BF16)
        gfin = g_final[None]

        h0p = jnp.zeros((B, STATE_COLS), F32)
        x1p, h2p, route_p, hist_p, st_p = _mixer_prompt(xp, ada_p, h0p, wts_p)
        h0s = _pack_state(state_ssm_re[l].astype(F32), state_ssm_im[l].astype(F32))
        w_small = w_masked[:, :Ls, :Ls].reshape(-1)
        b_small = b_s[l][:, :Ls].reshape(-1)
        x1s, h2s, route_s, hist_s, st_s, v_s = _mixer_sample(xs_t, ada_s, h0s, w_small, b_small, wts_s)

        (tables_p,) = _dispatch_tables([(route_p, hist_p)])
        (tables_s,) = _dispatch_tables([(route_s[None], hist_s)])
        xp = _moe(h2p, x1p, ada_p[:, None, :], *tables_p, gfin, w1b, w3b, w2b, n_tiles=B, chunk=256)
        xs_t = _moe(h2s, x1s, ada_s[None], *tables_s, gfin, w1b, w3b, w2b, n_tiles=1, chunk=Bs)
        hr, hi = _unpack_state(st_p)
        p_re.append(hr.astype(state_ssm_re.dtype))
        p_im.append(hi.astype(state_ssm_im.dtype))
        hr, hi = _unpack_state(st_s)
        s_re.append(hr.astype(state_ssm_re.dtype))
        s_im.append(hi.astype(state_ssm_im.dtype))
        s_v.append(v_s.transpose(1, 0, 2))

    y_prompt = xp
    y_sample = xs_t.transpose(1, 0, 2)
    return (y_prompt, y_sample, jnp.stack(p_re), jnp.stack(p_im), jnp.stack(s_re), jnp.stack(s_im), jnp.stack(s_v))
```

```python
import functools

import jax
import jax.numpy as jnp
from jax import lax
from jax.experimental import pallas as pl
from jax.experimental.pallas import tpu as pltpu

F32 = jnp.float32
BF16 = jnp.bfloat16

D_MODEL = 1024
D_A = 512
D_B = 512
N_HEADS = 4
HEAD_DIM = 128
CHUNK = 128
N_SSM_GROUPS = 32
SSM_GROUP = 16
SSM_STATE = 64
N_SLABS = 4
SLAB_GROUPS = N_SSM_GROUPS // N_SLABS
SLAB_STATES = SLAB_GROUPS * SSM_STATE
SLAB_COLS = 2 * SLAB_STATES
STATE_COLS = N_SLABS * SLAB_COLS
N_EXPERT_GROUPS = 4
EXPERTS_PER_GROUP = 8
N_EXPERTS = 32
D_EXPERT = 256
EPS = 1e-6

LANES = 128
SUBLANES = 8
ROUTE_LANES = LANES
MOE_BLOCK = 128
MOE_BLOCK_SHIFT = MOE_BLOCK.bit_length() - 1
assert 1 << MOE_BLOCK_SHIFT == MOE_BLOCK
EXPERTS_PER_STEP = 4
MOE_EPILOGUE_STEPS = 4
PAIR_FIELDS = 4
S_PITCH = CHUNK + SUBLANES
SMEM_PAD = 1024
VMEM_LIMIT = 58 * 1024 * 1024


def _rms(xf, g):
    ms = jnp.mean(xf * xf, axis=-1, keepdims=True)
    return xf * lax.rsqrt(ms + EPS) * g


def _ada_kernel(cp_ref, cs_ref, w_ref, b_ref, op_ref, os_ref):
    mp = cp_ref.shape[0]
    s = jax.nn.silu(jnp.concatenate([cp_ref[...], cs_ref[...]], axis=0)).astype(BF16)
    out = jnp.dot(s, w_ref[...].astype(BF16), preferred_element_type=F32) + b_ref[...]
    op_ref[...] = out[:mp]
    os_ref[...] = out[mp:]


def _ada(c_p, c_s, w, b):
    mp, ms = c_p.shape[0], c_s.shape[0]
    n = w.shape[1]
    bn = 1024
    return pl.pallas_call(
        _ada_kernel,
        grid=(n // bn,),
        in_specs=[pl.BlockSpec((mp, D_MODEL), lambda j: (0, 0)),
                  pl.BlockSpec((ms, D_MODEL), lambda j: (0, 0)),
                  pl.BlockSpec((D_MODEL, bn), lambda j: (0, j)),
                  pl.BlockSpec((1, bn), lambda j: (0, j))],
        out_specs=(pl.BlockSpec((mp, bn), lambda j: (0, j)), pl.BlockSpec((ms, bn), lambda j: (0, j))),
        out_shape=(jax.ShapeDtypeStruct((mp, n), F32), jax.ShapeDtypeStruct((ms, n), F32)),
        name="ada",
    )(c_p, c_s, w, b)


def _disc_kernel(lre_ref, lim_ref, ldt_ref, bre_ref, bim_ref, ar_ref, ai_ref, br_ref, bi_ref):
    dt = jnp.exp(ldt_ref[...])
    lr = lre_ref[...]
    li = lim_ref[...]
    mag = jnp.exp(lr * dt)
    ar = mag * jnp.cos(li * dt)
    ai = mag * jnp.sin(li * dt)
    den = lr * lr + li * li
    cr = ((ar - 1) * lr + ai * li) / den
    ci = (ai * lr - (ar - 1) * li) / den
    ar_ref[...] = ar
    ai_ref[...] = ai
    bre = bre_ref[...]
    bim = bim_ref[...]
    br_ref[...] = cr * bre - ci * bim
    bi_ref[...] = cr * bim + ci * bre


def _discretize(lam_re, lam_im, log_dt, b_re, b_im):
    g, p, h = b_re.shape
    o1 = jax.ShapeDtypeStruct((g, 1, p), F32)
    o2 = jax.ShapeDtypeStruct((g, h, p), F32)
    return pl.pallas_call(_disc_kernel, out_shape=(o1, o1, o2, o2), name="ssm_disc")(
        lam_re.reshape(g, 1, p), lam_im.reshape(g, 1, p), log_dt.reshape(g, 1, 1),
        b_re.transpose(0, 2, 1), b_im.transpose(0, 2, 1))


def _s_pieces(r0, nrows, grp, pitch):
    return [(i * grp, slice((r0 // grp + i) * pitch, (r0 // grp + i) * pitch + grp)) for i in range(nrows // grp)]


def _front(r0, nrows, h_scr, win_ref, lng_ref, lnb_ref, u_scr, vn_scr, s_scr, grp, pitch):
    rows = slice(r0, r0 + nrows)
    proj = jnp.dot(h_scr[rows, :], win_ref[...], preferred_element_type=F32)
    u_scr[rows, :] = jax.nn.gelu(proj[:, :D_A])
    vraw = jax.nn.gelu(proj[:, D_A:2 * D_A])
    for h in range(N_HEADS):
        cols = slice(h * HEAD_DIM, (h + 1) * HEAD_DIM)
        vh = vraw[:, cols]
        mu = jnp.mean(vh, axis=-1, keepdims=True)
        dv = vh - mu
        var = jnp.mean(dv * dv, axis=-1, keepdims=True)
        vn_scr[rows, cols] = dv * lax.rsqrt(var + EPS) * lng_ref[:, cols] + lnb_ref[:, cols]
    for k in range(N_SLABS):
        for off, prow in _s_pieces(r0, nrows, grp, pitch):
            s_scr[k, prow, :] = proj[off:off + grp, 2 * D_A + k * LANES:2 * D_A + (k + 1) * LANES]


def _scan_slab(bu, state, avec_ref, k, rows_per_step, steps):
    c_re = slice(0, SLAB_STATES)
    c_im = slice(SLAB_STATES, SLAB_COLS)
    s_re = slice(k * SLAB_STATES, (k + 1) * SLAB_STATES)
    s_im = slice(STATE_COLS // 2 + k * SLAB_STATES, STATE_COLS // 2 + (k + 1) * SLAB_STATES)
    ar = jnp.broadcast_to(avec_ref[:, s_re], (SUBLANES, SLAB_STATES))
    ai = jnp.broadcast_to(avec_ref[:, s_im], (SUBLANES, SLAB_STATES))
    for rc in range(rows_per_step // SUBLANES):
        r0 = rc * SUBLANES
        sr = state[r0:r0 + SUBLANES, s_re]
        si = state[r0:r0 + SUBLANES, s_im]
        for t in range(steps):
            rows = slice(t * rows_per_step + r0, t * rows_per_step + r0 + SUBLANES)
            nr = ar * sr - ai * si + bu[rows, c_re]
            ni = ar * si + ai * sr + bu[rows, c_im]
            bu[rows, c_re] = nr
            bu[rows, c_im] = ni
            sr, si = nr, ni
        state[r0:r0 + SUBLANES, s_re] = sr
        state[r0:r0 + SUBLANES, s_im] = si


def _s5(st_ref, bu_scr, yt_ref, state, avec_ref, wb_ref, wc_ref, rows_per_step, steps):
    for k in range(N_SLABS):
        bu = bu_scr.at[k % 2]
        bu[...] = jnp.dot(st_ref[k].astype(BF16), wb_ref[k], preferred_element_type=F32)
        _scan_slab(bu, state, avec_ref, k, rows_per_step, steps)
        yt_ref[k] = jnp.dot(bu[...].astype(BF16), wc_ref[k], preferred_element_type=F32)


def _back(r0, nrows, s_scr, ab_scr, wglu_ref, bglu_ref, wout_ref, grp, pitch):
    rows = slice(r0, r0 + nrows)
    pieces = _s_pieces(r0, nrows, grp, pitch)
    y = jax.nn.gelu(jnp.concatenate(
        [jnp.concatenate([s_scr[k, prow, :] for _, prow in pieces], axis=0) for k in range(N_SLABS)], axis=-1))
    gate = jnp.dot(y.astype(BF16), wglu_ref[...], preferred_element_type=F32) + bglu_ref[...]
    ab_scr[rows, D_A:] = (y * jax.nn.sigmoid(gate)).astype(BF16)
    return jnp.dot(ab_scr[rows, :], wout_ref[...], preferred_element_type=F32)


def _route(h2_bf, wrc_ref, brt_ref):
    both = jnp.dot(h2_bf, wrc_ref[...], preferred_element_type=F32)
    logits = both[:, :ROUTE_LANES] + both[:, ROUTE_LANES:] + brt_ref[...]
    n = logits.shape[0]
    lane = lax.broadcasted_iota(jnp.int32, (n, ROUTE_LANES), 1)
    lane_f = lane.astype(F32)
    big = jnp.float32(1e9)
    ninf = jnp.float32(-jnp.inf)
    is_g = lane < N_EXPERT_GROUPS
    gl = jnp.where(is_g, logits, ninf)
    gmax = jnp.max(gl, axis=-1, keepdims=True)
    gidx = jnp.min(jnp.where(gl == gmax, lane_f, big), axis=-1, keepdims=True)
    gsum = jnp.sum(jnp.where(is_g, jnp.exp(logits - gmax), 0.0), axis=-1, keepdims=True)
    g_w = 1.0 / gsum
    elo = N_EXPERT_GROUPS + EXPERTS_PER_GROUP * gidx
    emask = (lane_f >= elo) & (lane_f < elo + EXPERTS_PER_GROUP)
    el = jnp.where(emask, logits, ninf)
    t1 = jnp.max(el, axis=-1, keepdims=True)
    i1 = jnp.min(jnp.where(el == t1, lane_f, big), axis=-1, keepdims=True)
    el2 = jnp.where(lane_f == i1, ninf, el)
    t2 = jnp.max(el2, axis=-1, keepdims=True)
    i2 = jnp.min(jnp.where(el2 == t2, lane_f, big), axis=-1, keepdims=True)
    e21 = jnp.exp(t2 - t1)
    den = 1.0 + e21
    gate1 = g_w * (1.0 / den)
    gate2 = g_w * (e21 / den)
    e1 = i1 - N_EXPERT_GROUPS
    e2 = i2 - N_EXPERT_GROUPS
    packed = jnp.where(lane == 0, e1, jnp.where(lane == 1, e2, jnp.where(lane == 2, gate1,
                                                                           jnp.where(lane == 3, gate2, 0.0))))
    hist = jnp.sum(((lane_f == e1) | (lane_f == e2)).astype(F32), axis=0, keepdims=True)
    return packed.T[:SUBLANES], hist


def _mixer_prompt_kernel(x_ref, ada_ref, h0_ref, g1_ref, g2_ref, win_ref, lng_ref, lnb_ref, wsp_ref, bsp_ref,
                         avec_ref, wb_ref, wc_ref, dsk_ref, wglu_ref, bglu_ref, wout_ref, wrc_ref, brt_ref,
                         x1_ref, h2_ref, route_ref, hist_ref, state_ref,
                         h_scr, u_scr, vn_scr, s_scr, st_scr, yt_scr, bu_scr, ab_scr, hi_scr):
    nb = x_ref.shape[0]
    half = (nb // 2) * CHUNK
    D = D_MODEL
    pitch = S_PITCH
    step = pl.program_id(0)

    @pl.when(step == 0)
    def _():
        state_ref[...] = h0_ref[...]
        hist_ref[...] = jnp.zeros_like(hist_ref)
        hi_scr[...] = jnp.zeros_like(hi_scr)

    def mod(b, i):
        return ada_ref[b:b + 1, i * D:(i + 1) * D]

    def route_chunk(chunk, weight):
        for b in range(nb):
            rows = slice(b * CHUNK, (b + 1) * CHUNK)
            route, hist = _route(hi_scr[rows, :], wrc_ref, brt_ref)
            route_ref[b, chunk] = route
            hist_ref[b:b + 1, :] = hist_ref[b:b + 1, :] + weight * hist

    route_chunk(jnp.maximum(step - 1, 0), (step > 0).astype(F32))

    for b in range(nb):
        hb = _rms(x_ref[b], g1_ref[...]) * (1 + mod(b, 1)) + mod(b, 0)
        h_scr[b * CHUNK:(b + 1) * CHUNK, :] = hb.astype(BF16)

    for r0 in (0, half):
        _front(r0, half, h_scr, win_ref, lng_ref, lnb_ref, u_scr, vn_scr, s_scr, CHUNK, pitch)

    for b in range(nb):
        rows = slice(b * CHUNK, (b + 1) * CHUNK)
        for h in range(N_HEADS):
            cols = slice(h * HEAD_DIM, (h + 1) * HEAD_DIM)
            mixed = jnp.dot(wsp_ref[h], vn_scr[rows, cols].astype(BF16), preferred_element_type=F32) + bsp_ref[h]
            ab_scr[rows, cols] = (u_scr[rows, cols] * mixed).astype(BF16)

    for k in range(N_SLABS):
        for t in range(CHUNK):
            st_scr[k, t * nb:(t + 1) * nb, :] = s_scr[k, pl.ds(t, nb, stride=pitch), :]
    _s5(st_scr, bu_scr, yt_scr, state_ref, avec_ref, wb_ref, wc_ref, nb, CHUNK)
    for k in range(N_SLABS):
        dsk = dsk_ref[:, k * LANES:(k + 1) * LANES]
        for t in range(CHUNK):
            sel = pl.ds(t, nb, stride=pitch)
            s_scr[k, sel, :] = yt_scr[k, t * nb:(t + 1) * nb, :] + dsk * s_scr[k, sel, :]

    for r0 in (0, half):
        mix = _back(r0, half, s_scr, ab_scr, wglu_ref, bglu_ref, wout_ref, CHUNK, pitch)
        for bl in range(nb // 2):
            b = r0 // CHUNK + bl
            x1 = x_ref[b] + mod(b, 2) * mix[bl * CHUNK:(bl + 1) * CHUNK, :]
            x1_ref[b] = x1
            h2 = _rms(x1, g2_ref[...]) * (1 + mod(b, 4)) + mod(b, 3)
            hi = h2.astype(BF16)
            hi_scr[b * CHUNK:(b + 1) * CHUNK, :] = hi
            h2_ref[b] = hi

    @pl.when(step == pl.num_programs(0) - 1)
    def _():
        route_chunk(step, 1.0)


def _const_spec(shape):
    nd = len(shape)
    return pl.BlockSpec(shape, lambda *_: (0,) * nd, pipeline_mode=pl.Buffered(1))


def _mixer_prompt(x, ada, h0, wts):
    nb, seq, D = x.shape
    n_chunks = seq // CHUNK
    R = nb * CHUNK
    weight_specs = [_const_spec(w.shape) for w in wts]
    in_specs = [pl.BlockSpec((nb, CHUNK, D), lambda i: (0, i, 0)),
                _const_spec(ada.shape), _const_spec(h0.shape)] + weight_specs
    out_shape = (jax.ShapeDtypeStruct((nb, seq, D), F32),
                 jax.ShapeDtypeStruct((nb, seq, D), BF16),
                 jax.ShapeDtypeStruct((nb, n_chunks, SUBLANES, ROUTE_LANES), F32),
                 jax.ShapeDtypeStruct((nb, ROUTE_LANES), F32),
                 jax.ShapeDtypeStruct((nb, STATE_COLS), F32))
    out_specs = (pl.BlockSpec((nb, CHUNK, D), lambda i: (0, i, 0)),
                 pl.BlockSpec((nb, CHUNK, D), lambda i: (0, i, 0)),
                 pl.BlockSpec((nb, n_chunks, SUBLANES, ROUTE_LANES), lambda i: (0, 0, 0, 0)),
                 pl.BlockSpec((nb, ROUTE_LANES), lambda i: (0, 0)),
                 pl.BlockSpec((nb, STATE_COLS), lambda i: (0, 0)))
    scratch = [pltpu.VMEM((R, D), BF16),
               pltpu.VMEM((R, D_A), F32),
               pltpu.VMEM((R, D_A), F32),
               pltpu.VMEM((N_SLABS, nb * S_PITCH, LANES), F32),
               pltpu.VMEM((N_SLABS, R, LANES), F32),
               pltpu.VMEM((N_SLABS, R, LANES), F32),
               pltpu.VMEM((2, R, SLAB_COLS), F32),
               pltpu.VMEM((R, D), BF16),
               pltpu.VMEM((R, D), BF16)]
    return pl.pallas_call(
        _mixer_prompt_kernel,
        grid=(n_chunks,),
        in_specs=in_specs,
        out_specs=out_specs,
        out_shape=out_shape,
        scratch_shapes=scratch,
        compiler_params=pltpu.CompilerParams(dimension_semantics=("arbitrary",), vmem_limit_bytes=VMEM_LIMIT),
        name="mixer_prompt",
    )(x, ada, h0, *wts)


def _mixer_sample_kernel(wsm_ref, bsm_ref, x_ref, ada_ref, h0_ref, g1_ref, g2_ref, win_ref, lng_ref, lnb_ref,
                         avec_ref, wb_ref, wc_ref, dsk_ref, wglu_ref, bglu_ref, wout_ref, wrc_ref, brt_ref,
                         x1_ref, h2_ref, route_ref, hist_ref, state_ref, v_ref,
                         h_scr, u_scr, vn_scr, s_scr, yt_scr, bu_scr, ab_scr):
    T, nb, D = x_ref.shape
    R = T * nb
    half = R // 2

    def mod(i):
        return ada_ref[:, i * D:(i + 1) * D]

    state_ref[...] = h0_ref[...]
    for t in range(T):
        ht = _rms(x_ref[t], g1_ref[...]) * (1 + mod(1)) + mod(0)
        h_scr[t * nb:(t + 1) * nb, :] = ht.astype(BF16)

    for r0 in (0, half):
        _front(r0, half, h_scr, win_ref, lng_ref, lnb_ref, u_scr, vn_scr, s_scr, half, half)

    for t in range(T):
        rows = slice(t * nb, (t + 1) * nb)
        v_ref[t] = vn_scr[rows, :]
        for h in range(N_HEADS):
            cols = slice(h * HEAD_DIM, (h + 1) * HEAD_DIM)
            acc = jnp.full((nb, HEAD_DIM), bsm_ref[h * T + t], F32)
            for s in range(t + 1):
                acc = acc + wsm_ref[(h * T + t) * T + s] * vn_scr[s * nb:(s + 1) * nb, cols]
            ab_scr[rows, cols] = (u_scr[rows, cols] * acc).astype(BF16)

    _s5(s_scr, bu_scr, yt_scr, state_ref, avec_ref, wb_ref, wc_ref, nb, T)
    for k in range(N_SLABS):
        s_scr[k] = yt_scr[k] + dsk_ref[:, k * LANES:(k + 1) * LANES] * s_scr[k]

    hist_total = jnp.zeros((1, ROUTE_LANES), F32)
    for r0 in (0, half):
        mix = _back(r0, half, s_scr, ab_scr, wglu_ref, bglu_ref, wout_ref, half, half)
        for tl in range(T // 2):
            t = r0 // nb + tl
            x1 = x_ref[t] + mod(2) * mix[tl * nb:(tl + 1) * nb, :]
            x1_ref[t] = x1
            h2 = _rms(x1, g2_ref[...]) * (1 + mod(4)) + mod(3)
            hi = h2.astype(BF16)
            h2_ref[t] = hi
            route, hist = _route(hi, wrc_ref, brt_ref)
            route_ref[t] = route
            hist_total = hist_total + hist
    hist_ref[...] = hist_total


def _mixer_sample(x_t, ada, h0, w_small, b_small, wts):
    T, nb, D = x_t.shape
    R = T * nb
    smem = pl.BlockSpec(memory_space=pltpu.SMEM)
    out_shape = (jax.ShapeDtypeStruct((T, nb, D), F32),
                 jax.ShapeDtypeStruct((T, nb, D), BF16),
                 jax.ShapeDtypeStruct((T, SUBLANES, ROUTE_LANES), F32),
                 jax.ShapeDtypeStruct((1, ROUTE_LANES), F32),
                 jax.ShapeDtypeStruct((nb, STATE_COLS), F32),
                 jax.ShapeDtypeStruct((T, nb, D_A), F32))
    scratch = [pltpu.VMEM((R, D), BF16),
               pltpu.VMEM((R, D_A), F32),
               pltpu.VMEM((R, D_A), F32),
               pltpu.VMEM((N_SLABS, R, LANES), F32),
               pltpu.VMEM((N_SLABS, R, LANES), F32),
               pltpu.VMEM((2, R, SLAB_COLS), F32),
               pltpu.VMEM((R, D), BF16)]
    vmem = pl.BlockSpec(memory_space=pltpu.VMEM)
    return pl.pallas_call(
        _mixer_sample_kernel,
        in_specs=[smem, smem] + [vmem] * (3 + len(wts)),
        out_specs=(vmem,) * 6,
        out_shape=out_shape,
        scratch_shapes=scratch,
        compiler_params=pltpu.CompilerParams(vmem_limit_bytes=VMEM_LIMIT),
        name="mixer_sample",
    )(w_small, b_small, x_t, ada, h0, *wts)


def _rows_to_tiles(tiles_ref, row0, val):
    n = val.shape[0]
    for c in range(val.shape[1] // LANES):
        tiles_ref[pl.ds(row0 * SUBLANES + c, n, stride=SUBLANES), :] = val[:, c * LANES:(c + 1) * LANES]


def _tiles_to_rows(tiles_ref, row0, n):
    return jnp.concatenate([tiles_ref[pl.ds(row0 * SUBLANES + c, n, stride=SUBLANES), :] for c in range(SUBLANES)],
                           axis=-1)


def _moe_kernel(starts_ref, tok_ref, pos_ref, gate_ref, h2_ref, x1_ref, ada_ref, gfin_ref, w1_ref, w3_ref, w2_ref,
                out_ref, h_tiles, y_tiles, xb_even, xb_odd, cp_smem, tab_smem, *, chunk):
    t = pl.program_id(0)
    e = pl.program_id(1)
    n_expert_steps = N_EXPERTS // EXPERTS_PER_STEP
    ngrp, rows_g, D = h2_ref.shape
    n_tok = ngrp * rows_g
    gb, rb, _ = x1_ref.shape
    epi_tok = gb * rb
    mrows = ada_ref.shape[1]

    def tile_of(row):
        return pl.ds(pl.multiple_of(row * SUBLANES, SUBLANES), SUBLANES)

    def tile_at(row8):
        return pl.ds(pl.multiple_of(row8, SUBLANES), SUBLANES)

    @pl.when(e == 0)
    def _():
        for gi in range(ngrp):
            for ci in range(rows_g // chunk):
                h2 = h2_ref[gi, ci * chunk:(ci + 1) * chunk, :].astype(F32)
                _rows_to_tiles(h_tiles, gi * rows_g + ci * chunk, h2)

    @pl.when(e == 0)
    def _():
        gp = jnp.int32(0)
        for s in range(N_EXPERTS // EXPERTS_PER_STEP):
            cp_smem[s] = gp
            st = [starts_ref[t, s * EXPERTS_PER_STEP + i] for i in range(EXPERTS_PER_STEP + 1)]
            first = [jnp.int32(0)]
            for i in range(EXPERTS_PER_STEP):
                first.append(first[i] + lax.shift_right_logical(st[i + 1] - st[i] + (MOE_BLOCK - 1), MOE_BLOCK_SHIFT))
            nb = first[EXPERTS_PER_STEP]

            def locate(f, st=st, first=first):
                el, fb, sb = jnp.int32(0), first[0], st[0]
                for i in range(1, EXPERTS_PER_STEP):
                    hit = f >= first[i]
                    el = jnp.where(hit, i, el)
                    fb = jnp.where(hit, first[i], fb)
                    sb = jnp.where(hit, st[i], sb)
                return sb + (f - fb) * MOE_BLOCK, el

            def add_pair(p, gp, nb=nb, locate=locate):
                base_a, el_a = locate(2 * p)
                base_b, el_b = locate(jnp.minimum(2 * p + 1, nb - 1))
                for i, v in enumerate((base_a, base_b, el_a, el_b)):
                    tab_smem[PAIR_FIELDS * gp + i] = v
                return gp + 1

            gp = lax.fori_loop(0, lax.shift_right_logical(nb + 1, 1), add_pair, gp)
        cp_smem[N_EXPERTS // EXPERTS_PER_STEP] = gp
        for i in range(PAIR_FIELDS):
            tab_smem[PAIR_FIELDS * gp + i] = jnp.int32(0)

    def pair_entry(gp):
        return [tab_smem[PAIR_FIELDS * gp + i] for i in range(PAIR_FIELDS)]

    @pl.when(e == 0)
    def _():
        base_a, base_b, _, _ = pair_entry(0)

        def gather8(j8, c):
            for jj in range(SUBLANES):
                j = j8 * SUBLANES + jj
                xb_even[tile_of(j), :] = h_tiles[tile_at(tok_ref[base_a + j]), :]
                xb_even[tile_of(MOE_BLOCK + j), :] = h_tiles[tile_at(tok_ref[base_b + j]), :]
            return c

        lax.fori_loop(0, MOE_BLOCK // SUBLANES, gather8, 0)

    def pair_body(gp, cur_tiles, next_tiles):
        next_a, next_b, _, _ = pair_entry(gp + 1)
        n_slices = 8
        per = 2 * MOE_BLOCK // n_slices
        slices = iter(range(n_slices))

        def gather_slice():
            s = next(slices)
            for r in range(s * per, (s + 1) * per):
                tok8 = tok_ref[next_a + r] if r < MOE_BLOCK else tok_ref[next_b + r - MOE_BLOCK]
                next_tiles[r * SUBLANES:(r + 1) * SUBLANES, :] = h_tiles[tile_at(tok8), :]

        base_a, base_b, el_a, el_b = pair_entry(gp)
        xa = _tiles_to_rows(cur_tiles, 0, MOE_BLOCK).astype(BF16)
        gather_slice()
        a1 = jnp.dot(xa, w1_ref[el_a], preferred_element_type=F32)
        gather_slice()
        a3 = jnp.dot(xa, w3_ref[el_a], preferred_element_type=F32)
        gather_slice()
        xb = _tiles_to_rows(cur_tiles, MOE_BLOCK, MOE_BLOCK).astype(BF16)
        b1 = jnp.dot(xb, w1_ref[el_b], preferred_element_type=F32)
        gather_slice()
        b3 = jnp.dot(xb, w3_ref[el_b], preferred_element_type=F32)
        gather_slice()
        ya = jnp.dot((jax.nn.silu(a1) * a3).astype(BF16), w2_ref[el_a], preferred_element_type=F32)
        gather_slice()
        yb = jnp.dot((jax.nn.silu(b1) * b3).astype(BF16), w2_ref[el_b], preferred_element_type=F32)
        gather_slice()
        _rows_to_tiles(y_tiles, base_a, ya)
        gather_slice()
        _rows_to_tiles(y_tiles, base_b, yb)

    def pairs(gp, count):
        def run(first, second):
            bufs = (first, second)
            for i in range(count):
                pair_body(gp + i, bufs[i % 2], bufs[(i + 1) % 2])

        @pl.when((gp & 1) == 0)
        def _():
            run(xb_even, xb_odd)

        @pl.when((gp & 1) == 1)
        def _():
            run(xb_odd, xb_even)

    @pl.when(e < n_expert_steps)
    def _():
        first, last = cp_smem[e], cp_smem[e + 1]
        n_double = lax.shift_right_logical(last - first, 1)

        def double(i, carry):
            pairs(first + 2 * i, 2)
            return carry

        lax.fori_loop(0, n_double, double, 0)

        @pl.when(((last - first) & 1) == 1)
        def _():
            pairs(last - 1, 1)

    @pl.when(e >= n_expert_steps)
    def _():
        tok0 = (e - n_expert_steps) * epi_tok

        def combine8(t8, c):
            for tt in range(SUBLANES):
                tok = tok0 + t8 * SUBLANES + tt
                y0 = y_tiles[tile_at(pos_ref[tok]), :]
                y1 = y_tiles[tile_at(pos_ref[n_tok + tok]), :]
                h_tiles[tile_of(tok), :] = gate_ref[tok] * y0 + gate_ref[n_tok + tok] * y1
            return c

        lax.fori_loop(0, epi_tok // SUBLANES, combine8, 0)
        sub = min(rb, chunk)
        for gi in range(gb):
            for ci in range(rb // sub):
                rows = slice(ci * sub, (ci + 1) * sub)
                gt2 = ada_ref[0, :, 5 * D:6 * D] if mrows == 1 else ada_ref[0, rows, 5 * D:6 * D]
                moe = _tiles_to_rows(h_tiles, tok0 + gi * rb + ci * sub, sub)
                x2 = x1_ref[gi, rows, :] + gt2 * moe
                out_ref[gi, rows, :] = _rms(x2, gfin_ref[...])


def _moe(h2, x1, ada, starts, tok, pos, gate, gfin, w1, w3, w2, *, n_tiles, chunk):
    ngrp = x1.shape[0] // n_tiles
    rows_g, D = x1.shape[1], x1.shape[2]
    mrows = ada.shape[1]
    Tt = ngrp * rows_g
    n_expert_steps = N_EXPERTS // EXPERTS_PER_STEP
    epi_tok = Tt // MOE_EPILOGUE_STEPS
    if ngrp == 1:
        epi_block = (1, epi_tok, D)

        def epi_map(t, e, st):
            return (t, jnp.maximum(e - n_expert_steps, 0), 0)
    else:
        assert n_tiles == 1 and epi_tok % rows_g == 0
        epi_block = (epi_tok // rows_g, rows_g, D)

        def epi_map(t, e, st):
            return (jnp.maximum(e - n_expert_steps, 0), 0, 0)

    def weight_map(t, e, st):
        return (jnp.minimum(e, n_expert_steps - 1), 0, 0)
    lp = tok.shape[0] // n_tiles
    max_pairs = (2 * Tt // MOE_BLOCK + N_EXPERTS + N_EXPERTS // EXPERTS_PER_STEP) // 2 + 1
    grid_spec = pltpu.PrefetchScalarGridSpec(
        num_scalar_prefetch=1,
        grid=(n_tiles, n_expert_steps + MOE_EPILOGUE_STEPS),
        in_specs=[
            pl.BlockSpec((lp,), lambda t, e, st: (t,), memory_space=pltpu.SMEM),
            pl.BlockSpec((2 * Tt,), lambda t, e, st: (t,), memory_space=pltpu.SMEM),
            pl.BlockSpec((2 * Tt,), lambda t, e, st: (t,), memory_space=pltpu.SMEM),
            pl.BlockSpec((ngrp, rows_g, D), lambda t, e, st: (t, 0, 0)),
            pl.BlockSpec(epi_block, epi_map),
            pl.BlockSpec((1, mrows, 6 * D), lambda t, e, st: (t, 0, 0)),
            pl.BlockSpec((1, D), lambda t, e, st: (0, 0)),
            pl.BlockSpec((EXPERTS_PER_STEP, D, D_EXPERT), weight_map),
            pl.BlockSpec((EXPERTS_PER_STEP, D, D_EXPERT), weight_map),
            pl.BlockSpec((EXPERTS_PER_STEP, D_EXPERT, D), weight_map),
        ],
        out_specs=pl.BlockSpec(epi_block, epi_map),
        scratch_shapes=[pltpu.VMEM((Tt * SUBLANES, LANES), F32),
                        pltpu.VMEM(((2 * Tt + MOE_BLOCK) * SUBLANES, LANES), F32),
                        pltpu.VMEM((2 * MOE_BLOCK * SUBLANES, LANES), F32),
                        pltpu.VMEM((2 * MOE_BLOCK * SUBLANES, LANES), F32),
                        pltpu.SMEM((N_EXPERTS // EXPERTS_PER_STEP + 1,), jnp.int32),
                        pltpu.SMEM((max_pairs * PAIR_FIELDS,), jnp.int32)],
    )
    return pl.pallas_call(
        functools.partial(_moe_kernel, chunk=chunk),
        grid_spec=grid_spec,
        out_shape=jax.ShapeDtypeStruct(x1.shape, F32),
        compiler_params=pltpu.CompilerParams(dimension_semantics=("arbitrary", "arbitrary"),
                                             vmem_limit_bytes=VMEM_LIMIT),
        name="moe",
    )(starts, tok, pos, gate, h2, x1, ada, gfin, w1, w3, w2)


def _dispatch_tables(routes):
    tile_tokens = [r.shape[1] * ROUTE_LANES for r, _ in routes]
    t_max = max(tile_tokens)
    es, gs, masks = [], [], []
    for (route, _), Tt in zip(routes, tile_tokens):
        assert Tt & (Tt - 1) == 0
        n = route.shape[0]
        by_k = route[:, :, :4, :].transpose(0, 2, 1, 3).reshape(n, 4, Tt)
        extra = 2 * (t_max - Tt)
        es.append(jnp.pad(by_k[:, :2].astype(jnp.int32).reshape(n, 2 * Tt), ((0, 0), (0, extra)),
                          constant_values=N_EXPERTS))
        gs.append(jnp.pad(by_k[:, 2:].reshape(n, 2 * Tt), ((0, 0), (0, extra))))
        masks.append(jnp.full((n, 1), Tt - 1, jnp.int32))
    flat_e = jnp.concatenate(es, axis=0)
    flat_g = jnp.concatenate(gs, axis=0)
    n_tiles = flat_e.shape[0]
    order = jnp.argsort(flat_e, axis=-1, stable=False).astype(jnp.int32)
    slots = jnp.arange(order.shape[1], dtype=jnp.int32) * SUBLANES
    pos = jnp.zeros_like(order).at[jnp.arange(n_tiles)[:, None], order].set(
        jnp.broadcast_to(slots, order.shape), unique_indices=True, indices_are_sorted=False)
    tok_s = (order & jnp.concatenate(masks, axis=0)) * SUBLANES
    counts = jnp.concatenate([h[:, :N_EXPERTS] for _, h in routes], axis=0).astype(jnp.int32)
    starts = jnp.concatenate([jnp.zeros((n_tiles, 1), jnp.int32), jnp.cumsum(counts, axis=-1, dtype=jnp.int32)],
                             axis=-1)

    def tok_len(Tt):
        return -(-(2 * Tt + MOE_BLOCK) // SMEM_PAD) * SMEM_PAD

    tok_p = jnp.pad(tok_s, ((0, 0), (0, tok_len(t_max) - 2 * t_max)))
    Tt = t_max


    out, r0 = [], 0
    for (route, _), Tt in zip(routes, tile_tokens):
        r1 = r0 + route.shape[0]
        out.append((starts[r0:r1], tok_p[r0:r1, :tok_len(Tt)].reshape(-1),
                    pos[r0:r1, :2 * Tt].reshape(-1), flat_g[r0:r1, :2 * Tt].reshape(-1)))
        r0 = r1
    return out


def _pack_state(h_re, h_im):
    b = h_re.shape[0]
    return jnp.concatenate([h_re.reshape(b, STATE_COLS // 2), h_im.reshape(b, STATE_COLS // 2)], axis=-1)


def _unpack_state(st):
    b = st.shape[0]
    return (st[:, :STATE_COLS // 2].reshape(b, N_SSM_GROUPS, SSM_STATE),
            st[:, STATE_COLS // 2:].reshape(b, N_SSM_GROUPS, SSM_STATE))


def kernel(x_prompt, x_sample, state_ssm_re, state_ssm_im, c_prompt, c_sample, w_ada, b_ada, g_norm1, g_norm2, w_in, ln_g, ln_b, w_s, b_s, lam_re, lam_im, log_dt, ssm_b_re, ssm_b_im, ssm_c_re, ssm_c_im, ssm_d, w_glu, b_glu, w_out, w_group, b_group, w_expert, b_expert, w1, w3, w2, g_final):
    depth = w_ada.shape[0]
    assert depth == 1, "the final RMSNorm is fused into the (single) layer's MoE epilogue"
    B, L, D = x_prompt.shape
    Bs, Ls, _ = x_sample.shape
    xp = x_prompt
    xs_t = x_sample.transpose(1, 0, 2)
    eye = jnp.eye(SLAB_GROUPS, dtype=F32)
    tril = jnp.tril(jnp.ones((CHUNK, CHUNK), dtype=bool))
    p_re, p_im, s_re, s_im, s_v = [], [], [], [], []
    for l in range(depth):
        ada_p, ada_s = _ada(c_prompt, c_sample, w_ada[l], b_ada[l][None])

        ar, ai, br, bi = _discretize(lam_re[l], lam_im[l], log_dt[l], ssm_b_re[l], ssm_b_im[l])
        avec = jnp.concatenate([ar.reshape(1, STATE_COLS // 2), ai.reshape(1, STATE_COLS // 2)], axis=-1)

        def blockdiag_in(w):
            w4 = w.reshape(N_SLABS, SLAB_GROUPS, SSM_GROUP, SSM_STATE)
            return jnp.einsum('kghp,gG->kghGp', w4, eye).reshape(N_SLABS, LANES, SLAB_STATES)

        def blockdiag_out(w):
            w4 = w.reshape(N_SLABS, SLAB_GROUPS, SSM_GROUP, SSM_STATE)
            return jnp.einsum('kghp,gG->kgpGh', w4, eye).reshape(N_SLABS, SLAB_STATES, LANES)

        wb = jnp.concatenate([blockdiag_in(br), blockdiag_in(bi)], axis=-1).astype(BF16)
        wc = jnp.concatenate([blockdiag_out(ssm_c_re[l]), -blockdiag_out(ssm_c_im[l])], axis=1).astype(BF16)

        lane_pad = ROUTE_LANES - N_EXPERT_GROUPS - N_EXPERTS
        wr = jnp.pad(jnp.concatenate([w_group[l], w_expert[l]], axis=1), ((0, 0), (0, lane_pad)))
        wr_hi = wr.astype(BF16)
        wr_lo = (wr - wr_hi.astype(F32)).astype(BF16)
        br_t = jnp.pad(jnp.concatenate([b_group[l], b_expert[l]]), (0, lane_pad))[None]

        g1 = g_norm1[l][None]
        g2 = g_norm2[l][None]
        shared = dict(
            win=w_in[l].astype(BF16), lng=ln_g[l].reshape(1, D_A), lnb=ln_b[l].reshape(1, D_A),
            avec=avec, wb=wb, wc=wc, dsk=ssm_d[l].reshape(1, D_B), wglu=w_glu[l].astype(BF16),
            bglu=b_glu[l][None], wout=w_out[l].astype(BF16), wrc=jnp.concatenate([wr_hi, wr_lo], axis=1), brt=br_t)
        w_masked = jnp.where(tril[None], w_s[l], jnp.zeros_like(w_s[l]))
        wsp = w_masked.astype(BF16)
        bsp = jnp.broadcast_to(b_s[l][:, :, None], (N_HEADS, CHUNK, HEAD_DIM))
        wts_p = (g1, g2, shared['win'], shared['lng'], shared['lnb'], wsp, bsp, shared['avec'], shared['wb'],
                 shared['wc'], shared['dsk'], shared['wglu'], shared['bglu'], shared['wout'], shared['wrc'],
                 shared['brt'])
        wts_s = (g1, g2, shared['win'], shared['lng'], shared['lnb'], shared['avec'], shared['wb'],
                 shared['wc'], shared['dsk'], shared['wglu'], shared['bglu'], shared['wout'], shared['wrc'],
                 shared['brt'])

        w1b, w3b, w2b = w1[l].astype(BF16), w3[l].astype(BF16), w2[l].astype(BF16)
        gfin = g_final[None]

        h0p = jnp.zeros((B, STATE_COLS), F32)
        x1p, h2p, route_p, hist_p, st_p = _mixer_prompt(xp, ada_p, h0p, wts_p)
        h0s = _pack_state(state_ssm_re[l].astype(F32), state_ssm_im[l].astype(F32))
        w_small = w_masked[:, :Ls, :Ls].reshape(-1)
        b_small = b_s[l][:, :Ls].reshape(-1)
        x1s, h2s, route_s, hist_s, st_s, v_s = _mixer_sample(xs_t, ada_s, h0s, w_small, b_small, wts_s)

        (tables_p,) = _dispatch_tables([(route_p, hist_p)])
        (tables_s,) = _dispatch_tables([(route_s[None], hist_s)])
        xp = _moe(h2p, x1p, ada_p[:, None, :], *tables_p, gfin, w1b, w3b, w2b, n_tiles=B, chunk=256)
        xs_t = _moe(h2s, x1s, ada_s[None], *tables_s, gfin, w1b, w3b, w2b, n_tiles=1, chunk=Bs)
        hr, hi = _unpack_state(st_p)
        p_re.append(hr.astype(state_ssm_re.dtype))
        p_im.append(hi.astype(state_ssm_im.dtype))
        hr, hi = _unpack_state(st_s)
        s_re.append(hr.astype(state_ssm_re.dtype))
        s_im.append(hi.astype(state_ssm_im.dtype))
        s_v.append(v_s.transpose(1, 0, 2))

    y_prompt = xp
    y_sample = xs_t.transpose(1, 0, 2)
    return (y_prompt, y_sample, jnp.stack(p_re), jnp.stack(p_im), jnp.stack(s_re), jnp.stack(s_im), jnp.stack(s_v))
```

```python
import functools

import jax
import jax.numpy as jnp
from jax import lax
from jax.experimental import pallas as pl
from jax.experimental.pallas import tpu as pltpu

F32 = jnp.float32
BF16 = jnp.bfloat16

D_MODEL = 1024
D_A = 512
D_B = 512
N_HEADS = 4
HEAD_DIM = 128
CHUNK = 128
N_SSM_GROUPS = 32
SSM_GROUP = 16
SSM_STATE = 64
N_SLABS = 4
SLAB_GROUPS = N_SSM_GROUPS // N_SLABS
SLAB_STATES = SLAB_GROUPS * SSM_STATE
SLAB_COLS = 2 * SLAB_STATES
STATE_COLS = N_SLABS * SLAB_COLS
N_EXPERT_GROUPS = 4
EXPERTS_PER_GROUP = 8
N_EXPERTS = 32
D_EXPERT = 256
EPS = 1e-6

LANES = 128
SUBLANES = 8
ROUTE_LANES = LANES
MOE_BLOCK = 128
MOE_BLOCK_SHIFT = MOE_BLOCK.bit_length() - 1
assert 1 << MOE_BLOCK_SHIFT == MOE_BLOCK
EXPERTS_PER_STEP = 4
MOE_EPILOGUE_STEPS = 4
PAIR_FIELDS = 4
S_PITCH = CHUNK + SUBLANES
SMEM_PAD = 1024
VMEM_LIMIT = 58 * 1024 * 1024


def _gelu(x):
    c = 0.7978845608028654
    half_x = 0.5 * x
    return half_x + half_x * jnp.tanh(x * (c + (c * 0.044715) * (x * x)))


def _rms(xf, g):
    ms = jnp.mean(xf * xf, axis=-1, keepdims=True)
    return xf * lax.rsqrt(ms + EPS) * g


def _ada_kernel(cp_ref, cs_ref, w_ref, b_ref, op_ref, os_ref):
    mp = cp_ref.shape[0]
    s = jax.nn.silu(jnp.concatenate([cp_ref[...], cs_ref[...]], axis=0)).astype(BF16)
    out = jnp.dot(s, w_ref[...].astype(BF16), preferred_element_type=F32) + b_ref[...]
    op_ref[...] = out[:mp]
    os_ref[...] = out[mp:]


def _ada(c_p, c_s, w, b):
    mp, ms = c_p.shape[0], c_s.shape[0]
    n = w.shape[1]
    bn = 1024
    return pl.pallas_call(
        _ada_kernel,
        grid=(n // bn,),
        in_specs=[pl.BlockSpec((mp, D_MODEL), lambda j: (0, 0)),
                  pl.BlockSpec((ms, D_MODEL), lambda j: (0, 0)),
                  pl.BlockSpec((D_MODEL, bn), lambda j: (0, j)),
                  pl.BlockSpec((1, bn), lambda j: (0, j))],
        out_specs=(pl.BlockSpec((mp, bn), lambda j: (0, j)), pl.BlockSpec((ms, bn), lambda j: (0, j))),
        out_shape=(jax.ShapeDtypeStruct((mp, n), F32), jax.ShapeDtypeStruct((ms, n), F32)),
        name="ada",
    )(c_p, c_s, w, b)


def _disc_kernel(lre_ref, lim_ref, ldt_ref, bre_ref, bim_ref, ar_ref, ai_ref, br_ref, bi_ref):
    dt = jnp.exp(ldt_ref[...])
    lr = lre_ref[...]
    li = lim_ref[...]
    mag = jnp.exp(lr * dt)
    ar = mag * jnp.cos(li * dt)
    ai = mag * jnp.sin(li * dt)
    den = lr * lr + li * li
    cr = ((ar - 1) * lr + ai * li) / den
    ci = (ai * lr - (ar - 1) * li) / den
    ar_ref[...] = ar
    ai_ref[...] = ai
    bre = bre_ref[...]
    bim = bim_ref[...]
    br_ref[...] = cr * bre - ci * bim
    bi_ref[...] = cr * bim + ci * bre


def _discretize(lam_re, lam_im, log_dt, b_re, b_im):
    g, p, h = b_re.shape
    o1 = jax.ShapeDtypeStruct((g, 1, p), F32)
    o2 = jax.ShapeDtypeStruct((g, h, p), F32)
    return pl.pallas_call(_disc_kernel, out_shape=(o1, o1, o2, o2), name="ssm_disc")(
        lam_re.reshape(g, 1, p), lam_im.reshape(g, 1, p), log_dt.reshape(g, 1, 1),
        b_re.transpose(0, 2, 1), b_im.transpose(0, 2, 1))


def _s_pieces(r0, nrows, grp, pitch):
    return [(i * grp, slice((r0 // grp + i) * pitch, (r0 // grp + i) * pitch + grp)) for i in range(nrows // grp)]


def _front(r0, nrows, h_scr, win_ref, lng_ref, lnb_ref, u_scr, vn_scr, s_scr, grp, pitch):
    rows = slice(r0, r0 + nrows)
    proj = jnp.dot(h_scr[rows, :], win_ref[...], preferred_element_type=F32)
    u_scr[rows, :] = _gelu(proj[:, :D_A])
    vraw = _gelu(proj[:, D_A:2 * D_A])
    for h in range(N_HEADS):
        cols = slice(h * HEAD_DIM, (h + 1) * HEAD_DIM)
        vh = vraw[:, cols]
        mu = jnp.mean(vh, axis=-1, keepdims=True)
        dv = vh - mu
        var = jnp.mean(dv * dv, axis=-1, keepdims=True)
        vn_scr[rows, cols] = dv * lax.rsqrt(var + EPS) * lng_ref[:, cols] + lnb_ref[:, cols]
    for k in range(N_SLABS):
        for off, prow in _s_pieces(r0, nrows, grp, pitch):
            s_scr[k, prow, :] = proj[off:off + grp, 2 * D_A + k * LANES:2 * D_A + (k + 1) * LANES]


def _scan_slab(bu, state, avec_ref, k, rows_per_step, steps):
    c_re = slice(0, SLAB_STATES)
    c_im = slice(SLAB_STATES, SLAB_COLS)
    s_re = slice(k * SLAB_STATES, (k + 1) * SLAB_STATES)
    s_im = slice(STATE_COLS // 2 + k * SLAB_STATES, STATE_COLS // 2 + (k + 1) * SLAB_STATES)
    ar = jnp.broadcast_to(avec_ref[:, s_re], (SUBLANES, SLAB_STATES))
    ai = jnp.broadcast_to(avec_ref[:, s_im], (SUBLANES, SLAB_STATES))
    for rc in range(rows_per_step // SUBLANES):
        r0 = rc * SUBLANES
        sr = state[r0:r0 + SUBLANES, s_re]
        si = state[r0:r0 + SUBLANES, s_im]
        for t in range(steps):
            rows = slice(t * rows_per_step + r0, t * rows_per_step + r0 + SUBLANES)
            nr = ar * sr - ai * si + bu[rows, c_re]
            ni = ar * si + ai * sr + bu[rows, c_im]
            bu[rows, c_re] = nr
            bu[rows, c_im] = ni
            sr, si = nr, ni
        state[r0:r0 + SUBLANES, s_re] = sr
        state[r0:r0 + SUBLANES, s_im] = si


def _s5(st_ref, bu_scr, yt_ref, state, avec_ref, wb_ref, wc_ref, rows_per_step, steps):
    for k in range(N_SLABS):
        bu = bu_scr.at[k % 2]
        bu[...] = jnp.dot(st_ref[k].astype(BF16), wb_ref[k], preferred_element_type=F32)
        _scan_slab(bu, state, avec_ref, k, rows_per_step, steps)
        yt_ref[k] = jnp.dot(bu[...].astype(BF16), wc_ref[k], preferred_element_type=F32)


def _back(r0, nrows, s_scr, ab_scr, wglu_ref, bglu_ref, wout_ref, grp, pitch):
    rows = slice(r0, r0 + nrows)
    pieces = _s_pieces(r0, nrows, grp, pitch)
    y = _gelu(jnp.concatenate(
        [jnp.concatenate([s_scr[k, prow, :] for _, prow in pieces], axis=0) for k in range(N_SLABS)], axis=-1))
    gate = jnp.dot(y.astype(BF16), wglu_ref[...], preferred_element_type=F32) + bglu_ref[...]
    ab_scr[rows, D_A:] = (y * jax.nn.sigmoid(gate)).astype(BF16)
    return jnp.dot(ab_scr[rows, :], wout_ref[...], preferred_element_type=F32)


def _route(h2_bf, wrc_ref, brt_ref):
    both = jnp.dot(h2_bf, wrc_ref[...], preferred_element_type=F32)
    logits = both[:, :ROUTE_LANES] + both[:, ROUTE_LANES:] + brt_ref[...]
    n = logits.shape[0]
    lane = lax.broadcasted_iota(jnp.int32, (n, ROUTE_LANES), 1)
    lane_f = lane.astype(F32)
    big = jnp.float32(1e9)
    ninf = jnp.float32(-jnp.inf)
    is_g = lane < N_EXPERT_GROUPS
    gl = jnp.where(is_g, logits, ninf)
    gmax = jnp.max(gl, axis=-1, keepdims=True)
    gidx = jnp.min(jnp.where(gl == gmax, lane_f, big), axis=-1, keepdims=True)
    gsum = jnp.sum(jnp.where(is_g, jnp.exp(logits - gmax), 0.0), axis=-1, keepdims=True)
    g_w = 1.0 / gsum
    elo = N_EXPERT_GROUPS + EXPERTS_PER_GROUP * gidx
    emask = (lane_f >= elo) & (lane_f < elo + EXPERTS_PER_GROUP)
    el = jnp.where(emask, logits, ninf)
    t1 = jnp.max(el, axis=-1, keepdims=True)
    i1 = jnp.min(jnp.where(el == t1, lane_f, big), axis=-1, keepdims=True)
    el2 = jnp.where(lane_f == i1, ninf, el)
    t2 = jnp.max(el2, axis=-1, keepdims=True)
    i2 = jnp.min(jnp.where(el2 == t2, lane_f, big), axis=-1, keepdims=True)
    e21 = jnp.exp(t2 - t1)
    den = 1.0 + e21
    gate1 = g_w * (1.0 / den)
    gate2 = g_w * (e21 / den)
    e1 = i1 - N_EXPERT_GROUPS
    e2 = i2 - N_EXPERT_GROUPS
    packed = jnp.where(lane == 0, e1, jnp.where(lane == 1, e2, jnp.where(lane == 2, gate1,
                                                                           jnp.where(lane == 3, gate2, 0.0))))
    hist = jnp.sum(((lane_f == e1) | (lane_f == e2)).astype(F32), axis=0, keepdims=True)
    return packed.T[:SUBLANES], hist


def _mixer_prompt_kernel(x_ref, ada_ref, h0_ref, g1_ref, g2_ref, win_ref, lng_ref, lnb_ref, wsp_ref, bsp_ref,
                         avec_ref, wb_ref, wc_ref, dsk_ref, wglu_ref, bglu_ref, wout_ref, wrc_ref, brt_ref,
                         x1_ref, h2_ref, route_ref, hist_ref, state_ref,
                         h_scr, u_scr, vn_scr, s_scr, st_scr, yt_scr, bu_scr, ab_scr, hi_scr):
    nb = x_ref.shape[0]
    half = (nb // 2) * CHUNK
    D = D_MODEL
    pitch = S_PITCH
    step = pl.program_id(0)

    @pl.when(step == 0)
    def _():
        state_ref[...] = h0_ref[...]
        hist_ref[...] = jnp.zeros_like(hist_ref)
        hi_scr[...] = jnp.zeros_like(hi_scr)

    def mod(b, i):
        return ada_ref[b:b + 1, i * D:(i + 1) * D]

    def route_chunk(chunk, weight):
        for b in range(nb):
            rows = slice(b * CHUNK, (b + 1) * CHUNK)
            route, hist = _route(hi_scr[rows, :], wrc_ref, brt_ref)
            route_ref[b, chunk] = route
            hist_ref[b:b + 1, :] = hist_ref[b:b + 1, :] + weight * hist

    route_chunk(jnp.maximum(step - 1, 0), (step > 0).astype(F32))

    for b in range(nb):
        hb = _rms(x_ref[b], g1_ref[...]) * (1 + mod(b, 1)) + mod(b, 0)
        h_scr[b * CHUNK:(b + 1) * CHUNK, :] = hb.astype(BF16)

    for r0 in (0, half):
        _front(r0, half, h_scr, win_ref, lng_ref, lnb_ref, u_scr, vn_scr, s_scr, CHUNK, pitch)

    for b in range(nb):
        rows = slice(b * CHUNK, (b + 1) * CHUNK)
        for h in range(N_HEADS):
            cols = slice(h * HEAD_DIM, (h + 1) * HEAD_DIM)
            mixed = jnp.dot(wsp_ref[h], vn_scr[rows, cols].astype(BF16), preferred_element_type=F32) + bsp_ref[h]
            ab_scr[rows, cols] = (u_scr[rows, cols] * mixed).astype(BF16)

    for k in range(N_SLABS):
        for t in range(CHUNK):
            st_scr[k, t * nb:(t + 1) * nb, :] = s_scr[k, pl.ds(t, nb, stride=pitch), :]
    _s5(st_scr, bu_scr, yt_scr, state_ref, avec_ref, wb_ref, wc_ref, nb, CHUNK)
    for k in range(N_SLABS):
        dsk = dsk_ref[:, k * LANES:(k + 1) * LANES]
        for t in range(CHUNK):
            sel = pl.ds(t, nb, stride=pitch)
            s_scr[k, sel, :] = yt_scr[k, t * nb:(t + 1) * nb, :] + dsk * s_scr[k, sel, :]

    for r0 in (0, half):
        mix = _back(r0, half, s_scr, ab_scr, wglu_ref, bglu_ref, wout_ref, CHUNK, pitch)
        for bl in range(nb // 2):
            b = r0 // CHUNK + bl
            x1 = x_ref[b] + mod(b, 2) * mix[bl * CHUNK:(bl + 1) * CHUNK, :]
            x1_ref[b] = x1
            h2 = _rms(x1, g2_ref[...]) * (1 + mod(b, 4)) + mod(b, 3)
            hi = h2.astype(BF16)
            hi_scr[b * CHUNK:(b + 1) * CHUNK, :] = hi
            h2_ref[b] = hi

    @pl.when(step == pl.num_programs(0) - 1)
    def _():
        route_chunk(step, 1.0)


def _const_spec(shape):
    nd = len(shape)
    return pl.BlockSpec(shape, lambda *_: (0,) * nd, pipeline_mode=pl.Buffered(1))


def _mixer_prompt(x, ada, h0, wts):
    nb, seq, D = x.shape
    n_chunks = seq // CHUNK
    R = nb * CHUNK
    weight_specs = [_const_spec(w.shape) for w in wts]
    in_specs = [pl.BlockSpec((nb, CHUNK, D), lambda i: (0, i, 0)),
                _const_spec(ada.shape), _const_spec(h0.shape)] + weight_specs
    out_shape = (jax.ShapeDtypeStruct((nb, seq, D), F32),
                 jax.ShapeDtypeStruct((nb, seq, D), BF16),
                 jax.ShapeDtypeStruct((nb, n_chunks, SUBLANES, ROUTE_LANES), F32),
                 jax.ShapeDtypeStruct((nb, ROUTE_LANES), F32),
                 jax.ShapeDtypeStruct((nb, STATE_COLS), F32))
    out_specs = (pl.BlockSpec((nb, CHUNK, D), lambda i: (0, i, 0)),
                 pl.BlockSpec((nb, CHUNK, D), lambda i: (0, i, 0)),
                 pl.BlockSpec((nb, n_chunks, SUBLANES, ROUTE_LANES), lambda i: (0, 0, 0, 0)),
                 pl.BlockSpec((nb, ROUTE_LANES), lambda i: (0, 0)),
                 pl.BlockSpec((nb, STATE_COLS), lambda i: (0, 0)))
    scratch = [pltpu.VMEM((R, D), BF16),
               pltpu.VMEM((R, D_A), F32),
               pltpu.VMEM((R, D_A), F32),
               pltpu.VMEM((N_SLABS, nb * S_PITCH, LANES), F32),
               pltpu.VMEM((N_SLABS, R, LANES), F32),
               pltpu.VMEM((N_SLABS, R, LANES), F32),
               pltpu.VMEM((2, R, SLAB_COLS), F32),
               pltpu.VMEM((R, D), BF16),
               pltpu.VMEM((R, D), BF16)]
    return pl.pallas_call(
        _mixer_prompt_kernel,
        grid=(n_chunks,),
        in_specs=in_specs,
        out_specs=out_specs,
        out_shape=out_shape,
        scratch_shapes=scratch,
        compiler_params=pltpu.CompilerParams(dimension_semantics=("arbitrary",), vmem_limit_bytes=VMEM_LIMIT),
        name="mixer_prompt",
    )(x, ada, h0, *wts)


def _mixer_sample_kernel(wsm_ref, bsm_ref, x_ref, ada_ref, h0_ref, g1_ref, g2_ref, win_ref, lng_ref, lnb_ref,
                         avec_ref, wb_ref, wc_ref, dsk_ref, wglu_ref, bglu_ref, wout_ref, wrc_ref, brt_ref,
                         x1_ref, h2_ref, route_ref, hist_ref, state_ref, v_ref,
                         h_scr, u_scr, vn_scr, s_scr, yt_scr, bu_scr, ab_scr):
    T, nb, D = x_ref.shape
    R = T * nb
    half = R // 2

    def mod(i):
        return ada_ref[:, i * D:(i + 1) * D]

    state_ref[...] = h0_ref[...]
    for t in range(T):
        ht = _rms(x_ref[t], g1_ref[...]) * (1 + mod(1)) + mod(0)
        h_scr[t * nb:(t + 1) * nb, :] = ht.astype(BF16)

    for r0 in (0, half):
        _front(r0, half, h_scr, win_ref, lng_ref, lnb_ref, u_scr, vn_scr, s_scr, half, half)

    for t in range(T):
        rows = slice(t * nb, (t + 1) * nb)
        v_ref[t] = vn_scr[rows, :]
        for h in range(N_HEADS):
            cols = slice(h * HEAD_DIM, (h + 1) * HEAD_DIM)
            acc = jnp.full((nb, HEAD_DIM), bsm_ref[h * T + t], F32)
            for s in range(t + 1):
                acc = acc + wsm_ref[(h * T + t) * T + s] * vn_scr[s * nb:(s + 1) * nb, cols]
            ab_scr[rows, cols] = (u_scr[rows, cols] * acc).astype(BF16)

    _s5(s_scr, bu_scr, yt_scr, state_ref, avec_ref, wb_ref, wc_ref, nb, T)
    for k in range(N_SLABS):
        s_scr[k] = yt_scr[k] + dsk_ref[:, k * LANES:(k + 1) * LANES] * s_scr[k]

    hist_total = jnp.zeros((1, ROUTE_LANES), F32)
    for r0 in (0, half):
        mix = _back(r0, half, s_scr, ab_scr, wglu_ref, bglu_ref, wout_ref, half, half)
        for tl in range(T // 2):
            t = r0 // nb + tl
            x1 = x_ref[t] + mod(2) * mix[tl * nb:(tl + 1) * nb, :]
            x1_ref[t] = x1
            h2 = _rms(x1, g2_ref[...]) * (1 + mod(4)) + mod(3)
            hi = h2.astype(BF16)
            h2_ref[t] = hi
            route, hist = _route(hi, wrc_ref, brt_ref)
            route_ref[t] = route
            hist_total = hist_total + hist
    hist_ref[...] = hist_total


def _mixer_sample(x_t, ada, h0, w_small, b_small, wts):
    T, nb, D = x_t.shape
    R = T * nb
    smem = pl.BlockSpec(memory_space=pltpu.SMEM)
    out_shape = (jax.ShapeDtypeStruct((T, nb, D), F32),
                 jax.ShapeDtypeStruct((T, nb, D), BF16),
                 jax.ShapeDtypeStruct((T, SUBLANES, ROUTE_LANES), F32),
                 jax.ShapeDtypeStruct((1, ROUTE_LANES), F32),
                 jax.ShapeDtypeStruct((nb, STATE_COLS), F32),
                 jax.ShapeDtypeStruct((T, nb, D_A), F32))
    scratch = [pltpu.VMEM((R, D), BF16),
               pltpu.VMEM((R, D_A), F32),
               pltpu.VMEM((R, D_A), F32),
               pltpu.VMEM((N_SLABS, R, LANES), F32),
               pltpu.VMEM((N_SLABS, R, LANES), F32),
               pltpu.VMEM((2, R, SLAB_COLS), F32),
               pltpu.VMEM((R, D), BF16)]
    vmem = pl.BlockSpec(memory_space=pltpu.VMEM)
    return pl.pallas_call(
        _mixer_sample_kernel,
        in_specs=[smem, smem] + [vmem] * (3 + len(wts)),
        out_specs=(vmem,) * 6,
        out_shape=out_shape,
        scratch_shapes=scratch,
        compiler_params=pltpu.CompilerParams(vmem_limit_bytes=VMEM_LIMIT),
        name="mixer_sample",
    )(w_small, b_small, x_t, ada, h0, *wts)


def _rows_to_tiles(tiles_ref, row0, val):
    n = val.shape[0]
    for c in range(val.shape[1] // LANES):
        tiles_ref[pl.ds(row0 * SUBLANES + c, n, stride=SUBLANES), :] = val[:, c * LANES:(c + 1) * LANES]


def _tiles_to_rows(tiles_ref, row0, n):
    return jnp.concatenate([tiles_ref[pl.ds(row0 * SUBLANES + c, n, stride=SUBLANES), :] for c in range(SUBLANES)],
                           axis=-1)


def _moe_kernel(starts_ref, tok_ref, pos_ref, gate_ref, h2_ref, x1_ref, ada_ref, gfin_ref, w1_ref, w3_ref, w2_ref,
                out_ref, h_tiles, y_tiles, xb_even, xb_odd, cp_smem, tab_smem, *, chunk):
    t = pl.program_id(0)
    e = pl.program_id(1)
    n_expert_steps = N_EXPERTS // EXPERTS_PER_STEP
    ngrp, rows_g, D = h2_ref.shape
    n_tok = ngrp * rows_g
    gb, rb, _ = x1_ref.shape
    epi_tok = gb * rb
    mrows = ada_ref.shape[1]

    def tile_of(row):
        return pl.ds(pl.multiple_of(row * SUBLANES, SUBLANES), SUBLANES)

    def tile_at(row8):
        return pl.ds(pl.multiple_of(row8, SUBLANES), SUBLANES)

    @pl.when(e == 0)
    def _():
        for gi in range(ngrp):
            for ci in range(rows_g // chunk):
                h2 = h2_ref[gi, ci * chunk:(ci + 1) * chunk, :].astype(F32)
                _rows_to_tiles(h_tiles, gi * rows_g + ci * chunk, h2)

    @pl.when(e == 0)
    def _():
        gp = jnp.int32(0)
        for s in range(N_EXPERTS // EXPERTS_PER_STEP):
            cp_smem[s] = gp
            st = [starts_ref[t, s * EXPERTS_PER_STEP + i] for i in range(EXPERTS_PER_STEP + 1)]
            first = [jnp.int32(0)]
            for i in range(EXPERTS_PER_STEP):
                first.append(first[i] + lax.shift_right_logical(st[i + 1] - st[i] + (MOE_BLOCK - 1), MOE_BLOCK_SHIFT))
            nb = first[EXPERTS_PER_STEP]

            def locate(f, st=st, first=first):
                el, fb, sb = jnp.int32(0), first[0], st[0]
                for i in range(1, EXPERTS_PER_STEP):
                    hit = f >= first[i]
                    el = jnp.where(hit, i, el)
                    fb = jnp.where(hit, first[i], fb)
                    sb = jnp.where(hit, st[i], sb)
                return sb + (f - fb) * MOE_BLOCK, el

            def add_pair(p, gp, nb=nb, locate=locate):
                base_a, el_a = locate(2 * p)
                base_b, el_b = locate(jnp.minimum(2 * p + 1, nb - 1))
                for i, v in enumerate((base_a, base_b, el_a, el_b)):
                    tab_smem[PAIR_FIELDS * gp + i] = v
                return gp + 1

            gp = lax.fori_loop(0, lax.shift_right_logical(nb + 1, 1), add_pair, gp)
        cp_smem[N_EXPERTS // EXPERTS_PER_STEP] = gp
        for i in range(PAIR_FIELDS):
            tab_smem[PAIR_FIELDS * gp + i] = jnp.int32(0)

    def pair_entry(gp):
        return [tab_smem[PAIR_FIELDS * gp + i] for i in range(PAIR_FIELDS)]

    @pl.when(e == 0)
    def _():
        base_a, base_b, _, _ = pair_entry(0)

        def gather8(j8, c):
            for jj in range(SUBLANES):
                j = j8 * SUBLANES + jj
                xb_even[tile_of(j), :] = h_tiles[tile_at(tok_ref[base_a + j]), :]
                xb_even[tile_of(MOE_BLOCK + j), :] = h_tiles[tile_at(tok_ref[base_b + j]), :]
            return c

        lax.fori_loop(0, MOE_BLOCK // SUBLANES, gather8, 0)

    def pair_body(gp, cur_tiles, next_tiles):
        next_a, next_b, _, _ = pair_entry(gp + 1)
        n_slices = 8
        per = 2 * MOE_BLOCK // n_slices
        slices = iter(range(n_slices))

        def gather_slice():
            s = next(slices)
            for r in range(s * per, (s + 1) * per):
                tok8 = tok_ref[next_a + r] if r < MOE_BLOCK else tok_ref[next_b + r - MOE_BLOCK]
                next_tiles[r * SUBLANES:(r + 1) * SUBLANES, :] = h_tiles[tile_at(tok8), :]

        base_a, base_b, el_a, el_b = pair_entry(gp)
        xa = _tiles_to_rows(cur_tiles, 0, MOE_BLOCK).astype(BF16)
        gather_slice()
        a1 = jnp.dot(xa, w1_ref[el_a], preferred_element_type=F32)
        gather_slice()
        a3 = jnp.dot(xa, w3_ref[el_a], preferred_element_type=F32)
        gather_slice()
        xb = _tiles_to_rows(cur_tiles, MOE_BLOCK, MOE_BLOCK).astype(BF16)
        b1 = jnp.dot(xb, w1_ref[el_b], preferred_element_type=F32)
        gather_slice()
        b3 = jnp.dot(xb, w3_ref[el_b], preferred_element_type=F32)
        gather_slice()
        ya = jnp.dot((jax.nn.silu(a1) * a3).astype(BF16), w2_ref[el_a], preferred_element_type=F32)
        gather_slice()
        yb = jnp.dot((jax.nn.silu(b1) * b3).astype(BF16), w2_ref[el_b], preferred_element_type=F32)
        gather_slice()
        _rows_to_tiles(y_tiles, base_a, ya)
        gather_slice()
        _rows_to_tiles(y_tiles, base_b, yb)

    def pairs(gp, count):
        def run(first, second):
            bufs = (first, second)
            for i in range(count):
                pair_body(gp + i, bufs[i % 2], bufs[(i + 1) % 2])

        @pl.when((gp & 1) == 0)
        def _():
            run(xb_even, xb_odd)

        @pl.when((gp & 1) == 1)
        def _():
            run(xb_odd, xb_even)

    @pl.when(e < n_expert_steps)
    def _():
        first, last = cp_smem[e], cp_smem[e + 1]
        n_double = lax.shift_right_logical(last - first, 1)

        def double(i, carry):
            pairs(first + 2 * i, 2)
            return carry

        lax.fori_loop(0, n_double, double, 0)

        @pl.when(((last - first) & 1) == 1)
        def _():
            pairs(last - 1, 1)

    @pl.when(e >= n_expert_steps)
    def _():
        tok0 = (e - n_expert_steps) * epi_tok

        def combine8(t8, c):
            for tt in range(SUBLANES):
                tok = tok0 + t8 * SUBLANES + tt
                y0 = y_tiles[tile_at(pos_ref[tok]), :]
                y1 = y_tiles[tile_at(pos_ref[n_tok + tok]), :]
                h_tiles[tile_of(tok), :] = gate_ref[tok] * y0 + gate_ref[n_tok + tok] * y1
            return c

        lax.fori_loop(0, epi_tok // SUBLANES, combine8, 0)
        sub = min(rb, chunk)
        for gi in range(gb):
            for ci in range(rb // sub):
                rows = slice(ci * sub, (ci + 1) * sub)
                gt2 = ada_ref[0, :, 5 * D:6 * D] if mrows == 1 else ada_ref[0, rows, 5 * D:6 * D]
                moe = _tiles_to_rows(h_tiles, tok0 + gi * rb + ci * sub, sub)
                x2 = x1_ref[gi, rows, :] + gt2 * moe
                out_ref[gi, rows, :] = _rms(x2, gfin_ref[...])


def _moe(h2, x1, ada, starts, tok, pos, gate, gfin, w1, w3, w2, *, n_tiles, chunk):
    ngrp = x1.shape[0] // n_tiles
    rows_g, D = x1.shape[1], x1.shape[2]
    mrows = ada.shape[1]
    Tt = ngrp * rows_g
    n_expert_steps = N_EXPERTS // EXPERTS_PER_STEP
    epi_tok = Tt // MOE_EPILOGUE_STEPS
    if ngrp == 1:
        epi_block = (1, epi_tok, D)

        def epi_map(t, e, st):
            return (t, jnp.maximum(e - n_expert_steps, 0), 0)
    else:
        assert n_tiles == 1 and epi_tok % rows_g == 0
        epi_block = (epi_tok // rows_g, rows_g, D)

        def epi_map(t, e, st):
            return (jnp.maximum(e - n_expert_steps, 0), 0, 0)

    def weight_map(t, e, st):
        return (jnp.minimum(e, n_expert_steps - 1), 0, 0)
    lp = tok.shape[0] // n_tiles
    max_pairs = (2 * Tt // MOE_BLOCK + N_EXPERTS + N_EXPERTS // EXPERTS_PER_STEP) // 2 + 1
    grid_spec = pltpu.PrefetchScalarGridSpec(
        num_scalar_prefetch=1,
        grid=(n_tiles, n_expert_steps + MOE_EPILOGUE_STEPS),
        in_specs=[
            pl.BlockSpec((lp,), lambda t, e, st: (t,), memory_space=pltpu.SMEM),
            pl.BlockSpec((2 * Tt,), lambda t, e, st: (t,), memory_space=pltpu.SMEM),
            pl.BlockSpec((2 * Tt,), lambda t, e, st: (t,), memory_space=pltpu.SMEM),
            pl.BlockSpec((ngrp, rows_g, D), lambda t, e, st: (t, 0, 0)),
            pl.BlockSpec(epi_block, epi_map),
            pl.BlockSpec((1, mrows, 6 * D), lambda t, e, st: (t, 0, 0)),
            pl.BlockSpec((1, D), lambda t, e, st: (0, 0)),
            pl.BlockSpec((EXPERTS_PER_STEP, D, D_EXPERT), weight_map),
            pl.BlockSpec((EXPERTS_PER_STEP, D, D_EXPERT), weight_map),
            pl.BlockSpec((EXPERTS_PER_STEP, D_EXPERT, D), weight_map),
        ],
        out_specs=pl.BlockSpec(epi_block, epi_map),
        scratch_shapes=[pltpu.VMEM((Tt * SUBLANES, LANES), F32),
                        pltpu.VMEM(((2 * Tt + MOE_BLOCK) * SUBLANES, LANES), F32),
                        pltpu.VMEM((2 * MOE_BLOCK * SUBLANES, LANES), F32),
                        pltpu.VMEM((2 * MOE_BLOCK * SUBLANES, LANES), F32),
                        pltpu.SMEM((N_EXPERTS // EXPERTS_PER_STEP + 1,), jnp.int32),
                        pltpu.SMEM((max_pairs * PAIR_FIELDS,), jnp.int32)],
    )
    return pl.pallas_call(
        functools.partial(_moe_kernel, chunk=chunk),
        grid_spec=grid_spec,
        out_shape=jax.ShapeDtypeStruct(x1.shape, F32),
        compiler_params=pltpu.CompilerParams(dimension_semantics=("arbitrary", "arbitrary"),
                                             vmem_limit_bytes=VMEM_LIMIT),
        name="moe",
    )(starts, tok, pos, gate, h2, x1, ada, gfin, w1, w3, w2)


def _dispatch_tables(routes):
    tile_tokens = [r.shape[1] * ROUTE_LANES for r, _ in routes]
    t_max = max(tile_tokens)
    es, gs, masks = [], [], []
    for (route, _), Tt in zip(routes, tile_tokens):
        assert Tt & (Tt - 1) == 0
        n = route.shape[0]
        by_k = route[:, :, :4, :].transpose(0, 2, 1, 3).reshape(n, 4, Tt)
        extra = 2 * (t_max - Tt)
        es.append(jnp.pad(by_k[:, :2].astype(jnp.int32).reshape(n, 2 * Tt), ((0, 0), (0, extra)),
                          constant_values=N_EXPERTS))
        gs.append(jnp.pad(by_k[:, 2:].reshape(n, 2 * Tt), ((0, 0), (0, extra))))
        masks.append(jnp.full((n, 1), Tt - 1, jnp.int32))
    flat_e = jnp.concatenate(es, axis=0)
    flat_g = jnp.concatenate(gs, axis=0)
    n_tiles = flat_e.shape[0]
    order = jnp.argsort(flat_e, axis=-1, stable=False).astype(jnp.int32)
    pos = jnp.argsort(order, axis=-1, stable=False).astype(jnp.int32) * SUBLANES
    tok_s = (order & jnp.concatenate(masks, axis=0)) * SUBLANES
    counts = jnp.concatenate([h[:, :N_EXPERTS] for _, h in routes], axis=0).astype(jnp.int32)
    starts = jnp.concatenate([jnp.zeros((n_tiles, 1), jnp.int32), jnp.cumsum(counts, axis=-1, dtype=jnp.int32)],
                             axis=-1)

    def tok_len(Tt):
        return -(-(2 * Tt + MOE_BLOCK) // SMEM_PAD) * SMEM_PAD

    tok_p = jnp.pad(tok_s, ((0, 0), (0, tok_len(t_max) - 2 * t_max)))
    Tt = t_max


    out, r0 = [], 0
    for (route, _), Tt in zip(routes, tile_tokens):
        r1 = r0 + route.shape[0]
        out.append((starts[r0:r1], tok_p[r0:r1, :tok_len(Tt)].reshape(-1),
                    pos[r0:r1, :2 * Tt].reshape(-1), flat_g[r0:r1, :2 * Tt].reshape(-1)))
        r0 = r1
    return out


def _pack_state(h_re, h_im):
    b = h_re.shape[0]
    return jnp.concatenate([h_re.reshape(b, STATE_COLS // 2), h_im.reshape(b, STATE_COLS // 2)], axis=-1)


def _unpack_state(st):
    b = st.shape[0]
    return (st[:, :STATE_COLS // 2].reshape(b, N_SSM_GROUPS, SSM_STATE),
            st[:, STATE_COLS // 2:].reshape(b, N_SSM_GROUPS, SSM_STATE))


def kernel(x_prompt, x_sample, state_ssm_re, state_ssm_im, c_prompt, c_sample, w_ada, b_ada, g_norm1, g_norm2, w_in, ln_g, ln_b, w_s, b_s, lam_re, lam_im, log_dt, ssm_b_re, ssm_b_im, ssm_c_re, ssm_c_im, ssm_d, w_glu, b_glu, w_out, w_group, b_group, w_expert, b_expert, w1, w3, w2, g_final):
    depth = w_ada.shape[0]
    assert depth == 1, "the final RMSNorm is fused into the (single) layer's MoE epilogue"
    B, L, D = x_prompt.shape
    Bs, Ls, _ = x_sample.shape
    xp = x_prompt
    xs_t = x_sample.transpose(1, 0, 2)
    eye = jnp.eye(SLAB_GROUPS, dtype=F32)
    tril = jnp.tril(jnp.ones((CHUNK, CHUNK), dtype=bool))
    p_re, p_im, s_re, s_im, s_v = [], [], [], [], []
    for l in range(depth):
        ada_p, ada_s = _ada(c_prompt, c_sample, w_ada[l], b_ada[l][None])

        ar, ai, br, bi = _discretize(lam_re[l], lam_im[l], log_dt[l], ssm_b_re[l], ssm_b_im[l])
        avec = jnp.concatenate([ar.reshape(1, STATE_COLS // 2), ai.reshape(1, STATE_COLS // 2)], axis=-1)

        def blockdiag_in(w):
            w4 = w.reshape(N_SLABS, SLAB_GROUPS, SSM_GROUP, SSM_STATE)
            return jnp.einsum('kghp,gG->kghGp', w4, eye).reshape(N_SLABS, LANES, SLAB_STATES)

        def blockdiag_out(w):
            w4 = w.reshape(N_SLABS, SLAB_GROUPS, SSM_GROUP, SSM_STATE)
            return jnp.einsum('kghp,gG->kgpGh', w4, eye).reshape(N_SLABS, SLAB_STATES, LANES)

        wb = jnp.concatenate([blockdiag_in(br), blockdiag_in(bi)], axis=-1).astype(BF16)
        wc = jnp.concatenate([blockdiag_out(ssm_c_re[l]), -blockdiag_out(ssm_c_im[l])], axis=1).astype(BF16)

        lane_pad = ROUTE_LANES - N_EXPERT_GROUPS - N_EXPERTS
        wr = jnp.pad(jnp.concatenate([w_group[l], w_expert[l]], axis=1), ((0, 0), (0, lane_pad)))
        wr_hi = wr.astype(BF16)
        wr_lo = (wr - wr_hi.astype(F32)).astype(BF16)
        br_t = jnp.pad(jnp.concatenate([b_group[l], b_expert[l]]), (0, lane_pad))[None]

        g1 = g_norm1[l][None]
        g2 = g_norm2[l][None]
        shared = dict(
            win=w_in[l].astype(BF16), lng=ln_g[l].reshape(1, D_A), lnb=ln_b[l].reshape(1, D_A),
            avec=avec, wb=wb, wc=wc, dsk=ssm_d[l].reshape(1, D_B), wglu=w_glu[l].astype(BF16),
            bglu=b_glu[l][None], wout=w_out[l].astype(BF16), wrc=jnp.concatenate([wr_hi, wr_lo], axis=1), brt=br_t)
        w_masked = jnp.where(tril[None], w_s[l], jnp.zeros_like(w_s[l]))
        wsp = w_masked.astype(BF16)
        bsp = jnp.broadcast_to(b_s[l][:, :, None], (N_HEADS, CHUNK, HEAD_DIM))
        wts_p = (g1, g2, shared['win'], shared['lng'], shared['lnb'], wsp, bsp, shared['avec'], shared['wb'],
                 shared['wc'], shared['dsk'], shared['wglu'], shared['bglu'], shared['wout'], shared['wrc'],
                 shared['brt'])
        wts_s = (g1, g2, shared['win'], shared['lng'], shared['lnb'], shared['avec'], shared['wb'],
                 shared['wc'], shared['dsk'], shared['wglu'], shared['bglu'], shared['wout'], shared['wrc'],
                 shared['brt'])

        w1b, w3b, w2b = w1[l].astype(BF16), w3[l].astype(BF16), w2[l].astype(BF16)
        gfin = g_final[None]

        h0p = jnp.zeros((B, STATE_COLS), F32)
        x1p, h2p, route_p, hist_p, st_p = _mixer_prompt(xp, ada_p, h0p, wts_p)
        h0s = _pack_state(state_ssm_re[l].astype(F32), state_ssm_im[l].astype(F32))
        w_small = w_masked[:, :Ls, :Ls].reshape(-1)
        b_small = b_s[l][:, :Ls].reshape(-1)
        x1s, h2s, route_s, hist_s, st_s, v_s = _mixer_sample(xs_t, ada_s, h0s, w_small, b_small, wts_s)

        (tables_p,) = _dispatch_tables([(route_p, hist_p)])
        (tables_s,) = _dispatch_tables([(route_s[None], hist_s)])
        xp = _moe(h2p, x1p, ada_p[:, None, :], *tables_p, gfin, w1b, w3b, w2b, n_tiles=B, chunk=256)
        xs_t = _moe(h2s, x1s, ada_s[None], *tables_s, gfin, w1b, w3b, w2b, n_tiles=1, chunk=Bs)
        hr, hi = _unpack_state(st_p)
        p_re.append(hr.astype(state_ssm_re.dtype))
        p_im.append(hi.astype(state_ssm_im.dtype))
        hr, hi = _unpack_state(st_s)
        s_re.append(hr.astype(state_ssm_re.dtype))
        s_im.append(hi.astype(state_ssm_im.dtype))
        s_v.append(v_s.transpose(1, 0, 2))

    y_prompt = xp
    y_sample = xs_t.transpose(1, 0, 2)
    return (y_prompt, y_sample, jnp.stack(p_re), jnp.stack(p_im), jnp.stack(s_re), jnp.stack(s_im), jnp.stack(s_v))
```

```python
import functools

import jax
import jax.numpy as jnp
from jax import lax
from jax.experimental import pallas as pl
from jax.experimental.pallas import tpu as pltpu

F32 = jnp.float32
BF16 = jnp.bfloat16

D_MODEL = 1024
D_A = 512
D_B = 512
N_HEADS = 4
HEAD_DIM = 128
CHUNK = 128
N_SSM_GROUPS = 32
SSM_GROUP = 16
SSM_STATE = 64
N_SLABS = 4
SLAB_GROUPS = N_SSM_GROUPS // N_SLABS
SLAB_STATES = SLAB_GROUPS * SSM_STATE
SLAB_COLS = 2 * SLAB_STATES
STATE_COLS = N_SLABS * SLAB_COLS
N_EXPERT_GROUPS = 4
EXPERTS_PER_GROUP = 8
N_EXPERTS = 32
D_EXPERT = 256
EPS = 1e-6

LANES = 128
SUBLANES = 8
ROUTE_LANES = LANES
MOE_BLOCK = 128
MOE_BLOCK_SHIFT = MOE_BLOCK.bit_length() - 1
assert 1 << MOE_BLOCK_SHIFT == MOE_BLOCK
EXPERTS_PER_STEP = 4
MOE_EPILOGUE_STEPS = 4
PAIR_FIELDS = 4
S_PITCH = CHUNK + SUBLANES
SMEM_PAD = 1024
VMEM_LIMIT = 58 * 1024 * 1024


def _gelu(x):
    c = 0.7978845608028654
    half_x = 0.5 * x
    return half_x + half_x * jnp.tanh(x * (c + (c * 0.044715) * (x * x)))


def _modulate(xf, g, shift, scale):
    ms = jnp.mean(xf * xf, axis=-1, keepdims=True)
    return xf * lax.rsqrt(ms + EPS) * (g * (1 + scale)) + shift


def _rms(xf, g):
    ms = jnp.mean(xf * xf, axis=-1, keepdims=True)
    return xf * lax.rsqrt(ms + EPS) * g


def _ada_kernel(cp_ref, cs_ref, w_ref, b_ref, op_ref, os_ref):
    mp = cp_ref.shape[0]
    s = jax.nn.silu(jnp.concatenate([cp_ref[...], cs_ref[...]], axis=0)).astype(BF16)
    out = jnp.dot(s, w_ref[...].astype(BF16), preferred_element_type=F32) + b_ref[...]
    op_ref[...] = out[:mp]
    os_ref[...] = out[mp:]


def _ada(c_p, c_s, w, b):
    mp, ms = c_p.shape[0], c_s.shape[0]
    n = w.shape[1]
    bn = 1024
    return pl.pallas_call(
        _ada_kernel,
        grid=(n // bn,),
        in_specs=[pl.BlockSpec((mp, D_MODEL), lambda j: (0, 0)),
                  pl.BlockSpec((ms, D_MODEL), lambda j: (0, 0)),
                  pl.BlockSpec((D_MODEL, bn), lambda j: (0, j)),
                  pl.BlockSpec((1, bn), lambda j: (0, j))],
        out_specs=(pl.BlockSpec((mp, bn), lambda j: (0, j)), pl.BlockSpec((ms, bn), lambda j: (0, j))),
        out_shape=(jax.ShapeDtypeStruct((mp, n), F32), jax.ShapeDtypeStruct((ms, n), F32)),
        name="ada",
    )(c_p, c_s, w, b)


def _disc_kernel(lre_ref, lim_ref, ldt_ref, bre_ref, bim_ref, ar_ref, ai_ref, br_ref, bi_ref):
    dt = jnp.exp(ldt_ref[...])
    lr = lre_ref[...]
    li = lim_ref[...]
    mag = jnp.exp(lr * dt)
    ar = mag * jnp.cos(li * dt)
    ai = mag * jnp.sin(li * dt)
    den = lr * lr + li * li
    cr = ((ar - 1) * lr + ai * li) / den
    ci = (ai * lr - (ar - 1) * li) / den
    ar_ref[...] = ar
    ai_ref[...] = ai
    bre = bre_ref[...]
    bim = bim_ref[...]
    br_ref[...] = cr * bre - ci * bim
    bi_ref[...] = cr * bim + ci * bre


def _discretize(lam_re, lam_im, log_dt, b_re, b_im):
    g, p, h = b_re.shape
    o1 = jax.ShapeDtypeStruct((g, 1, p), F32)
    o2 = jax.ShapeDtypeStruct((g, h, p), F32)
    return pl.pallas_call(_disc_kernel, out_shape=(o1, o1, o2, o2), name="ssm_disc")(
        lam_re.reshape(g, 1, p), lam_im.reshape(g, 1, p), log_dt.reshape(g, 1, 1),
        b_re.transpose(0, 2, 1), b_im.transpose(0, 2, 1))


def _s_pieces(r0, nrows, grp, pitch):
    return [(i * grp, slice((r0 // grp + i) * pitch, (r0 // grp + i) * pitch + grp)) for i in range(nrows // grp)]


def _front(r0, nrows, h_scr, win_ref, lng_ref, lnb_ref, u_scr, vn_scr, s_scr, grp, pitch):
    rows = slice(r0, r0 + nrows)
    proj = jnp.dot(h_scr[rows, :], win_ref[...], preferred_element_type=F32)
    u_scr[rows, :] = _gelu(proj[:, :D_A])
    vraw = _gelu(proj[:, D_A:2 * D_A])
    for h in range(N_HEADS):
        cols = slice(h * HEAD_DIM, (h + 1) * HEAD_DIM)
        vh = vraw[:, cols]
        mu = jnp.mean(vh, axis=-1, keepdims=True)
        dv = vh - mu
        var = jnp.mean(dv * dv, axis=-1, keepdims=True)
        vn_scr[rows, cols] = dv * lax.rsqrt(var + EPS) * lng_ref[:, cols] + lnb_ref[:, cols]
    for k in range(N_SLABS):
        for off, prow in _s_pieces(r0, nrows, grp, pitch):
            s_scr[k, prow, :] = proj[off:off + grp, 2 * D_A + k * LANES:2 * D_A + (k + 1) * LANES]


def _scan_slab(bu, state, avec_ref, k, rows_per_step, steps):
    c_re = slice(0, SLAB_STATES)
    c_im = slice(SLAB_STATES, SLAB_COLS)
    s_re = slice(k * SLAB_STATES, (k + 1) * SLAB_STATES)
    s_im = slice(STATE_COLS // 2 + k * SLAB_STATES, STATE_COLS // 2 + (k + 1) * SLAB_STATES)
    ar = jnp.broadcast_to(avec_ref[:, s_re], (SUBLANES, SLAB_STATES))
    ai = jnp.broadcast_to(avec_ref[:, s_im], (SUBLANES, SLAB_STATES))
    for rc in range(rows_per_step // SUBLANES):
        r0 = rc * SUBLANES
        sr = state[r0:r0 + SUBLANES, s_re]
        si = state[r0:r0 + SUBLANES, s_im]
        for t in range(steps):
            rows = slice(t * rows_per_step + r0, t * rows_per_step + r0 + SUBLANES)
            nr = ar * sr - ai * si + bu[rows, c_re]
            ni = ar * si + ai * sr + bu[rows, c_im]
            bu[rows, c_re] = nr
            bu[rows, c_im] = ni
            sr, si = nr, ni
        state[r0:r0 + SUBLANES, s_re] = sr
        state[r0:r0 + SUBLANES, s_im] = si


def _s5(st_ref, bu_scr, yt_ref, state, avec_ref, wb_ref, wc_ref, rows_per_step, steps):
    for k in range(N_SLABS):
        bu = bu_scr.at[k % 2]
        bu[...] = jnp.dot(st_ref[k].astype(BF16), wb_ref[k], preferred_element_type=F32)
        _scan_slab(bu, state, avec_ref, k, rows_per_step, steps)
        yt_ref[k] = jnp.dot(bu[...].astype(BF16), wc_ref[k], preferred_element_type=F32)


def _back(r0, nrows, s_scr, ab_scr, wglu_ref, bglu_ref, wout_ref, grp, pitch):
    rows = slice(r0, r0 + nrows)
    pieces = _s_pieces(r0, nrows, grp, pitch)
    y = _gelu(jnp.concatenate(
        [jnp.concatenate([s_scr[k, prow, :] for _, prow in pieces], axis=0) for k in range(N_SLABS)], axis=-1))
    gate = jnp.dot(y.astype(BF16), wglu_ref[...], preferred_element_type=F32) + bglu_ref[...]
    ab_scr[rows, D_A:] = (y * jax.nn.sigmoid(gate)).astype(BF16)
    return jnp.dot(ab_scr[rows, :], wout_ref[...], preferred_element_type=F32)


def _route(h2_bf, wrc_ref, brt_ref):
    both = jnp.dot(h2_bf, wrc_ref[...], preferred_element_type=F32)
    logits = both[:, :ROUTE_LANES] + both[:, ROUTE_LANES:] + brt_ref[...]
    n = logits.shape[0]
    lane = lax.broadcasted_iota(jnp.int32, (n, ROUTE_LANES), 1)
    lane_f = lane.astype(F32)
    big = jnp.float32(1e9)
    ninf = jnp.float32(-jnp.inf)
    is_g = lane < N_EXPERT_GROUPS
    gl = jnp.where(is_g, logits, ninf)
    gmax = jnp.max(gl, axis=-1, keepdims=True)
    gidx = jnp.min(jnp.where(gl == gmax, lane_f, big), axis=-1, keepdims=True)
    gsum = jnp.sum(jnp.where(is_g, jnp.exp(logits - gmax), 0.0), axis=-1, keepdims=True)
    g_w = 1.0 / gsum
    elo = N_EXPERT_GROUPS + EXPERTS_PER_GROUP * gidx
    emask = (lane_f >= elo) & (lane_f < elo + EXPERTS_PER_GROUP)
    el = jnp.where(emask, logits, ninf)
    t1 = jnp.max(el, axis=-1, keepdims=True)
    i1 = jnp.min(jnp.where(el == t1, lane_f, big), axis=-1, keepdims=True)
    el2 = jnp.where(lane_f == i1, ninf, el)
    t2 = jnp.max(el2, axis=-1, keepdims=True)
    i2 = jnp.min(jnp.where(el2 == t2, lane_f, big), axis=-1, keepdims=True)
    e21 = jnp.exp(t2 - t1)
    den = 1.0 + e21
    gate1 = g_w * (1.0 / den)
    gate2 = g_w * (e21 / den)
    e1 = i1 - N_EXPERT_GROUPS
    e2 = i2 - N_EXPERT_GROUPS
    packed = jnp.where(lane == 0, e1, jnp.where(lane == 1, e2, jnp.where(lane == 2, gate1,
                                                                           jnp.where(lane == 3, gate2, 0.0))))
    hist = jnp.sum(((lane_f == e1) | (lane_f == e2)).astype(F32), axis=0, keepdims=True)
    return packed.T[:SUBLANES], hist


def _mixer_prompt_kernel(x_ref, ada_ref, h0_ref, g1_ref, g2_ref, win_ref, lng_ref, lnb_ref, wsp_ref, bsp_ref,
                         avec_ref, wb_ref, wc_ref, dsk_ref, wglu_ref, bglu_ref, wout_ref, wrc_ref, brt_ref,
                         x1_ref, h2_ref, route_ref, hist_ref, state_ref,
                         h_scr, u_scr, vn_scr, s_scr, st_scr, yt_scr, bu_scr, ab_scr, hi_scr):
    nb = x_ref.shape[0]
    half = (nb // 2) * CHUNK
    D = D_MODEL
    pitch = S_PITCH
    step = pl.program_id(0)

    @pl.when(step == 0)
    def _():
        state_ref[...] = h0_ref[...]
        hist_ref[...] = jnp.zeros_like(hist_ref)
        hi_scr[...] = jnp.zeros_like(hi_scr)

    def mod(b, i):
        return ada_ref[b:b + 1, i * D:(i + 1) * D]

    def route_chunk(chunk, weight):
        for b in range(nb):
            rows = slice(b * CHUNK, (b + 1) * CHUNK)
            route, hist = _route(hi_scr[rows, :], wrc_ref, brt_ref)
            route_ref[b, chunk] = route
            hist_ref[b:b + 1, :] = hist_ref[b:b + 1, :] + weight * hist

    route_chunk(jnp.maximum(step - 1, 0), (step > 0).astype(F32))

    for b in range(nb):
        hb = _modulate(x_ref[b], g1_ref[...], mod(b, 0), mod(b, 1))
        h_scr[b * CHUNK:(b + 1) * CHUNK, :] = hb.astype(BF16)

    for r0 in (0, half):
        _front(r0, half, h_scr, win_ref, lng_ref, lnb_ref, u_scr, vn_scr, s_scr, CHUNK, pitch)

    for b in range(nb):
        rows = slice(b * CHUNK, (b + 1) * CHUNK)
        for h in range(N_HEADS):
            cols = slice(h * HEAD_DIM, (h + 1) * HEAD_DIM)
            mixed = jnp.dot(wsp_ref[h], vn_scr[rows, cols].astype(BF16), preferred_element_type=F32) + bsp_ref[h]
            ab_scr[rows, cols] = (u_scr[rows, cols] * mixed).astype(BF16)

    for k in range(N_SLABS):
        for t in range(CHUNK):
            st_scr[k, t * nb:(t + 1) * nb, :] = s_scr[k, pl.ds(t, nb, stride=pitch), :]
    _s5(st_scr, bu_scr, yt_scr, state_ref, avec_ref, wb_ref, wc_ref, nb, CHUNK)
    for k in range(N_SLABS):
        dsk = dsk_ref[:, k * LANES:(k + 1) * LANES]
        for t in range(CHUNK):
            sel = pl.ds(t, nb, stride=pitch)
            s_scr[k, sel, :] = yt_scr[k, t * nb:(t + 1) * nb, :] + dsk * s_scr[k, sel, :]

    for r0 in (0, half):
        mix = _back(r0, half, s_scr, ab_scr, wglu_ref, bglu_ref, wout_ref, CHUNK, pitch)
        for bl in range(nb // 2):
            b = r0 // CHUNK + bl
            x1 = x_ref[b] + mod(b, 2) * mix[bl * CHUNK:(bl + 1) * CHUNK, :]
            x1_ref[b] = x1
            h2 = _modulate(x1, g2_ref[...], mod(b, 3), mod(b, 4))
            hi = h2.astype(BF16)
            hi_scr[b * CHUNK:(b + 1) * CHUNK, :] = hi
            h2_ref[b] = hi

    @pl.when(step == pl.num_programs(0) - 1)
    def _():
        route_chunk(step, 1.0)


def _const_spec(shape):
    nd = len(shape)
    return pl.BlockSpec(shape, lambda *_: (0,) * nd, pipeline_mode=pl.Buffered(1))


def _mixer_prompt(x, ada, h0, wts):
    nb, seq, D = x.shape
    n_chunks = seq // CHUNK
    R = nb * CHUNK
    weight_specs = [_const_spec(w.shape) for w in wts]
    in_specs = [pl.BlockSpec((nb, CHUNK, D), lambda i: (0, i, 0)),
                _const_spec(ada.shape), _const_spec(h0.shape)] + weight_specs
    out_shape = (jax.ShapeDtypeStruct((nb, seq, D), F32),
                 jax.ShapeDtypeStruct((nb, seq, D), BF16),
                 jax.ShapeDtypeStruct((nb, n_chunks, SUBLANES, ROUTE_LANES), F32),
                 jax.ShapeDtypeStruct((nb, ROUTE_LANES), F32),
                 jax.ShapeDtypeStruct((nb, STATE_COLS), F32))
    out_specs = (pl.BlockSpec((nb, CHUNK, D), lambda i: (0, i, 0)),
                 pl.BlockSpec((nb, CHUNK, D), lambda i: (0, i, 0)),
                 pl.BlockSpec((nb, n_chunks, SUBLANES, ROUTE_LANES), lambda i: (0, 0, 0, 0)),
                 pl.BlockSpec((nb, ROUTE_LANES), lambda i: (0, 0)),
                 pl.BlockSpec((nb, STATE_COLS), lambda i: (0, 0)))
    scratch = [pltpu.VMEM((R, D), BF16),
               pltpu.VMEM((R, D_A), F32),
               pltpu.VMEM((R, D_A), F32),
               pltpu.VMEM((N_SLABS, nb * S_PITCH, LANES), F32),
               pltpu.VMEM((N_SLABS, R, LANES), F32),
               pltpu.VMEM((N_SLABS, R, LANES), F32),
               pltpu.VMEM((2, R, SLAB_COLS), F32),
               pltpu.VMEM((R, D), BF16),
               pltpu.VMEM((R, D), BF16)]
    return pl.pallas_call(
        _mixer_prompt_kernel,
        grid=(n_chunks,),
        in_specs=in_specs,
        out_specs=out_specs,
        out_shape=out_shape,
        scratch_shapes=scratch,
        compiler_params=pltpu.CompilerParams(dimension_semantics=("arbitrary",), vmem_limit_bytes=VMEM_LIMIT),
        name="mixer_prompt",
    )(x, ada, h0, *wts)


def _mixer_sample_kernel(wsm_ref, bsm_ref, x_ref, ada_ref, h0_ref, g1_ref, g2_ref, win_ref, lng_ref, lnb_ref,
                         avec_ref, wb_ref, wc_ref, dsk_ref, wglu_ref, bglu_ref, wout_ref, wrc_ref, brt_ref,
                         x1_ref, h2_ref, route_ref, hist_ref, state_ref, v_ref,
                         h_scr, u_scr, vn_scr, s_scr, yt_scr, bu_scr, ab_scr):
    T, nb, D = x_ref.shape
    R = T * nb
    half = R // 2

    def mod(i):
        return ada_ref[:, i * D:(i + 1) * D]

    state_ref[...] = h0_ref[...]
    for t in range(T):
        ht = _rms(x_ref[t], g1_ref[...]) * (1 + mod(1)) + mod(0)
        h_scr[t * nb:(t + 1) * nb, :] = ht.astype(BF16)

    for r0 in (0, half):
        _front(r0, half, h_scr, win_ref, lng_ref, lnb_ref, u_scr, vn_scr, s_scr, half, half)

    for t in range(T):
        rows = slice(t * nb, (t + 1) * nb)
        v_ref[t] = vn_scr[rows, :]
        for h in range(N_HEADS):
            cols = slice(h * HEAD_DIM, (h + 1) * HEAD_DIM)
            acc = jnp.full((nb, HEAD_DIM), bsm_ref[h * T + t], F32)
            for s in range(t + 1):
                acc = acc + wsm_ref[(h * T + t) * T + s] * vn_scr[s * nb:(s + 1) * nb, cols]
            ab_scr[rows, cols] = (u_scr[rows, cols] * acc).astype(BF16)

    _s5(s_scr, bu_scr, yt_scr, state_ref, avec_ref, wb_ref, wc_ref, nb, T)
    for k in range(N_SLABS):
        s_scr[k] = yt_scr[k] + dsk_ref[:, k * LANES:(k + 1) * LANES] * s_scr[k]

    hist_total = jnp.zeros((1, ROUTE_LANES), F32)
    for r0 in (0, half):
        mix = _back(r0, half, s_scr, ab_scr, wglu_ref, bglu_ref, wout_ref, half, half)
        for tl in range(T // 2):
            t = r0 // nb + tl
            x1 = x_ref[t] + mod(2) * mix[tl * nb:(tl + 1) * nb, :]
            x1_ref[t] = x1
            h2 = _rms(x1, g2_ref[...]) * (1 + mod(4)) + mod(3)
            hi = h2.astype(BF16)
            h2_ref[t] = hi
            route, hist = _route(hi, wrc_ref, brt_ref)
            route_ref[t] = route
            hist_total = hist_total + hist
    hist_ref[...] = hist_total


def _mixer_sample(x_t, ada, h0, w_small, b_small, wts):
    T, nb, D = x_t.shape
    R = T * nb
    smem = pl.BlockSpec(memory_space=pltpu.SMEM)
    out_shape = (jax.ShapeDtypeStruct((T, nb, D), F32),
                 jax.ShapeDtypeStruct((T, nb, D), BF16),
                 jax.ShapeDtypeStruct((T, SUBLANES, ROUTE_LANES), F32),
                 jax.ShapeDtypeStruct((1, ROUTE_LANES), F32),
                 jax.ShapeDtypeStruct((nb, STATE_COLS), F32),
                 jax.ShapeDtypeStruct((T, nb, D_A), F32))
    scratch = [pltpu.VMEM((R, D), BF16),
               pltpu.VMEM((R, D_A), F32),
               pltpu.VMEM((R, D_A), F32),
               pltpu.VMEM((N_SLABS, R, LANES), F32),
               pltpu.VMEM((N_SLABS, R, LANES), F32),
               pltpu.VMEM((2, R, SLAB_COLS), F32),
               pltpu.VMEM((R, D), BF16)]
    vmem = pl.BlockSpec(memory_space=pltpu.VMEM)
    return pl.pallas_call(
        _mixer_sample_kernel,
        in_specs=[smem, smem] + [vmem] * (3 + len(wts)),
        out_specs=(vmem,) * 6,
        out_shape=out_shape,
        scratch_shapes=scratch,
        compiler_params=pltpu.CompilerParams(vmem_limit_bytes=VMEM_LIMIT),
        name="mixer_sample",
    )(w_small, b_small, x_t, ada, h0, *wts)


def _rows_to_tiles(tiles_ref, row0, val):
    n = val.shape[0]
    for c in range(val.shape[1] // LANES):
        tiles_ref[pl.ds(row0 * SUBLANES + c, n, stride=SUBLANES), :] = val[:, c * LANES:(c + 1) * LANES]


def _tiles_to_rows(tiles_ref, row0, n):
    return jnp.concatenate([tiles_ref[pl.ds(row0 * SUBLANES + c, n, stride=SUBLANES), :] for c in range(SUBLANES)],
                           axis=-1)


def _moe_kernel(starts_ref, tok_ref, pos_ref, gate_ref, h2_ref, x1_ref, ada_ref, gfin_ref, w1_ref, w3_ref, w2_ref,
                out_ref, h_tiles, y_tiles, xb_even, xb_odd, cp_smem, tab_smem, *, chunk):
    t = pl.program_id(0)
    e = pl.program_id(1)
    n_expert_steps = N_EXPERTS // EXPERTS_PER_STEP
    ngrp, rows_g, D = h2_ref.shape
    n_tok = ngrp * rows_g
    gb, rb, _ = x1_ref.shape
    epi_tok = gb * rb
    mrows = ada_ref.shape[1]

    def tile_of(row):
        return pl.ds(pl.multiple_of(row * SUBLANES, SUBLANES), SUBLANES)

    def tile_at(row8):
        return pl.ds(pl.multiple_of(row8, SUBLANES), SUBLANES)

    @pl.when(e == 0)
    def _():
        for gi in range(ngrp):
            for ci in range(rows_g // chunk):
                h2 = h2_ref[gi, ci * chunk:(ci + 1) * chunk, :].astype(F32)
                _rows_to_tiles(h_tiles, gi * rows_g + ci * chunk, h2)

    @pl.when(e == 0)
    def _():
        gp = jnp.int32(0)
        for s in range(N_EXPERTS // EXPERTS_PER_STEP):
            cp_smem[s] = gp
            st = [starts_ref[t, s * EXPERTS_PER_STEP + i] for i in range(EXPERTS_PER_STEP + 1)]
            first = [jnp.int32(0)]
            for i in range(EXPERTS_PER_STEP):
                first.append(first[i] + lax.shift_right_logical(st[i + 1] - st[i] + (MOE_BLOCK - 1), MOE_BLOCK_SHIFT))
            nb = first[EXPERTS_PER_STEP]

            def locate(f, st=st, first=first):
                el, fb, sb = jnp.int32(0), first[0], st[0]
                for i in range(1, EXPERTS_PER_STEP):
                    hit = f >= first[i]
                    el = jnp.where(hit, i, el)
                    fb = jnp.where(hit, first[i], fb)
                    sb = jnp.where(hit, st[i], sb)
                return sb + (f - fb) * MOE_BLOCK, el

            def add_pair(p, gp, nb=nb, locate=locate):
                base_a, el_a = locate(2 * p)
                base_b, el_b = locate(jnp.minimum(2 * p + 1, nb - 1))
                for i, v in enumerate((base_a, base_b, el_a, el_b)):
                    tab_smem[PAIR_FIELDS * gp + i] = v
                return gp + 1

            gp = lax.fori_loop(0, lax.shift_right_logical(nb + 1, 1), add_pair, gp)
        cp_smem[N_EXPERTS // EXPERTS_PER_STEP] = gp
        for i in range(PAIR_FIELDS):
            tab_smem[PAIR_FIELDS * gp + i] = jnp.int32(0)

    def pair_entry(gp):
        return [tab_smem[PAIR_FIELDS * gp + i] for i in range(PAIR_FIELDS)]

    @pl.when(e == 0)
    def _():
        base_a, base_b, _, _ = pair_entry(0)

        def gather8(j8, c):
            for jj in range(SUBLANES):
                j = j8 * SUBLANES + jj
                xb_even[tile_of(j), :] = h_tiles[tile_at(tok_ref[base_a + j]), :]
                xb_even[tile_of(MOE_BLOCK + j), :] = h_tiles[tile_at(tok_ref[base_b + j]), :]
            return c

        lax.fori_loop(0, MOE_BLOCK // SUBLANES, gather8, 0)

    def pair_body(gp, cur_tiles, next_tiles):
        next_a, next_b, _, _ = pair_entry(gp + 1)
        n_slices = 8
        per = 2 * MOE_BLOCK // n_slices
        slices = iter(range(n_slices))

        def gather_slice():
            s = next(slices)
            for r in range(s * per, (s + 1) * per):
                tok8 = tok_ref[next_a + r] if r < MOE_BLOCK else tok_ref[next_b + r - MOE_BLOCK]
                next_tiles[r * SUBLANES:(r + 1) * SUBLANES, :] = h_tiles[tile_at(tok8), :]

        base_a, base_b, el_a, el_b = pair_entry(gp)
        xa = _tiles_to_rows(cur_tiles, 0, MOE_BLOCK).astype(BF16)
        gather_slice()
        a1 = jnp.dot(xa, w1_ref[el_a], preferred_element_type=F32)
        gather_slice()
        a3 = jnp.dot(xa, w3_ref[el_a], preferred_element_type=F32)
        gather_slice()
        xb = _tiles_to_rows(cur_tiles, MOE_BLOCK, MOE_BLOCK).astype(BF16)
        b1 = jnp.dot(xb, w1_ref[el_b], preferred_element_type=F32)
        gather_slice()
        b3 = jnp.dot(xb, w3_ref[el_b], preferred_element_type=F32)
        gather_slice()
        ya = jnp.dot((jax.nn.silu(a1) * a3).astype(BF16), w2_ref[el_a], preferred_element_type=F32)
        gather_slice()
        yb = jnp.dot((jax.nn.silu(b1) * b3).astype(BF16), w2_ref[el_b], preferred_element_type=F32)
        gather_slice()
        _rows_to_tiles(y_tiles, base_a, ya)
        gather_slice()
        _rows_to_tiles(y_tiles, base_b, yb)

    def pairs(gp, count):
        def run(first, second):
            bufs = (first, second)
            for i in range(count):
                pair_body(gp + i, bufs[i % 2], bufs[(i + 1) % 2])

        @pl.when((gp & 1) == 0)
        def _():
            run(xb_even, xb_odd)

        @pl.when((gp & 1) == 1)
        def _():
            run(xb_odd, xb_even)

    @pl.when(e < n_expert_steps)
    def _():
        first, last = cp_smem[e], cp_smem[e + 1]
        n_double = lax.shift_right_logical(last - first, 1)

        def double(i, carry):
            pairs(first + 2 * i, 2)
            return carry

        lax.fori_loop(0, n_double, double, 0)

        @pl.when(((last - first) & 1) == 1)
        def _():
            pairs(last - 1, 1)

    @pl.when(e >= n_expert_steps)
    def _():
        tok0 = (e - n_expert_steps) * epi_tok

        def combine8(t8, c):
            for tt in range(SUBLANES):
                tok = tok0 + t8 * SUBLANES + tt
                y0 = y_tiles[tile_at(pos_ref[tok]), :]
                y1 = y_tiles[tile_at(pos_ref[n_tok + tok]), :]
                h_tiles[tile_of(tok), :] = gate_ref[tok] * y0 + gate_ref[n_tok + tok] * y1
            return c

        lax.fori_loop(0, epi_tok // SUBLANES, combine8, 0)
        sub = min(rb, chunk)
        for gi in range(gb):
            for ci in range(rb // sub):
                rows = slice(ci * sub, (ci + 1) * sub)
                gt2 = ada_ref[0, :, 5 * D:6 * D] if mrows == 1 else ada_ref[0, rows, 5 * D:6 * D]
                moe = _tiles_to_rows(h_tiles, tok0 + gi * rb + ci * sub, sub)
                x2 = x1_ref[gi, rows, :] + gt2 * moe
                out_ref[gi, rows, :] = _rms(x2, gfin_ref[...])


def _moe(h2, x1, ada, starts, tok, pos, gate, gfin, w1, w3, w2, *, n_tiles, chunk):
    ngrp = x1.shape[0] // n_tiles
    rows_g, D = x1.shape[1], x1.shape[2]
    mrows = ada.shape[1]
    Tt = ngrp * rows_g
    n_expert_steps = N_EXPERTS // EXPERTS_PER_STEP
    epi_tok = Tt // MOE_EPILOGUE_STEPS
    if ngrp == 1:
        epi_block = (1, epi_tok, D)

        def epi_map(t, e, st):
            return (t, jnp.maximum(e - n_expert_steps, 0), 0)
    else:
        assert n_tiles == 1 and epi_tok % rows_g == 0
        epi_block = (epi_tok // rows_g, rows_g, D)

        def epi_map(t, e, st):
            return (jnp.maximum(e - n_expert_steps, 0), 0, 0)

    def weight_map(t, e, st):
        return (jnp.minimum(e, n_expert_steps - 1), 0, 0)
    lp = tok.shape[0] // n_tiles
    max_pairs = (2 * Tt // MOE_BLOCK + N_EXPERTS + N_EXPERTS // EXPERTS_PER_STEP) // 2 + 1
    grid_spec = pltpu.PrefetchScalarGridSpec(
        num_scalar_prefetch=1,
        grid=(n_tiles, n_expert_steps + MOE_EPILOGUE_STEPS),
        in_specs=[
            pl.BlockSpec((lp,), lambda t, e, st: (t,), memory_space=pltpu.SMEM),
            pl.BlockSpec((2 * Tt,), lambda t, e, st: (t,), memory_space=pltpu.SMEM),
            pl.BlockSpec((2 * Tt,), lambda t, e, st: (t,), memory_space=pltpu.SMEM),
            pl.BlockSpec((ngrp, rows_g, D), lambda t, e, st: (t, 0, 0)),
            pl.BlockSpec(epi_block, epi_map),
            pl.BlockSpec((1, mrows, 6 * D), lambda t, e, st: (t, 0, 0)),
            pl.BlockSpec((1, D), lambda t, e, st: (0, 0)),
            pl.BlockSpec((EXPERTS_PER_STEP, D, D_EXPERT), weight_map),
            pl.BlockSpec((EXPERTS_PER_STEP, D, D_EXPERT), weight_map),
            pl.BlockSpec((EXPERTS_PER_STEP, D_EXPERT, D), weight_map),
        ],
        out_specs=pl.BlockSpec(epi_block, epi_map),
        scratch_shapes=[pltpu.VMEM((Tt * SUBLANES, LANES), F32),
                        pltpu.VMEM(((2 * Tt + MOE_BLOCK) * SUBLANES, LANES), F32),
                        pltpu.VMEM((2 * MOE_BLOCK * SUBLANES, LANES), F32),
                        pltpu.VMEM((2 * MOE_BLOCK * SUBLANES, LANES), F32),
                        pltpu.SMEM((N_EXPERTS // EXPERTS_PER_STEP + 1,), jnp.int32),
                        pltpu.SMEM((max_pairs * PAIR_FIELDS,), jnp.int32)],
    )
    return pl.pallas_call(
        functools.partial(_moe_kernel, chunk=chunk),
        grid_spec=grid_spec,
        out_shape=jax.ShapeDtypeStruct(x1.shape, F32),
        compiler_params=pltpu.CompilerParams(dimension_semantics=("arbitrary", "arbitrary"),
                                             vmem_limit_bytes=VMEM_LIMIT),
        name="moe",
    )(starts, tok, pos, gate, h2, x1, ada, gfin, w1, w3, w2)


def _dispatch_tables(routes):
    tile_tokens = [r.shape[1] * ROUTE_LANES for r, _ in routes]
    t_max = max(tile_tokens)
    es, gs, masks = [], [], []
    for (route, _), Tt in zip(routes, tile_tokens):
        assert Tt & (Tt - 1) == 0
        n = route.shape[0]
        by_k = route[:, :, :4, :].transpose(0, 2, 1, 3).reshape(n, 4, Tt)
        extra = 2 * (t_max - Tt)
        es.append(jnp.pad(by_k[:, :2].astype(jnp.int32).reshape(n, 2 * Tt), ((0, 0), (0, extra)),
                          constant_values=N_EXPERTS))
        gs.append(jnp.pad(by_k[:, 2:].reshape(n, 2 * Tt), ((0, 0), (0, extra))))
        masks.append(jnp.full((n, 1), Tt - 1, jnp.int32))
    flat_e = jnp.concatenate(es, axis=0)
    flat_g = jnp.concatenate(gs, axis=0)
    n_tiles = flat_e.shape[0]
    order = jnp.argsort(flat_e, axis=-1, stable=False).astype(jnp.int32)
    pos = jnp.argsort(order, axis=-1, stable=False).astype(jnp.int32) * SUBLANES
    tok_s = (order & jnp.concatenate(masks, axis=0)) * SUBLANES
    counts = jnp.concatenate([h[:, :N_EXPERTS] for _, h in routes], axis=0).astype(jnp.int32)
    starts = jnp.concatenate([jnp.zeros((n_tiles, 1), jnp.int32), jnp.cumsum(counts, axis=-1, dtype=jnp.int32)],
                             axis=-1)

    def tok_len(Tt):
        return -(-(2 * Tt + MOE_BLOCK) // SMEM_PAD) * SMEM_PAD

    tok_p = jnp.pad(tok_s, ((0, 0), (0, tok_len(t_max) - 2 * t_max)))
    Tt = t_max


    out, r0 = [], 0
    for (route, _), Tt in zip(routes, tile_tokens):
        r1 = r0 + route.shape[0]
        out.append((starts[r0:r1], tok_p[r0:r1, :tok_len(Tt)].reshape(-1),
                    pos[r0:r1, :2 * Tt].reshape(-1), flat_g[r0:r1, :2 * Tt].reshape(-1)))
        r0 = r1
    return out


def _pack_state(h_re, h_im):
    b = h_re.shape[0]
    return jnp.concatenate([h_re.reshape(b, STATE_COLS // 2), h_im.reshape(b, STATE_COLS // 2)], axis=-1)


def _unpack_state(st):
    b = st.shape[0]
    return (st[:, :STATE_COLS // 2].reshape(b, N_SSM_GROUPS, SSM_STATE),
            st[:, STATE_COLS // 2:].reshape(b, N_SSM_GROUPS, SSM_STATE))


def kernel(x_prompt, x_sample, state_ssm_re, state_ssm_im, c_prompt, c_sample, w_ada, b_ada, g_norm1, g_norm2, w_in, ln_g, ln_b, w_s, b_s, lam_re, lam_im, log_dt, ssm_b_re, ssm_b_im, ssm_c_re, ssm_c_im, ssm_d, w_glu, b_glu, w_out, w_group, b_group, w_expert, b_expert, w1, w3, w2, g_final):
    depth = w_ada.shape[0]
    assert depth == 1, "the final RMSNorm is fused into the (single) layer's MoE epilogue"
    B, L, D = x_prompt.shape
    Bs, Ls, _ = x_sample.shape
    xp = x_prompt
    xs_t = x_sample.transpose(1, 0, 2)
    eye = jnp.eye(SLAB_GROUPS, dtype=F32)
    tril = jnp.tril(jnp.ones((CHUNK, CHUNK), dtype=bool))
    p_re, p_im, s_re, s_im, s_v = [], [], [], [], []
    for l in range(depth):
        ada_p, ada_s = _ada(c_prompt, c_sample, w_ada[l], b_ada[l][None])

        ar, ai, br, bi = _discretize(lam_re[l], lam_im[l], log_dt[l], ssm_b_re[l], ssm_b_im[l])
        avec = jnp.concatenate([ar.reshape(1, STATE_COLS // 2), ai.reshape(1, STATE_COLS // 2)], axis=-1)

        def blockdiag_in(w):
            w4 = w.reshape(N_SLABS, SLAB_GROUPS, SSM_GROUP, SSM_STATE)
            return jnp.einsum('kghp,gG->kghGp', w4, eye).reshape(N_SLABS, LANES, SLAB_STATES)

        def blockdiag_out(w):
            w4 = w.reshape(N_SLABS, SLAB_GROUPS, SSM_GROUP, SSM_STATE)
            return jnp.einsum('kghp,gG->kgpGh', w4, eye).reshape(N_SLABS, SLAB_STATES, LANES)

        wb = jnp.concatenate([blockdiag_in(br), blockdiag_in(bi)], axis=-1).astype(BF16)
        wc = jnp.concatenate([blockdiag_out(ssm_c_re[l]), -blockdiag_out(ssm_c_im[l])], axis=1).astype(BF16)

        lane_pad = ROUTE_LANES - N_EXPERT_GROUPS - N_EXPERTS
        wr = jnp.pad(jnp.concatenate([w_group[l], w_expert[l]], axis=1), ((0, 0), (0, lane_pad)))
        wr_hi = wr.astype(BF16)
        wr_lo = (wr - wr_hi.astype(F32)).astype(BF16)
        br_t = jnp.pad(jnp.concatenate([b_group[l], b_expert[l]]), (0, lane_pad))[None]

        g1 = g_norm1[l][None]
        g2 = g_norm2[l][None]
        shared = dict(
            win=w_in[l].astype(BF16), lng=ln_g[l].reshape(1, D_A), lnb=ln_b[l].reshape(1, D_A),
            avec=avec, wb=wb, wc=wc, dsk=ssm_d[l].reshape(1, D_B), wglu=w_glu[l].astype(BF16),
            bglu=b_glu[l][None], wout=w_out[l].astype(BF16), wrc=jnp.concatenate([wr_hi, wr_lo], axis=1), brt=br_t)
        w_masked = jnp.where(tril[None], w_s[l], jnp.zeros_like(w_s[l]))
        wsp = w_masked.astype(BF16)
        bsp = jnp.broadcast_to(b_s[l][:, :, None], (N_HEADS, CHUNK, HEAD_DIM))
        wts_p = (g1, g2, shared['win'], shared['lng'], shared['lnb'], wsp, bsp, shared['avec'], shared['wb'],
                 shared['wc'], shared['dsk'], shared['wglu'], shared['bglu'], shared['wout'], shared['wrc'],
                 shared['brt'])
        wts_s = (g1, g2, shared['win'], shared['lng'], shared['lnb'], shared['avec'], shared['wb'],
                 shared['wc'], shared['dsk'], shared['wglu'], shared['bglu'], shared['wout'], shared['wrc'],
                 shared['brt'])

        w1b, w3b, w2b = w1[l].astype(BF16), w3[l].astype(BF16), w2[l].astype(BF16)
        gfin = g_final[None]

        h0p = jnp.zeros((B, STATE_COLS), F32)
        x1p, h2p, route_p, hist_p, st_p = _mixer_prompt(xp, ada_p, h0p, wts_p)
        h0s = _pack_state(state_ssm_re[l].astype(F32), state_ssm_im[l].astype(F32))
        w_small = w_masked[:, :Ls, :Ls].reshape(-1)
        b_small = b_s[l][:, :Ls].reshape(-1)
        x1s, h2s, route_s, hist_s, st_s, v_s = _mixer_sample(xs_t, ada_s, h0s, w_small, b_small, wts_s)

        (tables_p,) = _dispatch_tables([(route_p, hist_p)])
        (tables_s,) = _dispatch_tables([(route_s[None], hist_s)])
        xp = _moe(h2p, x1p, ada_p[:, None, :], *tables_p, gfin, w1b, w3b, w2b, n_tiles=B, chunk=256)
        xs_t = _moe(h2s, x1s, ada_s[None], *tables_s, gfin, w1b, w3b, w2b, n_tiles=1, chunk=Bs)
        hr, hi = _unpack_state(st_p)
        p_re.append(hr.astype(state_ssm_re.dtype))
        p_im.append(hi.astype(state_ssm_im.dtype))
        hr, hi = _unpack_state(st_s)
        s_re.append(hr.astype(state_ssm_re.dtype))
        s_im.append(hi.astype(state_ssm_im.dtype))
        s_v.append(v_s.transpose(1, 0, 2))

    y_prompt = xp
    y_sample = xs_t.transpose(1, 0, 2)
    return (y_prompt, y_sample, jnp.stack(p_re), jnp.stack(p_im), jnp.stack(s_re), jnp.stack(s_im), jnp.stack(s_v))
```

```python
import functools

import jax
import jax.numpy as jnp
from jax import lax
from jax.experimental import pallas as pl
from jax.experimental.pallas import tpu as pltpu

F32 = jnp.float32
BF16 = jnp.bfloat16

D_MODEL = 1024
D_A = 512
D_B = 512
N_HEADS = 4
HEAD_DIM = 128
CHUNK = 128
N_SSM_GROUPS = 32
SSM_GROUP = 16
SSM_STATE = 64
N_SLABS = 4
SLAB_GROUPS = N_SSM_GROUPS // N_SLABS
SLAB_STATES = SLAB_GROUPS * SSM_STATE
SLAB_COLS = 2 * SLAB_STATES
STATE_COLS = N_SLABS * SLAB_COLS
N_EXPERT_GROUPS = 4
EXPERTS_PER_GROUP = 8
N_EXPERTS = 32
D_EXPERT = 256
EPS = 1e-6

LANES = 128
SUBLANES = 8
ROUTE_LANES = LANES
MOE_BLOCK = 128
MOE_BLOCK_SHIFT = MOE_BLOCK.bit_length() - 1
assert 1 << MOE_BLOCK_SHIFT == MOE_BLOCK
EXPERTS_PER_STEP = 4
MOE_EPILOGUE_STEPS = 4
PAIR_FIELDS = 4
S_PITCH = CHUNK + SUBLANES
SMEM_PAD = 1024
VMEM_LIMIT = 58 * 1024 * 1024


def _gelu(x):
    c = 0.7978845608028654
    half_x = 0.5 * x
    return half_x + half_x * jnp.tanh(x * (c + (c * 0.044715) * (x * x)))


def _modulate(xf, gain, shift):
    ms = jnp.mean(xf * xf, axis=-1, keepdims=True)
    return xf * lax.rsqrt(ms + EPS) * gain + shift


def _rms(xf, g):
    ms = jnp.mean(xf * xf, axis=-1, keepdims=True)
    return xf * lax.rsqrt(ms + EPS) * g


def _ada_kernel(cp_ref, cs_ref, w_ref, b_ref, op_ref, os_ref):
    mp = cp_ref.shape[0]
    s = jax.nn.silu(jnp.concatenate([cp_ref[...], cs_ref[...]], axis=0)).astype(BF16)
    out = jnp.dot(s, w_ref[...].astype(BF16), preferred_element_type=F32) + b_ref[...]
    op_ref[...] = out[:mp]
    os_ref[...] = out[mp:]


def _ada(c_p, c_s, w, b):
    mp, ms = c_p.shape[0], c_s.shape[0]
    n = w.shape[1]
    bn = 1024
    return pl.pallas_call(
        _ada_kernel,
        grid=(n // bn,),
        in_specs=[pl.BlockSpec((mp, D_MODEL), lambda j: (0, 0)),
                  pl.BlockSpec((ms, D_MODEL), lambda j: (0, 0)),
                  pl.BlockSpec((D_MODEL, bn), lambda j: (0, j)),
                  pl.BlockSpec((1, bn), lambda j: (0, j))],
        out_specs=(pl.BlockSpec((mp, bn), lambda j: (0, j)), pl.BlockSpec((ms, bn), lambda j: (0, j))),
        out_shape=(jax.ShapeDtypeStruct((mp, n), F32), jax.ShapeDtypeStruct((ms, n), F32)),
        name="ada",
    )(c_p, c_s, w, b)


def _disc_kernel(lre_ref, lim_ref, ldt_ref, bre_ref, bim_ref, ar_ref, ai_ref, br_ref, bi_ref):
    dt = jnp.exp(ldt_ref[...])
    lr = lre_ref[...]
    li = lim_ref[...]
    mag = jnp.exp(lr * dt)
    ar = mag * jnp.cos(li * dt)
    ai = mag * jnp.sin(li * dt)
    den = lr * lr + li * li
    cr = ((ar - 1) * lr + ai * li) / den
    ci = (ai * lr - (ar - 1) * li) / den
    ar_ref[...] = ar
    ai_ref[...] = ai
    bre = bre_ref[...]
    bim = bim_ref[...]
    br_ref[...] = cr * bre - ci * bim
    bi_ref[...] = cr * bim + ci * bre


def _discretize(lam_re, lam_im, log_dt, b_re, b_im):
    g, p, h = b_re.shape
    o1 = jax.ShapeDtypeStruct((g, 1, p), F32)
    o2 = jax.ShapeDtypeStruct((g, h, p), F32)
    return pl.pallas_call(_disc_kernel, out_shape=(o1, o1, o2, o2), name="ssm_disc")(
        lam_re.reshape(g, 1, p), lam_im.reshape(g, 1, p), log_dt.reshape(g, 1, 1),
        b_re.transpose(0, 2, 1), b_im.transpose(0, 2, 1))


def _s_pieces(r0, nrows, grp, pitch):
    return [(i * grp, slice((r0 // grp + i) * pitch, (r0 // grp + i) * pitch + grp)) for i in range(nrows // grp)]


def _front(r0, nrows, h_scr, win_ref, lng_ref, lnb_ref, u_scr, vn_scr, s_scr, grp, pitch):
    rows = slice(r0, r0 + nrows)
    proj = jnp.dot(h_scr[rows, :], win_ref[...], preferred_element_type=F32)
    u_scr[rows, :] = _gelu(proj[:, :D_A])
    vraw = _gelu(proj[:, D_A:2 * D_A])
    for h in range(N_HEADS):
        cols = slice(h * HEAD_DIM, (h + 1) * HEAD_DIM)
        vh = vraw[:, cols]
        mu = jnp.mean(vh, axis=-1, keepdims=True)
        dv = vh - mu
        var = jnp.mean(dv * dv, axis=-1, keepdims=True)
        vn_scr[rows, cols] = dv * lax.rsqrt(var + EPS) * lng_ref[:, cols] + lnb_ref[:, cols]
    for k in range(N_SLABS):
        for off, prow in _s_pieces(r0, nrows, grp, pitch):
            s_scr[k, prow, :] = proj[off:off + grp, 2 * D_A + k * LANES:2 * D_A + (k + 1) * LANES]


def _scan_slab(bu, state, avec_ref, k, rows_per_step, steps):
    c_re = slice(0, SLAB_STATES)
    c_im = slice(SLAB_STATES, SLAB_COLS)
    s_re = slice(k * SLAB_STATES, (k + 1) * SLAB_STATES)
    s_im = slice(STATE_COLS // 2 + k * SLAB_STATES, STATE_COLS // 2 + (k + 1) * SLAB_STATES)
    ar = jnp.broadcast_to(avec_ref[:, s_re], (SUBLANES, SLAB_STATES))
    ai = jnp.broadcast_to(avec_ref[:, s_im], (SUBLANES, SLAB_STATES))
    for rc in range(rows_per_step // SUBLANES):
        r0 = rc * SUBLANES
        sr = state[r0:r0 + SUBLANES, s_re]
        si = state[r0:r0 + SUBLANES, s_im]
        for t in range(steps):
            rows = slice(t * rows_per_step + r0, t * rows_per_step + r0 + SUBLANES)
            nr = ar * sr - ai * si + bu[rows, c_re]
            ni = ar * si + ai * sr + bu[rows, c_im]
            bu[rows, c_re] = nr
            bu[rows, c_im] = ni
            sr, si = nr, ni
        state[r0:r0 + SUBLANES, s_re] = sr
        state[r0:r0 + SUBLANES, s_im] = si


def _s5(st_ref, bu_scr, yt_ref, state, avec_ref, wb_ref, wc_ref, rows_per_step, steps):
    for k in range(N_SLABS):
        bu = bu_scr.at[k % 2]
        bu[...] = jnp.dot(st_ref[k].astype(BF16), wb_ref[k], preferred_element_type=F32)
        _scan_slab(bu, state, avec_ref, k, rows_per_step, steps)
        yt_ref[k] = jnp.dot(bu[...].astype(BF16), wc_ref[k], preferred_element_type=F32)


def _back(r0, nrows, s_scr, ab_scr, wglu_ref, bglu_ref, wout_ref, grp, pitch):
    rows = slice(r0, r0 + nrows)
    pieces = _s_pieces(r0, nrows, grp, pitch)
    y = _gelu(jnp.concatenate(
        [jnp.concatenate([s_scr[k, prow, :] for _, prow in pieces], axis=0) for k in range(N_SLABS)], axis=-1))
    gate = jnp.dot(y.astype(BF16), wglu_ref[...], preferred_element_type=F32) + bglu_ref[...]
    ab_scr[rows, D_A:] = (y * jax.nn.sigmoid(gate)).astype(BF16)
    return jnp.dot(ab_scr[rows, :], wout_ref[...], preferred_element_type=F32)


def _route(h2_bf, wrc_ref, brt_ref):
    both = jnp.dot(h2_bf, wrc_ref[...], preferred_element_type=F32)
    logits = both[:, :ROUTE_LANES] + both[:, ROUTE_LANES:] + brt_ref[...]
    n = logits.shape[0]
    lane = lax.broadcasted_iota(jnp.int32, (n, ROUTE_LANES), 1)
    lane_f = lane.astype(F32)
    big = jnp.float32(1e9)
    ninf = jnp.float32(-jnp.inf)
    is_g = lane < N_EXPERT_GROUPS
    gl = jnp.where(is_g, logits, ninf)
    gmax = jnp.max(gl, axis=-1, keepdims=True)
    gidx = jnp.min(jnp.where(gl == gmax, lane_f, big), axis=-1, keepdims=True)
    gsum = jnp.sum(jnp.where(is_g, jnp.exp(logits - gmax), 0.0), axis=-1, keepdims=True)
    g_w = 1.0 / gsum
    elo = N_EXPERT_GROUPS + EXPERTS_PER_GROUP * gidx
    emask = (lane_f >= elo) & (lane_f < elo + EXPERTS_PER_GROUP)
    el = jnp.where(emask, logits, ninf)
    t1 = jnp.max(el, axis=-1, keepdims=True)
    i1 = jnp.min(jnp.where(el == t1, lane_f, big), axis=-1, keepdims=True)
    el2 = jnp.where(lane_f == i1, ninf, el)
    t2 = jnp.max(el2, axis=-1, keepdims=True)
    i2 = jnp.min(jnp.where(el2 == t2, lane_f, big), axis=-1, keepdims=True)
    e21 = jnp.exp(t2 - t1)
    den = 1.0 + e21
    gate1 = g_w * (1.0 / den)
    gate2 = g_w * (e21 / den)
    e1 = i1 - N_EXPERT_GROUPS
    e2 = i2 - N_EXPERT_GROUPS
    packed = jnp.where(lane == 0, e1, jnp.where(lane == 1, e2, jnp.where(lane == 2, gate1,
                                                                           jnp.where(lane == 3, gate2, 0.0))))
    hist = jnp.sum(((lane_f == e1) | (lane_f == e2)).astype(F32), axis=0, keepdims=True)
    return packed.T[:SUBLANES], hist


def _mixer_prompt_kernel(x_ref, ada_ref, h0_ref, g1_ref, g2_ref, win_ref, lng_ref, lnb_ref, wsp_ref, bsp_ref,
                         avec_ref, wb_ref, wc_ref, dsk_ref, wglu_ref, bglu_ref, wout_ref, wrc_ref, brt_ref,
                         x1_ref, h2_ref, route_ref, hist_ref, state_ref,
                         h_scr, u_scr, vn_scr, s_scr, st_scr, yt_scr, bu_scr, ab_scr, hi_scr):
    nb = x_ref.shape[0]
    half = (nb // 2) * CHUNK
    D = D_MODEL
    pitch = S_PITCH
    step = pl.program_id(0)

    @pl.when(step == 0)
    def _():
        state_ref[...] = h0_ref[...]
        hist_ref[...] = jnp.zeros_like(hist_ref)
        hi_scr[...] = jnp.zeros_like(hi_scr)

    def mod(b, i):
        return ada_ref[b:b + 1, i * D:(i + 1) * D]

    def route_chunk(chunk, weight):
        for b in range(nb):
            rows = slice(b * CHUNK, (b + 1) * CHUNK)
            route, hist = _route(hi_scr[rows, :], wrc_ref, brt_ref)
            route_ref[b, chunk] = route
            hist_ref[b:b + 1, :] = hist_ref[b:b + 1, :] + weight * hist

    route_chunk(jnp.maximum(step - 1, 0), (step > 0).astype(F32))

    for b in range(nb):
        hb = _modulate(x_ref[b], g1_ref[...] * (1 + mod(b, 1)), mod(b, 0))
        h_scr[b * CHUNK:(b + 1) * CHUNK, :] = hb.astype(BF16)

    for r0 in (0, half):
        _front(r0, half, h_scr, win_ref, lng_ref, lnb_ref, u_scr, vn_scr, s_scr, CHUNK, pitch)

    for b in range(nb):
        rows = slice(b * CHUNK, (b + 1) * CHUNK)
        for h in range(N_HEADS):
            cols = slice(h * HEAD_DIM, (h + 1) * HEAD_DIM)
            mixed = jnp.dot(wsp_ref[h], vn_scr[rows, cols].astype(BF16), preferred_element_type=F32) + bsp_ref[h]
            ab_scr[rows, cols] = (u_scr[rows, cols] * mixed).astype(BF16)

    for k in range(N_SLABS):
        for t in range(CHUNK):
            st_scr[k, t * nb:(t + 1) * nb, :] = s_scr[k, pl.ds(t, nb, stride=pitch), :]
    _s5(st_scr, bu_scr, yt_scr, state_ref, avec_ref, wb_ref, wc_ref, nb, CHUNK)
    for k in range(N_SLABS):
        dsk = dsk_ref[:, k * LANES:(k + 1) * LANES]
        for t in range(CHUNK):
            sel = pl.ds(t, nb, stride=pitch)
            s_scr[k, sel, :] = yt_scr[k, t * nb:(t + 1) * nb, :] + dsk * s_scr[k, sel, :]

    for r0 in (0, half):
        mix = _back(r0, half, s_scr, ab_scr, wglu_ref, bglu_ref, wout_ref, CHUNK, pitch)
        for bl in range(nb // 2):
            b = r0 // CHUNK + bl
            x1 = x_ref[b] + mod(b, 2) * mix[bl * CHUNK:(bl + 1) * CHUNK, :]
            x1_ref[b] = x1
            h2 = _modulate(x1, g2_ref[...] * (1 + mod(b, 4)), mod(b, 3))
            hi = h2.astype(BF16)
            hi_scr[b * CHUNK:(b + 1) * CHUNK, :] = hi
            h2_ref[b] = hi

    @pl.when(step == pl.num_programs(0) - 1)
    def _():
        route_chunk(step, 1.0)


def _const_spec(shape):
    nd = len(shape)
    return pl.BlockSpec(shape, lambda *_: (0,) * nd, pipeline_mode=pl.Buffered(1))


def _mixer_prompt(x, ada, h0, wts):
    nb, seq, D = x.shape
    n_chunks = seq // CHUNK
    R = nb * CHUNK
    weight_specs = [_const_spec(w.shape) for w in wts]
    in_specs = [pl.BlockSpec((nb, CHUNK, D), lambda i: (0, i, 0)),
                _const_spec(ada.shape), _const_spec(h0.shape)] + weight_specs
    out_shape = (jax.ShapeDtypeStruct((nb, seq, D), F32),
                 jax.ShapeDtypeStruct((nb, seq, D), BF16),
                 jax.ShapeDtypeStruct((nb, n_chunks, SUBLANES, ROUTE_LANES), F32),
                 jax.ShapeDtypeStruct((nb, ROUTE_LANES), F32),
                 jax.ShapeDtypeStruct((nb, STATE_COLS), F32))
    out_specs = (pl.BlockSpec((nb, CHUNK, D), lambda i: (0, i, 0)),
                 pl.BlockSpec((nb, CHUNK, D), lambda i: (0, i, 0)),
                 pl.BlockSpec((nb, n_chunks, SUBLANES, ROUTE_LANES), lambda i: (0, 0, 0, 0)),
                 pl.BlockSpec((nb, ROUTE_LANES), lambda i: (0, 0)),
                 pl.BlockSpec((nb, STATE_COLS), lambda i: (0, 0)))
    scratch = [pltpu.VMEM((R, D), BF16),
               pltpu.VMEM((R, D_A), F32),
               pltpu.VMEM((R, D_A), F32),
               pltpu.VMEM((N_SLABS, nb * S_PITCH, LANES), F32),
               pltpu.VMEM((N_SLABS, R, LANES), F32),
               pltpu.VMEM((N_SLABS, R, LANES), F32),
               pltpu.VMEM((2, R, SLAB_COLS), F32),
               pltpu.VMEM((R, D), BF16),
               pltpu.VMEM((R, D), BF16)]
    return pl.pallas_call(
        _mixer_prompt_kernel,
        grid=(n_chunks,),
        in_specs=in_specs,
        out_specs=out_specs,
        out_shape=out_shape,
        scratch_shapes=scratch,
        compiler_params=pltpu.CompilerParams(dimension_semantics=("arbitrary",), vmem_limit_bytes=VMEM_LIMIT),
        name="mixer_prompt",
    )(x, ada, h0, *wts)


def _mixer_sample_kernel(wsm_ref, bsm_ref, x_ref, ada_ref, h0_ref, g1_ref, g2_ref, win_ref, lng_ref, lnb_ref,
                         avec_ref, wb_ref, wc_ref, dsk_ref, wglu_ref, bglu_ref, wout_ref, wrc_ref, brt_ref,
                         x1_ref, h2_ref, route_ref, hist_ref, state_ref, v_ref,
                         h_scr, u_scr, vn_scr, s_scr, yt_scr, bu_scr, ab_scr):
    T, nb, D = x_ref.shape
    R = T * nb
    half = R // 2

    def mod(i):
        return ada_ref[:, i * D:(i + 1) * D]

    state_ref[...] = h0_ref[...]
    gain1 = g1_ref[...] * (1 + mod(1))
    gain2 = g2_ref[...] * (1 + mod(4))
    for t in range(T):
        ht = _modulate(x_ref[t], gain1, mod(0))
        h_scr[t * nb:(t + 1) * nb, :] = ht.astype(BF16)

    for r0 in (0, half):
        _front(r0, half, h_scr, win_ref, lng_ref, lnb_ref, u_scr, vn_scr, s_scr, half, half)

    for t in range(T):
        rows = slice(t * nb, (t + 1) * nb)
        v_ref[t] = vn_scr[rows, :]
        for h in range(N_HEADS):
            cols = slice(h * HEAD_DIM, (h + 1) * HEAD_DIM)
            acc = jnp.full((nb, HEAD_DIM), bsm_ref[h * T + t], F32)
            for s in range(t + 1):
                acc = acc + wsm_ref[(h * T + t) * T + s] * vn_scr[s * nb:(s + 1) * nb, cols]
            ab_scr[rows, cols] = (u_scr[rows, cols] * acc).astype(BF16)

    _s5(s_scr, bu_scr, yt_scr, state_ref, avec_ref, wb_ref, wc_ref, nb, T)
    for k in range(N_SLABS):
        s_scr[k] = yt_scr[k] + dsk_ref[:, k * LANES:(k + 1) * LANES] * s_scr[k]

    hist_total = jnp.zeros((1, ROUTE_LANES), F32)
    for r0 in (0, half):
        mix = _back(r0, half, s_scr, ab_scr, wglu_ref, bglu_ref, wout_ref, half, half)
        for tl in range(T // 2):
            t = r0 // nb + tl
            x1 = x_ref[t] + mod(2) * mix[tl * nb:(tl + 1) * nb, :]
            x1_ref[t] = x1
            h2 = _modulate(x1, gain2, mod(3))
            hi = h2.astype(BF16)
            h2_ref[t] = hi
            route, hist = _route(hi, wrc_ref, brt_ref)
            route_ref[t] = route
            hist_total = hist_total + hist
    hist_ref[...] = hist_total


def _mixer_sample(x_t, ada, h0, w_small, b_small, wts):
    T, nb, D = x_t.shape
    R = T * nb
    smem = pl.BlockSpec(memory_space=pltpu.SMEM)
    out_shape = (jax.ShapeDtypeStruct((T, nb, D), F32),
                 jax.ShapeDtypeStruct((T, nb, D), BF16),
                 jax.ShapeDtypeStruct((T, SUBLANES, ROUTE_LANES), F32),
                 jax.ShapeDtypeStruct((1, ROUTE_LANES), F32),
                 jax.ShapeDtypeStruct((nb, STATE_COLS), F32),
                 jax.ShapeDtypeStruct((T, nb, D_A), F32))
    scratch = [pltpu.VMEM((R, D), BF16),
               pltpu.VMEM((R, D_A), F32),
               pltpu.VMEM((R, D_A), F32),
               pltpu.VMEM((N_SLABS, R, LANES), F32),
               pltpu.VMEM((N_SLABS, R, LANES), F32),
               pltpu.VMEM((2, R, SLAB_COLS), F32),
               pltpu.VMEM((R, D), BF16)]
    vmem = pl.BlockSpec(memory_space=pltpu.VMEM)
    return pl.pallas_call(
        _mixer_sample_kernel,
        in_specs=[smem, smem] + [vmem] * (3 + len(wts)),
        out_specs=(vmem,) * 6,
        out_shape=out_shape,
        scratch_shapes=scratch,
        compiler_params=pltpu.CompilerParams(vmem_limit_bytes=VMEM_LIMIT),
        name="mixer_sample",
    )(w_small, b_small, x_t, ada, h0, *wts)


def _rows_to_tiles(tiles_ref, row0, val):
    n = val.shape[0]
    for c in range(val.shape[1] // LANES):
        tiles_ref[pl.ds(row0 * SUBLANES + c, n, stride=SUBLANES), :] = val[:, c * LANES:(c + 1) * LANES]


def _tiles_to_rows(tiles_ref, row0, n):
    return jnp.concatenate([tiles_ref[pl.ds(row0 * SUBLANES + c, n, stride=SUBLANES), :] for c in range(SUBLANES)],
                           axis=-1)


def _moe_kernel(starts_ref, tok_ref, pos_ref, gate_ref, h2_ref, x1_ref, ada_ref, gfin_ref, w1_ref, w3_ref, w2_ref,
                out_ref, h_tiles, y_tiles, xb_even, xb_odd, cp_smem, tab_smem, *, chunk):
    t = pl.program_id(0)
    e = pl.program_id(1)
    n_expert_steps = N_EXPERTS // EXPERTS_PER_STEP
    ngrp, rows_g, D = h2_ref.shape
    n_tok = ngrp * rows_g
    gb, rb, _ = x1_ref.shape
    epi_tok = gb * rb
    mrows = ada_ref.shape[1]

    def tile_of(row):
        return pl.ds(pl.multiple_of(row * SUBLANES, SUBLANES), SUBLANES)

    def tile_at(row8):
        return pl.ds(pl.multiple_of(row8, SUBLANES), SUBLANES)

    @pl.when(e == 0)
    def _():
        for gi in range(ngrp):
            for ci in range(rows_g // chunk):
                h2 = h2_ref[gi, ci * chunk:(ci + 1) * chunk, :].astype(F32)
                _rows_to_tiles(h_tiles, gi * rows_g + ci * chunk, h2)

    @pl.when(e == 0)
    def _():
        gp = jnp.int32(0)
        for s in range(N_EXPERTS // EXPERTS_PER_STEP):
            cp_smem[s] = gp
            st = [starts_ref[t, s * EXPERTS_PER_STEP + i] for i in range(EXPERTS_PER_STEP + 1)]
            first = [jnp.int32(0)]
            for i in range(EXPERTS_PER_STEP):
                first.append(first[i] + lax.shift_right_logical(st[i + 1] - st[i] + (MOE_BLOCK - 1), MOE_BLOCK_SHIFT))
            nb = first[EXPERTS_PER_STEP]

            def locate(f, st=st, first=first):
                el, fb, sb = jnp.int32(0), first[0], st[0]
                for i in range(1, EXPERTS_PER_STEP):
                    hit = f >= first[i]
                    el = jnp.where(hit, i, el)
                    fb = jnp.where(hit, first[i], fb)
                    sb = jnp.where(hit, st[i], sb)
                return sb + (f - fb) * MOE_BLOCK, el

            def add_pair(p, gp, nb=nb, locate=locate):
                base_a, el_a = locate(2 * p)
                base_b, el_b = locate(jnp.minimum(2 * p + 1, nb - 1))
                for i, v in enumerate((base_a, base_b, el_a, el_b)):
                    tab_smem[PAIR_FIELDS * gp + i] = v
                return gp + 1

            gp = lax.fori_loop(0, lax.shift_right_logical(nb + 1, 1), add_pair, gp)
        cp_smem[N_EXPERTS // EXPERTS_PER_STEP] = gp
        for i in range(PAIR_FIELDS):
            tab_smem[PAIR_FIELDS * gp + i] = jnp.int32(0)

    def pair_entry(gp):
        return [tab_smem[PAIR_FIELDS * gp + i] for i in range(PAIR_FIELDS)]

    @pl.when(e == 0)
    def _():
        base_a, base_b, _, _ = pair_entry(0)

        def gather8(j8, c):
            for jj in range(SUBLANES):
                j = j8 * SUBLANES + jj
                xb_even[tile_of(j), :] = h_tiles[tile_at(tok_ref[base_a + j]), :]
                xb_even[tile_of(MOE_BLOCK + j), :] = h_tiles[tile_at(tok_ref[base_b + j]), :]
            return c

        lax.fori_loop(0, MOE_BLOCK // SUBLANES, gather8, 0)

    def pair_body(gp, cur_tiles, next_tiles):
        next_a, next_b, _, _ = pair_entry(gp + 1)
        n_slices = 8
        per = 2 * MOE_BLOCK // n_slices
        slices = iter(range(n_slices))

        def gather_slice():
            s = next(slices)
            for r in range(s * per, (s + 1) * per):
                tok8 = tok_ref[next_a + r] if r < MOE_BLOCK else tok_ref[next_b + r - MOE_BLOCK]
                next_tiles[r * SUBLANES:(r + 1) * SUBLANES, :] = h_tiles[tile_at(tok8), :]

        base_a, base_b, el_a, el_b = pair_entry(gp)
        xa = _tiles_to_rows(cur_tiles, 0, MOE_BLOCK).astype(BF16)
        gather_slice()
        a1 = jnp.dot(xa, w1_ref[el_a], preferred_element_type=F32)
        gather_slice()
        a3 = jnp.dot(xa, w3_ref[el_a], preferred_element_type=F32)
        gather_slice()
        xb = _tiles_to_rows(cur_tiles, MOE_BLOCK, MOE_BLOCK).astype(BF16)
        b1 = jnp.dot(xb, w1_ref[el_b], preferred_element_type=F32)
        gather_slice()
        b3 = jnp.dot(xb, w3_ref[el_b], preferred_element_type=F32)
        gather_slice()
        ya = jnp.dot((jax.nn.silu(a1) * a3).astype(BF16), w2_ref[el_a], preferred_element_type=F32)
        gather_slice()
        yb = jnp.dot((jax.nn.silu(b1) * b3).astype(BF16), w2_ref[el_b], preferred_element_type=F32)
        gather_slice()
        _rows_to_tiles(y_tiles, base_a, ya)
        gather_slice()
        _rows_to_tiles(y_tiles, base_b, yb)

    def pairs(gp, count):
        def run(first, second):
            bufs = (first, second)
            for i in range(count):
                pair_body(gp + i, bufs[i % 2], bufs[(i + 1) % 2])

        @pl.when((gp & 1) == 0)
        def _():
            run(xb_even, xb_odd)

        @pl.when((gp & 1) == 1)
        def _():
            run(xb_odd, xb_even)

    @pl.when(e < n_expert_steps)
    def _():
        first, last = cp_smem[e], cp_smem[e + 1]
        n_double = lax.shift_right_logical(last - first, 1)

        def double(i, carry):
            pairs(first + 2 * i, 2)
            return carry

        lax.fori_loop(0, n_double, double, 0)

        @pl.when(((last - first) & 1) == 1)
        def _():
            pairs(last - 1, 1)

    @pl.when(e >= n_expert_steps)
    def _():
        tok0 = (e - n_expert_steps) * epi_tok

        def combine8(t8, c):
            for tt in range(SUBLANES):
                tok = tok0 + t8 * SUBLANES + tt
                y0 = y_tiles[tile_at(pos_ref[tok]), :]
                y1 = y_tiles[tile_at(pos_ref[n_tok + tok]), :]
                h_tiles[tile_of(tok), :] = gate_ref[tok] * y0 + gate_ref[n_tok + tok] * y1
            return c

        lax.fori_loop(0, epi_tok // SUBLANES, combine8, 0)
        sub = min(rb, chunk)
        for gi in range(gb):
            for ci in range(rb // sub):
                rows = slice(ci * sub, (ci + 1) * sub)
                gt2 = ada_ref[0, :, 5 * D:6 * D] if mrows == 1 else ada_ref[0, rows, 5 * D:6 * D]
                moe = _tiles_to_rows(h_tiles, tok0 + gi * rb + ci * sub, sub)
                x2 = x1_ref[gi, rows, :] + gt2 * moe
                out_ref[gi, rows, :] = _rms(x2, gfin_ref[...])


def _moe(h2, x1, ada, starts, tok, pos, gate, gfin, w1, w3, w2, *, n_tiles, chunk):
    ngrp = x1.shape[0] // n_tiles
    rows_g, D = x1.shape[1], x1.shape[2]
    mrows = ada.shape[1]
    Tt = ngrp * rows_g
    n_expert_steps = N_EXPERTS // EXPERTS_PER_STEP
    epi_tok = Tt // MOE_EPILOGUE_STEPS
    if ngrp == 1:
        epi_block = (1, epi_tok, D)

        def epi_map(t, e, st):
            return (t, jnp.maximum(e - n_expert_steps, 0), 0)
    else:
        assert n_tiles == 1 and epi_tok % rows_g == 0
        epi_block = (epi_tok // rows_g, rows_g, D)

        def epi_map(t, e, st):
            return (jnp.maximum(e - n_expert_steps, 0), 0, 0)

    def weight_map(t, e, st):
        return (jnp.minimum(e, n_expert_steps - 1), 0, 0)
    lp = tok.shape[0] // n_tiles
    max_pairs = (2 * Tt // MOE_BLOCK + N_EXPERTS + N_EXPERTS // EXPERTS_PER_STEP) // 2 + 1
    grid_spec = pltpu.PrefetchScalarGridSpec(
        num_scalar_prefetch=1,
        grid=(n_tiles, n_expert_steps + MOE_EPILOGUE_STEPS),
        in_specs=[
            pl.BlockSpec((lp,), lambda t, e, st: (t,), memory_space=pltpu.SMEM),
            pl.BlockSpec((2 * Tt,), lambda t, e, st: (t,), memory_space=pltpu.SMEM),
            pl.BlockSpec((2 * Tt,), lambda t, e, st: (t,), memory_space=pltpu.SMEM),
            pl.BlockSpec((ngrp, rows_g, D), lambda t, e, st: (t, 0, 0)),
            pl.BlockSpec(epi_block, epi_map),
            pl.BlockSpec((1, mrows, 6 * D), lambda t, e, st: (t, 0, 0)),
            pl.BlockSpec((1, D), lambda t, e, st: (0, 0)),
            pl.BlockSpec((EXPERTS_PER_STEP, D, D_EXPERT), weight_map),
            pl.BlockSpec((EXPERTS_PER_STEP, D, D_EXPERT), weight_map),
            pl.BlockSpec((EXPERTS_PER_STEP, D_EXPERT, D), weight_map),
        ],
        out_specs=pl.BlockSpec(epi_block, epi_map),
        scratch_shapes=[pltpu.VMEM((Tt * SUBLANES, LANES), F32),
                        pltpu.VMEM(((2 * Tt + MOE_BLOCK) * SUBLANES, LANES), F32),
                        pltpu.VMEM((2 * MOE_BLOCK * SUBLANES, LANES), F32),
                        pltpu.VMEM((2 * MOE_BLOCK * SUBLANES, LANES), F32),
                        pltpu.SMEM((N_EXPERTS // EXPERTS_PER_STEP + 1,), jnp.int32),
                        pltpu.SMEM((max_pairs * PAIR_FIELDS,), jnp.int32)],
    )
    return pl.pallas_call(
        functools.partial(_moe_kernel, chunk=chunk),
        grid_spec=grid_spec,
        out_shape=jax.ShapeDtypeStruct(x1.shape, F32),
        compiler_params=pltpu.CompilerParams(dimension_semantics=("arbitrary", "arbitrary"),
                                             vmem_limit_bytes=VMEM_LIMIT),
        name="moe",
    )(starts, tok, pos, gate, h2, x1, ada, gfin, w1, w3, w2)


def _dispatch_tables(routes):
    tile_tokens = [r.shape[1] * ROUTE_LANES for r, _ in routes]
    t_max = max(tile_tokens)
    es, gs, masks = [], [], []
    for (route, _), Tt in zip(routes, tile_tokens):
        assert Tt & (Tt - 1) == 0
        n = route.shape[0]
        by_k = route[:, :, :4, :].transpose(0, 2, 1, 3).reshape(n, 4, Tt)
        extra = 2 * (t_max - Tt)
        es.append(jnp.pad(by_k[:, :2].astype(jnp.int32).reshape(n, 2 * Tt), ((0, 0), (0, extra)),
                          constant_values=N_EXPERTS))
        gs.append(jnp.pad(by_k[:, 2:].reshape(n, 2 * Tt), ((0, 0), (0, extra))))
        masks.append(jnp.full((n, 1), Tt - 1, jnp.int32))
    flat_e = jnp.concatenate(es, axis=0)
    flat_g = jnp.concatenate(gs, axis=0)
    n_tiles = flat_e.shape[0]
    order = jnp.argsort(flat_e, axis=-1, stable=False).astype(jnp.int32)
    pos = jnp.argsort(order, axis=-1, stable=False).astype(jnp.int32) * SUBLANES
    tok_s = (order & jnp.concatenate(masks, axis=0)) * SUBLANES
    counts = jnp.concatenate([h[:, :N_EXPERTS] for _, h in routes], axis=0).astype(jnp.int32)
    starts = jnp.concatenate([jnp.zeros((n_tiles, 1), jnp.int32), jnp.cumsum(counts, axis=-1, dtype=jnp.int32)],
                             axis=-1)

    def tok_len(Tt):
        return -(-(2 * Tt + MOE_BLOCK) // SMEM_PAD) * SMEM_PAD

    tok_p = jnp.pad(tok_s, ((0, 0), (0, tok_len(t_max) - 2 * t_max)))
    Tt = t_max


    out, r0 = [], 0
    for (route, _), Tt in zip(routes, tile_tokens):
        r1 = r0 + route.shape[0]
        out.append((starts[r0:r1], tok_p[r0:r1, :tok_len(Tt)].reshape(-1),
                    pos[r0:r1, :2 * Tt].reshape(-1), flat_g[r0:r1, :2 * Tt].reshape(-1)))
        r0 = r1
    return out


def _pack_state(h_re, h_im):
    b = h_re.shape[0]
    return jnp.concatenate([h_re.reshape(b, STATE_COLS // 2), h_im.reshape(b, STATE_COLS // 2)], axis=-1)


def _unpack_state(st):
    b = st.shape[0]
    return (st[:, :STATE_COLS // 2].reshape(b, N_SSM_GROUPS, SSM_STATE),
            st[:, STATE_COLS // 2:].reshape(b, N_SSM_GROUPS, SSM_STATE))


def kernel(x_prompt, x_sample, state_ssm_re, state_ssm_im, c_prompt, c_sample, w_ada, b_ada, g_norm1, g_norm2, w_in, ln_g, ln_b, w_s, b_s, lam_re, lam_im, log_dt, ssm_b_re, ssm_b_im, ssm_c_re, ssm_c_im, ssm_d, w_glu, b_glu, w_out, w_group, b_group, w_expert, b_expert, w1, w3, w2, g_final):
    depth = w_ada.shape[0]
    assert depth == 1, "the final RMSNorm is fused into the (single) layer's MoE epilogue"
    B, L, D = x_prompt.shape
    Bs, Ls, _ = x_sample.shape
    xp = x_prompt
    xs_t = x_sample.transpose(1, 0, 2)
    eye = jnp.eye(SLAB_GROUPS, dtype=F32)
    tril = jnp.tril(jnp.ones((CHUNK, CHUNK), dtype=bool))
    p_re, p_im, s_re, s_im, s_v = [], [], [], [], []
    for l in range(depth):
        ada_p, ada_s = _ada(c_prompt, c_sample, w_ada[l], b_ada[l][None])

        ar, ai, br, bi = _discretize(lam_re[l], lam_im[l], log_dt[l], ssm_b_re[l], ssm_b_im[l])
        avec = jnp.concatenate([ar.reshape(1, STATE_COLS // 2), ai.reshape(1, STATE_COLS // 2)], axis=-1)

        def blockdiag_in(w):
            w4 = w.reshape(N_SLABS, SLAB_GROUPS, SSM_GROUP, SSM_STATE)
            return jnp.einsum('kghp,gG->kghGp', w4, eye).reshape(N_SLABS, LANES, SLAB_STATES)

        def blockdiag_out(w):
            w4 = w.reshape(N_SLABS, SLAB_GROUPS, SSM_GROUP, SSM_STATE)
            return jnp.einsum('kghp,gG->kgpGh', w4, eye).reshape(N_SLABS, SLAB_STATES, LANES)

        wb = jnp.concatenate([blockdiag_in(br), blockdiag_in(bi)], axis=-1).astype(BF16)
        wc = jnp.concatenate([blockdiag_out(ssm_c_re[l]), -blockdiag_out(ssm_c_im[l])], axis=1).astype(BF16)

        lane_pad = ROUTE_LANES - N_EXPERT_GROUPS - N_EXPERTS
        wr = jnp.pad(jnp.concatenate([w_group[l], w_expert[l]], axis=1), ((0, 0), (0, lane_pad)))
        wr_hi = wr.astype(BF16)
        wr_lo = (wr - wr_hi.astype(F32)).astype(BF16)
        br_t = jnp.pad(jnp.concatenate([b_group[l], b_expert[l]]), (0, lane_pad))[None]

        g1 = g_norm1[l][None]
        g2 = g_norm2[l][None]
        shared = dict(
            win=w_in[l].astype(BF16), lng=ln_g[l].reshape(1, D_A), lnb=ln_b[l].reshape(1, D_A),
            avec=avec, wb=wb, wc=wc, dsk=ssm_d[l].reshape(1, D_B), wglu=w_glu[l].astype(BF16),
            bglu=b_glu[l][None], wout=w_out[l].astype(BF16), wrc=jnp.concatenate([wr_hi, wr_lo], axis=1), brt=br_t)
        w_masked = jnp.where(tril[None], w_s[l], jnp.zeros_like(w_s[l]))
        wsp = w_masked.astype(BF16)
        bsp = jnp.broadcast_to(b_s[l][:, :, None], (N_HEADS, CHUNK, HEAD_DIM))
        wts_p = (g1, g2, shared['win'], shared['lng'], shared['lnb'], wsp, bsp, shared['avec'], shared['wb'],
                 shared['wc'], shared['dsk'], shared['wglu'], shared['bglu'], shared['wout'], shared['wrc'],
                 shared['brt'])
        wts_s = (g1, g2, shared['win'], shared['lng'], shared['lnb'], shared['avec'], shared['wb'],
                 shared['wc'], shared['dsk'], shared['wglu'], shared['bglu'], shared['wout'], shared['wrc'],
                 shared['brt'])

        w1b, w3b, w2b = w1[l].astype(BF16), w3[l].astype(BF16), w2[l].astype(BF16)
        gfin = g_final[None]

        h0p = jnp.zeros((B, STATE_COLS), F32)
        x1p, h2p, route_p, hist_p, st_p = _mixer_prompt(xp, ada_p, h0p, wts_p)
        h0s = _pack_state(state_ssm_re[l].astype(F32), state_ssm_im[l].astype(F32))
        w_small = w_masked[:, :Ls, :Ls].reshape(-1)
        b_small = b_s[l][:, :Ls].reshape(-1)
        x1s, h2s, route_s, hist_s, st_s, v_s = _mixer_sample(xs_t, ada_s, h0s, w_small, b_small, wts_s)

        (tables_p,) = _dispatch_tables([(route_p, hist_p)])
        (tables_s,) = _dispatch_tables([(route_s[None], hist_s)])
        xp = _moe(h2p, x1p, ada_p[:, None, :], *tables_p, gfin, w1b, w3b, w2b, n_tiles=B, chunk=256)
        xs_t = _moe(h2s, x1s, ada_s[None], *tables_s, gfin, w1b, w3b, w2b, n_tiles=1, chunk=Bs)
        hr, hi = _unpack_state(st_p)
        p_re.append(hr.astype(state_ssm_re.dtype))
        p_im.append(hi.astype(state_ssm_im.dtype))
        hr, hi = _unpack_state(st_s)
        s_re.append(hr.astype(state_ssm_re.dtype))
        s_im.append(hi.astype(state_ssm_im.dtype))
        s_v.append(v_s.transpose(1, 0, 2))

    y_prompt = xp
    y_sample = xs_t.transpose(1, 0, 2)
    return (y_prompt, y_sample, jnp.stack(p_re), jnp.stack(p_im), jnp.stack(s_re), jnp.stack(s_im), jnp.stack(s_v))
```

```python
import functools

import jax
import jax.numpy as jnp
from jax import lax
from jax.experimental import pallas as pl
from jax.experimental.pallas import tpu as pltpu

F32 = jnp.float32
BF16 = jnp.bfloat16

D_MODEL = 1024
D_A = 512
D_B = 512
N_HEADS = 4
HEAD_DIM = 128
CHUNK = 128
N_SSM_GROUPS = 32
SSM_GROUP = 16
SSM_STATE = 64
N_SLABS = 4
SLAB_GROUPS = N_SSM_GROUPS // N_SLABS
SLAB_STATES = SLAB_GROUPS * SSM_STATE
SLAB_COLS = 2 * SLAB_STATES
STATE_COLS = N_SLABS * SLAB_COLS
N_EXPERT_GROUPS = 4
EXPERTS_PER_GROUP = 8
N_EXPERTS = 32
D_EXPERT = 256
EPS = 1e-6

LANES = 128
SUBLANES = 8
ROUTE_LANES = LANES
MOE_BLOCK = 128
MOE_BLOCK_SHIFT = MOE_BLOCK.bit_length() - 1
assert 1 << MOE_BLOCK_SHIFT == MOE_BLOCK
EXPERTS_PER_STEP = 4
MOE_EPILOGUE_STEPS = 4
PAIR_FIELDS = 4
S_PITCH = CHUNK + SUBLANES
SMEM_PAD = 1024
VMEM_LIMIT = 58 * 1024 * 1024


def _gelu(x):
    c = 0.7978845608028654
    half_x = 0.5 * x
    return half_x + half_x * jnp.tanh(x * (c + (c * 0.044715) * (x * x)))


def _modulate(xf, gain, shift):
    ms = jnp.mean(xf * xf, axis=-1, keepdims=True)
    return xf * lax.rsqrt(ms + EPS) * gain + shift


def _rms(xf, g):
    ms = jnp.mean(xf * xf, axis=-1, keepdims=True)
    return xf * lax.rsqrt(ms + EPS) * g


def _ada_kernel(cp_ref, cs_ref, w_ref, b_ref, op_ref, os_ref):
    mp = cp_ref.shape[0]
    s = jax.nn.silu(jnp.concatenate([cp_ref[...], cs_ref[...]], axis=0)).astype(BF16)
    out = jnp.dot(s, w_ref[...].astype(BF16), preferred_element_type=F32) + b_ref[...]
    op_ref[...] = out[:mp]
    os_ref[...] = out[mp:]


def _ada(c_p, c_s, w, b):
    mp, ms = c_p.shape[0], c_s.shape[0]
    n = w.shape[1]
    bn = 1024
    return pl.pallas_call(
        _ada_kernel,
        grid=(n // bn,),
        in_specs=[pl.BlockSpec((mp, D_MODEL), lambda j: (0, 0)),
                  pl.BlockSpec((ms, D_MODEL), lambda j: (0, 0)),
                  pl.BlockSpec((D_MODEL, bn), lambda j: (0, j)),
                  pl.BlockSpec((1, bn), lambda j: (0, j))],
        out_specs=(pl.BlockSpec((mp, bn), lambda j: (0, j)), pl.BlockSpec((ms, bn), lambda j: (0, j))),
        out_shape=(jax.ShapeDtypeStruct((mp, n), F32), jax.ShapeDtypeStruct((ms, n), F32)),
        name="ada",
    )(c_p, c_s, w, b)


def _disc_kernel(lre_ref, lim_ref, ldt_ref, bre_ref, bim_ref, ar_ref, ai_ref, br_ref, bi_ref):
    dt = jnp.exp(ldt_ref[...])
    lr = lre_ref[...]
    li = lim_ref[...]
    mag = jnp.exp(lr * dt)
    ar = mag * jnp.cos(li * dt)
    ai = mag * jnp.sin(li * dt)
    den = lr * lr + li * li
    cr = ((ar - 1) * lr + ai * li) / den
    ci = (ai * lr - (ar - 1) * li) / den
    ar_ref[...] = ar
    ai_ref[...] = ai
    bre = bre_ref[...]
    bim = bim_ref[...]
    br_ref[...] = cr * bre - ci * bim
    bi_ref[...] = cr * bim + ci * bre


def _discretize(lam_re, lam_im, log_dt, b_re, b_im):
    g, p, h = b_re.shape
    o1 = jax.ShapeDtypeStruct((g, 1, p), F32)
    o2 = jax.ShapeDtypeStruct((g, h, p), F32)
    return pl.pallas_call(_disc_kernel, out_shape=(o1, o1, o2, o2), name="ssm_disc")(
        lam_re.reshape(g, 1, p), lam_im.reshape(g, 1, p), log_dt.reshape(g, 1, 1),
        b_re.transpose(0, 2, 1), b_im.transpose(0, 2, 1))


def _s_pieces(r0, nrows, grp, pitch):
    return [(i * grp, slice((r0 // grp + i) * pitch, (r0 // grp + i) * pitch + grp)) for i in range(nrows // grp)]


def _front(r0, nrows, h_scr, win_ref, lng_ref, lnb_ref, u_scr, vn_scr, s_scr, grp, pitch):
    rows = slice(r0, r0 + nrows)
    proj = jnp.dot(h_scr[rows, :], win_ref[...], preferred_element_type=F32)
    u_scr[rows, :] = _gelu(proj[:, :D_A])
    vraw = _gelu(proj[:, D_A:2 * D_A])
    for h in range(N_HEADS):
        cols = slice(h * HEAD_DIM, (h + 1) * HEAD_DIM)
        vh = vraw[:, cols]
        mu = jnp.mean(vh, axis=-1, keepdims=True)
        dv = vh - mu
        var = jnp.mean(dv * dv, axis=-1, keepdims=True)
        vn_scr[rows, cols] = dv * lax.rsqrt(var + EPS) * lng_ref[:, cols] + lnb_ref[:, cols]
    for k in range(N_SLABS):
        for off, prow in _s_pieces(r0, nrows, grp, pitch):
            s_scr[k, prow, :] = proj[off:off + grp, 2 * D_A + k * LANES:2 * D_A + (k + 1) * LANES]


def _scan_slab(bu, state, avec_ref, k, rows_per_step, steps):
    c_re = slice(0, SLAB_STATES)
    c_im = slice(SLAB_STATES, SLAB_COLS)
    s_re = slice(k * SLAB_STATES, (k + 1) * SLAB_STATES)
    s_im = slice(STATE_COLS // 2 + k * SLAB_STATES, STATE_COLS // 2 + (k + 1) * SLAB_STATES)
    ar = jnp.broadcast_to(avec_ref[:, s_re], (SUBLANES, SLAB_STATES))
    ai = jnp.broadcast_to(avec_ref[:, s_im], (SUBLANES, SLAB_STATES))
    for rc in range(rows_per_step // SUBLANES):
        r0 = rc * SUBLANES
        sr = state[r0:r0 + SUBLANES, s_re]
        si = state[r0:r0 + SUBLANES, s_im]
        for t in range(steps):
            rows = slice(t * rows_per_step + r0, t * rows_per_step + r0 + SUBLANES)
            nr = ar * sr - ai * si + bu[rows, c_re]
            ni = ar * si + ai * sr + bu[rows, c_im]
            bu[rows, c_re] = nr
            bu[rows, c_im] = ni
            sr, si = nr, ni
        state[r0:r0 + SUBLANES, s_re] = sr
        state[r0:r0 + SUBLANES, s_im] = si


def _s5(st_ref, bu_scr, yt_ref, state, avec_ref, wb_ref, wc_ref, rows_per_step, steps):
    for k in range(N_SLABS):
        bu = bu_scr.at[k % 2]
        bu[...] = jnp.dot(st_ref[k].astype(BF16), wb_ref[k], preferred_element_type=F32)
        _scan_slab(bu, state, avec_ref, k, rows_per_step, steps)
        yt_ref[k] = jnp.dot(bu[...].astype(BF16), wc_ref[k], preferred_element_type=F32)


def _back(r0, nrows, s_scr, ab_scr, wglu_ref, bglu_ref, wout_ref, grp, pitch):
    rows = slice(r0, r0 + nrows)
    pieces = _s_pieces(r0, nrows, grp, pitch)
    y = _gelu(jnp.concatenate(
        [jnp.concatenate([s_scr[k, prow, :] for _, prow in pieces], axis=0) for k in range(N_SLABS)], axis=-1))
    gate = jnp.dot(y.astype(BF16), wglu_ref[...], preferred_element_type=F32) + bglu_ref[...]
    ab_scr[rows, D_A:] = (y * jax.nn.sigmoid(gate)).astype(BF16)
    return jnp.dot(ab_scr[rows, :], wout_ref[...], preferred_element_type=F32)


def _route(h2_bf, wrc_ref, brt_ref):
    both = jnp.dot(h2_bf, wrc_ref[...], preferred_element_type=F32)
    logits = both[:, :ROUTE_LANES] + both[:, ROUTE_LANES:] + brt_ref[...]
    n = logits.shape[0]
    lane = lax.broadcasted_iota(jnp.int32, (n, ROUTE_LANES), 1)
    lane_f = lane.astype(F32)
    big = jnp.float32(1e9)
    ninf = jnp.float32(-jnp.inf)
    is_g = lane < N_EXPERT_GROUPS
    gl = jnp.where(is_g, logits, ninf)
    gmax = jnp.max(gl, axis=-1, keepdims=True)
    gidx = jnp.min(jnp.where(gl == gmax, lane_f, big), axis=-1, keepdims=True)
    gsum = jnp.sum(jnp.where(is_g, jnp.exp(logits - gmax), 0.0), axis=-1, keepdims=True)
    g_w = 1.0 / gsum
    elo = N_EXPERT_GROUPS + EXPERTS_PER_GROUP * gidx
    emask = (lane_f >= elo) & (lane_f < elo + EXPERTS_PER_GROUP)
    el = jnp.where(emask, logits, ninf)
    t1 = jnp.max(el, axis=-1, keepdims=True)
    i1 = jnp.min(jnp.where(el == t1, lane_f, big), axis=-1, keepdims=True)
    el2 = jnp.where(lane_f == i1, ninf, el)
    t2 = jnp.max(el2, axis=-1, keepdims=True)
    i2 = jnp.min(jnp.where(el2 == t2, lane_f, big), axis=-1, keepdims=True)
    e21 = jnp.exp(t2 - t1)
    den = 1.0 + e21
    gate1 = g_w * (1.0 / den)
    gate2 = g_w * (e21 / den)
    e1 = i1 - N_EXPERT_GROUPS
    e2 = i2 - N_EXPERT_GROUPS
    packed = jnp.where(lane == 0, e1, jnp.where(lane == 1, e2, jnp.where(lane == 2, gate1,
                                                                           jnp.where(lane == 3, gate2, 0.0))))
    hist = jnp.sum(((lane_f == e1) | (lane_f == e2)).astype(F32), axis=0, keepdims=True)
    return packed.T[:SUBLANES], hist


def _mixer_prompt_kernel(x_ref, ada_ref, h0_ref, g1_ref, g2_ref, win_ref, lng_ref, lnb_ref, wsp_ref, bsp_ref,
                         avec_ref, wb_ref, wc_ref, dsk_ref, wglu_ref, bglu_ref, wout_ref, wrc_ref, brt_ref,
                         x1_ref, h2_ref, route_ref, hist_ref, state_ref,
                         h_scr, u_scr, vn_scr, s_scr, st_scr, yt_scr, bu_scr, ab_scr, hi_scr):
    nb = x_ref.shape[0]
    half = (nb // 2) * CHUNK
    D = D_MODEL
    pitch = S_PITCH
    step = pl.program_id(0)

    @pl.when(step == 0)
    def _():
        state_ref[...] = h0_ref[...]
        hist_ref[...] = jnp.zeros_like(hist_ref)
        hi_scr[...] = jnp.zeros_like(hi_scr)

    def mod(b, i):
        return ada_ref[b:b + 1, i * D:(i + 1) * D]

    def route_chunk(chunk, weight):
        for b in range(nb):
            rows = slice(b * CHUNK, (b + 1) * CHUNK)
            route, hist = _route(hi_scr[rows, :], wrc_ref, brt_ref)
            route_ref[b, chunk] = route
            hist_ref[b:b + 1, :] = hist_ref[b:b + 1, :] + weight * hist

    route_chunk(jnp.maximum(step - 1, 0), (step > 0).astype(F32))

    for b in range(nb):
        hb = _modulate(x_ref[b], g1_ref[...] * (1 + mod(b, 1)), mod(b, 0))
        h_scr[b * CHUNK:(b + 1) * CHUNK, :] = hb.astype(BF16)

    for r0 in (0, half):
        _front(r0, half, h_scr, win_ref, lng_ref, lnb_ref, u_scr, vn_scr, s_scr, CHUNK, pitch)

    for b in range(nb):
        rows = slice(b * CHUNK, (b + 1) * CHUNK)
        for h in range(N_HEADS):
            cols = slice(h * HEAD_DIM, (h + 1) * HEAD_DIM)
            mixed = jnp.dot(wsp_ref[h], vn_scr[rows, cols].astype(BF16), preferred_element_type=F32) + bsp_ref[h]
            ab_scr[rows, cols] = (u_scr[rows, cols] * mixed).astype(BF16)

    for k in range(N_SLABS):
        for t in range(CHUNK):
            st_scr[k, t * nb:(t + 1) * nb, :] = s_scr[k, pl.ds(t, nb, stride=pitch), :]
    _s5(st_scr, bu_scr, yt_scr, state_ref, avec_ref, wb_ref, wc_ref, nb, CHUNK)
    for k in range(N_SLABS):
        dsk = dsk_ref[:, k * LANES:(k + 1) * LANES]
        for t in range(CHUNK):
            sel = pl.ds(t, nb, stride=pitch)
            s_scr[k, sel, :] = yt_scr[k, t * nb:(t + 1) * nb, :] + dsk * s_scr[k, sel, :]

    for r0 in (0, half):
        mix = _back(r0, half, s_scr, ab_scr, wglu_ref, bglu_ref, wout_ref, CHUNK, pitch)
        for bl in range(nb // 2):
            b = r0 // CHUNK + bl
            x1 = x_ref[b] + mod(b, 2) * mix[bl * CHUNK:(bl + 1) * CHUNK, :]
            x1_ref[b] = x1
            h2 = _modulate(x1, g2_ref[...] * (1 + mod(b, 4)), mod(b, 3))
            hi = h2.astype(BF16)
            hi_scr[b * CHUNK:(b + 1) * CHUNK, :] = hi
            h2_ref[b] = hi

    @pl.when(step == pl.num_programs(0) - 1)
    def _():
        route_chunk(step, 1.0)


def _const_spec(shape):
    nd = len(shape)
    return pl.BlockSpec(shape, lambda *_: (0,) * nd, pipeline_mode=pl.Buffered(1))


def _mixer_prompt(x, ada, h0, wts):
    nb, seq, D = x.shape
    n_chunks = seq // CHUNK
    R = nb * CHUNK
    weight_specs = [_const_spec(w.shape) for w in wts]
    in_specs = [pl.BlockSpec((nb, CHUNK, D), lambda i: (0, i, 0)),
                _const_spec(ada.shape), _const_spec(h0.shape)] + weight_specs
    out_shape = (jax.ShapeDtypeStruct((nb, seq, D), F32),
                 jax.ShapeDtypeStruct((nb, seq, D), BF16),
                 jax.ShapeDtypeStruct((nb, n_chunks, SUBLANES, ROUTE_LANES), F32),
                 jax.ShapeDtypeStruct((nb, ROUTE_LANES), F32),
                 jax.ShapeDtypeStruct((nb, STATE_COLS), F32))
    out_specs = (pl.BlockSpec((nb, CHUNK, D), lambda i: (0, i, 0)),
                 pl.BlockSpec((nb, CHUNK, D), lambda i: (0, i, 0)),
                 pl.BlockSpec((nb, n_chunks, SUBLANES, ROUTE_LANES), lambda i: (0, 0, 0, 0)),
                 pl.BlockSpec((nb, ROUTE_LANES), lambda i: (0, 0)),
                 pl.BlockSpec((nb, STATE_COLS), lambda i: (0, 0)))
    scratch = [pltpu.VMEM((R, D), BF16),
               pltpu.VMEM((R, D_A), F32),
               pltpu.VMEM((R, D_A), F32),
               pltpu.VMEM((N_SLABS, nb * S_PITCH, LANES), F32),
               pltpu.VMEM((N_SLABS, R, LANES), F32),
               pltpu.VMEM((N_SLABS, R, LANES), F32),
               pltpu.VMEM((2, R, SLAB_COLS), F32),
               pltpu.VMEM((R, D), BF16),
               pltpu.VMEM((R, D), BF16)]
    return pl.pallas_call(
        _mixer_prompt_kernel,
        grid=(n_chunks,),
        in_specs=in_specs,
        out_specs=out_specs,
        out_shape=out_shape,
        scratch_shapes=scratch,
        compiler_params=pltpu.CompilerParams(
            dimension_semantics=("arbitrary",), vmem_limit_bytes=VMEM_LIMIT,
            allow_input_fusion=[i in (5, 14, 16) for i in range(3 + len(wts))]),
        name="mixer_prompt",
    )(x, ada, h0, *wts)


def _mixer_sample_kernel(wsm_ref, bsm_ref, x_ref, ada_ref, h0_ref, g1_ref, g2_ref, win_ref, lng_ref, lnb_ref,
                         avec_ref, wb_ref, wc_ref, dsk_ref, wglu_ref, bglu_ref, wout_ref, wrc_ref, brt_ref,
                         x1_ref, h2_ref, route_ref, hist_ref, state_ref, v_ref,
                         h_scr, u_scr, vn_scr, s_scr, yt_scr, bu_scr, ab_scr):
    T, nb, D = x_ref.shape
    R = T * nb
    half = R // 2

    def mod(i):
        return ada_ref[:, i * D:(i + 1) * D]

    state_ref[...] = h0_ref[...]
    gain1 = g1_ref[...] * (1 + mod(1))
    gain2 = g2_ref[...] * (1 + mod(4))
    for t in range(T):
        ht = _modulate(x_ref[t], gain1, mod(0))
        h_scr[t * nb:(t + 1) * nb, :] = ht.astype(BF16)

    for r0 in (0, half):
        _front(r0, half, h_scr, win_ref, lng_ref, lnb_ref, u_scr, vn_scr, s_scr, half, half)

    for t in range(T):
        rows = slice(t * nb, (t + 1) * nb)
        v_ref[t] = vn_scr[rows, :]
        for h in range(N_HEADS):
            cols = slice(h * HEAD_DIM, (h + 1) * HEAD_DIM)
            acc = jnp.full((nb, HEAD_DIM), bsm_ref[h * T + t], F32)
            for s in range(t + 1):
                acc = acc + wsm_ref[(h * T + t) * T + s] * vn_scr[s * nb:(s + 1) * nb, cols]
            ab_scr[rows, cols] = (u_scr[rows, cols] * acc).astype(BF16)

    _s5(s_scr, bu_scr, yt_scr, state_ref, avec_ref, wb_ref, wc_ref, nb, T)
    for k in range(N_SLABS):
        s_scr[k] = yt_scr[k] + dsk_ref[:, k * LANES:(k + 1) * LANES] * s_scr[k]

    hist_total = jnp.zeros((1, ROUTE_LANES), F32)
    for r0 in (0, half):
        mix = _back(r0, half, s_scr, ab_scr, wglu_ref, bglu_ref, wout_ref, half, half)
        for tl in range(T // 2):
            t = r0 // nb + tl
            x1 = x_ref[t] + mod(2) * mix[tl * nb:(tl + 1) * nb, :]
            x1_ref[t] = x1
            h2 = _modulate(x1, gain2, mod(3))
            hi = h2.astype(BF16)
            h2_ref[t] = hi
            route, hist = _route(hi, wrc_ref, brt_ref)
            route_ref[t] = route
            hist_total = hist_total + hist
    hist_ref[...] = hist_total


def _mixer_sample(x_t, ada, h0, w_small, b_small, wts):
    T, nb, D = x_t.shape
    R = T * nb
    smem = pl.BlockSpec(memory_space=pltpu.SMEM)
    out_shape = (jax.ShapeDtypeStruct((T, nb, D), F32),
                 jax.ShapeDtypeStruct((T, nb, D), BF16),
                 jax.ShapeDtypeStruct((T, SUBLANES, ROUTE_LANES), F32),
                 jax.ShapeDtypeStruct((1, ROUTE_LANES), F32),
                 jax.ShapeDtypeStruct((nb, STATE_COLS), F32),
                 jax.ShapeDtypeStruct((T, nb, D_A), F32))
    scratch = [pltpu.VMEM((R, D), BF16),
               pltpu.VMEM((R, D_A), F32),
               pltpu.VMEM((R, D_A), F32),
               pltpu.VMEM((N_SLABS, R, LANES), F32),
               pltpu.VMEM((N_SLABS, R, LANES), F32),
               pltpu.VMEM((2, R, SLAB_COLS), F32),
               pltpu.VMEM((R, D), BF16)]
    vmem = pl.BlockSpec(memory_space=pltpu.VMEM)
    return pl.pallas_call(
        _mixer_sample_kernel,
        in_specs=[smem, smem] + [vmem] * (3 + len(wts)),
        out_specs=(vmem,) * 6,
        out_shape=out_shape,
        scratch_shapes=scratch,
        compiler_params=pltpu.CompilerParams(vmem_limit_bytes=VMEM_LIMIT),
        name="mixer_sample",
    )(w_small, b_small, x_t, ada, h0, *wts)


def _rows_to_tiles(tiles_ref, row0, val):
    n = val.shape[0]
    for c in range(val.shape[1] // LANES):
        tiles_ref[pl.ds(row0 * SUBLANES + c, n, stride=SUBLANES), :] = val[:, c * LANES:(c + 1) * LANES]


def _tiles_to_rows(tiles_ref, row0, n):
    return jnp.concatenate([tiles_ref[pl.ds(row0 * SUBLANES + c, n, stride=SUBLANES), :] for c in range(SUBLANES)],
                           axis=-1)


def _moe_kernel(starts_ref, tok_ref, pos_ref, gate_ref, h2_ref, x1_ref, ada_ref, gfin_ref, w1_ref, w3_ref, w2_ref,
                out_ref, h_tiles, y_tiles, xb_even, xb_odd, cp_smem, tab_smem, *, chunk):
    t = pl.program_id(0)
    e = pl.program_id(1)
    n_expert_steps = N_EXPERTS // EXPERTS_PER_STEP
    ngrp, rows_g, D = h2_ref.shape
    n_tok = ngrp * rows_g
    gb, rb, _ = x1_ref.shape
    epi_tok = gb * rb
    mrows = ada_ref.shape[1]

    def tile_of(row):
        return pl.ds(pl.multiple_of(row * SUBLANES, SUBLANES), SUBLANES)

    def tile_at(row8):
        return pl.ds(pl.multiple_of(row8, SUBLANES), SUBLANES)

    @pl.when(e == 0)
    def _():
        for gi in range(ngrp):
            for ci in range(rows_g // chunk):
                h2 = h2_ref[gi, ci * chunk:(ci + 1) * chunk, :].astype(F32)
                _rows_to_tiles(h_tiles, gi * rows_g + ci * chunk, h2)

    @pl.when(e == 0)
    def _():
        gp = jnp.int32(0)
        for s in range(N_EXPERTS // EXPERTS_PER_STEP):
            cp_smem[s] = gp
            st = [starts_ref[t, s * EXPERTS_PER_STEP + i] for i in range(EXPERTS_PER_STEP + 1)]
            first = [jnp.int32(0)]
            for i in range(EXPERTS_PER_STEP):
                first.append(first[i] + lax.shift_right_logical(st[i + 1] - st[i] + (MOE_BLOCK - 1), MOE_BLOCK_SHIFT))
            nb = first[EXPERTS_PER_STEP]

            def locate(f, st=st, first=first):
                el, fb, sb = jnp.int32(0), first[0], st[0]
                for i in range(1, EXPERTS_PER_STEP):
                    hit = f >= first[i]
                    el = jnp.where(hit, i, el)
                    fb = jnp.where(hit, first[i], fb)
                    sb = jnp.where(hit, st[i], sb)
                return sb + (f - fb) * MOE_BLOCK, el

            def add_pair(p, gp, nb=nb, locate=locate):
                base_a, el_a = locate(2 * p)
                base_b, el_b = locate(jnp.minimum(2 * p + 1, nb - 1))
                for i, v in enumerate((base_a, base_b, el_a, el_b)):
                    tab_smem[PAIR_FIELDS * gp + i] = v
                return gp + 1

            gp = lax.fori_loop(0, lax.shift_right_logical(nb + 1, 1), add_pair, gp)
        cp_smem[N_EXPERTS // EXPERTS_PER_STEP] = gp
        for i in range(PAIR_FIELDS):
            tab_smem[PAIR_FIELDS * gp + i] = jnp.int32(0)

    def pair_entry(gp):
        return [tab_smem[PAIR_FIELDS * gp + i] for i in range(PAIR_FIELDS)]

    @pl.when(e == 0)
    def _():
        base_a, base_b, _, _ = pair_entry(0)

        def gather8(j8, c):
            for jj in range(SUBLANES):
                j = j8 * SUBLANES + jj
                xb_even[tile_of(j), :] = h_tiles[tile_at(tok_ref[base_a + j]), :]
                xb_even[tile_of(MOE_BLOCK + j), :] = h_tiles[tile_at(tok_ref[base_b + j]), :]
            return c

        lax.fori_loop(0, MOE_BLOCK // SUBLANES, gather8, 0)

    def pair_body(gp, cur_tiles, next_tiles):
        next_a, next_b, _, _ = pair_entry(gp + 1)
        n_slices = 8
        per = 2 * MOE_BLOCK // n_slices
        slices = iter(range(n_slices))

        def gather_slice():
            s = next(slices)
            for r in range(s * per, (s + 1) * per):
                tok8 = tok_ref[next_a + r] if r < MOE_BLOCK else tok_ref[next_b + r - MOE_BLOCK]
                next_tiles[r * SUBLANES:(r + 1) * SUBLANES, :] = h_tiles[tile_at(tok8), :]

        base_a, base_b, el_a, el_b = pair_entry(gp)
        xa = _tiles_to_rows(cur_tiles, 0, MOE_BLOCK).astype(BF16)
        gather_slice()
        a1 = jnp.dot(xa, w1_ref[el_a], preferred_element_type=F32)
        gather_slice()
        a3 = jnp.dot(xa, w3_ref[el_a], preferred_element_type=F32)
        gather_slice()
        xb = _tiles_to_rows(cur_tiles, MOE_BLOCK, MOE_BLOCK).astype(BF16)
        b1 = jnp.dot(xb, w1_ref[el_b], preferred_element_type=F32)
        gather_slice()
        b3 = jnp.dot(xb, w3_ref[el_b], preferred_element_type=F32)
        gather_slice()
        ya = jnp.dot((jax.nn.silu(a1) * a3).astype(BF16), w2_ref[el_a], preferred_element_type=F32)
        gather_slice()
        yb = jnp.dot((jax.nn.silu(b1) * b3).astype(BF16), w2_ref[el_b], preferred_element_type=F32)
        gather_slice()
        _rows_to_tiles(y_tiles, base_a, ya)
        gather_slice()
        _rows_to_tiles(y_tiles, base_b, yb)

    def pairs(gp, count):
        def run(first, second):
            bufs = (first, second)
            for i in range(count):
                pair_body(gp + i, bufs[i % 2], bufs[(i + 1) % 2])

        @pl.when((gp & 1) == 0)
        def _():
            run(xb_even, xb_odd)

        @pl.when((gp & 1) == 1)
        def _():
            run(xb_odd, xb_even)

    @pl.when(e < n_expert_steps)
    def _():
        first, last = cp_smem[e], cp_smem[e + 1]
        n_double = lax.shift_right_logical(last - first, 1)

        def double(i, carry):
            pairs(first + 2 * i, 2)
            return carry

        lax.fori_loop(0, n_double, double, 0)

        @pl.when(((last - first) & 1) == 1)
        def _():
            pairs(last - 1, 1)

    @pl.when(e >= n_expert_steps)
    def _():
        tok0 = (e - n_expert_steps) * epi_tok

        def combine8(t8, c):
            for tt in range(SUBLANES):
                tok = tok0 + t8 * SUBLANES + tt
                y0 = y_tiles[tile_at(pos_ref[tok]), :]
                y1 = y_tiles[tile_at(pos_ref[n_tok + tok]), :]
                h_tiles[tile_of(tok), :] = gate_ref[tok] * y0 + gate_ref[n_tok + tok] * y1
            return c

        lax.fori_loop(0, epi_tok // SUBLANES, combine8, 0)
        sub = min(rb, chunk)
        for gi in range(gb):
            for ci in range(rb // sub):
                rows = slice(ci * sub, (ci + 1) * sub)
                gt2 = ada_ref[0, :, 5 * D:6 * D] if mrows == 1 else ada_ref[0, rows, 5 * D:6 * D]
                moe = _tiles_to_rows(h_tiles, tok0 + gi * rb + ci * sub, sub)
                x2 = x1_ref[gi, rows, :] + gt2 * moe
                out_ref[gi, rows, :] = _rms(x2, gfin_ref[...])


def _moe(h2, x1, ada, starts, tok, pos, gate, gfin, w1, w3, w2, *, n_tiles, chunk):
    ngrp = x1.shape[0] // n_tiles
    rows_g, D = x1.shape[1], x1.shape[2]
    mrows = ada.shape[1]
    Tt = ngrp * rows_g
    n_expert_steps = N_EXPERTS // EXPERTS_PER_STEP
    epi_tok = Tt // MOE_EPILOGUE_STEPS
    if ngrp == 1:
        epi_block = (1, epi_tok, D)

        def epi_map(t, e, st):
            return (t, jnp.maximum(e - n_expert_steps, 0), 0)
    else:
        assert n_tiles == 1 and epi_tok % rows_g == 0
        epi_block = (epi_tok // rows_g, rows_g, D)

        def epi_map(t, e, st):
            return (jnp.maximum(e - n_expert_steps, 0), 0, 0)

    def weight_map(t, e, st):
        return (jnp.minimum(e, n_expert_steps - 1), 0, 0)
    lp = tok.shape[0] // n_tiles
    max_pairs = (2 * Tt // MOE_BLOCK + N_EXPERTS + N_EXPERTS // EXPERTS_PER_STEP) // 2 + 1
    grid_spec = pltpu.PrefetchScalarGridSpec(
        num_scalar_prefetch=1,
        grid=(n_tiles, n_expert_steps + MOE_EPILOGUE_STEPS),
        in_specs=[
            pl.BlockSpec((lp,), lambda t, e, st: (t,), memory_space=pltpu.SMEM),
            pl.BlockSpec((2 * Tt,), lambda t, e, st: (t,), memory_space=pltpu.SMEM),
            pl.BlockSpec((2 * Tt,), lambda t, e, st: (t,), memory_space=pltpu.SMEM),
            pl.BlockSpec((ngrp, rows_g, D), lambda t, e, st: (t, 0, 0)),
            pl.BlockSpec(epi_block, epi_map),
            pl.BlockSpec((1, mrows, 6 * D), lambda t, e, st: (t, 0, 0)),
            pl.BlockSpec((1, D), lambda t, e, st: (0, 0)),
            pl.BlockSpec((EXPERTS_PER_STEP, D, D_EXPERT), weight_map),
            pl.BlockSpec((EXPERTS_PER_STEP, D, D_EXPERT), weight_map),
            pl.BlockSpec((EXPERTS_PER_STEP, D_EXPERT, D), weight_map),
        ],
        out_specs=pl.BlockSpec(epi_block, epi_map),
        scratch_shapes=[pltpu.VMEM((Tt * SUBLANES, LANES), F32),
                        pltpu.VMEM(((2 * Tt + MOE_BLOCK) * SUBLANES, LANES), F32),
                        pltpu.VMEM((2 * MOE_BLOCK * SUBLANES, LANES), F32),
                        pltpu.VMEM((2 * MOE_BLOCK * SUBLANES, LANES), F32),
                        pltpu.SMEM((N_EXPERTS // EXPERTS_PER_STEP + 1,), jnp.int32),
                        pltpu.SMEM((max_pairs * PAIR_FIELDS,), jnp.int32)],
    )
    return pl.pallas_call(
        functools.partial(_moe_kernel, chunk=chunk),
        grid_spec=grid_spec,
        out_shape=jax.ShapeDtypeStruct(x1.shape, F32),
        compiler_params=pltpu.CompilerParams(dimension_semantics=("arbitrary", "arbitrary"),
                                             vmem_limit_bytes=VMEM_LIMIT),
        name="moe",
    )(starts, tok, pos, gate, h2, x1, ada, gfin, w1, w3, w2)


def _dispatch_tables(routes):
    tile_tokens = [r.shape[1] * ROUTE_LANES for r, _ in routes]
    t_max = max(tile_tokens)
    es, gs, masks = [], [], []
    for (route, _), Tt in zip(routes, tile_tokens):
        assert Tt & (Tt - 1) == 0
        n = route.shape[0]
        by_k = route[:, :, :4, :].transpose(0, 2, 1, 3).reshape(n, 4, Tt)
        extra = 2 * (t_max - Tt)
        es.append(jnp.pad(by_k[:, :2].astype(jnp.int32).reshape(n, 2 * Tt), ((0, 0), (0, extra)),
                          constant_values=N_EXPERTS))
        gs.append(jnp.pad(by_k[:, 2:].reshape(n, 2 * Tt), ((0, 0), (0, extra))))
        masks.append(jnp.full((n, 1), Tt - 1, jnp.int32))
    flat_e = jnp.concatenate(es, axis=0)
    flat_g = jnp.concatenate(gs, axis=0)
    n_tiles = flat_e.shape[0]
    order = jnp.argsort(flat_e, axis=-1, stable=False).astype(jnp.int32)
    pos = jnp.argsort(order, axis=-1, stable=False).astype(jnp.int32) * SUBLANES
    tok_s = (order & jnp.concatenate(masks, axis=0)) * SUBLANES
    counts = jnp.concatenate([h[:, :N_EXPERTS] for _, h in routes], axis=0).astype(jnp.int32)
    starts = jnp.concatenate([jnp.zeros((n_tiles, 1), jnp.int32), jnp.cumsum(counts, axis=-1, dtype=jnp.int32)],
                             axis=-1)

    def tok_len(Tt):
        return -(-(2 * Tt + MOE_BLOCK) // SMEM_PAD) * SMEM_PAD

    tok_p = jnp.pad(tok_s, ((0, 0), (0, tok_len(t_max) - 2 * t_max)))
    Tt = t_max


    out, r0 = [], 0
    for (route, _), Tt in zip(routes, tile_tokens):
        r1 = r0 + route.shape[0]
        out.append((starts[r0:r1], tok_p[r0:r1, :tok_len(Tt)].reshape(-1),
                    pos[r0:r1, :2 * Tt].reshape(-1), flat_g[r0:r1, :2 * Tt].reshape(-1)))
        r0 = r1
    return out


def _pack_state(h_re, h_im):
    b = h_re.shape[0]
    return jnp.concatenate([h_re.reshape(b, STATE_COLS // 2), h_im.reshape(b, STATE_COLS // 2)], axis=-1)


def _unpack_state(st):
    b = st.shape[0]
    return (st[:, :STATE_COLS // 2].reshape(b, N_SSM_GROUPS, SSM_STATE),
            st[:, STATE_COLS // 2:].reshape(b, N_SSM_GROUPS, SSM_STATE))


def kernel(x_prompt, x_sample, state_ssm_re, state_ssm_im, c_prompt, c_sample, w_ada, b_ada, g_norm1, g_norm2, w_in, ln_g, ln_b, w_s, b_s, lam_re, lam_im, log_dt, ssm_b_re, ssm_b_im, ssm_c_re, ssm_c_im, ssm_d, w_glu, b_glu, w_out, w_group, b_group, w_expert, b_expert, w1, w3, w2, g_final):
    depth = w_ada.shape[0]
    assert depth == 1, "the final RMSNorm is fused into the (single) layer's MoE epilogue"
    B, L, D = x_prompt.shape
    Bs, Ls, _ = x_sample.shape
    xp = x_prompt
    xs_t = x_sample.transpose(1, 0, 2)
    eye = jnp.eye(SLAB_GROUPS, dtype=F32)
    tril = jnp.tril(jnp.ones((CHUNK, CHUNK), dtype=bool))
    p_re, p_im, s_re, s_im, s_v = [], [], [], [], []
    for l in range(depth):
        ada_p, ada_s = _ada(c_prompt, c_sample, w_ada[l], b_ada[l][None])

        ar, ai, br, bi = _discretize(lam_re[l], lam_im[l], log_dt[l], ssm_b_re[l], ssm_b_im[l])
        avec = jnp.concatenate([ar.reshape(1, STATE_COLS // 2), ai.reshape(1, STATE_COLS // 2)], axis=-1)

        def blockdiag_in(w):
            w4 = w.reshape(N_SLABS, SLAB_GROUPS, SSM_GROUP, SSM_STATE)
            return jnp.einsum('kghp,gG->kghGp', w4, eye).reshape(N_SLABS, LANES, SLAB_STATES)

        def blockdiag_out(w):
            w4 = w.reshape(N_SLABS, SLAB_GROUPS, SSM_GROUP, SSM_STATE)
            return jnp.einsum('kghp,gG->kgpGh', w4, eye).reshape(N_SLABS, SLAB_STATES, LANES)

        wb = jnp.concatenate([blockdiag_in(br), blockdiag_in(bi)], axis=-1).astype(BF16)
        wc = jnp.concatenate([blockdiag_out(ssm_c_re[l]), -blockdiag_out(ssm_c_im[l])], axis=1).astype(BF16)

        lane_pad = ROUTE_LANES - N_EXPERT_GROUPS - N_EXPERTS
        wr = jnp.pad(jnp.concatenate([w_group[l], w_expert[l]], axis=1), ((0, 0), (0, lane_pad)))
        wr_hi = wr.astype(BF16)
        wr_lo = (wr - wr_hi.astype(F32)).astype(BF16)
        br_t = jnp.pad(jnp.concatenate([b_group[l], b_expert[l]]), (0, lane_pad))[None]

        g1 = g_norm1[l][None]
        g2 = g_norm2[l][None]
        shared = dict(
            win=w_in[l].astype(BF16), lng=ln_g[l].reshape(1, D_A), lnb=ln_b[l].reshape(1, D_A),
            avec=avec, wb=wb, wc=wc, dsk=ssm_d[l].reshape(1, D_B), wglu=w_glu[l].astype(BF16),
            bglu=b_glu[l][None], wout=w_out[l].astype(BF16), wrc=jnp.concatenate([wr_hi, wr_lo], axis=1), brt=br_t)
        w_masked = jnp.where(tril[None], w_s[l], jnp.zeros_like(w_s[l]))
        wsp = w_masked.astype(BF16)
        bsp = jnp.broadcast_to(b_s[l][:, :, None], (N_HEADS, CHUNK, HEAD_DIM))
        wts_p = (g1, g2, shared['win'], shared['lng'], shared['lnb'], wsp, bsp, shared['avec'], shared['wb'],
                 shared['wc'], shared['dsk'], shared['wglu'], shared['bglu'], shared['wout'], shared['wrc'],
                 shared['brt'])
        wts_s = (g1, g2, shared['win'], shared['lng'], shared['lnb'], shared['avec'], shared['wb'],
                 shared['wc'], shared['dsk'], shared['wglu'], shared['bglu'], shared['wout'], shared['wrc'],
                 shared['brt'])

        w1b, w3b, w2b = w1[l].astype(BF16), w3[l].astype(BF16), w2[l].astype(BF16)
        gfin = g_final[None]

        h0p = jnp.zeros((B, STATE_COLS), F32)
        x1p, h2p, route_p, hist_p, st_p = _mixer_prompt(xp, ada_p, h0p, wts_p)
        h0s = _pack_state(state_ssm_re[l].astype(F32), state_ssm_im[l].astype(F32))
        w_small = w_masked[:, :Ls, :Ls].reshape(-1)
        b_small = b_s[l][:, :Ls].reshape(-1)
        x1s, h2s, route_s, hist_s, st_s, v_s = _mixer_sample(xs_t, ada_s, h0s, w_small, b_small, wts_s)

        (tables_p,) = _dispatch_tables([(route_p, hist_p)])
        (tables_s,) = _dispatch_tables([(route_s[None], hist_s)])
        xp = _moe(h2p, x1p, ada_p[:, None, :], *tables_p, gfin, w1b, w3b, w2b, n_tiles=B, chunk=256)
        xs_t = _moe(h2s, x1s, ada_s[None], *tables_s, gfin, w1b, w3b, w2b, n_tiles=1, chunk=Bs)
        hr, hi = _unpack_state(st_p)
        p_re.append(hr.astype(state_ssm_re.dtype))
        p_im.append(hi.astype(state_ssm_im.dtype))
        hr, hi = _unpack_state(st_s)
        s_re.append(hr.astype(state_ssm_re.dtype))
        s_im.append(hi.astype(state_ssm_im.dtype))
        s_v.append(v_s.transpose(1, 0, 2))

    y_prompt = xp
    y_sample = xs_t.transpose(1, 0, 2)
    return (y_prompt, y_sample, jnp.stack(p_re), jnp.stack(p_im), jnp.stack(s_re), jnp.stack(s_im), jnp.stack(s_v))
```
